```python
import jax, jax.numpy as jnp
from jax import lax
import numpy as np


D_MODEL = 1024
BATCH = 16
SEQ = 2048
DEPTH = 2

GRID_W = 64
CTX_LEN = 256
D_MIX = D_MODEL
EPS = 1e-6

A_GROUPS = 4
A_WIDTH = D_MIX // 2
A_GROUP_DIM = A_WIDTH // A_GROUPS
CHUNK_A = 128
ROWS_PER_CHUNK = CHUNK_A // GRID_W

B_HEADS = 4
B_WIDTH = D_MIX - A_WIDTH
B_DV = B_WIDTH // B_HEADS
B_DK = B_DV // 2
B_KEY_WIDTH = B_HEADS * B_DK
GATE_RANK = 16
GATE_TAU = 16.0
CHUNK_B = 64

N_GROUPS = 8
EXPERTS_PER_GROUP = 8
N_EXPERTS = N_GROUPS * EXPERTS_PER_GROUP
TOP_K = 2
D_EXPERT = D_MODEL // 2
MOE_BLOCK = 128

OFF_AU = 0
OFF_AV = OFF_AU + A_WIDTH
OFF_BQ = OFF_AV + A_WIDTH
OFF_BR = OFF_BQ + B_KEY_WIDTH
OFF_BK = OFF_BR + B_WIDTH
OFF_BV = OFF_BK + B_KEY_WIDTH
OFF_BG = OFF_BV + B_WIDTH
D_IN = OFF_BG + 2 * GATE_RANK

kernel_name = 'hymba_gmlp_gla_hmoe_dit'


def rmsnorm(x, g):
    xf = x.astype(jnp.float32)
    y = xf * lax.rsqrt(jnp.mean(xf * xf, axis=-1, keepdims=True) + EPS)
    return (y * g.astype(jnp.float32)).astype(x.dtype)


def modulate(h, shift, scale):
    return h * (1 + scale) + shift


def spatial_gating(zu, zv, n_chunks, ln_g, ln_b, w_sp, b_sp):
    Bn, T, _ = zu.shape
    u = jax.nn.gelu(zu)
    v = jax.nn.gelu(zv).reshape(Bn, n_chunks, CHUNK_A, A_GROUPS, A_GROUP_DIM).astype(jnp.float32)
    mu = jnp.mean(v, axis=-1, keepdims=True)
    var = jnp.mean(jnp.square(v - mu), axis=-1, keepdims=True)
    v = ((v - mu) * lax.rsqrt(var + EPS)).astype(zv.dtype)
    v = v * ln_g.reshape(A_GROUPS, A_GROUP_DIM) + ln_b.reshape(A_GROUPS, A_GROUP_DIM)
    s = jnp.einsum('gpq,bnqgc->bnpgc', w_sp, v) + b_sp.T[:, :, None]
    return u * s.reshape(Bn, T, A_WIDTH)


def heads(z, d):
    Bn, T, W = z.shape
    return z.reshape(Bn, T, W // d, d).transpose(0, 2, 1, 3).astype(jnp.float32)


def flip(t):
    return jnp.flip(t, axis=2)


def gla_logdecay(zg, w_up, b_up):
    Bn, T, _ = zg.shape
    logit = (zg @ w_up + b_up).astype(jnp.float32)
    la = jax.nn.log_sigmoid(logit) / GATE_TAU
    return la.reshape(Bn, T, B_HEADS, B_DK).transpose(0, 2, 1, 3)


def gla_inputs(zkvg, w_gate_up, b_gate):
    Bn, T, _ = zkvg.shape
    k = heads(zkvg[..., :B_KEY_WIDTH], B_DK)
    v = heads(zkvg[..., B_KEY_WIDTH:B_KEY_WIDTH + B_WIDTH], B_DV)
    zg = zkvg[..., B_KEY_WIDTH + B_WIDTH:].reshape(Bn, T, 2, GATE_RANK)
    la_f = gla_logdecay(zg[:, :, 0], w_gate_up[0], b_gate[0])
    la_b = gla_logdecay(zg[:, :, 1], w_gate_up[1], b_gate[1])
    return k, v, la_f, la_b


def to_chunks(t):
    Bn, H, T, d = t.shape
    return t.reshape(Bn, H, T // CHUNK_B, CHUNK_B, d)


def gla_states(k, v, la, s0):
    kc, vc, lc = to_chunks(k), to_chunks(v), to_chunks(la)
    b = jnp.cumsum(lc, axis=3)
    b_last = b[..., -1:, :]
    kv = jnp.einsum('bhncd,bhnce->bhnde', kc * jnp.exp(b_last - b), vc)
    decay = jnp.exp(b_last[..., 0, :])

    def step(s, inp):
        dec, kvn = inp
        return dec[..., None] * s + kvn, s

    s_final, s_prev = lax.scan(step, s0, (jnp.moveaxis(decay, 2, 0), jnp.moveaxis(kv, 2, 0)))
    return jnp.moveaxis(s_prev, 0, 2), s_final, b


def gla_direction(q, k, v, la, s0):
    s_prev, s_final, b = gla_states(k, v, la, s0)
    qc, kc, vc = to_chunks(q), to_chunks(k), to_chunks(v)
    qe = qc * jnp.exp(b)
    ke = kc * jnp.exp(-b)
    att = jnp.einsum('bhnid,bhnjd->bhnij', qe, ke)
    mask = jnp.tril(jnp.ones((CHUNK_B, CHUNK_B), dtype=bool))
    att = jnp.where(mask, att, 0.0)
    o = jnp.einsum('bhnij,bhnje->bhnie', att, vc) + jnp.einsum('bhnid,bhnde->bhnie', qe, s_prev)
    Bn, H, T, _ = q.shape
    return o.reshape(Bn, H, T, B_DV), s_final


def gla_bidir(q, k, v, la_f, la_b, s0_f, s0_b):
    o_f, s_f = gla_direction(q, k, v, la_f, s0_f)
    o_b, s_b = gla_direction(flip(q), flip(k), flip(v), flip(la_b), s0_b)
    return o_f + flip(o_b), s_f, s_b


def gla_merge(o, r, g_gla):
    o = o.transpose(0, 2, 1, 3)
    o = o * lax.rsqrt(jnp.mean(o * o, axis=-1, keepdims=True) + EPS) * g_gla.reshape(B_HEADS, B_DV).astype(jnp.float32)
    Bn, T = o.shape[0], o.shape[1]
    return o.reshape(Bn, T, B_WIDTH).astype(r.dtype) * jax.nn.silu(r)


def hier_moe(t, wrg, brg, wre, bre, w1, w3, w2):
    T, D = t.shape
    pg = jax.nn.softmax((t @ wrg + brg).astype(jnp.float32), axis=-1)
    p_top, g_idx = lax.top_k(pg, 1)
    le = (t @ wre + bre).astype(jnp.float32).reshape(T, N_GROUPS, EXPERTS_PER_GROUP)
    le_sel = jnp.take_along_axis(le, g_idx[:, :, None], axis=1)[:, 0]
    l_top, e_local = lax.top_k(le_sel, TOP_K)
    gate = p_top * jax.nn.softmax(l_top, axis=-1)
    expert = g_idx * EXPERTS_PER_GROUP + e_local
    M = T * TOP_K
    e_flat = expert.reshape(-1)
    tok_flat = jnp.repeat(jnp.arange(T, dtype=jnp.int32), TOP_K)
    gate_flat = gate.reshape(-1)
    order = jnp.argsort(e_flat)
    e_sorted = e_flat[order]
    counts = jnp.bincount(e_flat, length=N_EXPERTS)
    starts = jnp.cumsum(counts) - counts
    padded = (counts + MOE_BLOCK - 1) // MOE_BLOCK * MOE_BLOCK
    pad_ends = jnp.cumsum(padded)
    pad_starts = pad_ends - padded
    dest = pad_starts[e_sorted] + jnp.arange(M, dtype=jnp.int32) - starts[e_sorted]
    n_blocks = (M + N_EXPERTS * (MOE_BLOCK - 1) + MOE_BLOCK - 1) // MOE_BLOCK
    P = n_blocks * MOE_BLOCK
    buf_tok = jnp.full((P,), T, dtype=jnp.int32).at[dest].set(tok_flat[order])
    buf_gate = jnp.zeros((P,), jnp.float32).at[dest].set(gate_flat[order])
    block_start = jnp.arange(n_blocks, dtype=jnp.int32) * MOE_BLOCK
    block_expert = jnp.minimum(jnp.searchsorted(pad_ends, block_start, side='right'), N_EXPERTS - 1)
    t_pad = jnp.concatenate([t, jnp.zeros((1, D), t.dtype)], axis=0)
    xb = t_pad[buf_tok].reshape(n_blocks, MOE_BLOCK, D)

    def expert_block(args):
        xblk, e = args
        hblk = jax.nn.silu(xblk @ w1[e]) * (xblk @ w3[e])
        return hblk @ w2[e]

    yb = lax.map(expert_block, (xb, block_expert)).reshape(P, D)
    yb = yb * buf_gate[:, None].astype(yb.dtype)
    return jax.ops.segment_sum(yb, buf_tok, num_segments=T + 1)[:T]


def setup_inputs(seed: int = 0) -> dict:
    key = jax.random.key(seed)
    ks = jax.random.split(key, 32)
    L, D = DEPTH, D_MODEL
    nrm = lambda k, shape, s: jax.random.normal(k, shape, jnp.float32) * s
    return {
        'x': nrm(ks[0], (BATCH, SEQ, D), 1.0),
        'c': nrm(ks[1], (BATCH, D), 1.0),
        'ctx': nrm(ks[2], (BATCH, CTX_LEN, D), 1.0),
        'c_ctx': nrm(ks[3], (D,), 1.0),
        'w_mod': nrm(ks[4], (L, D, 6 * D), 0.5 * D ** -0.5),
        'b_mod': nrm(ks[5], (L, 6 * D), 0.01),
        'g_norm1': 1.0 + nrm(ks[6], (L, D), 0.01),
        'w_in': nrm(ks[7], (L, D, D_IN), D ** -0.5),
        'ln_v_g': 1.0 + nrm(ks[8], (L, A_WIDTH), 0.01),
        'ln_v_b': nrm(ks[9], (L, A_WIDTH), 0.01),
        'w_sp': nrm(ks[10], (L, A_GROUPS, CHUNK_A, CHUNK_A), CHUNK_A ** -0.5),
        'b_sp': 1.0 + nrm(ks[11], (L, A_GROUPS, CHUNK_A), 0.01),
        'w_gate_up': nrm(ks[12], (L, 2, GATE_RANK, B_KEY_WIDTH), GATE_RANK ** -0.5),
        'b_gate': nrm(ks[13], (L, 2, B_KEY_WIDTH), 0.01),
        'g_gla': 1.0 + nrm(ks[14], (L, B_WIDTH), 0.01),
        'w_out': nrm(ks[15], (L, D_MIX, D), D_MIX ** -0.5),
        'g_norm2': 1.0 + nrm(ks[16], (L, D), 0.01),
        'w_router_g': nrm(ks[17], (L, D, N_GROUPS), D ** -0.5),
        'b_router_g': nrm(ks[18], (L, N_GROUPS), 0.01),
        'w_router_e': nrm(ks[19], (L, D, N_EXPERTS), D ** -0.5),
        'b_router_e': nrm(ks[20], (L, N_EXPERTS), 0.01),
        'w1': nrm(ks[21], (L, N_EXPERTS, D, D_EXPERT), D ** -0.5),
        'w3': nrm(ks[22], (L, N_EXPERTS, D, D_EXPERT), D ** -0.5),
        'w2': nrm(ks[23], (L, N_EXPERTS, D_EXPERT, D), D_EXPERT ** -0.5),
        'g_final': 1.0 + nrm(ks[24], (D,), 0.01),
    }


def reference(x, c, ctx, c_ctx, w_mod, b_mod, g_norm1, w_in, ln_v_g, ln_v_b, w_sp, b_sp,
              w_gate_up, b_gate, g_gla, w_out, g_norm2, w_router_g, b_router_g,
              w_router_e, b_router_e, w1, w3, w2, g_final):
    Bn, S, D = x.shape
    Lc = ctx.shape[1]
    ROWS = S // GRID_W
    n_chunks_lat = ROWS // ROWS_PER_CHUNK
    n_chunks_ctx = Lc // CHUNK_A
    zero_state = jnp.zeros((Bn, B_HEADS, B_DK, B_DV), jnp.float32)
    h = x
    cx = ctx
    for l in range(DEPTH):
        last = l == DEPTH - 1
        m = jax.nn.silu(c) @ w_mod[l] + b_mod[l]
        sh1, sc1, g1, sh2, sc2, g2 = jnp.split(m[:, None, :], 6, axis=-1)
        mc = jax.nn.silu(c_ctx) @ w_mod[l] + b_mod[l]
        csh1, csc1, cg1, csh2, csc2, cg2 = jnp.split(mc, 6, axis=-1)

        hn = modulate(rmsnorm(h, g_norm1[l]), sh1, sc1)
        cn = modulate(rmsnorm(cx, g_norm1[l]), csh1, csc1)
        z = hn @ w_in[l]

        if last:
            zc_kvg = cn @ w_in[l][:, OFF_BK:]
            kc, vc, laf_c, lab_c = gla_inputs(zc_kvg, w_gate_up[l], b_gate[l])
            _, s_ctx_f, _ = gla_states(kc, vc, laf_c, zero_state)
            _, s_ctx_b, _ = gla_states(flip(kc), flip(vc), flip(lab_c), zero_state)
        else:
            zc = cn @ w_in[l]
            kc, vc, laf_c, lab_c = gla_inputs(zc[..., OFF_BK:], w_gate_up[l], b_gate[l])
            qc = heads(zc[..., OFF_BQ:OFF_BR], B_DK) * B_DK ** -0.5
            oc, s_ctx_f, s_ctx_b = gla_bidir(qc, kc, vc, laf_c, lab_c, zero_state, zero_state)
            bc_out = gla_merge(oc, zc[..., OFF_BR:OFF_BK], g_gla[l])
            ac_out = spatial_gating(zc[..., OFF_AU:OFF_AV], zc[..., OFF_AV:OFF_BQ], n_chunks_ctx,
                                    ln_v_g[l], ln_v_b[l], w_sp[l], b_sp[l])
            mix_c = jnp.concatenate([ac_out, bc_out], axis=-1) @ w_out[l]
            cx = cx + cg1 * mix_c

        a_out = spatial_gating(z[..., OFF_AU:OFF_AV], z[..., OFF_AV:OFF_BQ], n_chunks_lat,
                               ln_v_g[l], ln_v_b[l], w_sp[l], b_sp[l])
        q = heads(z[..., OFF_BQ:OFF_BR], B_DK) * B_DK ** -0.5
        k, v, la_f, la_b = gla_inputs(z[..., OFF_BK:], w_gate_up[l], b_gate[l])
        o, _, _ = gla_bidir(q, k, v, la_f, la_b, s_ctx_f, s_ctx_b)
        b_out = gla_merge(o, z[..., OFF_BR:OFF_BK], g_gla[l])
        mix = jnp.concatenate([a_out, b_out], axis=-1) @ w_out[l]
        h = h + g1 * mix

        hn2 = modulate(rmsnorm(h, g_norm2[l]), sh2, sc2).reshape(Bn * S, D)
        moe_args = (w_router_g[l], b_router_g[l], w_router_e[l], b_router_e[l], w1[l], w3[l], w2[l])
        if last:
            y = hier_moe(hn2, *moe_args)
            h = h + g2 * y.reshape(Bn, S, D)
        else:
            cn2 = modulate(rmsnorm(cx, g_norm2[l]), csh2, csc2).reshape(Bn * Lc, D)
            y = hier_moe(jnp.concatenate([hn2, cn2], axis=0), *moe_args)
            h = h + g2 * y[:Bn * S].reshape(Bn, S, D)
            cx = cx + cg2 * y[Bn * S:].reshape(Bn, Lc, D)
    return rmsnorm(h, g_final)
```

```python
import functools

import jax
import jax.numpy as jnp
from jax import lax
from jax.experimental import pallas as pl
from jax.experimental.pallas import tpu as pltpu

F32 = jnp.float32
BF16 = jnp.bfloat16
I32 = jnp.int32

EPS = 1e-6
LANE = 128
SUBLANE = 8

A_GROUPS = 4
A_GROUP_DIM = 128
A_WIDTH = A_GROUPS * A_GROUP_DIM
CHUNK_A = 128
B_HEADS = 4
B_DK = 64
B_DV = 128
B_KEY_WIDTH = B_HEADS * B_DK
B_WIDTH = B_HEADS * B_DV
GATE_RANK = 16
GATE_TAU = 16.0
CHUNK_B = 64
N_GROUPS = 8
EXPERTS_PER_GROUP = 8
N_EXPERTS = N_GROUPS * EXPERTS_PER_GROUP

TT = 256
CPT = TT // CHUNK_B
MOE_BLK = 256
DT = 512
ROW_SUB = 8

VMEM_LIMIT = 48 * 1024 * 1024


def _cparams(sem):
    return pltpu.CompilerParams(dimension_semantics=sem, vmem_limit_bytes=VMEM_LIMIT)


def _mod_kernel(c_ref, w_ref, b_ref, o_ref):
    c = c_ref[...]
    s = (c * jax.nn.sigmoid(c)).astype(BF16)
    o_ref[0] = jnp.dot(s, w_ref[0].astype(BF16), preferred_element_type=F32) + b_ref[0]


def _modulation(cc, w_mod, b_mod):
    L, D, D6 = w_mod.shape
    R = cc.shape[0]
    tn = 1536
    return pl.pallas_call(
        _mod_kernel,
        grid=(L, D6 // tn),
        in_specs=[
            pl.BlockSpec((R, D), lambda l, j: (0, 0)),
            pl.BlockSpec((1, D, tn), lambda l, j: (l, 0, j)),
            pl.BlockSpec((1, 1, tn), lambda l, j: (l, 0, j)),
        ],
        out_specs=pl.BlockSpec((1, R, tn), lambda l, j: (l, 0, j)),
        out_shape=jax.ShapeDtypeStruct((L, R, D6), F32),
        compiler_params=_cparams(("arbitrary", "arbitrary")),
        name="modulation",
    )(cc, w_mod, b_mod.reshape(L, 1, D6))


def _pre_kernel(h_ref, mod_ref, g1_ref, win_ref, lng_ref, lnb_ref, wsp_ref, bsp_ref, wg_ref, bg_ref,
                trif_ref, trib_ref,
                a_ref, qe_ref, oi_ref, sr_ref, kv_ref, dec_ref):
    D = h_ref.shape[1]
    h = h_ref[...]
    mod = mod_ref[0]
    sh1 = mod[:, 0:D]
    sc1 = mod[:, D:2 * D]
    ms = jnp.mean(h * h, axis=-1, keepdims=True)
    hn = h * lax.rsqrt(ms + EPS) * g1_ref[...]
    hn = hn * (1.0 + sc1) + sh1
    z = jnp.dot(hn.astype(BF16), win_ref[...], preferred_element_type=F32)
    o_av = A_WIDTH
    o_q = 2 * A_WIDTH
    o_r = o_q + B_KEY_WIDTH
    o_k = o_r + B_WIDTH
    o_v = o_k + B_KEY_WIDTH
    o_g = o_v + B_WIDTH
    zu = z[:, 0:o_av]
    zv = z[:, o_av:o_q]
    q = z[:, o_q:o_r] * (B_DK ** -0.5)
    zr = z[:, o_r:o_k]
    k = z[:, o_k:o_v]
    vv = z[:, o_v:o_g]
    zg = z[:, o_g:o_g + LANE]

    u = jax.nn.gelu(zu)
    v = jax.nn.gelu(zv)
    s_groups = []
    for g in range(A_GROUPS):
        sl = slice(g * A_GROUP_DIM, (g + 1) * A_GROUP_DIM)
        vg = v[:, sl]
        mu = jnp.mean(vg, axis=-1, keepdims=True)
        dv = vg - mu
        var = jnp.mean(dv * dv, axis=-1, keepdims=True)
        vn = (dv * lax.rsqrt(var + EPS)) * lng_ref[:, sl] + lnb_ref[:, sl]
        vnb = vn.astype(BF16)
        rows = []
        for c in range(TT // CHUNK_A):
            rows.append(jnp.dot(wsp_ref[g], vnb[c * CHUNK_A:(c + 1) * CHUNK_A, :],
                                preferred_element_type=F32) + bsp_ref[:, sl])
        s_groups.append(jnp.concatenate(rows, axis=0))
    a_ref[...] = (u * jnp.concatenate(s_groups, axis=1)).astype(BF16)

    sr_ref[...] = (zr * jax.nn.sigmoid(zr)).astype(BF16)
    lg = jnp.dot(zg.astype(BF16), wg_ref[...], preferred_element_type=F32) + bg_ref[...]
    la = (jnp.minimum(lg, 0.0) - jnp.log1p(jnp.exp(-jnp.abs(lg)))) * (1.0 / GATE_TAU)
    la_hi = la.astype(BF16)
    la_lo = (la - la_hi.astype(F32)).astype(BF16)
    KW = B_KEY_WIDTH
    bf = (jnp.dot(trif_ref[...], la_hi[:, :KW], preferred_element_type=F32)
          + jnp.dot(trif_ref[...], la_lo[:, :KW], preferred_element_type=F32))
    bb = (jnp.dot(trib_ref[...], la_hi[:, KW:], preferred_element_type=F32)
          + jnp.dot(trib_ref[...], la_lo[:, KW:], preferred_element_type=F32))
    blf = jnp.concatenate(
        [jnp.broadcast_to(bf[c * CHUNK_B + CHUNK_B - 1:(c + 1) * CHUNK_B, :], (CHUNK_B, KW)) for c in range(CPT)], axis=0)
    blb = jnp.concatenate(
        [jnp.broadcast_to(bb[c * CHUNK_B:c * CHUNK_B + 1, :], (CHUNK_B, KW)) for c in range(CPT)], axis=0)
    qe_f = q * jnp.exp(bf)
    qe_b = q * jnp.exp(bb)
    ke_f = (k * jnp.exp(-bf)).astype(BF16)
    ke_b = (k * jnp.exp(-bb)).astype(BF16)
    kd_f = k * jnp.exp(blf - bf)
    kd_b = k * jnp.exp(blb - bb)
    qe_ref[...] = jnp.concatenate([qe_f, qe_b], axis=1).astype(BF16)
    dec_ref[0, 0] = jnp.exp(blf).T
    dec_ref[0, 1] = jnp.exp(blb).T

    row = lax.broadcasted_iota(I32, (TT, TT), 0)
    col = lax.broadcasted_iota(I32, (TT, TT), 1)
    same = lax.shift_right_logical(row, 6) == lax.shift_right_logical(col, 6)
    m_f = same & (row >= col)
    m_b = same & (row <= col)
    lane_head = lax.shift_right_logical(lax.broadcasted_iota(I32, (1, KW), 1), 6)
    tok_chunk = lax.shift_right_logical(lax.broadcasted_iota(I32, (1, TT), 1), 6)
    vb = vv.astype(BF16)
    kdT_f = kd_f.T
    kdT_b = kd_b.T
    nt = (((1,), (1,)), ((), ()))
    oi_heads = []
    for hd in range(B_HEADS):
        hm = lane_head == hd
        qf = jnp.where(hm, qe_f, 0.0).astype(BF16)
        qb = jnp.where(hm, qe_b, 0.0).astype(BF16)
        att_f = lax.dot_general(qf, ke_f, nt, preferred_element_type=F32)
        att_b = lax.dot_general(qb, ke_b, nt, preferred_element_type=F32)
        att = jnp.where(m_f, att_f, 0.0) + jnp.where(m_b, att_b, 0.0)
        v_h = vb[:, hd * B_DV:(hd + 1) * B_DV]
        oi_heads.append(jnp.dot(att.astype(BF16), v_h, preferred_element_type=F32))
        parts = []
        for kdT in (kdT_f, kdT_b):
            kh = kdT[hd * B_DK:(hd + 1) * B_DK, :]
            for c in range(CPT):
                parts.append(jnp.where(tok_chunk == c, kh, 0.0))
        lhs = jnp.concatenate(parts, axis=0).astype(BF16)
        kv_ref[0, hd] = jnp.dot(lhs, v_h, preferred_element_type=F32)
    oi_ref[...] = jnp.concatenate(oi_heads, axis=1)


def _pre_call(h_all, mod3, lw, n_lat_tiles, tiles_per_sample, n_mod_ctx):
    N, D = h_all.shape
    n_tiles = N // TT
    DINP = lw["w_in"].shape[1]
    KVR = 2 * CPT * B_DK

    def mod_map(i):
        return (jnp.where(i < n_lat_tiles, i // tiles_per_sample, n_mod_ctx), 0, 0)

    const2 = lambda i: (0, 0)
    const3 = lambda i: (0, 0, 0)
    tile2 = lambda i: (i, 0)
    return pl.pallas_call(
        _pre_kernel,
        grid=(n_tiles,),
        in_specs=[
            pl.BlockSpec((TT, D), tile2),
            pl.BlockSpec((1, 1, 6 * D), mod_map),
            pl.BlockSpec((1, D), const2),
            pl.BlockSpec((D, DINP), const2),
            pl.BlockSpec((1, A_WIDTH), const2),
            pl.BlockSpec((1, A_WIDTH), const2),
            pl.BlockSpec((A_GROUPS, CHUNK_A, CHUNK_A), const3),
            pl.BlockSpec((CHUNK_A, A_WIDTH), const2),
            pl.BlockSpec((LANE, 2 * B_KEY_WIDTH), const2),
            pl.BlockSpec((1, 2 * B_KEY_WIDTH), const2),
            pl.BlockSpec((TT, TT), const2),
            pl.BlockSpec((TT, TT), const2),
        ],
        out_specs=[
            pl.BlockSpec((TT, A_WIDTH), tile2),
            pl.BlockSpec((TT, 2 * B_KEY_WIDTH), tile2),
            pl.BlockSpec((TT, B_WIDTH), tile2),
            pl.BlockSpec((TT, B_WIDTH), tile2),
            pl.BlockSpec((1, B_HEADS, KVR, B_DV), lambda i: (i, 0, 0, 0)),
            pl.BlockSpec((1, 2, B_KEY_WIDTH, TT), lambda i: (i, 0, 0, 0)),
        ],
        out_shape=[
            jax.ShapeDtypeStruct((N, A_WIDTH), BF16),
            jax.ShapeDtypeStruct((N, 2 * B_KEY_WIDTH), BF16),
            jax.ShapeDtypeStruct((N, B_WIDTH), F32),
            jax.ShapeDtypeStruct((N, B_WIDTH), BF16),
            jax.ShapeDtypeStruct((n_tiles, B_HEADS, KVR, B_DV), F32),
            jax.ShapeDtypeStruct((n_tiles, 2, B_KEY_WIDTH, TT), F32),
        ],
        compiler_params=_cparams(("arbitrary",)),
        name="mix_pre",
    )(h_all, mod3, lw["g_norm1"], lw["w_in"], lw["ln_g"], lw["ln_b"], lw["w_sp"], lw["b_sp"],
      lw["w_gate"], lw["b_gate"], lw["tri_f"], lw["tri_b"])


def _scan_kernel(kvc_ref, kvl_ref, dcc_ref, dcl_ref, sc_ref, sl_ref):
    n_lat = kvl_ref.shape[0]
    fwd = [(kvc_ref, dcc_ref, sc_ref, 0, c) for c in range(CPT)]
    fwd += [(kvl_ref, dcl_ref, sl_ref, t, c) for t in range(n_lat) for c in range(CPT)]
    bwd = [(kvc_ref, dcc_ref, sc_ref, 0, c) for c in reversed(range(CPT))]
    bwd += [(kvl_ref, dcl_ref, sl_ref, t, c) for t in reversed(range(n_lat)) for c in reversed(range(CPT))]
    for hd in range(B_HEADS):
        for d, seq in enumerate((fwd, bwd)):
            s = jnp.zeros((B_DK, B_DV), F32)
            for kv_ref, dc_ref, out_ref, t, c in seq:
                r0 = (d * CPT + c) * B_DK
                out_ref[t, hd, r0:r0 + B_DK, :] = s.astype(BF16)
                dcol = dc_ref[t, d, hd * B_DK:(hd + 1) * B_DK, c * CHUNK_B:c * CHUNK_B + 1]
                s = dcol * s + kv_ref[t, hd, r0:r0 + B_DK, :]


def _scan_call(kv, dec, n_samples, n_lat_tiles, tiles_per_sample):
    n_tiles, _, KVR, _ = kv.shape
    kv_c = pl.BlockSpec((1, B_HEADS, KVR, B_DV), lambda b: (n_lat_tiles + b, 0, 0, 0))
    kv_l = pl.BlockSpec((tiles_per_sample, B_HEADS, KVR, B_DV), lambda b: (b, 0, 0, 0))
    dc_c = pl.BlockSpec((1, 2, B_KEY_WIDTH, TT), lambda b: (n_lat_tiles + b, 0, 0, 0))
    dc_l = pl.BlockSpec((tiles_per_sample, 2, B_KEY_WIDTH, TT), lambda b: (b, 0, 0, 0))
    s_ctx, s_lat = pl.pallas_call(
        _scan_kernel,
        grid=(n_samples,),
        in_specs=[kv_c, kv_l, dc_c, dc_l],
        out_specs=[
            pl.BlockSpec((1, B_HEADS, KVR, B_DV), lambda b: (b, 0, 0, 0)),
            pl.BlockSpec((tiles_per_sample, B_HEADS, KVR, B_DV), lambda b: (b, 0, 0, 0)),
        ],
        out_shape=[
            jax.ShapeDtypeStruct((n_samples, B_HEADS, KVR, B_DV), BF16),
            jax.ShapeDtypeStruct((n_lat_tiles, B_HEADS, KVR, B_DV), BF16),
        ],
        compiler_params=_cparams(("arbitrary",)),
        name="gla_scan",
    )(kv, kv, dec, dec)
    return jnp.concatenate([s_lat, s_ctx], axis=0)


def _post_kernel(h_ref, mod_ref, a_ref, qe_ref, oi_ref, sr_ref, st_ref, ggla_ref, wout_ref, g2_ref,
                 wr_ref, br_ref, lstr_ref,
                 h1_ref, hn2_ref, route_ref, cnt_ref,
                 rhs_scr, cnt_scr):
    D = h_ref.shape[1]
    i = pl.program_id(0)

    @pl.when(i == 0)
    def _():
        rhs_scr[...] = jnp.zeros(rhs_scr.shape, rhs_scr.dtype)
        cnt_scr[...] = jnp.zeros(cnt_scr.shape, cnt_scr.dtype)

    qe = qe_ref[...]
    o_parts = []
    for c in range(CPT):
        for d in range(2):
            for hd in range(B_HEADS):
                r0 = d * B_KEY_WIDTH + hd * B_DK
                s0 = (d * CPT + c) * B_DK
                rhs_scr[c, r0:r0 + B_DK, hd * B_DV:(hd + 1) * B_DV] = st_ref[0, hd, s0:s0 + B_DK, :]
        o_parts.append(jnp.dot(qe[c * CHUNK_B:(c + 1) * CHUNK_B, :], rhs_scr[c], preferred_element_type=F32))
    o = oi_ref[...] + jnp.concatenate(o_parts, axis=0)

    heads = []
    for hd in range(B_HEADS):
        sl = slice(hd * B_DV, (hd + 1) * B_DV)
        oh = o[:, sl]
        msq = jnp.mean(oh * oh, axis=-1, keepdims=True)
        heads.append(oh * lax.rsqrt(msq + EPS) * ggla_ref[:, sl])
    b_out = jnp.concatenate(heads, axis=1) * sr_ref[...].astype(F32)
    mix_in = jnp.concatenate([a_ref[...], b_out.astype(BF16)], axis=1)
    mix = jnp.dot(mix_in, wout_ref[...], preferred_element_type=F32)

    mod = mod_ref[0]
    g1 = mod[:, 2 * D:3 * D]
    sh2 = mod[:, 3 * D:4 * D]
    sc2 = mod[:, 4 * D:5 * D]
    h1 = h_ref[...] + g1 * mix
    h1_ref[...] = h1
    msq = jnp.mean(h1 * h1, axis=-1, keepdims=True)
    hn2 = h1 * lax.rsqrt(msq + EPS) * g2_ref[...]
    hn2 = hn2 * (1.0 + sc2) + sh2
    for s in range(ROW_SUB):
        hn2_ref[:, s, :] = hn2[:, s * LANE:(s + 1) * LANE]

    lg = jnp.dot(hn2.astype(BF16), wr_ref[...], preferred_element_type=F32) + br_ref[...]
    lane = lax.broadcasted_iota(I32, lg.shape, 1)
    lanef = lane.astype(F32)
    neg = jnp.float32(-3.0e38)
    big = jnp.float32(1.0e9)
    is_g = (lane >= N_EXPERTS) & (lane < N_EXPERTS + N_GROUPS)
    gl = jnp.where(is_g, lg, neg)
    gmax = jnp.max(gl, axis=1, keepdims=True)
    gsum = jnp.sum(jnp.where(is_g, jnp.exp(gl - gmax), 0.0), axis=1, keepdims=True)
    p_top = 1.0 / gsum
    gidx = jnp.min(jnp.where(is_g & (gl == gmax), lanef, big), axis=1, keepdims=True) - float(N_EXPERTS)
    in_g = (lane < N_EXPERTS) & (lax.shift_right_logical(lane, 3).astype(F32) == gidx)
    el = jnp.where(in_g, lg, neg)
    m1 = jnp.max(el, axis=1, keepdims=True)
    i1 = jnp.min(jnp.where(in_g & (el == m1), lanef, big), axis=1, keepdims=True)
    el2 = jnp.where(lanef == i1, neg, el)
    m2 = jnp.max(el2, axis=1, keepdims=True)
    i2 = jnp.min(jnp.where(in_g & (el2 == m2), lanef, big), axis=1, keepdims=True)
    t = jnp.exp(m2 - m1)
    gate1 = p_top / (1.0 + t)
    gate2 = p_top * t / (1.0 + t)

    sel1 = lanef == i1
    sel2 = lanef == (i2 + float(N_EXPERTS))
    oh = jnp.where(sel1, 1.0, 0.0) + jnp.where(sel2, 1.0, 0.0)
    prefix = jnp.dot(lstr_ref[...], oh.astype(BF16), preferred_element_type=F32)
    tot = jnp.sum(oh, axis=0, keepdims=True)
    tot_sw = pltpu.roll(tot, N_EXPERTS, axis=1)
    base = cnt_scr[...]
    val = prefix + base + jnp.where(lane[0:1, :] >= N_EXPERTS, tot_sw, 0.0)
    r1 = jnp.sum(jnp.where(sel1, val, 0.0), axis=1, keepdims=True)
    r2 = jnp.sum(jnp.where(sel2, val, 0.0), axis=1, keepdims=True)
    new_cnt = base + tot + tot_sw
    cnt_scr[...] = new_cnt
    cnt_ref[...] = new_cnt
    route = (jnp.where(lane == 0, i1, 0.0) + jnp.where(lane == 1, i2, 0.0)
             + jnp.where(lane == 2, gate1, 0.0) + jnp.where(lane == 3, gate2, 0.0)
             + jnp.where(lane == 4, r1, 0.0) + jnp.where(lane == 5, r2, 0.0))
    route_ref[...] = route


def _post_call(h_all, mod3, pre_outs, states, lw, n_post_tiles, n_lat_tiles, tiles_per_sample, n_mod_ctx):
    a_out, qe, oi, sr = pre_outs
    D = h_all.shape[1]
    KVR = 2 * CPT * B_DK
    NP = n_post_tiles * TT

    def mod_map(i):
        return (jnp.where(i < n_lat_tiles, i // tiles_per_sample, n_mod_ctx), 0, 0)

    const2 = lambda i: (0, 0)
    tile2 = lambda i: (i, 0)
    return pl.pallas_call(
        _post_kernel,
        grid=(n_post_tiles,),
        in_specs=[
            pl.BlockSpec((TT, D), tile2),
            pl.BlockSpec((1, 1, 6 * D), mod_map),
            pl.BlockSpec((TT, A_WIDTH), tile2),
            pl.BlockSpec((TT, 2 * B_KEY_WIDTH), tile2),
            pl.BlockSpec((TT, B_WIDTH), tile2),
            pl.BlockSpec((TT, B_WIDTH), tile2),
            pl.BlockSpec((1, B_HEADS, KVR, B_DV), lambda i: (i, 0, 0, 0)),
            pl.BlockSpec((1, B_WIDTH), const2),
            pl.BlockSpec((D, D), const2),
            pl.BlockSpec((1, D), const2),
            pl.BlockSpec((D, LANE), const2),
            pl.BlockSpec((1, LANE), const2),
            pl.BlockSpec((TT, TT), const2),
        ],
        out_specs=[
            pl.BlockSpec((TT, D), tile2),
            pl.BlockSpec((TT, ROW_SUB, LANE), lambda i: (i, 0, 0)),
            pl.BlockSpec((TT, LANE), tile2),
            pl.BlockSpec((1, LANE), const2),
        ],
        out_shape=[
            jax.ShapeDtypeStruct((NP, D), F32),
            jax.ShapeDtypeStruct((NP, ROW_SUB, LANE), F32),
            jax.ShapeDtypeStruct((NP, LANE), F32),
            jax.ShapeDtypeStruct((1, LANE), F32),
        ],
        scratch_shapes=[
            pltpu.VMEM((CPT, 2 * B_KEY_WIDTH, B_WIDTH), BF16),
            pltpu.VMEM((1, LANE), F32),
        ],
        compiler_params=_cparams(("arbitrary",)),
        name="mix_post",
    )(h_all, mod3, a_out, qe, oi, sr, states, lw["g_gla"], lw["w_out"], lw["g_norm2"],
      lw["w_router"], lw["b_router"], lw["l_strict"])


def _row_loop(n_rows, fn):
    unroll = 8

    def body(jo, carry):
        for u in range(unroll):
            fn(jo * unroll + u, jo, u)
        return carry

    lax.fori_loop(0, n_rows // unroll, body, 0)


def _dispatch_kernel(idx_hbm, x_ref, xs_in, xs_out, idx_smem, sem_i, sem_o):
    del xs_in
    i = pl.program_id(0)
    cp = pltpu.make_async_copy(idx_hbm.at[i], idx_smem, sem_i)
    cp.start()
    cp.wait()
    per_row = LANE // 8

    def issue(j, jo, u):
        r = lax.shift_right_logical(jo, 4)
        c = (jo & (per_row - 1)) * 8 + u
        d0 = idx_smem[r, c]
        d1 = idx_smem[r + DT // LANE, c]
        pltpu.make_async_copy(x_ref.at[j], xs_out.at[d0], sem_o).start()
        pltpu.make_async_copy(x_ref.at[j], xs_out.at[d1], sem_o).start()

    _row_loop(DT, issue)
    for _ in range(2):
        pltpu.make_async_copy(x_ref, xs_out.at[pl.ds(0, DT)], sem_o).wait()


def _dispatch_call(idx, x_rows, n_buf_rows):
    n_tiles = x_rows.shape[0] // DT
    xs0 = jnp.zeros((n_buf_rows, ROW_SUB, LANE), F32)
    return pl.pallas_call(
        _dispatch_kernel,
        grid=(n_tiles,),
        in_specs=[
            pl.BlockSpec(memory_space=pl.ANY),
            pl.BlockSpec((DT, ROW_SUB, LANE), lambda i: (i, 0, 0)),
            pl.BlockSpec(memory_space=pl.ANY),
        ],
        out_specs=pl.BlockSpec(memory_space=pl.ANY),
        out_shape=jax.ShapeDtypeStruct((n_buf_rows, ROW_SUB, LANE), F32),
        input_output_aliases={2: 0},
        scratch_shapes=[
            pltpu.SMEM((2 * DT // LANE, LANE), I32),
            pltpu.SemaphoreType.DMA,
            pltpu.SemaphoreType.DMA,
        ],
        compiler_params=_cparams(("arbitrary",)),
        name="moe_dispatch",
    )(idx, x_rows, xs0)


def _ffn_kernel(be_ref, nu_ref, x_ref, w1_ref, w3_ref, w2_ref, y_ref):
    i = pl.program_id(0)

    @pl.when(i < nu_ref[0])
    def _():
        x = jnp.concatenate([x_ref[:, s, :] for s in range(ROW_SUB)], axis=1).astype(BF16)
        h1 = jnp.dot(x, w1_ref[0].astype(BF16), preferred_element_type=F32)
        h3 = jnp.dot(x, w3_ref[0].astype(BF16), preferred_element_type=F32)
        hh = (h1 * jax.nn.sigmoid(h1) * h3).astype(BF16)
        y = jnp.dot(hh, w2_ref[0].astype(BF16), preferred_element_type=F32)
        for s in range(ROW_SUB):
            y_ref[:, s, :] = y[:, s * LANE:(s + 1) * LANE]

    @pl.when(i >= nu_ref[0])
    def _():
        y_ref[...] = jnp.zeros(y_ref.shape, y_ref.dtype)


def _ffn_call(block_expert, n_used, xs, w1, w3, w2):
    P = xs.shape[0]
    NB = P // MOE_BLK
    _, D, DE = w1.shape

    def blk(i, be, nu):
        return jnp.minimum(i, nu[0] - 1)

    grid_spec = pltpu.PrefetchScalarGridSpec(
        num_scalar_prefetch=2,
        grid=(NB,),
        in_specs=[
            pl.BlockSpec((MOE_BLK, ROW_SUB, LANE), lambda i, be, nu: (blk(i, be, nu), 0, 0)),
            pl.BlockSpec((1, D, DE), lambda i, be, nu: (be[blk(i, be, nu)], 0, 0)),
            pl.BlockSpec((1, D, DE), lambda i, be, nu: (be[blk(i, be, nu)], 0, 0)),
            pl.BlockSpec((1, DE, D), lambda i, be, nu: (be[blk(i, be, nu)], 0, 0)),
        ],
        out_specs=pl.BlockSpec((MOE_BLK, ROW_SUB, LANE), lambda i, be, nu: (i, 0, 0)),
    )
    return pl.pallas_call(
        _ffn_kernel,
        grid_spec=grid_spec,
        out_shape=jax.ShapeDtypeStruct((P, ROW_SUB, LANE), F32),
        compiler_params=_cparams(("arbitrary",)),
        name="moe_ffn",
    )(block_expert, n_used, xs, w1, w3, w2)


def _combine_kernel(idx_hbm, h1_ref, route_ref, mod_ref, gf_ref, y_hbm, o_ref, idx_smem, rows, sem_i, sem_g, *, final):
    D = h1_ref.shape[1]
    i = pl.program_id(0)
    cp = pltpu.make_async_copy(idx_hbm.at[i], idx_smem, sem_i)
    cp.start()
    cp.wait()
    per_row = LANE // 8

    def issue(j, jo, u):
        r = lax.shift_right_logical(jo, 4)
        c = (jo & (per_row - 1)) * 8 + u
        d0 = idx_smem[r, c]
        d1 = idx_smem[r + DT // LANE, c]
        pltpu.make_async_copy(y_hbm.at[d0], rows.at[0, j], sem_g).start()
        pltpu.make_async_copy(y_hbm.at[d1], rows.at[1, j], sem_g).start()

    _row_loop(DT, issue)
    for kk in range(2):
        pltpu.make_async_copy(y_hbm.at[pl.ds(0, DT)], rows.at[kk], sem_g).wait()

    route = route_ref[...]
    gate1 = route[:, 2:3]
    gate2 = route[:, 3:4]
    y1 = jnp.concatenate([rows[0, :, s, :] for s in range(ROW_SUB)], axis=1)
    y2 = jnp.concatenate([rows[1, :, s, :] for s in range(ROW_SUB)], axis=1)
    y = gate1 * y1 + gate2 * y2
    g2 = mod_ref[0][:, 5 * D:6 * D]
    out = h1_ref[...] + g2 * y
    if final:
        msq = jnp.mean(out * out, axis=-1, keepdims=True)
        out = out * lax.rsqrt(msq + EPS) * gf_ref[...]
    o_ref[...] = out


def _combine_call(idx, h1, route, mod3, g_final, yb, n_lat_dt, dt_per_sample, n_mod_ctx, final):
    N, D = h1.shape
    n_tiles = N // DT

    def mod_map(i):
        return (jnp.where(i < n_lat_dt, i // dt_per_sample, n_mod_ctx), 0, 0)

    return pl.pallas_call(
        functools.partial(_combine_kernel, final=final),
        grid=(n_tiles,),
        in_specs=[
            pl.BlockSpec(memory_space=pl.ANY),
            pl.BlockSpec((DT, D), lambda i: (i, 0)),
            pl.BlockSpec((DT, LANE), lambda i: (i, 0)),
            pl.BlockSpec((1, 1, 6 * D), mod_map),
            pl.BlockSpec((1, D), lambda i: (0, 0)),
            pl.BlockSpec(memory_space=pl.ANY),
        ],
        out_specs=pl.BlockSpec((DT, D), lambda i: (i, 0)),
        out_shape=jax.ShapeDtypeStruct((N, D), F32),
        scratch_shapes=[
            pltpu.SMEM((2 * DT // LANE, LANE), I32),
            pltpu.VMEM((2, DT, ROW_SUB, LANE), F32),
            pltpu.SemaphoreType.DMA,
            pltpu.SemaphoreType.DMA,
        ],
        compiler_params=_cparams(("arbitrary",)),
        name="moe_combine",
    )(idx, h1, route, mod3, g_final, yb)


def _layer_weights(l, w_in, g_norm1, ln_v_g, ln_v_b, w_sp, b_sp, w_gate_up, b_gate, g_gla, w_out, g_norm2,
                   w_router_g, b_router_g, w_router_e, b_router_e):
    D = w_in.shape[1]
    d_in = w_in.shape[2]
    pad = (-d_in) % LANE
    w_in_p = jnp.pad(w_in[l], ((0, 0), (0, pad))).astype(BF16)
    KW = B_KEY_WIDTH
    wg = jnp.zeros((LANE, 2 * KW), F32)
    wg = wg.at[0:GATE_RANK, 0:KW].set(w_gate_up[l, 0])
    wg = wg.at[GATE_RANK:2 * GATE_RANK, KW:2 * KW].set(w_gate_up[l, 1])
    bg = jnp.concatenate([b_gate[l, 0], b_gate[l, 1]])[None, :]
    bsp = jnp.repeat(b_sp[l].T, A_GROUP_DIM, axis=1)
    wr = jnp.zeros((D, LANE), F32)
    wr = wr.at[:, 0:N_EXPERTS].set(w_router_e[l])
    wr = wr.at[:, N_EXPERTS:N_EXPERTS + N_GROUPS].set(w_router_g[l])
    br = jnp.zeros((1, LANE), F32)
    br = br.at[0, 0:N_EXPERTS].set(b_router_e[l])
    br = br.at[0, N_EXPERTS:N_EXPERTS + N_GROUPS].set(b_router_g[l])
    r = jnp.arange(TT, dtype=I32)
    same = (r[:, None] // CHUNK_B) == (r[None, :] // CHUNK_B)
    tri_f = (same & (r[:, None] >= r[None, :])).astype(BF16)
    tri_b = (same & (r[:, None] <= r[None, :])).astype(BF16)
    l_strict = (r[:, None] > r[None, :]).astype(BF16)
    return dict(
        w_in=w_in_p, g_norm1=g_norm1[l][None, :], ln_g=ln_v_g[l][None, :], ln_b=ln_v_b[l][None, :],
        w_sp=w_sp[l].astype(BF16), b_sp=bsp, w_gate=wg.astype(BF16), b_gate=bg,
        g_gla=g_gla[l][None, :], w_out=w_out[l].astype(BF16), g_norm2=g_norm2[l][None, :],
        w_router=wr.astype(BF16), b_router=br, tri_f=tri_f, tri_b=tri_b, l_strict=l_strict)


def _moe_plan(route, counts_row, n_tokens):
    e1 = route[:, 0].astype(I32)
    e2 = route[:, 1].astype(I32)
    r1 = route[:, 4].astype(I32)
    r2 = route[:, 5].astype(I32)
    counts = counts_row[0, :N_EXPERTS].astype(I32)
    padded = (counts + MOE_BLK - 1) // MOE_BLK * MOE_BLK
    pad_ends = jnp.cumsum(padded)
    pad_starts = pad_ends - padded
    d1 = pad_starts[e1] + r1
    d2 = pad_starts[e2] + r2
    n_blocks = (n_tokens * 2 + N_EXPERTS * (MOE_BLK - 1)) // MOE_BLK + 1
    block_start = jnp.arange(n_blocks, dtype=I32) * MOE_BLK
    block_expert = jnp.minimum(jnp.searchsorted(pad_ends, block_start, side="right"), N_EXPERTS - 1).astype(I32)
    n_used = (pad_ends[-1] // MOE_BLK).astype(I32)[None]
    n_dt = n_tokens // DT
    idx = jnp.concatenate([d1.reshape(n_dt, DT), d2.reshape(n_dt, DT)], axis=1).reshape(n_dt, 2 * DT // LANE, LANE)
    return idx, block_expert, n_used, n_blocks * MOE_BLK


def kernel(x, c, ctx, c_ctx, w_mod, b_mod, g_norm1, w_in, ln_v_g, ln_v_b, w_sp, b_sp, w_gate_up, b_gate, g_gla,
           w_out, g_norm2, w_router_g, b_router_g, w_router_e, b_router_e, w1, w3, w2, g_final):
    Bn, S, D = x.shape
    Lc = ctx.shape[1]
    depth = w_mod.shape[0]
    assert S % DT == 0 and Lc % TT == 0 and (Bn * Lc) % DT == 0 and Lc == TT
    n_lat = Bn * S
    n_ctx = Bn * Lc
    n_lat_tiles = n_lat // TT
    tiles_per_sample = S // TT

    cc = jnp.concatenate([c, c_ctx[None, :]], axis=0)
    mod_all = _modulation(cc, w_mod, b_mod)
    gf = g_final[None, :]

    h_all = jnp.concatenate([x.reshape(n_lat, D), ctx.reshape(n_ctx, D)], axis=0)
    for l in range(depth):
        last = l == depth - 1
        lw = _layer_weights(l, w_in, g_norm1, ln_v_g, ln_v_b, w_sp, b_sp, w_gate_up, b_gate, g_gla, w_out,
                            g_norm2, w_router_g, b_router_g, w_router_e, b_router_e)
        mod3 = mod_all[l].reshape(Bn + 1, 1, 6 * D)
        a_out, qe, oi, sr, kv, dec = _pre_call(h_all, mod3, lw, n_lat_tiles, tiles_per_sample, Bn)
        states = _scan_call(kv, dec, Bn, n_lat_tiles, tiles_per_sample)
        n_tok = n_lat if last else n_lat + n_ctx
        h1, hn2_rows, route, counts = _post_call(h_all, mod3, (a_out, qe, oi, sr), states, lw, n_tok // TT,
                                                 n_lat_tiles, tiles_per_sample, Bn)
        idx, block_expert, n_used, n_buf_rows = _moe_plan(route, counts, n_tok)
        xs = _dispatch_call(idx, hn2_rows, n_buf_rows)
        yb = _ffn_call(block_expert, n_used, xs, w1[l], w3[l], w2[l])
        h_all = _combine_call(idx, h1, route, mod3, gf, yb, n_lat // DT, S // DT, Bn, last)
    return h_all.reshape(Bn, S, D)
```

```python
import functools

import jax
import jax.numpy as jnp
from jax import lax
from jax.experimental import pallas as pl
from jax.experimental.pallas import tpu as pltpu

F32 = jnp.float32
BF16 = jnp.bfloat16
I32 = jnp.int32

EPS = 1e-6
LANE = 128
SUBLANE = 8

A_GROUPS = 4
A_GROUP_DIM = 128
A_WIDTH = A_GROUPS * A_GROUP_DIM
CHUNK_A = 128
B_HEADS = 4
B_DK = 64
B_DV = 128
B_KEY_WIDTH = B_HEADS * B_DK
B_WIDTH = B_HEADS * B_DV
GATE_RANK = 16
GATE_TAU = 16.0
CHUNK_B = 64
N_GROUPS = 8
EXPERTS_PER_GROUP = 8
N_EXPERTS = N_GROUPS * EXPERTS_PER_GROUP

TT = 256
CPT = TT // CHUNK_B
MOE_BLK = 256
DT = 512
ROW_SUB = 8

VMEM_LIMIT = 48 * 1024 * 1024


def _cparams(sem):
    return pltpu.CompilerParams(dimension_semantics=sem, vmem_limit_bytes=VMEM_LIMIT)


def _load_rows(ref, n):
    return jnp.concatenate([ref[pl.ds(s, n, stride=ROW_SUB), :] for s in range(ROW_SUB)], axis=1)


def _store_rows(ref, val):
    n = val.shape[0]
    for s in range(ROW_SUB):
        ref[pl.ds(s, n, stride=ROW_SUB), :] = val[:, s * LANE:(s + 1) * LANE]


def _mod_kernel(c_ref, w_ref, b_ref, o_ref):
    c = c_ref[...]
    s = (c * jax.nn.sigmoid(c)).astype(BF16)
    o_ref[0] = jnp.dot(s, w_ref[0].astype(BF16), preferred_element_type=F32) + b_ref[0]


def _modulation(cc, w_mod, b_mod):
    L, D, D6 = w_mod.shape
    R = cc.shape[0]
    tn = 1536
    return pl.pallas_call(
        _mod_kernel,
        grid=(L, D6 // tn),
        in_specs=[
            pl.BlockSpec((R, D), lambda l, j: (0, 0)),
            pl.BlockSpec((1, D, tn), lambda l, j: (l, 0, j)),
            pl.BlockSpec((1, 1, tn), lambda l, j: (l, 0, j)),
        ],
        out_specs=pl.BlockSpec((1, R, tn), lambda l, j: (l, 0, j)),
        out_shape=jax.ShapeDtypeStruct((L, R, D6), F32),
        compiler_params=_cparams(("arbitrary", "arbitrary")),
        name="modulation",
    )(cc, w_mod, b_mod.reshape(L, 1, D6))


def _select_tile(lat_ref, ctx_ref, n_lat_tiles):
    return jnp.where(pl.program_id(0) < n_lat_tiles, lat_ref[...], ctx_ref[...])


def _stream_specs(D, n_lat_tiles, ctx_off):
    lat = pl.BlockSpec((TT, D), lambda i: (jnp.minimum(i, n_lat_tiles - 1), 0))
    ctx = pl.BlockSpec((TT, D), lambda i: (ctx_off + jnp.maximum(i - n_lat_tiles, 0), 0))
    return lat, ctx


def _pre_kernel(lat_ref, ctx_ref, mod_ref, g1_ref, win_ref, lng_ref, lnb_ref, wsp_ref, bsp_ref, wg_ref, bg_ref,
                trif_ref, trib_ref,
                a_ref, qe_ref, oi_ref, sr_ref, kv_ref, dec_ref, *, n_lat_tiles):
    D = lat_ref.shape[1]
    h = _select_tile(lat_ref, ctx_ref, n_lat_tiles)
    mod = mod_ref[0]
    sh1 = mod[:, 0:D]
    sc1 = mod[:, D:2 * D]
    ms = jnp.mean(h * h, axis=-1, keepdims=True)
    hn = h * lax.rsqrt(ms + EPS) * g1_ref[...]
    hn = hn * (1.0 + sc1) + sh1
    z = jnp.dot(hn.astype(BF16), win_ref[...], preferred_element_type=F32)
    o_av = A_WIDTH
    o_q = 2 * A_WIDTH
    o_r = o_q + B_KEY_WIDTH
    o_k = o_r + B_WIDTH
    o_v = o_k + B_KEY_WIDTH
    o_g = o_v + B_WIDTH
    zu = z[:, 0:o_av]
    zv = z[:, o_av:o_q]
    q = z[:, o_q:o_r] * (B_DK ** -0.5)
    zr = z[:, o_r:o_k]
    k = z[:, o_k:o_v]
    vv = z[:, o_v:o_g]
    zg = z[:, o_g:o_g + LANE]

    u = jax.nn.gelu(zu)
    v = jax.nn.gelu(zv)
    s_groups = []
    for g in range(A_GROUPS):
        sl = slice(g * A_GROUP_DIM, (g + 1) * A_GROUP_DIM)
        vg = v[:, sl]
        mu = jnp.mean(vg, axis=-1, keepdims=True)
        dv = vg - mu
        var = jnp.mean(dv * dv, axis=-1, keepdims=True)
        vn = (dv * lax.rsqrt(var + EPS)) * lng_ref[:, sl] + lnb_ref[:, sl]
        vnb = vn.astype(BF16)
        rows = []
        for c in range(TT // CHUNK_A):
            rows.append(jnp.dot(wsp_ref[g], vnb[c * CHUNK_A:(c + 1) * CHUNK_A, :],
                                preferred_element_type=F32) + bsp_ref[:, sl])
        s_groups.append(jnp.concatenate(rows, axis=0))
    a_ref[...] = (u * jnp.concatenate(s_groups, axis=1)).astype(BF16)

    sr_ref[...] = (zr * jax.nn.sigmoid(zr)).astype(BF16)
    lg = jnp.dot(zg.astype(BF16), wg_ref[...], preferred_element_type=F32) + bg_ref[...]
    la = (jnp.minimum(lg, 0.0) - jnp.log1p(jnp.exp(-jnp.abs(lg)))) * (1.0 / GATE_TAU)
    la_hi = la.astype(BF16)
    la_lo = (la - la_hi.astype(F32)).astype(BF16)
    KW = B_KEY_WIDTH
    bf = (jnp.dot(trif_ref[...], la_hi[:, :KW], preferred_element_type=F32)
          + jnp.dot(trif_ref[...], la_lo[:, :KW], preferred_element_type=F32))
    bb = (jnp.dot(trib_ref[...], la_hi[:, KW:], preferred_element_type=F32)
          + jnp.dot(trib_ref[...], la_lo[:, KW:], preferred_element_type=F32))
    blf = jnp.concatenate(
        [jnp.broadcast_to(bf[c * CHUNK_B + CHUNK_B - 1:(c + 1) * CHUNK_B, :], (CHUNK_B, KW)) for c in range(CPT)], axis=0)
    blb = jnp.concatenate(
        [jnp.broadcast_to(bb[c * CHUNK_B:c * CHUNK_B + 1, :], (CHUNK_B, KW)) for c in range(CPT)], axis=0)
    qe_f = q * jnp.exp(bf)
    qe_b = q * jnp.exp(bb)
    ke_f = (k * jnp.exp(-bf)).astype(BF16)
    ke_b = (k * jnp.exp(-bb)).astype(BF16)
    kd_f = k * jnp.exp(blf - bf)
    kd_b = k * jnp.exp(blb - bb)
    qe_ref[...] = jnp.concatenate([qe_f, qe_b], axis=1).astype(BF16)
    dec_ref[0, 0] = jnp.exp(blf).T
    dec_ref[0, 1] = jnp.exp(blb).T

    row = lax.broadcasted_iota(I32, (TT, TT), 0)
    col = lax.broadcasted_iota(I32, (TT, TT), 1)
    same = lax.shift_right_logical(row, 6) == lax.shift_right_logical(col, 6)
    m_f = same & (row >= col)
    m_b = same & (row <= col)
    lane_head = lax.shift_right_logical(lax.broadcasted_iota(I32, (1, KW), 1), 6)
    tok_chunk = lax.shift_right_logical(lax.broadcasted_iota(I32, (1, TT), 1), 6)
    vb = vv.astype(BF16)
    kdT_f = kd_f.T
    kdT_b = kd_b.T
    nt = (((1,), (1,)), ((), ()))
    oi_heads = []
    for hd in range(B_HEADS):
        hm = lane_head == hd
        qf = jnp.where(hm, qe_f, 0.0).astype(BF16)
        qb = jnp.where(hm, qe_b, 0.0).astype(BF16)
        att_f = lax.dot_general(qf, ke_f, nt, preferred_element_type=F32)
        att_b = lax.dot_general(qb, ke_b, nt, preferred_element_type=F32)
        att = jnp.where(m_f, att_f, 0.0) + jnp.where(m_b, att_b, 0.0)
        v_h = vb[:, hd * B_DV:(hd + 1) * B_DV]
        oi_heads.append(jnp.dot(att.astype(BF16), v_h, preferred_element_type=F32))
        parts = []
        for kdT in (kdT_f, kdT_b):
            kh = kdT[hd * B_DK:(hd + 1) * B_DK, :]
            for c in range(CPT):
                parts.append(jnp.where(tok_chunk == c, kh, 0.0))
        lhs = jnp.concatenate(parts, axis=0).astype(BF16)
        kv_ref[0, hd] = jnp.dot(lhs, v_h, preferred_element_type=F32)
    oi_ref[...] = jnp.concatenate(oi_heads, axis=1)


def _pre_call(lat, ctx, ctx_off, n_ctx_tiles, mod3, lw, n_lat_tiles, tiles_per_sample, n_mod_ctx):
    D = lat.shape[1]
    n_tiles = n_lat_tiles + n_ctx_tiles
    N = n_tiles * TT
    lat_spec, ctx_spec = _stream_specs(D, n_lat_tiles, ctx_off)
    DINP = lw["w_in"].shape[1]
    KVR = 2 * CPT * B_DK

    def mod_map(i):
        return (jnp.where(i < n_lat_tiles, i // tiles_per_sample, n_mod_ctx), 0, 0)

    const2 = lambda i: (0, 0)
    const3 = lambda i: (0, 0, 0)
    tile2 = lambda i: (i, 0)
    return pl.pallas_call(
        functools.partial(_pre_kernel, n_lat_tiles=n_lat_tiles),
        grid=(n_tiles,),
        in_specs=[
            lat_spec,
            ctx_spec,
            pl.BlockSpec((1, 1, 6 * D), mod_map),
            pl.BlockSpec((1, D), const2),
            pl.BlockSpec((D, DINP), const2),
            pl.BlockSpec((1, A_WIDTH), const2),
            pl.BlockSpec((1, A_WIDTH), const2),
            pl.BlockSpec((A_GROUPS, CHUNK_A, CHUNK_A), const3),
            pl.BlockSpec((CHUNK_A, A_WIDTH), const2),
            pl.BlockSpec((LANE, 2 * B_KEY_WIDTH), const2),
            pl.BlockSpec((1, 2 * B_KEY_WIDTH), const2),
            pl.BlockSpec((TT, TT), const2),
            pl.BlockSpec((TT, TT), const2),
        ],
        out_specs=[
            pl.BlockSpec((TT, A_WIDTH), tile2),
            pl.BlockSpec((TT, 2 * B_KEY_WIDTH), tile2),
            pl.BlockSpec((TT, B_WIDTH), tile2),
            pl.BlockSpec((TT, B_WIDTH), tile2),
            pl.BlockSpec((1, B_HEADS, KVR, B_DV), lambda i: (i, 0, 0, 0)),
            pl.BlockSpec((1, 2, B_KEY_WIDTH, TT), lambda i: (i, 0, 0, 0)),
        ],
        out_shape=[
            jax.ShapeDtypeStruct((N, A_WIDTH), BF16),
            jax.ShapeDtypeStruct((N, 2 * B_KEY_WIDTH), BF16),
            jax.ShapeDtypeStruct((N, B_WIDTH), F32),
            jax.ShapeDtypeStruct((N, B_WIDTH), BF16),
            jax.ShapeDtypeStruct((n_tiles, B_HEADS, KVR, B_DV), F32),
            jax.ShapeDtypeStruct((n_tiles, 2, B_KEY_WIDTH, TT), F32),
        ],
        compiler_params=_cparams(("arbitrary",)),
        name="mix_pre",
    )(lat, ctx, mod3, lw["g_norm1"], lw["w_in"], lw["ln_g"], lw["ln_b"], lw["w_sp"], lw["b_sp"],
      lw["w_gate"], lw["b_gate"], lw["tri_f"], lw["tri_b"])


def _scan_kernel(kvc_ref, kvl_ref, dcc_ref, dcl_ref, sc_ref, sl_ref):
    n_lat = kvl_ref.shape[0]
    fwd = [(kvc_ref, dcc_ref, sc_ref, 0, c) for c in range(CPT)]
    fwd += [(kvl_ref, dcl_ref, sl_ref, t, c) for t in range(n_lat) for c in range(CPT)]
    bwd = [(kvc_ref, dcc_ref, sc_ref, 0, c) for c in reversed(range(CPT))]
    bwd += [(kvl_ref, dcl_ref, sl_ref, t, c) for t in reversed(range(n_lat)) for c in reversed(range(CPT))]
    for hd in range(B_HEADS):
        for d, seq in enumerate((fwd, bwd)):
            s = jnp.zeros((B_DK, B_DV), F32)
            for kv_ref, dc_ref, out_ref, t, c in seq:
                r0 = (d * CPT + c) * B_DK
                out_ref[t, hd, r0:r0 + B_DK, :] = s.astype(BF16)
                dcol = dc_ref[t, d, hd * B_DK:(hd + 1) * B_DK, c * CHUNK_B:c * CHUNK_B + 1]
                s = dcol * s + kv_ref[t, hd, r0:r0 + B_DK, :]


def _scan_call(kv, dec, n_samples, n_lat_tiles, tiles_per_sample):
    n_tiles, _, KVR, _ = kv.shape
    kv_c = pl.BlockSpec((1, B_HEADS, KVR, B_DV), lambda b: (n_lat_tiles + b, 0, 0, 0))
    kv_l = pl.BlockSpec((tiles_per_sample, B_HEADS, KVR, B_DV), lambda b: (b, 0, 0, 0))
    dc_c = pl.BlockSpec((1, 2, B_KEY_WIDTH, TT), lambda b: (n_lat_tiles + b, 0, 0, 0))
    dc_l = pl.BlockSpec((tiles_per_sample, 2, B_KEY_WIDTH, TT), lambda b: (b, 0, 0, 0))
    s_ctx, s_lat = pl.pallas_call(
        _scan_kernel,
        grid=(n_samples,),
        in_specs=[kv_c, kv_l, dc_c, dc_l],
        out_specs=[
            pl.BlockSpec((1, B_HEADS, KVR, B_DV), lambda b: (b, 0, 0, 0)),
            pl.BlockSpec((tiles_per_sample, B_HEADS, KVR, B_DV), lambda b: (b, 0, 0, 0)),
        ],
        out_shape=[
            jax.ShapeDtypeStruct((n_samples, B_HEADS, KVR, B_DV), BF16),
            jax.ShapeDtypeStruct((n_lat_tiles, B_HEADS, KVR, B_DV), BF16),
        ],
        compiler_params=_cparams(("arbitrary",)),
        name="gla_scan",
    )(kv, kv, dec, dec)
    return jnp.concatenate([s_lat, s_ctx], axis=0)


def _post_kernel(lat_ref, ctx_ref, mod_ref, a_ref, qe_ref, oi_ref, sr_ref, st_ref, ggla_ref, wout_ref, g2_ref,
                 wr_ref, br_ref, lstr_ref,
                 h1_ref, hn2_ref, route_ref, routet_ref, cnt_ref,
                 rhs_scr, cnt_scr, *, n_lat_tiles):
    D = lat_ref.shape[1]
    i = pl.program_id(0)

    @pl.when(i == 0)
    def _():
        rhs_scr[...] = jnp.zeros(rhs_scr.shape, rhs_scr.dtype)
        cnt_scr[...] = jnp.zeros(cnt_scr.shape, cnt_scr.dtype)

    qe = qe_ref[...]
    o_parts = []
    for c in range(CPT):
        for d in range(2):
            for hd in range(B_HEADS):
                r0 = d * B_KEY_WIDTH + hd * B_DK
                s0 = (d * CPT + c) * B_DK
                rhs_scr[c, r0:r0 + B_DK, hd * B_DV:(hd + 1) * B_DV] = st_ref[0, hd, s0:s0 + B_DK, :]
        o_parts.append(jnp.dot(qe[c * CHUNK_B:(c + 1) * CHUNK_B, :], rhs_scr[c], preferred_element_type=F32))
    o = oi_ref[...] + jnp.concatenate(o_parts, axis=0)

    heads = []
    for hd in range(B_HEADS):
        sl = slice(hd * B_DV, (hd + 1) * B_DV)
        oh = o[:, sl]
        msq = jnp.mean(oh * oh, axis=-1, keepdims=True)
        heads.append(oh * lax.rsqrt(msq + EPS) * ggla_ref[:, sl])
    b_out = jnp.concatenate(heads, axis=1) * sr_ref[...].astype(F32)
    mix_in = jnp.concatenate([a_ref[...], b_out.astype(BF16)], axis=1)
    mix = jnp.dot(mix_in, wout_ref[...], preferred_element_type=F32)

    mod = mod_ref[0]
    g1 = mod[:, 2 * D:3 * D]
    sh2 = mod[:, 3 * D:4 * D]
    sc2 = mod[:, 4 * D:5 * D]
    h1 = _select_tile(lat_ref, ctx_ref, n_lat_tiles) + g1 * mix
    h1_ref[...] = h1
    msq = jnp.mean(h1 * h1, axis=-1, keepdims=True)
    hn2 = h1 * lax.rsqrt(msq + EPS) * g2_ref[...]
    hn2 = hn2 * (1.0 + sc2) + sh2
    _store_rows(hn2_ref, hn2)

    lg = jnp.dot(hn2.astype(BF16), wr_ref[...], preferred_element_type=F32) + br_ref[...]
    lane = lax.broadcasted_iota(I32, lg.shape, 1)
    lanef = lane.astype(F32)
    neg = jnp.float32(-3.0e38)
    big = jnp.float32(1.0e9)
    is_g = (lane >= N_EXPERTS) & (lane < N_EXPERTS + N_GROUPS)
    gl = jnp.where(is_g, lg, neg)
    gmax = jnp.max(gl, axis=1, keepdims=True)
    gsum = jnp.sum(jnp.where(is_g, jnp.exp(gl - gmax), 0.0), axis=1, keepdims=True)
    p_top = 1.0 / gsum
    gidx = jnp.min(jnp.where(is_g & (gl == gmax), lanef, big), axis=1, keepdims=True) - float(N_EXPERTS)
    in_g = (lane < N_EXPERTS) & (lax.shift_right_logical(lane, 3).astype(F32) == gidx)
    el = jnp.where(in_g, lg, neg)
    m1 = jnp.max(el, axis=1, keepdims=True)
    i1 = jnp.min(jnp.where(in_g & (el == m1), lanef, big), axis=1, keepdims=True)
    el2 = jnp.where(lanef == i1, neg, el)
    m2 = jnp.max(el2, axis=1, keepdims=True)
    i2 = jnp.min(jnp.where(in_g & (el2 == m2), lanef, big), axis=1, keepdims=True)
    t = jnp.exp(m2 - m1)
    gate1 = p_top / (1.0 + t)
    gate2 = p_top * t / (1.0 + t)

    sel1 = lanef == i1
    sel2 = lanef == (i2 + float(N_EXPERTS))
    oh = jnp.where(sel1, 1.0, 0.0) + jnp.where(sel2, 1.0, 0.0)
    prefix = jnp.dot(lstr_ref[...], oh.astype(BF16), preferred_element_type=F32)
    tot = jnp.sum(oh, axis=0, keepdims=True)
    tot_sw = pltpu.roll(tot, N_EXPERTS, axis=1)
    base = cnt_scr[...]
    val = prefix + base + jnp.where(lane[0:1, :] >= N_EXPERTS, tot_sw, 0.0)
    r1 = jnp.sum(jnp.where(sel1, val, 0.0), axis=1, keepdims=True)
    r2 = jnp.sum(jnp.where(sel2, val, 0.0), axis=1, keepdims=True)
    new_cnt = base + tot + tot_sw
    cnt_scr[...] = new_cnt
    cnt_ref[...] = new_cnt
    route = (jnp.where(lane == 0, i1, 0.0) + jnp.where(lane == 1, i2, 0.0)
             + jnp.where(lane == 2, gate1, 0.0) + jnp.where(lane == 3, gate2, 0.0)
             + jnp.where(lane == 4, r1, 0.0) + jnp.where(lane == 5, r2, 0.0))
    route_ref[...] = route
    routet_ref[...] = route.T[0:SUBLANE, :]


def _post_call(lat, ctx, ctx_off, mod3, pre_outs, states, lw, n_post_tiles, n_lat_tiles, tiles_per_sample, n_mod_ctx):
    a_out, qe, oi, sr = pre_outs
    D = lat.shape[1]
    KVR = 2 * CPT * B_DK
    NP = n_post_tiles * TT
    lat_spec, ctx_spec = _stream_specs(D, n_lat_tiles, ctx_off)

    def mod_map(i):
        return (jnp.where(i < n_lat_tiles, i // tiles_per_sample, n_mod_ctx), 0, 0)

    const2 = lambda i: (0, 0)
    tile2 = lambda i: (i, 0)
    return pl.pallas_call(
        functools.partial(_post_kernel, n_lat_tiles=n_lat_tiles),
        grid=(n_post_tiles,),
        in_specs=[
            lat_spec,
            ctx_spec,
            pl.BlockSpec((1, 1, 6 * D), mod_map),
            pl.BlockSpec((TT, A_WIDTH), tile2),
            pl.BlockSpec((TT, 2 * B_KEY_WIDTH), tile2),
            pl.BlockSpec((TT, B_WIDTH), tile2),
            pl.BlockSpec((TT, B_WIDTH), tile2),
            pl.BlockSpec((1, B_HEADS, KVR, B_DV), lambda i: (i, 0, 0, 0)),
            pl.BlockSpec((1, B_WIDTH), const2),
            pl.BlockSpec((D, D), const2),
            pl.BlockSpec((1, D), const2),
            pl.BlockSpec((D, LANE), const2),
            pl.BlockSpec((1, LANE), const2),
            pl.BlockSpec((TT, TT), const2),
        ],
        out_specs=[
            pl.BlockSpec((TT, D), tile2),
            pl.BlockSpec((TT * ROW_SUB, LANE), tile2),
            pl.BlockSpec((TT, LANE), tile2),
            pl.BlockSpec((SUBLANE, TT), lambda i: (0, i)),
            pl.BlockSpec((1, LANE), const2),
        ],
        out_shape=[
            jax.ShapeDtypeStruct((NP, D), F32),
            jax.ShapeDtypeStruct((NP * ROW_SUB, LANE), F32),
            jax.ShapeDtypeStruct((NP, LANE), F32),
            jax.ShapeDtypeStruct((SUBLANE, NP), F32),
            jax.ShapeDtypeStruct((1, LANE), F32),
        ],
        scratch_shapes=[
            pltpu.VMEM((CPT, 2 * B_KEY_WIDTH, B_WIDTH), BF16),
            pltpu.VMEM((1, LANE), F32),
        ],
        compiler_params=_cparams(("arbitrary",)),
        name="mix_post",
    )(lat, ctx, mod3, a_out, qe, oi, sr, states, lw["g_gla"], lw["w_out"], lw["g_norm2"],
      lw["w_router"], lw["b_router"], lw["l_strict"])


def _plan_kernel(rt_ref, ps_ref, idx_ref):
    sub = lax.broadcasted_iota(I32, (N_EXPERTS, LANE), 0).astype(F32)
    ps = ps_ref[...]
    segs = DT // LANE
    for k in range(2):
        for seg in range(segs):
            sl = slice(seg * LANE, (seg + 1) * LANE)
            e = rt_ref[k:k + 1, sl]
            r = rt_ref[4 + k:5 + k, sl]
            base = jnp.sum(jnp.where(sub == e, ps, 0.0), axis=0, keepdims=True)
            idx_ref[0, k * segs + seg:k * segs + seg + 1, :] = ((base + r) * float(ROW_SUB)).astype(I32)


def _plan_call(route_t, pad_starts):
    n_dt = route_t.shape[1] // DT
    return pl.pallas_call(
        _plan_kernel,
        grid=(n_dt,),
        in_specs=[
            pl.BlockSpec((SUBLANE, DT), lambda i: (0, i)),
            pl.BlockSpec((N_EXPERTS, 1), lambda i: (0, 0)),
        ],
        out_specs=pl.BlockSpec((1, 2 * DT // LANE, LANE), lambda i: (i, 0, 0)),
        out_shape=jax.ShapeDtypeStruct((n_dt, 2 * DT // LANE, LANE), I32),
        compiler_params=_cparams(("arbitrary",)),
        name="moe_plan",
    )(route_t, pad_starts.astype(F32)[:, None])


def _row_loop(n_rows, fn):
    unroll = 8

    def body(jo, carry):
        for u in range(unroll):
            fn(jo * unroll + u, jo, u)
        return carry

    lax.fori_loop(0, n_rows // unroll, body, 0)


def _tile_indices(idx_smem, jo, u):
    r = lax.shift_right_logical(jo, 4)
    c = (jo & (LANE // 8 - 1)) * 8 + u
    d0 = pl.multiple_of(idx_smem[r, c], ROW_SUB)
    d1 = pl.multiple_of(idx_smem[r + DT // LANE, c], ROW_SUB)
    return d0, d1


def _dispatch_kernel(pe_ref, idx_hbm, x_ref, xs_out, idx_smem, zbuf, sem_i, sem_o, sem_z):
    i = pl.program_id(0)
    cp = pltpu.make_async_copy(idx_hbm.at[i], idx_smem, sem_i)
    cp.start()
    blk_rows = MOE_BLK * ROW_SUB

    @pl.when(i == 0)
    def _():
        zbuf[...] = jnp.zeros(zbuf.shape, zbuf.dtype)
        for e in range(N_EXPERTS):
            start = pl.multiple_of(jnp.maximum(pe_ref[e] - MOE_BLK, 0) * ROW_SUB, blk_rows)
            pltpu.make_async_copy(zbuf, xs_out.at[pl.ds(start, blk_rows), :], sem_z).start()
        for e in range(N_EXPERTS):
            pltpu.make_async_copy(zbuf, xs_out.at[pl.ds(0, blk_rows), :], sem_z).wait()
        first_free = pe_ref[N_EXPERTS - 1] // MOE_BLK
        n_blocks = xs_out.shape[0] // blk_rows

        def clear(b, carry):
            start = pl.multiple_of(b * blk_rows, blk_rows)
            pltpu.make_async_copy(zbuf, xs_out.at[pl.ds(start, blk_rows), :], sem_z).start()
            return carry

        def clear_wait(b, carry):
            pltpu.make_async_copy(zbuf, xs_out.at[pl.ds(0, blk_rows), :], sem_z).wait()
            return carry

        lax.fori_loop(first_free, n_blocks, clear, 0)
        lax.fori_loop(first_free, n_blocks, clear_wait, 0)

    cp.wait()

    def issue(j, jo, u):
        d0, d1 = _tile_indices(idx_smem, jo, u)
        src = x_ref.at[pl.ds(pl.multiple_of(j * ROW_SUB, ROW_SUB), ROW_SUB), :]
        pltpu.make_async_copy(src, xs_out.at[pl.ds(d0, ROW_SUB), :], sem_o).start(priority=0)
        pltpu.make_async_copy(src, xs_out.at[pl.ds(d1, ROW_SUB), :], sem_o).start(priority=1)

    _row_loop(DT, issue)
    for _ in range(2):
        pltpu.make_async_copy(x_ref, xs_out.at[pl.ds(0, DT * ROW_SUB), :], sem_o).wait()


def _dispatch_call(pad_ends, idx, x_rows, n_buf_rows):
    n_tiles = x_rows.shape[0] // (DT * ROW_SUB)
    grid_spec = pltpu.PrefetchScalarGridSpec(
        num_scalar_prefetch=1,
        grid=(n_tiles,),
        in_specs=[
            pl.BlockSpec(memory_space=pl.ANY),
            pl.BlockSpec((DT * ROW_SUB, LANE), lambda i, pe: (i, 0)),
        ],
        out_specs=pl.BlockSpec(memory_space=pl.ANY),
        scratch_shapes=[
            pltpu.SMEM((2 * DT // LANE, LANE), I32),
            pltpu.VMEM((MOE_BLK * ROW_SUB, LANE), F32),
            pltpu.SemaphoreType.DMA,
            pltpu.SemaphoreType.DMA,
            pltpu.SemaphoreType.DMA,
        ],
    )
    return pl.pallas_call(
        _dispatch_kernel,
        grid_spec=grid_spec,
        out_shape=jax.ShapeDtypeStruct((n_buf_rows * ROW_SUB, LANE), F32),
        compiler_params=_cparams(("arbitrary",)),
        name="moe_dispatch",
    )(pad_ends, idx, x_rows)


def _ffn_kernel(be_ref, nu_ref, x_ref, w1_ref, w3_ref, w2_ref, y_ref):
    i = pl.program_id(0)

    @pl.when(i < nu_ref[0])
    def _():
        x = _load_rows(x_ref, MOE_BLK).astype(BF16)
        h1 = jnp.dot(x, w1_ref[0].astype(BF16), preferred_element_type=F32)
        h3 = jnp.dot(x, w3_ref[0].astype(BF16), preferred_element_type=F32)
        hh = (h1 * jax.nn.sigmoid(h1) * h3).astype(BF16)
        y = jnp.dot(hh, w2_ref[0].astype(BF16), preferred_element_type=F32)
        _store_rows(y_ref, y)

    @pl.when(i >= nu_ref[0])
    def _():
        y_ref[...] = jnp.zeros(y_ref.shape, y_ref.dtype)


def _ffn_call(block_expert, n_used, xs, w1, w3, w2):
    blk_rows = MOE_BLK * ROW_SUB
    NB = xs.shape[0] // blk_rows
    _, D, DE = w1.shape

    def blk(i, be, nu):
        return jnp.minimum(i, nu[0] - 1)

    grid_spec = pltpu.PrefetchScalarGridSpec(
        num_scalar_prefetch=2,
        grid=(NB,),
        in_specs=[
            pl.BlockSpec((blk_rows, LANE), lambda i, be, nu: (blk(i, be, nu), 0)),
            pl.BlockSpec((1, D, DE), lambda i, be, nu: (be[blk(i, be, nu)], 0, 0)),
            pl.BlockSpec((1, D, DE), lambda i, be, nu: (be[blk(i, be, nu)], 0, 0)),
            pl.BlockSpec((1, DE, D), lambda i, be, nu: (be[blk(i, be, nu)], 0, 0)),
        ],
        out_specs=pl.BlockSpec((blk_rows, LANE), lambda i, be, nu: (i, 0)),
    )
    return pl.pallas_call(
        _ffn_kernel,
        grid_spec=grid_spec,
        out_shape=jax.ShapeDtypeStruct(xs.shape, F32),
        compiler_params=_cparams(("arbitrary",)),
        name="moe_ffn",
    )(block_expert, n_used, xs, w1, w3, w2)


def _combine_kernel(idx_hbm, h1_ref, route_ref, mod_ref, gf_ref, y_hbm, o_ref, idx_smem, rows0, rows1, sem_i, sem_g,
                    *, final):
    D = h1_ref.shape[1]
    i = pl.program_id(0)
    cp = pltpu.make_async_copy(idx_hbm.at[i], idx_smem, sem_i)
    cp.start()
    cp.wait()

    def issue(j, jo, u):
        d0, d1 = _tile_indices(idx_smem, jo, u)
        dst = pl.ds(pl.multiple_of(j * ROW_SUB, ROW_SUB), ROW_SUB)
        pltpu.make_async_copy(y_hbm.at[pl.ds(d0, ROW_SUB), :], rows0.at[dst, :], sem_g).start(priority=0)
        pltpu.make_async_copy(y_hbm.at[pl.ds(d1, ROW_SUB), :], rows1.at[dst, :], sem_g).start(priority=1)

    _row_loop(DT, issue)
    for rows in (rows0, rows1):
        pltpu.make_async_copy(y_hbm.at[pl.ds(0, DT * ROW_SUB), :], rows, sem_g).wait()

    route = route_ref[...]
    gate1 = route[:, 2:3]
    gate2 = route[:, 3:4]
    y = gate1 * _load_rows(rows0, DT) + gate2 * _load_rows(rows1, DT)
    g2 = mod_ref[0][:, 5 * D:6 * D]
    out = h1_ref[...] + g2 * y
    if final:
        msq = jnp.mean(out * out, axis=-1, keepdims=True)
        out = out * lax.rsqrt(msq + EPS) * gf_ref[...]
    o_ref[...] = out


def _combine_call(idx, h1, route, mod3, g_final, yb, n_lat_dt, dt_per_sample, n_mod_ctx, final):
    N, D = h1.shape
    n_tiles = N // DT

    def mod_map(i):
        return (jnp.where(i < n_lat_dt, i // dt_per_sample, n_mod_ctx), 0, 0)

    return pl.pallas_call(
        functools.partial(_combine_kernel, final=final),
        grid=(n_tiles,),
        in_specs=[
            pl.BlockSpec(memory_space=pl.ANY),
            pl.BlockSpec((DT, D), lambda i: (i, 0)),
            pl.BlockSpec((DT, LANE), lambda i: (i, 0)),
            pl.BlockSpec((1, 1, 6 * D), mod_map),
            pl.BlockSpec((1, D), lambda i: (0, 0)),
            pl.BlockSpec(memory_space=pl.ANY),
        ],
        out_specs=pl.BlockSpec((DT, D), lambda i: (i, 0)),
        out_shape=jax.ShapeDtypeStruct((N, D), F32),
        scratch_shapes=[
            pltpu.SMEM((2 * DT // LANE, LANE), I32),
            pltpu.VMEM((DT * ROW_SUB, LANE), F32),
            pltpu.VMEM((DT * ROW_SUB, LANE), F32),
            pltpu.SemaphoreType.DMA,
            pltpu.SemaphoreType.DMA,
        ],
        compiler_params=_cparams(("arbitrary",)),
        name="moe_combine",
    )(idx, h1, route, mod3, g_final, yb)


def _layer_weights(l, w_in, g_norm1, ln_v_g, ln_v_b, w_sp, b_sp, w_gate_up, b_gate, g_gla, w_out, g_norm2,
                   w_router_g, b_router_g, w_router_e, b_router_e):
    D = w_in.shape[1]
    d_in = w_in.shape[2]
    pad = (-d_in) % LANE
    w_in_p = jnp.pad(w_in[l], ((0, 0), (0, pad))).astype(BF16)
    KW = B_KEY_WIDTH
    wg = jnp.zeros((LANE, 2 * KW), F32)
    wg = wg.at[0:GATE_RANK, 0:KW].set(w_gate_up[l, 0])
    wg = wg.at[GATE_RANK:2 * GATE_RANK, KW:2 * KW].set(w_gate_up[l, 1])
    bg = jnp.concatenate([b_gate[l, 0], b_gate[l, 1]])[None, :]
    bsp = jnp.repeat(b_sp[l].T, A_GROUP_DIM, axis=1)
    wr = jnp.zeros((D, LANE), F32)
    wr = wr.at[:, 0:N_EXPERTS].set(w_router_e[l])
    wr = wr.at[:, N_EXPERTS:N_EXPERTS + N_GROUPS].set(w_router_g[l])
    br = jnp.zeros((1, LANE), F32)
    br = br.at[0, 0:N_EXPERTS].set(b_router_e[l])
    br = br.at[0, N_EXPERTS:N_EXPERTS + N_GROUPS].set(b_router_g[l])
    r = jnp.arange(TT, dtype=I32)
    same = (r[:, None] // CHUNK_B) == (r[None, :] // CHUNK_B)
    tri_f = (same & (r[:, None] >= r[None, :])).astype(BF16)
    tri_b = (same & (r[:, None] <= r[None, :])).astype(BF16)
    l_strict = (r[:, None] > r[None, :]).astype(BF16)
    return dict(
        w_in=w_in_p, g_norm1=g_norm1[l][None, :], ln_g=ln_v_g[l][None, :], ln_b=ln_v_b[l][None, :],
        w_sp=w_sp[l].astype(BF16), b_sp=bsp, w_gate=wg.astype(BF16), b_gate=bg,
        g_gla=g_gla[l][None, :], w_out=w_out[l].astype(BF16), g_norm2=g_norm2[l][None, :],
        w_router=wr.astype(BF16), b_router=br, tri_f=tri_f, tri_b=tri_b, l_strict=l_strict)


def _segment_layout(counts_row, n_tokens):
    counts = counts_row[0, :N_EXPERTS].astype(I32)
    padded = (counts + MOE_BLK - 1) // MOE_BLK * MOE_BLK
    pad_ends = jnp.cumsum(padded)
    pad_starts = pad_ends - padded
    n_blocks = (n_tokens * 2 + N_EXPERTS * (MOE_BLK - 1)) // MOE_BLK + 1
    block_start = jnp.arange(n_blocks, dtype=I32) * MOE_BLK
    block_expert = jnp.sum((pad_ends[None, :] <= block_start[:, None]).astype(I32), axis=1)
    block_expert = jnp.minimum(block_expert, N_EXPERTS - 1)
    n_used = (pad_ends[-1] // MOE_BLK).astype(I32)[None]
    return pad_starts, pad_ends, block_expert, n_used, n_blocks * MOE_BLK


def kernel(x, c, ctx, c_ctx, w_mod, b_mod, g_norm1, w_in, ln_v_g, ln_v_b, w_sp, b_sp, w_gate_up, b_gate, g_gla,
           w_out, g_norm2, w_router_g, b_router_g, w_router_e, b_router_e, w1, w3, w2, g_final):
    Bn, S, D = x.shape
    Lc = ctx.shape[1]
    depth = w_mod.shape[0]
    assert S % DT == 0 and Lc % TT == 0 and (Bn * Lc) % DT == 0 and Lc == TT
    n_lat = Bn * S
    n_ctx = Bn * Lc
    n_lat_tiles = n_lat // TT
    tiles_per_sample = S // TT

    cc = jnp.concatenate([c, c_ctx[None, :]], axis=0)
    mod_all = _modulation(cc, w_mod, b_mod)
    gf = g_final[None, :]

    lat, cx, ctx_off = x.reshape(n_lat, D), ctx.reshape(n_ctx, D), 0
    for l in range(depth):
        last = l == depth - 1
        lw = _layer_weights(l, w_in, g_norm1, ln_v_g, ln_v_b, w_sp, b_sp, w_gate_up, b_gate, g_gla, w_out,
                            g_norm2, w_router_g, b_router_g, w_router_e, b_router_e)
        mod3 = mod_all[l].reshape(Bn + 1, 1, 6 * D)
        a_out, qe, oi, sr, kv, dec = _pre_call(lat, cx, ctx_off, n_ctx // TT, mod3, lw, n_lat_tiles,
                                               tiles_per_sample, Bn)
        states = _scan_call(kv, dec, Bn, n_lat_tiles, tiles_per_sample)
        n_tok = n_lat if last else n_lat + n_ctx
        h1, hn2_rows, route, route_t, counts = _post_call(lat, cx, ctx_off, mod3, (a_out, qe, oi, sr), states, lw,
                                                          n_tok // TT, n_lat_tiles, tiles_per_sample, Bn)
        pad_starts, pad_ends, block_expert, n_used, n_buf_rows = _segment_layout(counts, n_tok)
        idx = _plan_call(route_t, pad_starts)
        xs = _dispatch_call(pad_ends, idx, hn2_rows, n_buf_rows)
        yb = _ffn_call(block_expert, n_used, xs, w1[l], w3[l], w2[l])
        h_all = _combine_call(idx, h1, route, mod3, gf, yb, n_lat // DT, S // DT, Bn, last)
        lat, cx, ctx_off = h_all, h_all, n_lat_tiles
    return h_all.reshape(Bn, S, D)
```

```python
import functools

import jax
import jax.numpy as jnp
from jax import lax
from jax.experimental import pallas as pl
from jax.experimental.pallas import tpu as pltpu

F32 = jnp.float32
BF16 = jnp.bfloat16
I32 = jnp.int32
U32 = jnp.uint32

EPS = 1e-6
LANE = 128
SUBLANE = 8

A_GROUPS = 4
A_GROUP_DIM = 128
A_WIDTH = A_GROUPS * A_GROUP_DIM
CHUNK_A = 128
B_HEADS = 4
B_DK = 64
B_DV = 128
B_KEY_WIDTH = B_HEADS * B_DK
B_WIDTH = B_HEADS * B_DV
GATE_RANK = 16
GATE_TAU = 16.0
CHUNK_B = 64
N_GROUPS = 8
EXPERTS_PER_GROUP = 8
N_EXPERTS = N_GROUPS * EXPERTS_PER_GROUP

TT = 256
CPT = TT // CHUNK_B
MOE_BLK = 256
DT = 512
ROW_SUB = 8
XROW_SUB = 4

VMEM_LIMIT = 48 * 1024 * 1024


def _cparams(sem):
    return pltpu.CompilerParams(dimension_semantics=sem, vmem_limit_bytes=VMEM_LIMIT)


def _load_rows(ref, n):
    return jnp.concatenate([ref[pl.ds(s, n, stride=ROW_SUB), :] for s in range(ROW_SUB)], axis=1)


def _store_rows(ref, val):
    n = val.shape[0]
    for s in range(ROW_SUB):
        ref[pl.ds(s, n, stride=ROW_SUB), :] = val[:, s * LANE:(s + 1) * LANE]


def _mod_kernel(c_ref, w_ref, b_ref, o_ref):
    c = c_ref[...]
    s = (c * jax.nn.sigmoid(c)).astype(BF16)
    o_ref[0] = jnp.dot(s, w_ref[0].astype(BF16), preferred_element_type=F32) + b_ref[0]


def _modulation(cc, w_mod, b_mod):
    L, D, D6 = w_mod.shape
    R = cc.shape[0]
    tn = 1536
    return pl.pallas_call(
        _mod_kernel,
        grid=(L, D6 // tn),
        in_specs=[
            pl.BlockSpec((R, D), lambda l, j: (0, 0)),
            pl.BlockSpec((1, D, tn), lambda l, j: (l, 0, j)),
            pl.BlockSpec((1, 1, tn), lambda l, j: (l, 0, j)),
        ],
        out_specs=pl.BlockSpec((1, R, tn), lambda l, j: (l, 0, j)),
        out_shape=jax.ShapeDtypeStruct((L, R, D6), F32),
        compiler_params=_cparams(("arbitrary", "arbitrary")),
        name="modulation",
    )(cc, w_mod, b_mod.reshape(L, 1, D6))


def _select_tile(lat_ref, ctx_ref, n_lat_tiles):
    return jnp.where(pl.program_id(0) < n_lat_tiles, lat_ref[...], ctx_ref[...])


def _stream_specs(D, n_lat_tiles, ctx_off):
    lat = pl.BlockSpec((TT, D), lambda i: (jnp.minimum(i, n_lat_tiles - 1), 0))
    ctx = pl.BlockSpec((TT, D), lambda i: (ctx_off + jnp.maximum(i - n_lat_tiles, 0), 0))
    return lat, ctx


def _pre_kernel(lat_ref, ctx_ref, mod_ref, g1_ref, win_ref, lng_ref, lnb_ref, wsp_ref, bsp_ref, wg_ref, bg_ref,
                trif_ref, trib_ref,
                a_ref, qe_ref, oi_ref, sr_ref, kv_ref, dec_ref, *, n_lat_tiles):
    D = lat_ref.shape[1]
    h = _select_tile(lat_ref, ctx_ref, n_lat_tiles)
    mod = mod_ref[0]
    sh1 = mod[:, 0:D]
    sc1 = mod[:, D:2 * D]
    ms = jnp.mean(h * h, axis=-1, keepdims=True)
    hn = h * lax.rsqrt(ms + EPS) * g1_ref[...]
    hn = hn * (1.0 + sc1) + sh1
    z = jnp.dot(hn.astype(BF16), win_ref[...], preferred_element_type=F32)
    o_av = A_WIDTH
    o_q = 2 * A_WIDTH
    o_r = o_q + B_KEY_WIDTH
    o_k = o_r + B_WIDTH
    o_v = o_k + B_KEY_WIDTH
    o_g = o_v + B_WIDTH
    zu = z[:, 0:o_av]
    zv = z[:, o_av:o_q]
    q = z[:, o_q:o_r] * (B_DK ** -0.5)
    zr = z[:, o_r:o_k]
    k = z[:, o_k:o_v]
    vv = z[:, o_v:o_g]
    zg = z[:, o_g:o_g + LANE]

    u = jax.nn.gelu(zu)
    v = jax.nn.gelu(zv)
    s_groups = []
    for g in range(A_GROUPS):
        sl = slice(g * A_GROUP_DIM, (g + 1) * A_GROUP_DIM)
        vg = v[:, sl]
        mu = jnp.mean(vg, axis=-1, keepdims=True)
        dv = vg - mu
        var = jnp.mean(dv * dv, axis=-1, keepdims=True)
        vn = (dv * lax.rsqrt(var + EPS)) * lng_ref[:, sl] + lnb_ref[:, sl]
        vnb = vn.astype(BF16)
        rows = []
        for c in range(TT // CHUNK_A):
            rows.append(jnp.dot(wsp_ref[g], vnb[c * CHUNK_A:(c + 1) * CHUNK_A, :],
                                preferred_element_type=F32) + bsp_ref[:, sl])
        s_groups.append(jnp.concatenate(rows, axis=0))
    a_ref[...] = (u * jnp.concatenate(s_groups, axis=1)).astype(BF16)

    sr_ref[...] = (zr * jax.nn.sigmoid(zr)).astype(BF16)
    lg = jnp.dot(zg.astype(BF16), wg_ref[...], preferred_element_type=F32) + bg_ref[...]
    la = (jnp.minimum(lg, 0.0) - jnp.log1p(jnp.exp(-jnp.abs(lg)))) * (1.0 / GATE_TAU)
    la_hi = la.astype(BF16)
    la_lo = (la - la_hi.astype(F32)).astype(BF16)
    KW = B_KEY_WIDTH
    bf = (jnp.dot(trif_ref[...], la_hi[:, :KW], preferred_element_type=F32)
          + jnp.dot(trif_ref[...], la_lo[:, :KW], preferred_element_type=F32))
    bb = (jnp.dot(trib_ref[...], la_hi[:, KW:], preferred_element_type=F32)
          + jnp.dot(trib_ref[...], la_lo[:, KW:], preferred_element_type=F32))
    blf = jnp.concatenate(
        [jnp.broadcast_to(bf[c * CHUNK_B + CHUNK_B - 1:(c + 1) * CHUNK_B, :], (CHUNK_B, KW)) for c in range(CPT)], axis=0)
    blb = jnp.concatenate(
        [jnp.broadcast_to(bb[c * CHUNK_B:c * CHUNK_B + 1, :], (CHUNK_B, KW)) for c in range(CPT)], axis=0)
    qe_f = q * jnp.exp(bf)
    qe_b = q * jnp.exp(bb)
    ke_f = (k * jnp.exp(-bf)).astype(BF16)
    ke_b = (k * jnp.exp(-bb)).astype(BF16)
    kd_f = k * jnp.exp(blf - bf)
    kd_b = k * jnp.exp(blb - bb)
    qe_ref[...] = jnp.concatenate([qe_f, qe_b], axis=1).astype(BF16)
    dec_ref[0, 0] = jnp.exp(blf).T
    dec_ref[0, 1] = jnp.exp(blb).T

    row = lax.broadcasted_iota(I32, (TT, TT), 0)
    col = lax.broadcasted_iota(I32, (TT, TT), 1)
    same = lax.shift_right_logical(row, 6) == lax.shift_right_logical(col, 6)
    m_f = same & (row >= col)
    m_b = same & (row <= col)
    lane_head = lax.shift_right_logical(lax.broadcasted_iota(I32, (1, KW), 1), 6)
    tok_chunk = lax.shift_right_logical(lax.broadcasted_iota(I32, (1, TT), 1), 6)
    vb = vv.astype(BF16)
    kdT_f = kd_f.T
    kdT_b = kd_b.T
    nt = (((1,), (1,)), ((), ()))
    oi_heads = []
    for hd in range(B_HEADS):
        hm = lane_head == hd
        qf = jnp.where(hm, qe_f, 0.0).astype(BF16)
        qb = jnp.where(hm, qe_b, 0.0).astype(BF16)
        att_f = lax.dot_general(qf, ke_f, nt, preferred_element_type=F32)
        att_b = lax.dot_general(qb, ke_b, nt, preferred_element_type=F32)
        att = jnp.where(m_f, att_f, 0.0) + jnp.where(m_b, att_b, 0.0)
        v_h = vb[:, hd * B_DV:(hd + 1) * B_DV]
        oi_heads.append(jnp.dot(att.astype(BF16), v_h, preferred_element_type=F32))
        parts = []
        for kdT in (kdT_f, kdT_b):
            kh = kdT[hd * B_DK:(hd + 1) * B_DK, :]
            for c in range(CPT):
                parts.append(jnp.where(tok_chunk == c, kh, 0.0))
        lhs = jnp.concatenate(parts, axis=0).astype(BF16)
        kv_ref[0, hd] = jnp.dot(lhs, v_h, preferred_element_type=F32)
    oi_ref[...] = jnp.concatenate(oi_heads, axis=1)


def _pre_call(lat, ctx, ctx_off, n_ctx_tiles, mod3, lw, n_lat_tiles, tiles_per_sample, n_mod_ctx):
    D = lat.shape[1]
    n_tiles = n_lat_tiles + n_ctx_tiles
    N = n_tiles * TT
    lat_spec, ctx_spec = _stream_specs(D, n_lat_tiles, ctx_off)
    DINP = lw["w_in"].shape[1]
    KVR = 2 * CPT * B_DK

    def mod_map(i):
        return (jnp.where(i < n_lat_tiles, i // tiles_per_sample, n_mod_ctx), 0, 0)

    const2 = lambda i: (0, 0)
    const3 = lambda i: (0, 0, 0)
    tile2 = lambda i: (i, 0)
    return pl.pallas_call(
        functools.partial(_pre_kernel, n_lat_tiles=n_lat_tiles),
        grid=(n_tiles,),
        in_specs=[
            lat_spec,
            ctx_spec,
            pl.BlockSpec((1, 1, 6 * D), mod_map),
            pl.BlockSpec((1, D), const2),
            pl.BlockSpec((D, DINP), const2),
            pl.BlockSpec((1, A_WIDTH), const2),
            pl.BlockSpec((1, A_WIDTH), const2),
            pl.BlockSpec((A_GROUPS, CHUNK_A, CHUNK_A), const3),
            pl.BlockSpec((CHUNK_A, A_WIDTH), const2),
            pl.BlockSpec((LANE, 2 * B_KEY_WIDTH), const2),
            pl.BlockSpec((1, 2 * B_KEY_WIDTH), const2),
            pl.BlockSpec((TT, TT), const2),
            pl.BlockSpec((TT, TT), const2),
        ],
        out_specs=[
            pl.BlockSpec((TT, A_WIDTH), tile2),
            pl.BlockSpec((TT, 2 * B_KEY_WIDTH), tile2),
            pl.BlockSpec((TT, B_WIDTH), tile2),
            pl.BlockSpec((TT, B_WIDTH), tile2),
            pl.BlockSpec((1, B_HEADS, KVR, B_DV), lambda i: (i, 0, 0, 0)),
            pl.BlockSpec((1, 2, B_KEY_WIDTH, TT), lambda i: (i, 0, 0, 0)),
        ],
        out_shape=[
            jax.ShapeDtypeStruct((N, A_WIDTH), BF16),
            jax.ShapeDtypeStruct((N, 2 * B_KEY_WIDTH), BF16),
            jax.ShapeDtypeStruct((N, B_WIDTH), F32),
            jax.ShapeDtypeStruct((N, B_WIDTH), BF16),
            jax.ShapeDtypeStruct((n_tiles, B_HEADS, KVR, B_DV), F32),
            jax.ShapeDtypeStruct((n_tiles, 2, B_KEY_WIDTH, TT), F32),
        ],
        compiler_params=_cparams(("arbitrary",)),
        name="mix_pre",
    )(lat, ctx, mod3, lw["g_norm1"], lw["w_in"], lw["ln_g"], lw["ln_b"], lw["w_sp"], lw["b_sp"],
      lw["w_gate"], lw["b_gate"], lw["tri_f"], lw["tri_b"])


def _scan_kernel(kvc_ref, kvl_ref, dcc_ref, dcl_ref, sc_ref, sl_ref):
    n_lat = kvl_ref.shape[0]
    fwd = [(kvc_ref, dcc_ref, sc_ref, 0, c) for c in range(CPT)]
    fwd += [(kvl_ref, dcl_ref, sl_ref, t, c) for t in range(n_lat) for c in range(CPT)]
    bwd = [(kvc_ref, dcc_ref, sc_ref, 0, c) for c in reversed(range(CPT))]
    bwd += [(kvl_ref, dcl_ref, sl_ref, t, c) for t in reversed(range(n_lat)) for c in reversed(range(CPT))]
    for hd in range(B_HEADS):
        for d, seq in enumerate((fwd, bwd)):
            s = jnp.zeros((B_DK, B_DV), F32)
            for kv_ref, dc_ref, out_ref, t, c in seq:
                r0 = (d * CPT + c) * B_DK
                out_ref[t, hd, r0:r0 + B_DK, :] = s.astype(BF16)
                dcol = dc_ref[t, d, hd * B_DK:(hd + 1) * B_DK, c * CHUNK_B:c * CHUNK_B + 1]
                s = dcol * s + kv_ref[t, hd, r0:r0 + B_DK, :]


def _scan_call(kv, dec, n_samples, n_lat_tiles, tiles_per_sample):
    n_tiles, _, KVR, _ = kv.shape
    kv_c = pl.BlockSpec((1, B_HEADS, KVR, B_DV), lambda b: (n_lat_tiles + b, 0, 0, 0))
    kv_l = pl.BlockSpec((tiles_per_sample, B_HEADS, KVR, B_DV), lambda b: (b, 0, 0, 0))
    dc_c = pl.BlockSpec((1, 2, B_KEY_WIDTH, TT), lambda b: (n_lat_tiles + b, 0, 0, 0))
    dc_l = pl.BlockSpec((tiles_per_sample, 2, B_KEY_WIDTH, TT), lambda b: (b, 0, 0, 0))
    s_ctx, s_lat = pl.pallas_call(
        _scan_kernel,
        grid=(n_samples,),
        in_specs=[kv_c, kv_l, dc_c, dc_l],
        out_specs=[
            pl.BlockSpec((1, B_HEADS, KVR, B_DV), lambda b: (b, 0, 0, 0)),
            pl.BlockSpec((tiles_per_sample, B_HEADS, KVR, B_DV), lambda b: (b, 0, 0, 0)),
        ],
        out_shape=[
            jax.ShapeDtypeStruct((n_samples, B_HEADS, KVR, B_DV), BF16),
            jax.ShapeDtypeStruct((n_lat_tiles, B_HEADS, KVR, B_DV), BF16),
        ],
        compiler_params=_cparams(("arbitrary",)),
        name="gla_scan",
    )(kv, kv, dec, dec)
    return jnp.concatenate([s_lat, s_ctx], axis=0)


def _post_kernel(lat_ref, ctx_ref, mod_ref, a_ref, qe_ref, oi_ref, sr_ref, st_ref, ggla_ref, wout_ref, g2_ref,
                 wr_ref, br_ref, lstr_ref,
                 h1_ref, hn2_ref, route_ref, routet_ref, cnt_ref,
                 rhs_scr, cnt_scr, *, n_lat_tiles):
    D = lat_ref.shape[1]
    i = pl.program_id(0)

    @pl.when(i == 0)
    def _():
        rhs_scr[...] = jnp.zeros(rhs_scr.shape, rhs_scr.dtype)
        cnt_scr[...] = jnp.zeros(cnt_scr.shape, cnt_scr.dtype)

    qe = qe_ref[...]
    o_parts = []
    for c in range(CPT):
        for d in range(2):
            for hd in range(B_HEADS):
                r0 = d * B_KEY_WIDTH + hd * B_DK
                s0 = (d * CPT + c) * B_DK
                rhs_scr[c, r0:r0 + B_DK, hd * B_DV:(hd + 1) * B_DV] = st_ref[0, hd, s0:s0 + B_DK, :]
        o_parts.append(jnp.dot(qe[c * CHUNK_B:(c + 1) * CHUNK_B, :], rhs_scr[c], preferred_element_type=F32))
    o = oi_ref[...] + jnp.concatenate(o_parts, axis=0)

    heads = []
    for hd in range(B_HEADS):
        sl = slice(hd * B_DV, (hd + 1) * B_DV)
        oh = o[:, sl]
        msq = jnp.mean(oh * oh, axis=-1, keepdims=True)
        heads.append(oh * lax.rsqrt(msq + EPS) * ggla_ref[:, sl])
    b_out = jnp.concatenate(heads, axis=1) * sr_ref[...].astype(F32)
    mix_in = jnp.concatenate([a_ref[...], b_out.astype(BF16)], axis=1)
    mix = jnp.dot(mix_in, wout_ref[...], preferred_element_type=F32)

    mod = mod_ref[0]
    g1 = mod[:, 2 * D:3 * D]
    sh2 = mod[:, 3 * D:4 * D]
    sc2 = mod[:, 4 * D:5 * D]
    h1 = _select_tile(lat_ref, ctx_ref, n_lat_tiles) + g1 * mix
    h1_ref[...] = h1
    msq = jnp.mean(h1 * h1, axis=-1, keepdims=True)
    hn2 = h1 * lax.rsqrt(msq + EPS) * g2_ref[...]
    hn2 = hn2 * (1.0 + sc2) + sh2
    _pack_rows(hn2_ref, hn2)

    lg = jnp.dot(hn2.astype(BF16), wr_ref[...], preferred_element_type=F32) + br_ref[...]
    lane = lax.broadcasted_iota(I32, lg.shape, 1)
    lanef = lane.astype(F32)
    neg = jnp.float32(-3.0e38)
    big = jnp.float32(1.0e9)
    is_g = (lane >= N_EXPERTS) & (lane < N_EXPERTS + N_GROUPS)
    gl = jnp.where(is_g, lg, neg)
    gmax = jnp.max(gl, axis=1, keepdims=True)
    gsum = jnp.sum(jnp.where(is_g, jnp.exp(gl - gmax), 0.0), axis=1, keepdims=True)
    p_top = 1.0 / gsum
    gidx = jnp.min(jnp.where(is_g & (gl == gmax), lanef, big), axis=1, keepdims=True) - float(N_EXPERTS)
    in_g = (lane < N_EXPERTS) & (lax.shift_right_logical(lane, 3).astype(F32) == gidx)
    el = jnp.where(in_g, lg, neg)
    m1 = jnp.max(el, axis=1, keepdims=True)
    i1 = jnp.min(jnp.where(in_g & (el == m1), lanef, big), axis=1, keepdims=True)
    el2 = jnp.where(lanef == i1, neg, el)
    m2 = jnp.max(el2, axis=1, keepdims=True)
    i2 = jnp.min(jnp.where(in_g & (el2 == m2), lanef, big), axis=1, keepdims=True)
    t = jnp.exp(m2 - m1)
    gate1 = p_top / (1.0 + t)
    gate2 = p_top * t / (1.0 + t)

    sel1 = lanef == i1
    sel2 = lanef == (i2 + float(N_EXPERTS))
    oh = jnp.where(sel1, 1.0, 0.0) + jnp.where(sel2, 1.0, 0.0)
    prefix = jnp.dot(lstr_ref[...], oh.astype(BF16), preferred_element_type=F32)
    tot = jnp.sum(oh, axis=0, keepdims=True)
    tot_sw = pltpu.roll(tot, N_EXPERTS, axis=1)
    base = cnt_scr[...]
    val = prefix + base + jnp.where(lane[0:1, :] >= N_EXPERTS, tot_sw, 0.0)
    r1 = jnp.sum(jnp.where(sel1, val, 0.0), axis=1, keepdims=True)
    r2 = jnp.sum(jnp.where(sel2, val, 0.0), axis=1, keepdims=True)
    new_cnt = base + tot + tot_sw
    cnt_scr[...] = new_cnt
    cnt_ref[...] = new_cnt
    route = (jnp.where(lane == 0, i1, 0.0) + jnp.where(lane == 1, i2, 0.0)
             + jnp.where(lane == 2, gate1, 0.0) + jnp.where(lane == 3, gate2, 0.0)
             + jnp.where(lane == 4, r1, 0.0) + jnp.where(lane == 5, r2, 0.0))
    route_ref[...] = route
    routet_ref[...] = route.T[0:SUBLANE, :]


def _post_call(lat, ctx, ctx_off, mod3, pre_outs, states, lw, n_post_tiles, n_lat_tiles, tiles_per_sample, n_mod_ctx):
    a_out, qe, oi, sr = pre_outs
    D = lat.shape[1]
    KVR = 2 * CPT * B_DK
    NP = n_post_tiles * TT
    lat_spec, ctx_spec = _stream_specs(D, n_lat_tiles, ctx_off)

    def mod_map(i):
        return (jnp.where(i < n_lat_tiles, i // tiles_per_sample, n_mod_ctx), 0, 0)

    const2 = lambda i: (0, 0)
    tile2 = lambda i: (i, 0)
    return pl.pallas_call(
        functools.partial(_post_kernel, n_lat_tiles=n_lat_tiles),
        grid=(n_post_tiles,),
        in_specs=[
            lat_spec,
            ctx_spec,
            pl.BlockSpec((1, 1, 6 * D), mod_map),
            pl.BlockSpec((TT, A_WIDTH), tile2),
            pl.BlockSpec((TT, 2 * B_KEY_WIDTH), tile2),
            pl.BlockSpec((TT, B_WIDTH), tile2),
            pl.BlockSpec((TT, B_WIDTH), tile2),
            pl.BlockSpec((1, B_HEADS, KVR, B_DV), lambda i: (i, 0, 0, 0)),
            pl.BlockSpec((1, B_WIDTH), const2),
            pl.BlockSpec((D, D), const2),
            pl.BlockSpec((1, D), const2),
            pl.BlockSpec((D, LANE), const2),
            pl.BlockSpec((1, LANE), const2),
            pl.BlockSpec((TT, TT), const2),
        ],
        out_specs=[
            pl.BlockSpec((TT, D), tile2),
            pl.BlockSpec((TT * XROW_SUB, LANE), tile2),
            pl.BlockSpec((TT, LANE), tile2),
            pl.BlockSpec((SUBLANE, TT), lambda i: (0, i)),
            pl.BlockSpec((1, LANE), const2),
        ],
        out_shape=[
            jax.ShapeDtypeStruct((NP, D), F32),
            jax.ShapeDtypeStruct((NP * XROW_SUB, LANE), U32),
            jax.ShapeDtypeStruct((NP, LANE), F32),
            jax.ShapeDtypeStruct((SUBLANE, NP), F32),
            jax.ShapeDtypeStruct((1, LANE), F32),
        ],
        scratch_shapes=[
            pltpu.VMEM((CPT, 2 * B_KEY_WIDTH, B_WIDTH), BF16),
            pltpu.VMEM((1, LANE), F32),
        ],
        compiler_params=_cparams(("arbitrary",)),
        name="mix_post",
    )(lat, ctx, mod3, a_out, qe, oi, sr, states, lw["g_gla"], lw["w_out"], lw["g_norm2"],
      lw["w_router"], lw["b_router"], lw["l_strict"])


def _plan_kernel(rt_ref, ps_ref, init_hbm, idx_ref, inv_hbm, idx_smem, inv_smem, sem):
    i = pl.program_id(0)

    @pl.when(i == 0)
    def _():
        cp0 = pltpu.make_async_copy(init_hbm, inv_smem, sem)
        cp0.start()
        cp0.wait()

    sub = lax.broadcasted_iota(I32, (N_EXPERTS, LANE), 0).astype(F32)
    ps = ps_ref[...]
    segs = DT // LANE
    for k in range(2):
        for seg in range(segs):
            sl = slice(seg * LANE, (seg + 1) * LANE)
            e = rt_ref[k:k + 1, sl]
            r = rt_ref[4 + k:5 + k, sl]
            base = jnp.sum(jnp.where(sub == e, ps, 0.0), axis=0, keepdims=True)
            idx_ref[0, k * segs + seg:k * segs + seg + 1, :] = ((base + r) * float(ROW_SUB)).astype(I32)

    cp = pltpu.make_async_copy(idx_ref.at[0], idx_smem, sem)
    cp.start()
    cp.wait()
    row_shift = ROW_SUB.bit_length() - 1
    lane_shift = LANE.bit_length() - 1

    def scatter(j, jo, u):
        r = lax.shift_right_logical(jo, 4)
        c = (jo & (LANE // 8 - 1)) * 8 + u
        src = (i * DT + j) * XROW_SUB
        for k in range(2):
            d = lax.shift_right_logical(idx_smem[r + k * segs, c], row_shift)
            inv_smem[lax.shift_right_logical(d, lane_shift), d & (LANE - 1)] = src

    _row_loop(DT, scatter)

    @pl.when(i == pl.num_programs(0) - 1)
    def _():
        cp1 = pltpu.make_async_copy(inv_smem, inv_hbm, sem)
        cp1.start()
        cp1.wait()


def _plan_call(route_t, pad_starts, n_buf_rows):
    n_dt = route_t.shape[1] // DT
    inv_rows = n_buf_rows // LANE
    init = jnp.zeros((inv_rows, LANE), I32)
    return pl.pallas_call(
        _plan_kernel,
        grid=(n_dt,),
        in_specs=[
            pl.BlockSpec((SUBLANE, DT), lambda i: (0, i)),
            pl.BlockSpec((N_EXPERTS, 1), lambda i: (0, 0)),
            pl.BlockSpec(memory_space=pl.ANY),
        ],
        out_specs=[
            pl.BlockSpec((1, 2 * DT // LANE, LANE), lambda i: (i, 0, 0)),
            pl.BlockSpec(memory_space=pl.ANY),
        ],
        out_shape=[
            jax.ShapeDtypeStruct((n_dt, 2 * DT // LANE, LANE), I32),
            jax.ShapeDtypeStruct((inv_rows, LANE), I32),
        ],
        scratch_shapes=[
            pltpu.SMEM((2 * DT // LANE, LANE), I32),
            pltpu.SMEM((inv_rows, LANE), I32),
            pltpu.SemaphoreType.DMA,
        ],
        compiler_params=_cparams(("arbitrary",)),
        name="moe_plan",
    )(route_t, pad_starts.astype(F32)[:, None], init)


def _row_loop(n_rows, fn):
    unroll = 8

    def body(jo, carry):
        for u in range(unroll):
            fn(jo * unroll + u, jo, u)
        return carry

    lax.fori_loop(0, n_rows // unroll, body, 0)


def _tile_indices(idx_smem, jo, u):
    r = lax.shift_right_logical(jo, 4)
    c = (jo & (LANE // 8 - 1)) * 8 + u
    d0 = pl.multiple_of(idx_smem[r, c], ROW_SUB)
    d1 = pl.multiple_of(idx_smem[r + DT // LANE, c], ROW_SUB)
    return d0, d1


def _unpack_rows(ref, n):
    w = jnp.concatenate([ref[pl.ds(s, n, stride=XROW_SUB), :] for s in range(XROW_SUB)], axis=1)
    lo = lax.bitcast_convert_type(lax.shift_left(w, jnp.uint32(16)), F32)
    hi = lax.bitcast_convert_type(w & jnp.uint32(0xFFFF0000), F32)
    return jnp.concatenate([lo, hi], axis=1).astype(BF16)


def _pack_rows(ref, val):
    n, d = val.shape
    bits = lax.bitcast_convert_type(val.astype(BF16).astype(F32), U32)
    w = lax.shift_right_logical(bits[:, :d // 2], jnp.uint32(16)) | bits[:, d // 2:]
    for s in range(XROW_SUB):
        ref[pl.ds(s, n, stride=XROW_SUB), :] = w[:, s * LANE:(s + 1) * LANE]


def _ffn_kernel(be_ref, nu_ref, inv_hbm, xp_hbm, w1_ref, w3_ref, w2_ref, y_ref,
                inv_smem, xbuf, sem_inv, sem_x):
    i = pl.program_id(0)
    nu = nu_ref[0]
    xrows = MOE_BLK * XROW_SUB

    def inv_copy(blk, slot):
        return pltpu.make_async_copy(inv_hbm.at[blk], inv_smem.at[slot], sem_inv.at[slot])

    def gather(slot):
        for j in range(MOE_BLK):
            src = pl.multiple_of(inv_smem[slot, j // LANE, j % LANE], XROW_SUB)
            pltpu.make_async_copy(xp_hbm.at[pl.ds(src, XROW_SUB), :],
                                  xbuf.at[slot, pl.ds(j * XROW_SUB, XROW_SUB), :],
                                  sem_x.at[slot]).start(priority=j % 2)

    def gather_wait(slot):
        pltpu.make_async_copy(xp_hbm.at[pl.ds(0, xrows), :], xbuf.at[slot], sem_x.at[slot]).wait()

    @pl.when(i == 0)
    def _():
        cp = inv_copy(0, 0)
        cp.start()
        cp.wait()
        gather(0)
        inv_copy(jnp.minimum(1, nu - 1), 1).start()

    @pl.when(i < nu)
    def _():
        cur = i & 1
        nxt = 1 - cur
        gather_wait(cur)
        inv_copy(0, nxt).wait()
        x = _unpack_rows(xbuf.at[cur], MOE_BLK)
        h1 = jnp.dot(x, w1_ref[0, 0].astype(BF16), preferred_element_type=F32)
        h3 = jnp.dot(x, w3_ref[0, 0].astype(BF16), preferred_element_type=F32)
        hh = (h1 * jax.nn.sigmoid(h1) * h3).astype(BF16)
        y = jnp.dot(hh, w2_ref[0, 0].astype(BF16), preferred_element_type=F32)
        _store_rows(y_ref, y)
        gather(nxt)
        inv_copy(jnp.minimum(i + 2, nu - 1), cur).start()

        @pl.when(i == nu - 1)
        def _():
            gather_wait(nxt)
            inv_copy(0, cur).wait()

    @pl.when(i >= nu)
    def _():
        y_ref[...] = jnp.zeros(y_ref.shape, y_ref.dtype)


def _ffn_call(block_expert, n_used, inv, xp, w1, w3, w2, l):
    rows_per_blk = MOE_BLK // LANE
    NB = inv.shape[0] // rows_per_blk
    inv3 = inv.reshape(NB, rows_per_blk, LANE)
    _, _, D, DE = w1.shape

    def blk(i, be, nu):
        return jnp.minimum(i, nu[0] - 1)

    grid_spec = pltpu.PrefetchScalarGridSpec(
        num_scalar_prefetch=2,
        grid=(NB,),
        in_specs=[
            pl.BlockSpec(memory_space=pl.ANY),
            pl.BlockSpec(memory_space=pl.ANY),
            pl.BlockSpec((1, 1, D, DE), lambda i, be, nu: (l, be[blk(i, be, nu)], 0, 0)),
            pl.BlockSpec((1, 1, D, DE), lambda i, be, nu: (l, be[blk(i, be, nu)], 0, 0)),
            pl.BlockSpec((1, 1, DE, D), lambda i, be, nu: (l, be[blk(i, be, nu)], 0, 0)),
        ],
        out_specs=pl.BlockSpec((MOE_BLK * ROW_SUB, LANE), lambda i, be, nu: (i, 0)),
        scratch_shapes=[
            pltpu.SMEM((2, rows_per_blk, LANE), I32),
            pltpu.VMEM((2, MOE_BLK * XROW_SUB, LANE), U32),
            pltpu.SemaphoreType.DMA((2,)),
            pltpu.SemaphoreType.DMA((2,)),
        ],
    )
    return pl.pallas_call(
        _ffn_kernel,
        grid_spec=grid_spec,
        out_shape=jax.ShapeDtypeStruct((NB * MOE_BLK * ROW_SUB, LANE), F32),
        compiler_params=_cparams(("arbitrary",)),
        name="moe_ffn",
    )(block_expert, n_used, inv3, xp, w1, w3, w2)


def _combine_kernel(idx_hbm, h1_ref, route_ref, mod_ref, gf_ref, y_hbm, o_ref, idx_smem, rows0, rows1, sem_i, sem_g,
                    *, final):
    D = h1_ref.shape[1]
    i = pl.program_id(0)
    cp = pltpu.make_async_copy(idx_hbm.at[i], idx_smem, sem_i)
    cp.start()
    cp.wait()

    def issue(j, jo, u):
        d0, d1 = _tile_indices(idx_smem, jo, u)
        dst = pl.ds(pl.multiple_of(j * ROW_SUB, ROW_SUB), ROW_SUB)
        pltpu.make_async_copy(y_hbm.at[pl.ds(d0, ROW_SUB), :], rows0.at[dst, :], sem_g).start(priority=0)
        pltpu.make_async_copy(y_hbm.at[pl.ds(d1, ROW_SUB), :], rows1.at[dst, :], sem_g).start(priority=1)

    _row_loop(DT, issue)
    for rows in (rows0, rows1):
        pltpu.make_async_copy(y_hbm.at[pl.ds(0, DT * ROW_SUB), :], rows, sem_g).wait()

    route = route_ref[...]
    gate1 = route[:, 2:3]
    gate2 = route[:, 3:4]
    y = gate1 * _load_rows(rows0, DT) + gate2 * _load_rows(rows1, DT)
    g2 = mod_ref[0][:, 5 * D:6 * D]
    out = h1_ref[...] + g2 * y
    if final:
        msq = jnp.mean(out * out, axis=-1, keepdims=True)
        out = out * lax.rsqrt(msq + EPS) * gf_ref[...]
    o_ref[...] = out


def _combine_call(idx, h1, route, mod3, g_final, yb, n_lat_dt, dt_per_sample, n_mod_ctx, final):
    N, D = h1.shape
    n_tiles = N // DT

    def mod_map(i):
        return (jnp.where(i < n_lat_dt, i // dt_per_sample, n_mod_ctx), 0, 0)

    return pl.pallas_call(
        functools.partial(_combine_kernel, final=final),
        grid=(n_tiles,),
        in_specs=[
            pl.BlockSpec(memory_space=pl.ANY),
            pl.BlockSpec((DT, D), lambda i: (i, 0)),
            pl.BlockSpec((DT, LANE), lambda i: (i, 0)),
            pl.BlockSpec((1, 1, 6 * D), mod_map),
            pl.BlockSpec((1, D), lambda i: (0, 0)),
            pl.BlockSpec(memory_space=pl.ANY),
        ],
        out_specs=pl.BlockSpec((DT, D), lambda i: (i, 0)),
        out_shape=jax.ShapeDtypeStruct((N, D), F32),
        scratch_shapes=[
            pltpu.SMEM((2 * DT // LANE, LANE), I32),
            pltpu.VMEM((DT * ROW_SUB, LANE), F32),
            pltpu.VMEM((DT * ROW_SUB, LANE), F32),
            pltpu.SemaphoreType.DMA,
            pltpu.SemaphoreType.DMA,
        ],
        compiler_params=_cparams(("arbitrary",)),
        name="moe_combine",
    )(idx, h1, route, mod3, g_final, yb)


def _layer_weights(l, w_in, g_norm1, ln_v_g, ln_v_b, w_sp, b_sp, w_gate_up, b_gate, g_gla, w_out, g_norm2,
                   w_router_g, b_router_g, w_router_e, b_router_e):
    D = w_in.shape[1]
    d_in = w_in.shape[2]
    pad = (-d_in) % LANE
    w_in_p = jnp.pad(w_in[l], ((0, 0), (0, pad))).astype(BF16)
    KW = B_KEY_WIDTH
    wg = jnp.zeros((LANE, 2 * KW), F32)
    wg = wg.at[0:GATE_RANK, 0:KW].set(w_gate_up[l, 0])
    wg = wg.at[GATE_RANK:2 * GATE_RANK, KW:2 * KW].set(w_gate_up[l, 1])
    bg = jnp.concatenate([b_gate[l, 0], b_gate[l, 1]])[None, :]
    bsp = jnp.repeat(b_sp[l].T, A_GROUP_DIM, axis=1)
    wr = jnp.zeros((D, LANE), F32)
    wr = wr.at[:, 0:N_EXPERTS].set(w_router_e[l])
    wr = wr.at[:, N_EXPERTS:N_EXPERTS + N_GROUPS].set(w_router_g[l])
    br = jnp.zeros((1, LANE), F32)
    br = br.at[0, 0:N_EXPERTS].set(b_router_e[l])
    br = br.at[0, N_EXPERTS:N_EXPERTS + N_GROUPS].set(b_router_g[l])
    r = jnp.arange(TT, dtype=I32)
    same = (r[:, None] // CHUNK_B) == (r[None, :] // CHUNK_B)
    tri_f = (same & (r[:, None] >= r[None, :])).astype(BF16)
    tri_b = (same & (r[:, None] <= r[None, :])).astype(BF16)
    l_strict = (r[:, None] > r[None, :]).astype(BF16)
    return dict(
        w_in=w_in_p, g_norm1=g_norm1[l][None, :], ln_g=ln_v_g[l][None, :], ln_b=ln_v_b[l][None, :],
        w_sp=w_sp[l].astype(BF16), b_sp=bsp, w_gate=wg.astype(BF16), b_gate=bg,
        g_gla=g_gla[l][None, :], w_out=w_out[l].astype(BF16), g_norm2=g_norm2[l][None, :],
        w_router=wr.astype(BF16), b_router=br, tri_f=tri_f, tri_b=tri_b, l_strict=l_strict)


def _segment_layout(counts_row, n_tokens):
    counts = counts_row[0, :N_EXPERTS].astype(I32)
    padded = (counts + MOE_BLK - 1) // MOE_BLK * MOE_BLK
    pad_ends = jnp.cumsum(padded)
    pad_starts = pad_ends - padded
    n_blocks = (n_tokens * 2 + N_EXPERTS * (MOE_BLK - 1)) // MOE_BLK + 1
    block_start = jnp.arange(n_blocks, dtype=I32) * MOE_BLK
    block_expert = jnp.sum((pad_ends[None, :] <= block_start[:, None]).astype(I32), axis=1)
    block_expert = jnp.minimum(block_expert, N_EXPERTS - 1)
    n_used = (pad_ends[-1] // MOE_BLK).astype(I32)[None]
    return pad_starts, pad_ends, block_expert, n_used, n_blocks * MOE_BLK


def kernel(x, c, ctx, c_ctx, w_mod, b_mod, g_norm1, w_in, ln_v_g, ln_v_b, w_sp, b_sp, w_gate_up, b_gate, g_gla,
           w_out, g_norm2, w_router_g, b_router_g, w_router_e, b_router_e, w1, w3, w2, g_final):
    Bn, S, D = x.shape
    Lc = ctx.shape[1]
    depth = w_mod.shape[0]
    assert S % DT == 0 and Lc % TT == 0 and (Bn * Lc) % DT == 0 and Lc == TT
    n_lat = Bn * S
    n_ctx = Bn * Lc
    n_lat_tiles = n_lat // TT
    tiles_per_sample = S // TT

    cc = jnp.concatenate([c, c_ctx[None, :]], axis=0)
    mod_all = _modulation(cc, w_mod, b_mod)
    gf = g_final[None, :]

    lat, cx, ctx_off = x.reshape(n_lat, D), ctx.reshape(n_ctx, D), 0
    for l in range(depth):
        last = l == depth - 1
        lw = _layer_weights(l, w_in, g_norm1, ln_v_g, ln_v_b, w_sp, b_sp, w_gate_up, b_gate, g_gla, w_out,
                            g_norm2, w_router_g, b_router_g, w_router_e, b_router_e)
        mod3 = mod_all[l].reshape(Bn + 1, 1, 6 * D)
        a_out, qe, oi, sr, kv, dec = _pre_call(lat, cx, ctx_off, n_ctx // TT, mod3, lw, n_lat_tiles,
                                               tiles_per_sample, Bn)
        states = _scan_call(kv, dec, Bn, n_lat_tiles, tiles_per_sample)
        n_tok = n_lat if last else n_lat + n_ctx
        h1, hn2_rows, route, route_t, counts = _post_call(lat, cx, ctx_off, mod3, (a_out, qe, oi, sr), states, lw,
                                                          n_tok // TT, n_lat_tiles, tiles_per_sample, Bn)
        pad_starts, pad_ends, block_expert, n_used, n_buf_rows = _segment_layout(counts, n_tok)
        idx, inv = _plan_call(route_t, pad_starts, n_buf_rows)
        yb = _ffn_call(block_expert, n_used, inv, hn2_rows, w1, w3, w2, l)
        h_all = _combine_call(idx, h1, route, mod3, gf, yb, n_lat // DT, S // DT, Bn, last)
        lat, cx, ctx_off = h_all, h_all, n_lat_tiles
    return h_all.reshape(Bn, S, D)
```

```python
import functools

import jax
import jax.numpy as jnp
from jax import lax
from jax.experimental import pallas as pl
from jax.experimental.pallas import tpu as pltpu

F32 = jnp.float32
BF16 = jnp.bfloat16
I32 = jnp.int32

EPS = 1e-6
LANE = 128
SUBLANE = 8

A_GROUPS = 4
A_GROUP_DIM = 128
A_WIDTH = A_GROUPS * A_GROUP_DIM
CHUNK_A = 128
B_HEADS = 4
B_DK = 64
B_DV = 128
B_KEY_WIDTH = B_HEADS * B_DK
B_WIDTH = B_HEADS * B_DV
GATE_RANK = 16
GATE_TAU = 16.0
CHUNK_B = 64
N_GROUPS = 8
EXPERTS_PER_GROUP = 8
N_EXPERTS = N_GROUPS * EXPERTS_PER_GROUP

TT = 256
CPT = TT // CHUNK_B
MOE_BLK = 256
DT = 512
ROW_SUB = 8
FFN_SLOTS = 3

VMEM_LIMIT = 48 * 1024 * 1024


def _cparams(sem):
    return pltpu.CompilerParams(dimension_semantics=sem, vmem_limit_bytes=VMEM_LIMIT)


def _load_rows(ref, n):
    return jnp.concatenate([ref[pl.ds(s, n, stride=ROW_SUB), :] for s in range(ROW_SUB)], axis=1)


def _store_rows(ref, val):
    n = val.shape[0]
    for s in range(ROW_SUB):
        ref[pl.ds(s, n, stride=ROW_SUB), :] = val[:, s * LANE:(s + 1) * LANE]


def _mod_kernel(c_ref, w_ref, b_ref, o_ref):
    c = c_ref[...]
    s = (c * jax.nn.sigmoid(c)).astype(BF16)
    o_ref[0] = jnp.dot(s, w_ref[0].astype(BF16), preferred_element_type=F32) + b_ref[0]


def _modulation(cc, w_mod, b_mod):
    L, D, D6 = w_mod.shape
    R = cc.shape[0]
    tn = 1536
    return pl.pallas_call(
        _mod_kernel,
        grid=(L, D6 // tn),
        in_specs=[
            pl.BlockSpec((R, D), lambda l, j: (0, 0)),
            pl.BlockSpec((1, D, tn), lambda l, j: (l, 0, j)),
            pl.BlockSpec((1, 1, tn), lambda l, j: (l, 0, j)),
        ],
        out_specs=pl.BlockSpec((1, R, tn), lambda l, j: (l, 0, j)),
        out_shape=jax.ShapeDtypeStruct((L, R, D6), F32),
        compiler_params=_cparams(("arbitrary", "arbitrary")),
        name="modulation",
    )(cc, w_mod, b_mod.reshape(L, 1, D6))


def _select_tile(lat_ref, ctx_ref, n_lat_tiles):
    return jnp.where(pl.program_id(0) < n_lat_tiles, lat_ref[...], ctx_ref[...])


def _stream_specs(D, n_lat_tiles, ctx_off):
    lat = pl.BlockSpec((TT, D), lambda i: (jnp.minimum(i, n_lat_tiles - 1), 0))
    ctx = pl.BlockSpec((TT, D), lambda i: (ctx_off + jnp.maximum(i - n_lat_tiles, 0), 0))
    return lat, ctx


def _pre_kernel(lat_ref, ctx_ref, mod_ref, g1_ref, win_ref, lng_ref, lnb_ref, wsp_ref, bsp_ref, wg_ref, bg_ref,
                trif_ref, trib_ref,
                a_ref, qe_ref, oi_ref, sr_ref, kv_ref, dec_ref, *, n_lat_tiles):
    D = lat_ref.shape[1]
    h = _select_tile(lat_ref, ctx_ref, n_lat_tiles)
    mod = mod_ref[0]
    sh1 = mod[:, 0:D]
    sc1 = mod[:, D:2 * D]
    ms = jnp.mean(h * h, axis=-1, keepdims=True)
    hn = h * lax.rsqrt(ms + EPS) * g1_ref[...]
    hn = hn * (1.0 + sc1) + sh1
    z = jnp.dot(hn.astype(BF16), win_ref[...], preferred_element_type=F32)
    o_av = A_WIDTH
    o_q = 2 * A_WIDTH
    o_r = o_q + B_KEY_WIDTH
    o_k = o_r + B_WIDTH
    o_v = o_k + B_KEY_WIDTH
    o_g = o_v + B_WIDTH
    zu = z[:, 0:o_av]
    zv = z[:, o_av:o_q]
    q = z[:, o_q:o_r] * (B_DK ** -0.5)
    zr = z[:, o_r:o_k]
    k = z[:, o_k:o_v]
    vv = z[:, o_v:o_g]
    zg = z[:, o_g:o_g + LANE]

    u = jax.nn.gelu(zu)
    v = jax.nn.gelu(zv)
    s_groups = []
    for g in range(A_GROUPS):
        sl = slice(g * A_GROUP_DIM, (g + 1) * A_GROUP_DIM)
        vg = v[:, sl]
        mu = jnp.mean(vg, axis=-1, keepdims=True)
        dv = vg - mu
        var = jnp.mean(dv * dv, axis=-1, keepdims=True)
        vn = (dv * lax.rsqrt(var + EPS)) * lng_ref[:, sl] + lnb_ref[:, sl]
        vnb = vn.astype(BF16)
        rows = []
        for c in range(TT // CHUNK_A):
            rows.append(jnp.dot(wsp_ref[g], vnb[c * CHUNK_A:(c + 1) * CHUNK_A, :],
                                preferred_element_type=F32) + bsp_ref[:, sl])
        s_groups.append(jnp.concatenate(rows, axis=0))
    a_ref[...] = (u * jnp.concatenate(s_groups, axis=1)).astype(BF16)

    sr_ref[...] = (zr * jax.nn.sigmoid(zr)).astype(BF16)
    lg = jnp.dot(zg.astype(BF16), wg_ref[...], preferred_element_type=F32) + bg_ref[...]
    la = (jnp.minimum(lg, 0.0) - jnp.log1p(jnp.exp(-jnp.abs(lg)))) * (1.0 / GATE_TAU)
    la_hi = la.astype(BF16)
    la_lo = (la - la_hi.astype(F32)).astype(BF16)
    KW = B_KEY_WIDTH
    bf = (jnp.dot(trif_ref[...], la_hi[:, :KW], preferred_element_type=F32)
          + jnp.dot(trif_ref[...], la_lo[:, :KW], preferred_element_type=F32))
    bb = (jnp.dot(trib_ref[...], la_hi[:, KW:], preferred_element_type=F32)
          + jnp.dot(trib_ref[...], la_lo[:, KW:], preferred_element_type=F32))
    blf = jnp.concatenate(
        [jnp.broadcast_to(bf[c * CHUNK_B + CHUNK_B - 1:(c + 1) * CHUNK_B, :], (CHUNK_B, KW)) for c in range(CPT)], axis=0)
    blb = jnp.concatenate(
        [jnp.broadcast_to(bb[c * CHUNK_B:c * CHUNK_B + 1, :], (CHUNK_B, KW)) for c in range(CPT)], axis=0)
    qe_f = q * jnp.exp(bf)
    qe_b = q * jnp.exp(bb)
    ke_f = (k * jnp.exp(-bf)).astype(BF16)
    ke_b = (k * jnp.exp(-bb)).astype(BF16)
    kd_f = k * jnp.exp(blf - bf)
    kd_b = k * jnp.exp(blb - bb)
    qe_ref[...] = jnp.concatenate([qe_f, qe_b], axis=1).astype(BF16)
    dec_ref[0, 0] = jnp.exp(blf).T
    dec_ref[0, 1] = jnp.exp(blb).T

    row = lax.broadcasted_iota(I32, (TT, TT), 0)
    col = lax.broadcasted_iota(I32, (TT, TT), 1)
    same = lax.shift_right_logical(row, 6) == lax.shift_right_logical(col, 6)
    m_f = same & (row >= col)
    m_b = same & (row <= col)
    lane_head = lax.shift_right_logical(lax.broadcasted_iota(I32, (1, KW), 1), 6)
    tok_chunk = lax.shift_right_logical(lax.broadcasted_iota(I32, (1, TT), 1), 6)
    vb = vv.astype(BF16)
    kdT_f = kd_f.T
    kdT_b = kd_b.T
    nt = (((1,), (1,)), ((), ()))
    oi_heads = []
    for hd in range(B_HEADS):
        hm = lane_head == hd
        qf = jnp.where(hm, qe_f, 0.0).astype(BF16)
        qb = jnp.where(hm, qe_b, 0.0).astype(BF16)
        att_f = lax.dot_general(qf, ke_f, nt, preferred_element_type=F32)
        att_b = lax.dot_general(qb, ke_b, nt, preferred_element_type=F32)
        att = jnp.where(m_f, att_f, 0.0) + jnp.where(m_b, att_b, 0.0)
        v_h = vb[:, hd * B_DV:(hd + 1) * B_DV]
        oi_heads.append(jnp.dot(att.astype(BF16), v_h, preferred_element_type=F32))
        parts = []
        for kdT in (kdT_f, kdT_b):
            kh = kdT[hd * B_DK:(hd + 1) * B_DK, :]
            for c in range(CPT):
                parts.append(jnp.where(tok_chunk == c, kh, 0.0))
        lhs = jnp.concatenate(parts, axis=0).astype(BF16)
        kv_ref[0, hd] = jnp.dot(lhs, v_h, preferred_element_type=F32)
    oi_ref[...] = jnp.concatenate(oi_heads, axis=1)


def _pre_call(lat, ctx, ctx_off, n_ctx_tiles, mod3, lw, n_lat_tiles, tiles_per_sample, n_mod_ctx):
    D = lat.shape[1]
    n_tiles = n_lat_tiles + n_ctx_tiles
    N = n_tiles * TT
    lat_spec, ctx_spec = _stream_specs(D, n_lat_tiles, ctx_off)
    DINP = lw["w_in"].shape[1]
    KVR = 2 * CPT * B_DK

    def mod_map(i):
        return (jnp.where(i < n_lat_tiles, i // tiles_per_sample, n_mod_ctx), 0, 0)

    const2 = lambda i: (0, 0)
    const3 = lambda i: (0, 0, 0)
    tile2 = lambda i: (i, 0)
    return pl.pallas_call(
        functools.partial(_pre_kernel, n_lat_tiles=n_lat_tiles),
        grid=(n_tiles,),
        in_specs=[
            lat_spec,
            ctx_spec,
            pl.BlockSpec((1, 1, 6 * D), mod_map),
            pl.BlockSpec((1, D), const2),
            pl.BlockSpec((D, DINP), const2),
            pl.BlockSpec((1, A_WIDTH), const2),
            pl.BlockSpec((1, A_WIDTH), const2),
            pl.BlockSpec((A_GROUPS, CHUNK_A, CHUNK_A), const3),
            pl.BlockSpec((CHUNK_A, A_WIDTH), const2),
            pl.BlockSpec((LANE, 2 * B_KEY_WIDTH), const2),
            pl.BlockSpec((1, 2 * B_KEY_WIDTH), const2),
            pl.BlockSpec((TT, TT), const2),
            pl.BlockSpec((TT, TT), const2),
        ],
        out_specs=[
            pl.BlockSpec((TT, A_WIDTH), tile2),
            pl.BlockSpec((TT, 2 * B_KEY_WIDTH), tile2),
            pl.BlockSpec((TT, B_WIDTH), tile2),
            pl.BlockSpec((TT, B_WIDTH), tile2),
            pl.BlockSpec((1, B_HEADS, KVR, B_DV), lambda i: (i, 0, 0, 0)),
            pl.BlockSpec((1, 2, B_KEY_WIDTH, TT), lambda i: (i, 0, 0, 0)),
        ],
        out_shape=[
            jax.ShapeDtypeStruct((N, A_WIDTH), BF16),
            jax.ShapeDtypeStruct((N, 2 * B_KEY_WIDTH), BF16),
            jax.ShapeDtypeStruct((N, B_WIDTH), F32),
            jax.ShapeDtypeStruct((N, B_WIDTH), BF16),
            jax.ShapeDtypeStruct((n_tiles, B_HEADS, KVR, B_DV), F32),
            jax.ShapeDtypeStruct((n_tiles, 2, B_KEY_WIDTH, TT), F32),
        ],
        compiler_params=_cparams(("arbitrary",)),
        name="mix_pre",
    )(lat, ctx, mod3, lw["g_norm1"], lw["w_in"], lw["ln_g"], lw["ln_b"], lw["w_sp"], lw["b_sp"],
      lw["w_gate"], lw["b_gate"], lw["tri_f"], lw["tri_b"])


def _scan_kernel(kvc_ref, kvl_ref, dcc_ref, dcl_ref, sc_ref, sl_ref):
    n_lat = kvl_ref.shape[0]
    fwd = [(kvc_ref, dcc_ref, sc_ref, 0, c) for c in range(CPT)]
    fwd += [(kvl_ref, dcl_ref, sl_ref, t, c) for t in range(n_lat) for c in range(CPT)]
    bwd = [(kvc_ref, dcc_ref, sc_ref, 0, c) for c in reversed(range(CPT))]
    bwd += [(kvl_ref, dcl_ref, sl_ref, t, c) for t in reversed(range(n_lat)) for c in reversed(range(CPT))]
    for hd in range(B_HEADS):
        for d, seq in enumerate((fwd, bwd)):
            s = jnp.zeros((B_DK, B_DV), F32)
            for kv_ref, dc_ref, out_ref, t, c in seq:
                r0 = (d * CPT + c) * B_DK
                out_ref[t, hd, r0:r0 + B_DK, :] = s.astype(BF16)
                dcol = dc_ref[t, d, hd * B_DK:(hd + 1) * B_DK, c * CHUNK_B:c * CHUNK_B + 1]
                s = dcol * s + kv_ref[t, hd, r0:r0 + B_DK, :]


def _scan_call(kv, dec, n_samples, n_lat_tiles, tiles_per_sample):
    n_tiles, _, KVR, _ = kv.shape
    kv_c = pl.BlockSpec((1, B_HEADS, KVR, B_DV), lambda b: (n_lat_tiles + b, 0, 0, 0))
    kv_l = pl.BlockSpec((tiles_per_sample, B_HEADS, KVR, B_DV), lambda b: (b, 0, 0, 0))
    dc_c = pl.BlockSpec((1, 2, B_KEY_WIDTH, TT), lambda b: (n_lat_tiles + b, 0, 0, 0))
    dc_l = pl.BlockSpec((tiles_per_sample, 2, B_KEY_WIDTH, TT), lambda b: (b, 0, 0, 0))
    s_ctx, s_lat = pl.pallas_call(
        _scan_kernel,
        grid=(n_samples,),
        in_specs=[kv_c, kv_l, dc_c, dc_l],
        out_specs=[
            pl.BlockSpec((1, B_HEADS, KVR, B_DV), lambda b: (b, 0, 0, 0)),
            pl.BlockSpec((tiles_per_sample, B_HEADS, KVR, B_DV), lambda b: (b, 0, 0, 0)),
        ],
        out_shape=[
            jax.ShapeDtypeStruct((n_samples, B_HEADS, KVR, B_DV), BF16),
            jax.ShapeDtypeStruct((n_lat_tiles, B_HEADS, KVR, B_DV), BF16),
        ],
        compiler_params=_cparams(("arbitrary",)),
        name="gla_scan",
    )(kv, kv, dec, dec)
    return jnp.concatenate([s_lat, s_ctx], axis=0)


def _post_kernel(lat_ref, ctx_ref, mod_ref, a_ref, qe_ref, oi_ref, sr_ref, st_ref, ggla_ref, wout_ref, g2_ref,
                 wr_ref, br_ref, lstr_ref,
                 h1_ref, hn2_ref, route_ref, routet_ref, cnt_ref,
                 rhs_scr, cnt_scr, *, n_lat_tiles):
    D = lat_ref.shape[1]
    i = pl.program_id(0)

    @pl.when(i == 0)
    def _():
        rhs_scr[...] = jnp.zeros(rhs_scr.shape, rhs_scr.dtype)
        cnt_scr[...] = jnp.zeros(cnt_scr.shape, cnt_scr.dtype)

    qe = qe_ref[...]
    o_parts = []
    for c in range(CPT):
        for d in range(2):
            for hd in range(B_HEADS):
                r0 = d * B_KEY_WIDTH + hd * B_DK
                s0 = (d * CPT + c) * B_DK
                rhs_scr[c, r0:r0 + B_DK, hd * B_DV:(hd + 1) * B_DV] = st_ref[0, hd, s0:s0 + B_DK, :]
        o_parts.append(jnp.dot(qe[c * CHUNK_B:(c + 1) * CHUNK_B, :], rhs_scr[c], preferred_element_type=F32))
    o = oi_ref[...] + jnp.concatenate(o_parts, axis=0)

    heads = []
    for hd in range(B_HEADS):
        sl = slice(hd * B_DV, (hd + 1) * B_DV)
        oh = o[:, sl]
        msq = jnp.mean(oh * oh, axis=-1, keepdims=True)
        heads.append(oh * lax.rsqrt(msq + EPS) * ggla_ref[:, sl])
    b_out = jnp.concatenate(heads, axis=1) * sr_ref[...].astype(F32)
    mix_in = jnp.concatenate([a_ref[...], b_out.astype(BF16)], axis=1)
    mix = jnp.dot(mix_in, wout_ref[...], preferred_element_type=F32)

    mod = mod_ref[0]
    g1 = mod[:, 2 * D:3 * D]
    sh2 = mod[:, 3 * D:4 * D]
    sc2 = mod[:, 4 * D:5 * D]
    h1 = _select_tile(lat_ref, ctx_ref, n_lat_tiles) + g1 * mix
    h1_ref[...] = h1
    msq = jnp.mean(h1 * h1, axis=-1, keepdims=True)
    hn2 = h1 * lax.rsqrt(msq + EPS) * g2_ref[...]
    hn2 = hn2 * (1.0 + sc2) + sh2
    _store_rows(hn2_ref, hn2)

    lg = jnp.dot(hn2.astype(BF16), wr_ref[...], preferred_element_type=F32) + br_ref[...]
    lane = lax.broadcasted_iota(I32, lg.shape, 1)
    lanef = lane.astype(F32)
    neg = jnp.float32(-3.0e38)
    big = jnp.float32(1.0e9)
    is_g = (lane >= N_EXPERTS) & (lane < N_EXPERTS + N_GROUPS)
    gl = jnp.where(is_g, lg, neg)
    gmax = jnp.max(gl, axis=1, keepdims=True)
    gsum = jnp.sum(jnp.where(is_g, jnp.exp(gl - gmax), 0.0), axis=1, keepdims=True)
    p_top = 1.0 / gsum
    gidx = jnp.min(jnp.where(is_g & (gl == gmax), lanef, big), axis=1, keepdims=True) - float(N_EXPERTS)
    in_g = (lane < N_EXPERTS) & (lax.shift_right_logical(lane, 3).astype(F32) == gidx)
    el = jnp.where(in_g, lg, neg)
    m1 = jnp.max(el, axis=1, keepdims=True)
    i1 = jnp.min(jnp.where(in_g & (el == m1), lanef, big), axis=1, keepdims=True)
    el2 = jnp.where(lanef == i1, neg, el)
    m2 = jnp.max(el2, axis=1, keepdims=True)
    i2 = jnp.min(jnp.where(in_g & (el2 == m2), lanef, big), axis=1, keepdims=True)
    t = jnp.exp(m2 - m1)
    gate1 = p_top / (1.0 + t)
    gate2 = p_top * t / (1.0 + t)

    sel1 = lanef == i1
    sel2 = lanef == (i2 + float(N_EXPERTS))
    oh = jnp.where(sel1, 1.0, 0.0) + jnp.where(sel2, 1.0, 0.0)
    prefix = jnp.dot(lstr_ref[...], oh.astype(BF16), preferred_element_type=F32)
    tot = jnp.sum(oh, axis=0, keepdims=True)
    tot_sw = pltpu.roll(tot, N_EXPERTS, axis=1)
    base = cnt_scr[...]
    val = prefix + base + jnp.where(lane[0:1, :] >= N_EXPERTS, tot_sw, 0.0)
    r1 = jnp.sum(jnp.where(sel1, val, 0.0), axis=1, keepdims=True)
    r2 = jnp.sum(jnp.where(sel2, val, 0.0), axis=1, keepdims=True)
    new_cnt = base + tot + tot_sw
    cnt_scr[...] = new_cnt
    cnt_ref[...] = new_cnt
    route = (jnp.where(lane == 0, i1, 0.0) + jnp.where(lane == 1, i2, 0.0)
             + jnp.where(lane == 2, gate1, 0.0) + jnp.where(lane == 3, gate2, 0.0)
             + jnp.where(lane == 4, r1, 0.0) + jnp.where(lane == 5, r2, 0.0))
    route_ref[...] = route
    routet_ref[...] = route.T[0:SUBLANE, :]


def _post_call(lat, ctx, ctx_off, mod3, pre_outs, states, lw, n_post_tiles, n_lat_tiles, tiles_per_sample, n_mod_ctx):
    a_out, qe, oi, sr = pre_outs
    D = lat.shape[1]
    KVR = 2 * CPT * B_DK
    NP = n_post_tiles * TT
    lat_spec, ctx_spec = _stream_specs(D, n_lat_tiles, ctx_off)

    def mod_map(i):
        return (jnp.where(i < n_lat_tiles, i // tiles_per_sample, n_mod_ctx), 0, 0)

    const2 = lambda i: (0, 0)
    tile2 = lambda i: (i, 0)
    return pl.pallas_call(
        functools.partial(_post_kernel, n_lat_tiles=n_lat_tiles),
        grid=(n_post_tiles,),
        in_specs=[
            lat_spec,
            ctx_spec,
            pl.BlockSpec((1, 1, 6 * D), mod_map),
            pl.BlockSpec((TT, A_WIDTH), tile2),
            pl.BlockSpec((TT, 2 * B_KEY_WIDTH), tile2),
            pl.BlockSpec((TT, B_WIDTH), tile2),
            pl.BlockSpec((TT, B_WIDTH), tile2),
            pl.BlockSpec((1, B_HEADS, KVR, B_DV), lambda i: (i, 0, 0, 0)),
            pl.BlockSpec((1, B_WIDTH), const2),
            pl.BlockSpec((D, D), const2),
            pl.BlockSpec((1, D), const2),
            pl.BlockSpec((D, LANE), const2),
            pl.BlockSpec((1, LANE), const2),
            pl.BlockSpec((TT, TT), const2),
        ],
        out_specs=[
            pl.BlockSpec((TT, D), tile2),
            pl.BlockSpec((TT * ROW_SUB, LANE), tile2),
            pl.BlockSpec((TT, LANE), tile2),
            pl.BlockSpec((SUBLANE, TT), lambda i: (0, i)),
            pl.BlockSpec((1, LANE), const2),
        ],
        out_shape=[
            jax.ShapeDtypeStruct((NP, D), F32),
            jax.ShapeDtypeStruct((NP * ROW_SUB, LANE), F32),
            jax.ShapeDtypeStruct((NP, LANE), F32),
            jax.ShapeDtypeStruct((SUBLANE, NP), F32),
            jax.ShapeDtypeStruct((1, LANE), F32),
        ],
        scratch_shapes=[
            pltpu.VMEM((CPT, 2 * B_KEY_WIDTH, B_WIDTH), BF16),
            pltpu.VMEM((1, LANE), F32),
        ],
        compiler_params=_cparams(("arbitrary",)),
        name="mix_post",
    )(lat, ctx, mod3, a_out, qe, oi, sr, states, lw["g_gla"], lw["w_out"], lw["g_norm2"],
      lw["w_router"], lw["b_router"], lw["l_strict"])


def _plan_kernel(rt_ref, ps_ref, init_hbm, idx_ref, inv_hbm, idx_smem, inv_smem, sem):
    i = pl.program_id(0)

    @pl.when(i == 0)
    def _():
        cp0 = pltpu.make_async_copy(init_hbm, inv_smem, sem)
        cp0.start()
        cp0.wait()

    sub = lax.broadcasted_iota(I32, (N_EXPERTS, LANE), 0).astype(F32)
    ps = ps_ref[...]
    segs = DT // LANE
    for k in range(2):
        for seg in range(segs):
            sl = slice(seg * LANE, (seg + 1) * LANE)
            e = rt_ref[k:k + 1, sl]
            r = rt_ref[4 + k:5 + k, sl]
            base = jnp.sum(jnp.where(sub == e, ps, 0.0), axis=0, keepdims=True)
            idx_ref[0, k * segs + seg:k * segs + seg + 1, :] = ((base + r) * float(ROW_SUB)).astype(I32)

    cp = pltpu.make_async_copy(idx_ref.at[0], idx_smem, sem)
    cp.start()
    cp.wait()
    row_shift = ROW_SUB.bit_length() - 1

    batch = 16
    src0 = i * (DT * ROW_SUB)
    for k in range(2):
        for j0 in range(0, DT, batch):
            dests = [lax.shift_right_logical(idx_smem[k * segs + j // LANE, j % LANE], row_shift)
                     for j in range(j0, j0 + batch)]
            for n, d in enumerate(dests):
                inv_smem[d] = src0 + (j0 + n) * ROW_SUB

    @pl.when(i == pl.num_programs(0) - 1)
    def _():
        cp1 = pltpu.make_async_copy(inv_smem, inv_hbm, sem)
        cp1.start()
        cp1.wait()


def _plan_call(route_t, pad_starts, n_buf_rows):
    n_dt = route_t.shape[1] // DT
    init = jnp.zeros((n_buf_rows,), I32)
    return pl.pallas_call(
        _plan_kernel,
        grid=(n_dt,),
        in_specs=[
            pl.BlockSpec((SUBLANE, DT), lambda i: (0, i)),
            pl.BlockSpec((N_EXPERTS, 1), lambda i: (0, 0)),
            pl.BlockSpec(memory_space=pl.ANY),
        ],
        out_specs=[
            pl.BlockSpec((1, 2 * DT // LANE, LANE), lambda i: (i, 0, 0)),
            pl.BlockSpec(memory_space=pl.ANY),
        ],
        out_shape=[
            jax.ShapeDtypeStruct((n_dt, 2 * DT // LANE, LANE), I32),
            jax.ShapeDtypeStruct((n_buf_rows,), I32),
        ],
        scratch_shapes=[
            pltpu.SMEM((2 * DT // LANE, LANE), I32),
            pltpu.SMEM((n_buf_rows,), I32),
            pltpu.SemaphoreType.DMA,
        ],
        compiler_params=_cparams(("arbitrary",)),
        name="moe_plan",
    )(route_t, pad_starts.astype(F32)[:, None], init)


def _row_loop(n_rows, fn):
    unroll = 8

    def body(jo, carry):
        for u in range(unroll):
            fn(jo * unroll + u, jo, u)
        return carry

    lax.fori_loop(0, n_rows // unroll, body, 0)


def _tile_indices(idx_smem, jo, u):
    r = lax.shift_right_logical(jo, 4)
    c = (jo & (LANE // 8 - 1)) * 8 + u
    d0 = pl.multiple_of(idx_smem[r, c], ROW_SUB)
    d1 = pl.multiple_of(idx_smem[r + DT // LANE, c], ROW_SUB)
    return d0, d1


def _ffn_kernel(be_ref, nu_ref, inv_hbm, x_hbm, w1_ref, w3_ref, w2_ref, y_ref,
                inv_smem, xbuf, sem_inv, sem_x):
    i = pl.program_id(0)
    nu = nu_ref[0]
    last = nu - 1
    xrows = MOE_BLK * ROW_SUB

    def inv_copy(blk, slot):
        return pltpu.make_async_copy(inv_hbm.at[blk], inv_smem.at[slot], sem_inv.at[slot])

    def gather(slot):
        for j in range(MOE_BLK):
            src = pl.multiple_of(inv_smem[slot, j // LANE, j % LANE], ROW_SUB)
            pltpu.make_async_copy(x_hbm.at[pl.ds(src, ROW_SUB), :],
                                  xbuf.at[slot, pl.ds(j * ROW_SUB, ROW_SUB), :],
                                  sem_x.at[slot]).start(priority=j % 2)

    def gather_wait(slot):
        pltpu.make_async_copy(x_hbm.at[pl.ds(0, xrows), :], xbuf.at[slot], sem_x.at[slot]).wait()

    @pl.when(i == 0)
    def _():
        for b in range(FFN_SLOTS - 1):
            cp = inv_copy(jnp.minimum(b, last), b)
            cp.start()
            cp.wait()
            gather(b)
        inv_copy(jnp.minimum(FFN_SLOTS - 1, last), FFN_SLOTS - 1).start()

    @pl.when(i < nu)
    def _():
        cur = lax.rem(i, FFN_SLOTS)
        ahead = lax.rem(i + FFN_SLOTS - 1, FFN_SLOTS)
        gather_wait(cur)
        inv_copy(0, ahead).wait()
        x = _load_rows(xbuf.at[cur], MOE_BLK).astype(BF16)
        h1 = jnp.dot(x, w1_ref[0, 0].astype(BF16), preferred_element_type=F32)
        h3 = jnp.dot(x, w3_ref[0, 0].astype(BF16), preferred_element_type=F32)
        hh = (h1 * jax.nn.sigmoid(h1) * h3).astype(BF16)
        y = jnp.dot(hh, w2_ref[0, 0].astype(BF16), preferred_element_type=F32)
        _store_rows(y_ref, y)
        gather(ahead)
        inv_copy(jnp.minimum(i + FFN_SLOTS, last), cur).start()

        @pl.when(i == last)
        def _():
            for d in range(1, FFN_SLOTS):
                gather_wait(lax.rem(i + d, FFN_SLOTS))
            inv_copy(0, cur).wait()

    @pl.when(i >= nu)
    def _():
        y_ref[...] = jnp.zeros(y_ref.shape, y_ref.dtype)


def _ffn_call(block_expert, n_used, inv, xp, w1, w3, w2, l):
    rows_per_blk = MOE_BLK // LANE
    NB = inv.shape[0] // MOE_BLK
    inv3 = inv.reshape(NB, rows_per_blk, LANE)
    _, _, D, DE = w1.shape

    def blk(i, be, nu):
        return jnp.minimum(i, nu[0] - 1)

    grid_spec = pltpu.PrefetchScalarGridSpec(
        num_scalar_prefetch=2,
        grid=(NB,),
        in_specs=[
            pl.BlockSpec(memory_space=pl.ANY),
            pl.BlockSpec(memory_space=pl.ANY),
            pl.BlockSpec((1, 1, D, DE), lambda i, be, nu: (l, be[blk(i, be, nu)], 0, 0)),
            pl.BlockSpec((1, 1, D, DE), lambda i, be, nu: (l, be[blk(i, be, nu)], 0, 0)),
            pl.BlockSpec((1, 1, DE, D), lambda i, be, nu: (l, be[blk(i, be, nu)], 0, 0)),
        ],
        out_specs=pl.BlockSpec((MOE_BLK * ROW_SUB, LANE), lambda i, be, nu: (i, 0)),
        scratch_shapes=[
            pltpu.SMEM((FFN_SLOTS, rows_per_blk, LANE), I32),
            pltpu.VMEM((FFN_SLOTS, MOE_BLK * ROW_SUB, LANE), F32),
            pltpu.SemaphoreType.DMA((FFN_SLOTS,)),
            pltpu.SemaphoreType.DMA((FFN_SLOTS,)),
        ],
    )
    return pl.pallas_call(
        _ffn_kernel,
        grid_spec=grid_spec,
        out_shape=jax.ShapeDtypeStruct((NB * MOE_BLK * ROW_SUB, LANE), F32),
        compiler_params=_cparams(("arbitrary",)),
        name="moe_ffn",
    )(block_expert, n_used, inv3, xp, w1, w3, w2)


def _combine_kernel(idx_hbm, h1_ref, route_ref, mod_ref, gf_ref, y_hbm, o_ref, idx_smem, rows0, rows1, sem_i, sem_g,
                    *, final):
    D = h1_ref.shape[1]
    i = pl.program_id(0)
    cp = pltpu.make_async_copy(idx_hbm.at[i], idx_smem, sem_i)
    cp.start()
    cp.wait()

    def issue(j, jo, u):
        d0, d1 = _tile_indices(idx_smem, jo, u)
        dst = pl.ds(pl.multiple_of(j * ROW_SUB, ROW_SUB), ROW_SUB)
        pltpu.make_async_copy(y_hbm.at[pl.ds(d0, ROW_SUB), :], rows0.at[dst, :], sem_g).start(priority=0)
        pltpu.make_async_copy(y_hbm.at[pl.ds(d1, ROW_SUB), :], rows1.at[dst, :], sem_g).start(priority=1)

    _row_loop(DT, issue)
    for rows in (rows0, rows1):
        pltpu.make_async_copy(y_hbm.at[pl.ds(0, DT * ROW_SUB), :], rows, sem_g).wait()

    route = route_ref[...]
    gate1 = route[:, 2:3]
    gate2 = route[:, 3:4]
    y = gate1 * _load_rows(rows0, DT) + gate2 * _load_rows(rows1, DT)
    g2 = mod_ref[0][:, 5 * D:6 * D]
    out = h1_ref[...] + g2 * y
    if final:
        msq = jnp.mean(out * out, axis=-1, keepdims=True)
        out = out * lax.rsqrt(msq + EPS) * gf_ref[...]
    o_ref[...] = out


def _combine_call(idx, h1, route, mod3, g_final, yb, n_lat_dt, dt_per_sample, n_mod_ctx, final):
    N, D = h1.shape
    n_tiles = N // DT

    def mod_map(i):
        return (jnp.where(i < n_lat_dt, i // dt_per_sample, n_mod_ctx), 0, 0)

    return pl.pallas_call(
        functools.partial(_combine_kernel, final=final),
        grid=(n_tiles,),
        in_specs=[
            pl.BlockSpec(memory_space=pl.ANY),
            pl.BlockSpec((DT, D), lambda i: (i, 0)),
            pl.BlockSpec((DT, LANE), lambda i: (i, 0)),
            pl.BlockSpec((1, 1, 6 * D), mod_map),
            pl.BlockSpec((1, D), lambda i: (0, 0)),
            pl.BlockSpec(memory_space=pl.ANY),
        ],
        out_specs=pl.BlockSpec((DT, D), lambda i: (i, 0)),
        out_shape=jax.ShapeDtypeStruct((N, D), F32),
        scratch_shapes=[
            pltpu.SMEM((2 * DT // LANE, LANE), I32),
            pltpu.VMEM((DT * ROW_SUB, LANE), F32),
            pltpu.VMEM((DT * ROW_SUB, LANE), F32),
            pltpu.SemaphoreType.DMA,
            pltpu.SemaphoreType.DMA,
        ],
        compiler_params=_cparams(("arbitrary",)),
        name="moe_combine",
    )(idx, h1, route, mod3, g_final, yb)


def _layer_weights(l, w_in, g_norm1, ln_v_g, ln_v_b, w_sp, b_sp, w_gate_up, b_gate, g_gla, w_out, g_norm2,
                   w_router_g, b_router_g, w_router_e, b_router_e):
    D = w_in.shape[1]
    d_in = w_in.shape[2]
    pad = (-d_in) % LANE
    w_in_p = jnp.pad(w_in[l], ((0, 0), (0, pad))).astype(BF16)
    KW = B_KEY_WIDTH
    wg = jnp.zeros((LANE, 2 * KW), F32)
    wg = wg.at[0:GATE_RANK, 0:KW].set(w_gate_up[l, 0])
    wg = wg.at[GATE_RANK:2 * GATE_RANK, KW:2 * KW].set(w_gate_up[l, 1])
    bg = jnp.concatenate([b_gate[l, 0], b_gate[l, 1]])[None, :]
    bsp = jnp.repeat(b_sp[l].T, A_GROUP_DIM, axis=1)
    wr = jnp.zeros((D, LANE), F32)
    wr = wr.at[:, 0:N_EXPERTS].set(w_router_e[l])
    wr = wr.at[:, N_EXPERTS:N_EXPERTS + N_GROUPS].set(w_router_g[l])
    br = jnp.zeros((1, LANE), F32)
    br = br.at[0, 0:N_EXPERTS].set(b_router_e[l])
    br = br.at[0, N_EXPERTS:N_EXPERTS + N_GROUPS].set(b_router_g[l])
    r = jnp.arange(TT, dtype=I32)
    same = (r[:, None] // CHUNK_B) == (r[None, :] // CHUNK_B)
    tri_f = (same & (r[:, None] >= r[None, :])).astype(BF16)
    tri_b = (same & (r[:, None] <= r[None, :])).astype(BF16)
    l_strict = (r[:, None] > r[None, :]).astype(BF16)
    return dict(
        w_in=w_in_p, g_norm1=g_norm1[l][None, :], ln_g=ln_v_g[l][None, :], ln_b=ln_v_b[l][None, :],
        w_sp=w_sp[l].astype(BF16), b_sp=bsp, w_gate=wg.astype(BF16), b_gate=bg,
        g_gla=g_gla[l][None, :], w_out=w_out[l].astype(BF16), g_norm2=g_norm2[l][None, :],
        w_router=wr.astype(BF16), b_router=br, tri_f=tri_f, tri_b=tri_b, l_strict=l_strict)


def _segment_layout(counts_row, n_tokens):
    counts = counts_row[0, :N_EXPERTS].astype(I32)
    padded = (counts + MOE_BLK - 1) // MOE_BLK * MOE_BLK
    pad_ends = jnp.cumsum(padded)
    pad_starts = pad_ends - padded
    n_blocks = (n_tokens * 2 + N_EXPERTS * (MOE_BLK - 1)) // MOE_BLK + 1
    block_start = jnp.arange(n_blocks, dtype=I32) * MOE_BLK
    block_expert = jnp.sum((pad_ends[None, :] <= block_start[:, None]).astype(I32), axis=1)
    block_expert = jnp.minimum(block_expert, N_EXPERTS - 1)
    n_used = (pad_ends[-1] // MOE_BLK).astype(I32)[None]
    return pad_starts, pad_ends, block_expert, n_used, n_blocks * MOE_BLK


def kernel(x, c, ctx, c_ctx, w_mod, b_mod, g_norm1, w_in, ln_v_g, ln_v_b, w_sp, b_sp, w_gate_up, b_gate, g_gla,
           w_out, g_norm2, w_router_g, b_router_g, w_router_e, b_router_e, w1, w3, w2, g_final):
    Bn, S, D = x.shape
    Lc = ctx.shape[1]
    depth = w_mod.shape[0]
    assert S % DT == 0 and Lc % TT == 0 and (Bn * Lc) % DT == 0 and Lc == TT
    n_lat = Bn * S
    n_ctx = Bn * Lc
    n_lat_tiles = n_lat // TT
    tiles_per_sample = S // TT

    cc = jnp.concatenate([c, c_ctx[None, :]], axis=0)
    mod_all = _modulation(cc, w_mod, b_mod)
    gf = g_final[None, :]

    lat, cx, ctx_off = x.reshape(n_lat, D), ctx.reshape(n_ctx, D), 0
    for l in range(depth):
        last = l == depth - 1
        lw = _layer_weights(l, w_in, g_norm1, ln_v_g, ln_v_b, w_sp, b_sp, w_gate_up, b_gate, g_gla, w_out,
                            g_norm2, w_router_g, b_router_g, w_router_e, b_router_e)
        mod3 = mod_all[l].reshape(Bn + 1, 1, 6 * D)
        a_out, qe, oi, sr, kv, dec = _pre_call(lat, cx, ctx_off, n_ctx // TT, mod3, lw, n_lat_tiles,
                                               tiles_per_sample, Bn)
        states = _scan_call(kv, dec, Bn, n_lat_tiles, tiles_per_sample)
        n_tok = n_lat if last else n_lat + n_ctx
        h1, hn2_rows, route, route_t, counts = _post_call(lat, cx, ctx_off, mod3, (a_out, qe, oi, sr), states, lw,
                                                          n_tok // TT, n_lat_tiles, tiles_per_sample, Bn)
        pad_starts, pad_ends, block_expert, n_used, n_buf_rows = _segment_layout(counts, n_tok)
        idx, inv = _plan_call(route_t, pad_starts, n_buf_rows)
        yb = _ffn_call(block_expert, n_used, inv, hn2_rows, w1, w3, w2, l)
        h_all = _combine_call(idx, h1, route, mod3, gf, yb, n_lat // DT, S // DT, Bn, last)
        lat, cx, ctx_off = h_all, h_all, n_lat_tiles
    return h_all.reshape(Bn, S, D)
```

```python
import functools

import jax
import jax.numpy as jnp
from jax import lax
from jax.experimental import pallas as pl
from jax.experimental.pallas import tpu as pltpu

F32 = jnp.float32
BF16 = jnp.bfloat16
I32 = jnp.int32

EPS = 1e-6
LANE = 128
SUBLANE = 8

A_GROUPS = 4
A_GROUP_DIM = 128
A_WIDTH = A_GROUPS * A_GROUP_DIM
CHUNK_A = 128
B_HEADS = 4
B_DK = 64
B_DV = 128
B_KEY_WIDTH = B_HEADS * B_DK
B_WIDTH = B_HEADS * B_DV
GATE_RANK = 16
GATE_TAU = 16.0
CHUNK_B = 64
N_GROUPS = 8
EXPERTS_PER_GROUP = 8
N_EXPERTS = N_GROUPS * EXPERTS_PER_GROUP

TT = 256
CPT = TT // CHUNK_B
MOE_BLK = 256
DT = 512
ROW_SUB = 8

VMEM_LIMIT = 48 * 1024 * 1024


def _cparams(sem):
    return pltpu.CompilerParams(dimension_semantics=sem, vmem_limit_bytes=VMEM_LIMIT)


def _load_rows(ref, n):
    return jnp.concatenate([ref[pl.ds(s, n, stride=ROW_SUB), :] for s in range(ROW_SUB)], axis=1)


def _store_rows(ref, val):
    n = val.shape[0]
    for s in range(ROW_SUB):
        ref[pl.ds(s, n, stride=ROW_SUB), :] = val[:, s * LANE:(s + 1) * LANE]


def _mod_kernel(c_ref, w_ref, b_ref, o_ref):
    c = c_ref[...]
    s = (c * jax.nn.sigmoid(c)).astype(BF16)
    o_ref[0] = jnp.dot(s, w_ref[0].astype(BF16), preferred_element_type=F32) + b_ref[0]


def _modulation(cc, w_mod, b_mod):
    L, D, D6 = w_mod.shape
    R = cc.shape[0]
    tn = 1536
    return pl.pallas_call(
        _mod_kernel,
        grid=(L, D6 // tn),
        in_specs=[
            pl.BlockSpec((R, D), lambda l, j: (0, 0)),
            pl.BlockSpec((1, D, tn), lambda l, j: (l, 0, j)),
            pl.BlockSpec((1, 1, tn), lambda l, j: (l, 0, j)),
        ],
        out_specs=pl.BlockSpec((1, R, tn), lambda l, j: (l, 0, j)),
        out_shape=jax.ShapeDtypeStruct((L, R, D6), F32),
        compiler_params=_cparams(("arbitrary", "arbitrary")),
        name="modulation",
    )(cc, w_mod, b_mod.reshape(L, 1, D6))


def _select_tile(lat_ref, ctx_ref, n_lat_tiles):
    return jnp.where(pl.program_id(0) < n_lat_tiles, lat_ref[...], ctx_ref[...])


def _stream_specs(D, n_lat_tiles, ctx_off):
    lat = pl.BlockSpec((TT, D), lambda i: (jnp.minimum(i, n_lat_tiles - 1), 0))
    ctx = pl.BlockSpec((TT, D), lambda i: (ctx_off + jnp.maximum(i - n_lat_tiles, 0), 0))
    return lat, ctx


def _pre_kernel(lat_ref, ctx_ref, mod_ref, g1_ref, win_ref, lng_ref, lnb_ref, wsp_ref, bsp_ref, wg_ref, bg_ref,
                trif_ref, trib_ref,
                a_ref, qe_ref, oi_ref, sr_ref, kv_ref, dec_ref, *, n_lat_tiles):
    D = lat_ref.shape[1]
    h = _select_tile(lat_ref, ctx_ref, n_lat_tiles)
    mod = mod_ref[0]
    sh1 = mod[:, 0:D]
    sc1 = mod[:, D:2 * D]
    ms = jnp.mean(h * h, axis=-1, keepdims=True)
    hn = h * lax.rsqrt(ms + EPS) * g1_ref[...]
    hn = hn * (1.0 + sc1) + sh1
    z = jnp.dot(hn.astype(BF16), win_ref[...], preferred_element_type=F32)
    o_av = A_WIDTH
    o_q = 2 * A_WIDTH
    o_r = o_q + B_KEY_WIDTH
    o_k = o_r + B_WIDTH
    o_v = o_k + B_KEY_WIDTH
    o_g = o_v + B_WIDTH
    zu = z[:, 0:o_av]
    zv = z[:, o_av:o_q]
    q = z[:, o_q:o_r] * (B_DK ** -0.5)
    zr = z[:, o_r:o_k]
    k = z[:, o_k:o_v]
    vv = z[:, o_v:o_g]
    zg = z[:, o_g:o_g + LANE]

    u = jax.nn.gelu(zu)
    v = jax.nn.gelu(zv)
    s_groups = []
    for g in range(A_GROUPS):
        sl = slice(g * A_GROUP_DIM, (g + 1) * A_GROUP_DIM)
        vg = v[:, sl]
        mu = jnp.mean(vg, axis=-1, keepdims=True)
        dv = vg - mu
        var = jnp.mean(dv * dv, axis=-1, keepdims=True)
        vn = (dv * lax.rsqrt(var + EPS)) * lng_ref[:, sl] + lnb_ref[:, sl]
        vnb = vn.astype(BF16)
        rows = []
        for c in range(TT // CHUNK_A):
            rows.append(jnp.dot(wsp_ref[g], vnb[c * CHUNK_A:(c + 1) * CHUNK_A, :],
                                preferred_element_type=F32) + bsp_ref[:, sl])
        s_groups.append(jnp.concatenate(rows, axis=0))
    a_ref[...] = (u * jnp.concatenate(s_groups, axis=1)).astype(BF16)

    sr_ref[...] = (zr * jax.nn.sigmoid(zr)).astype(BF16)
    lg = jnp.dot(zg.astype(BF16), wg_ref[...], preferred_element_type=F32) + bg_ref[...]
    la = (jnp.minimum(lg, 0.0) - jnp.log1p(jnp.exp(-jnp.abs(lg)))) * (1.0 / GATE_TAU)
    la_hi = la.astype(BF16)
    la_lo = (la - la_hi.astype(F32)).astype(BF16)
    KW = B_KEY_WIDTH
    bf = (jnp.dot(trif_ref[...], la_hi[:, :KW], preferred_element_type=F32)
          + jnp.dot(trif_ref[...], la_lo[:, :KW], preferred_element_type=F32))
    bb = (jnp.dot(trib_ref[...], la_hi[:, KW:], preferred_element_type=F32)
          + jnp.dot(trib_ref[...], la_lo[:, KW:], preferred_element_type=F32))
    blf = jnp.concatenate(
        [jnp.broadcast_to(bf[c * CHUNK_B + CHUNK_B - 1:(c + 1) * CHUNK_B, :], (CHUNK_B, KW)) for c in range(CPT)], axis=0)
    blb = jnp.concatenate(
        [jnp.broadcast_to(bb[c * CHUNK_B:c * CHUNK_B + 1, :], (CHUNK_B, KW)) for c in range(CPT)], axis=0)
    qe_f = q * jnp.exp(bf)
    qe_b = q * jnp.exp(bb)
    ke_f = (k * jnp.exp(-bf)).astype(BF16)
    ke_b = (k * jnp.exp(-bb)).astype(BF16)
    kd_f = k * jnp.exp(blf - bf)
    kd_b = k * jnp.exp(blb - bb)
    qe_ref[...] = jnp.concatenate([qe_f, qe_b], axis=1).astype(BF16)
    dec_ref[0, 0] = jnp.exp(blf).T
    dec_ref[0, 1] = jnp.exp(blb).T

    row = lax.broadcasted_iota(I32, (TT, TT), 0)
    col = lax.broadcasted_iota(I32, (TT, TT), 1)
    same = lax.shift_right_logical(row, 6) == lax.shift_right_logical(col, 6)
    m_f = same & (row >= col)
    m_b = same & (row <= col)
    lane_head = lax.shift_right_logical(lax.broadcasted_iota(I32, (1, KW), 1), 6)
    tok_chunk = lax.shift_right_logical(lax.broadcasted_iota(I32, (1, TT), 1), 6)
    vb = vv.astype(BF16)
    kdT_f = kd_f.T
    kdT_b = kd_b.T
    nt = (((1,), (1,)), ((), ()))
    oi_heads = []
    for hd in range(B_HEADS):
        hm = lane_head == hd
        qf = jnp.where(hm, qe_f, 0.0).astype(BF16)
        qb = jnp.where(hm, qe_b, 0.0).astype(BF16)
        att_f = lax.dot_general(qf, ke_f, nt, preferred_element_type=F32)
        att_b = lax.dot_general(qb, ke_b, nt, preferred_element_type=F32)
        att = jnp.where(m_f, att_f, 0.0) + jnp.where(m_b, att_b, 0.0)
        v_h = vb[:, hd * B_DV:(hd + 1) * B_DV]
        oi_heads.append(jnp.dot(att.astype(BF16), v_h, preferred_element_type=F32))
        parts = []
        for kdT in (kdT_f, kdT_b):
            kh = kdT[hd * B_DK:(hd + 1) * B_DK, :]
            for c in range(CPT):
                parts.append(jnp.where(tok_chunk == c, kh, 0.0))
        lhs = jnp.concatenate(parts, axis=0).astype(BF16)
        kv_ref[0, hd] = jnp.dot(lhs, v_h, preferred_element_type=F32)
    oi_ref[...] = jnp.concatenate(oi_heads, axis=1)


def _pre_call(lat, ctx, ctx_off, n_ctx_tiles, mod3, lw, n_lat_tiles, tiles_per_sample, n_mod_ctx):
    D = lat.shape[1]
    n_tiles = n_lat_tiles + n_ctx_tiles
    N = n_tiles * TT
    lat_spec, ctx_spec = _stream_specs(D, n_lat_tiles, ctx_off)
    DINP = lw["w_in"].shape[1]
    KVR = 2 * CPT * B_DK

    def mod_map(i):
        return (jnp.where(i < n_lat_tiles, i // tiles_per_sample, n_mod_ctx), 0, 0)

    const2 = lambda i: (0, 0)
    const3 = lambda i: (0, 0, 0)
    tile2 = lambda i: (i, 0)
    return pl.pallas_call(
        functools.partial(_pre_kernel, n_lat_tiles=n_lat_tiles),
        grid=(n_tiles,),
        in_specs=[
            lat_spec,
            ctx_spec,
            pl.BlockSpec((1, 1, 6 * D), mod_map),
            pl.BlockSpec((1, D), const2),
            pl.BlockSpec((D, DINP), const2),
            pl.BlockSpec((1, A_WIDTH), const2),
            pl.BlockSpec((1, A_WIDTH), const2),
            pl.BlockSpec((A_GROUPS, CHUNK_A, CHUNK_A), const3),
            pl.BlockSpec((CHUNK_A, A_WIDTH), const2),
            pl.BlockSpec((LANE, 2 * B_KEY_WIDTH), const2),
            pl.BlockSpec((1, 2 * B_KEY_WIDTH), const2),
            pl.BlockSpec((TT, TT), const2),
            pl.BlockSpec((TT, TT), const2),
        ],
        out_specs=[
            pl.BlockSpec((TT, A_WIDTH), tile2),
            pl.BlockSpec((TT, 2 * B_KEY_WIDTH), tile2),
            pl.BlockSpec((TT, B_WIDTH), tile2),
            pl.BlockSpec((TT, B_WIDTH), tile2),
            pl.BlockSpec((1, B_HEADS, KVR, B_DV), lambda i: (i, 0, 0, 0)),
            pl.BlockSpec((1, 2, B_KEY_WIDTH, TT), lambda i: (i, 0, 0, 0)),
        ],
        out_shape=[
            jax.ShapeDtypeStruct((N, A_WIDTH), BF16),
            jax.ShapeDtypeStruct((N, 2 * B_KEY_WIDTH), BF16),
            jax.ShapeDtypeStruct((N, B_WIDTH), F32),
            jax.ShapeDtypeStruct((N, B_WIDTH), BF16),
            jax.ShapeDtypeStruct((n_tiles, B_HEADS, KVR, B_DV), F32),
            jax.ShapeDtypeStruct((n_tiles, 2, B_KEY_WIDTH, TT), F32),
        ],
        compiler_params=_cparams(("arbitrary",)),
        name="mix_pre",
    )(lat, ctx, mod3, lw["g_norm1"], lw["w_in"], lw["ln_g"], lw["ln_b"], lw["w_sp"], lw["b_sp"],
      lw["w_gate"], lw["b_gate"], lw["tri_f"], lw["tri_b"])


def _scan_kernel(kvc_ref, kvl_ref, dcc_ref, dcl_ref, sc_ref, sl_ref):
    n_lat = kvl_ref.shape[0]
    fwd = [(kvc_ref, dcc_ref, sc_ref, 0, c) for c in range(CPT)]
    fwd += [(kvl_ref, dcl_ref, sl_ref, t, c) for t in range(n_lat) for c in range(CPT)]
    bwd = [(kvc_ref, dcc_ref, sc_ref, 0, c) for c in reversed(range(CPT))]
    bwd += [(kvl_ref, dcl_ref, sl_ref, t, c) for t in reversed(range(n_lat)) for c in reversed(range(CPT))]
    for hd in range(B_HEADS):
        for d, seq in enumerate((fwd, bwd)):
            s = jnp.zeros((B_DK, B_DV), F32)
            for kv_ref, dc_ref, out_ref, t, c in seq:
                r0 = (d * CPT + c) * B_DK
                out_ref[t, hd, r0:r0 + B_DK, :] = s.astype(BF16)
                dcol = dc_ref[t, d, hd * B_DK:(hd + 1) * B_DK, c * CHUNK_B:c * CHUNK_B + 1]
                s = dcol * s + kv_ref[t, hd, r0:r0 + B_DK, :]


def _scan_call(kv, dec, n_samples, n_lat_tiles, tiles_per_sample):
    n_tiles, _, KVR, _ = kv.shape
    kv_c = pl.BlockSpec((1, B_HEADS, KVR, B_DV), lambda b: (n_lat_tiles + b, 0, 0, 0))
    kv_l = pl.BlockSpec((tiles_per_sample, B_HEADS, KVR, B_DV), lambda b: (b, 0, 0, 0))
    dc_c = pl.BlockSpec((1, 2, B_KEY_WIDTH, TT), lambda b: (n_lat_tiles + b, 0, 0, 0))
    dc_l = pl.BlockSpec((tiles_per_sample, 2, B_KEY_WIDTH, TT), lambda b: (b, 0, 0, 0))
    s_ctx, s_lat = pl.pallas_call(
        _scan_kernel,
        grid=(n_samples,),
        in_specs=[kv_c, kv_l, dc_c, dc_l],
        out_specs=[
            pl.BlockSpec((1, B_HEADS, KVR, B_DV), lambda b: (b, 0, 0, 0)),
            pl.BlockSpec((tiles_per_sample, B_HEADS, KVR, B_DV), lambda b: (b, 0, 0, 0)),
        ],
        out_shape=[
            jax.ShapeDtypeStruct((n_samples, B_HEADS, KVR, B_DV), BF16),
            jax.ShapeDtypeStruct((n_lat_tiles, B_HEADS, KVR, B_DV), BF16),
        ],
        compiler_params=_cparams(("arbitrary",)),
        name="gla_scan",
    )(kv, kv, dec, dec)
    return jnp.concatenate([s_lat, s_ctx], axis=0)


def _post_kernel(lat_ref, ctx_ref, mod_ref, a_ref, qe_ref, oi_ref, sr_ref, st_ref, ggla_ref, wout_ref, g2_ref,
                 wr_ref, br_ref, lstr_ref,
                 h1_ref, hn2_ref, route_ref, routet_ref, cnt_ref,
                 rhs_scr, cnt_scr, *, n_lat_tiles):
    D = lat_ref.shape[1]
    i = pl.program_id(0)

    @pl.when(i == 0)
    def _():
        rhs_scr[...] = jnp.zeros(rhs_scr.shape, rhs_scr.dtype)
        cnt_scr[...] = jnp.zeros(cnt_scr.shape, cnt_scr.dtype)

    qe = qe_ref[...]
    o_parts = []
    for c in range(CPT):
        for d in range(2):
            for hd in range(B_HEADS):
                r0 = d * B_KEY_WIDTH + hd * B_DK
                s0 = (d * CPT + c) * B_DK
                rhs_scr[c, r0:r0 + B_DK, hd * B_DV:(hd + 1) * B_DV] = st_ref[0, hd, s0:s0 + B_DK, :]
        o_parts.append(jnp.dot(qe[c * CHUNK_B:(c + 1) * CHUNK_B, :], rhs_scr[c], preferred_element_type=F32))
    o = oi_ref[...] + jnp.concatenate(o_parts, axis=0)

    heads = []
    for hd in range(B_HEADS):
        sl = slice(hd * B_DV, (hd + 1) * B_DV)
        oh = o[:, sl]
        msq = jnp.mean(oh * oh, axis=-1, keepdims=True)
        heads.append(oh * lax.rsqrt(msq + EPS) * ggla_ref[:, sl])
    b_out = jnp.concatenate(heads, axis=1) * sr_ref[...].astype(F32)
    mix_in = jnp.concatenate([a_ref[...], b_out.astype(BF16)], axis=1)
    mix = jnp.dot(mix_in, wout_ref[...], preferred_element_type=F32)

    mod = mod_ref[0]
    g1 = mod[:, 2 * D:3 * D]
    sh2 = mod[:, 3 * D:4 * D]
    sc2 = mod[:, 4 * D:5 * D]
    h1 = _select_tile(lat_ref, ctx_ref, n_lat_tiles) + g1 * mix
    h1_ref[...] = h1
    msq = jnp.mean(h1 * h1, axis=-1, keepdims=True)
    hn2 = h1 * lax.rsqrt(msq + EPS) * g2_ref[...]
    hn2 = hn2 * (1.0 + sc2) + sh2
    _store_rows(hn2_ref, hn2)

    lg = jnp.dot(hn2.astype(BF16), wr_ref[...], preferred_element_type=F32) + br_ref[...]
    lane = lax.broadcasted_iota(I32, lg.shape, 1)
    lanef = lane.astype(F32)
    neg = jnp.float32(-3.0e38)
    big = jnp.float32(1.0e9)
    is_g = (lane >= N_EXPERTS) & (lane < N_EXPERTS + N_GROUPS)
    gl = jnp.where(is_g, lg, neg)
    gmax = jnp.max(gl, axis=1, keepdims=True)
    gsum = jnp.sum(jnp.where(is_g, jnp.exp(gl - gmax), 0.0), axis=1, keepdims=True)
    p_top = 1.0 / gsum
    gidx = jnp.min(jnp.where(is_g & (gl == gmax), lanef, big), axis=1, keepdims=True) - float(N_EXPERTS)
    in_g = (lane < N_EXPERTS) & (lax.shift_right_logical(lane, 3).astype(F32) == gidx)
    el = jnp.where(in_g, lg, neg)
    m1 = jnp.max(el, axis=1, keepdims=True)
    i1 = jnp.min(jnp.where(in_g & (el == m1), lanef, big), axis=1, keepdims=True)
    el2 = jnp.where(lanef == i1, neg, el)
    m2 = jnp.max(el2, axis=1, keepdims=True)
    i2 = jnp.min(jnp.where(in_g & (el2 == m2), lanef, big), axis=1, keepdims=True)
    t = jnp.exp(m2 - m1)
    gate1 = p_top / (1.0 + t)
    gate2 = p_top * t / (1.0 + t)

    sel1 = lanef == i1
    sel2 = lanef == (i2 + float(N_EXPERTS))
    oh = jnp.where(sel1, 1.0, 0.0) + jnp.where(sel2, 1.0, 0.0)
    prefix = jnp.dot(lstr_ref[...], oh.astype(BF16), preferred_element_type=F32)
    tot = jnp.sum(oh, axis=0, keepdims=True)
    tot_sw = pltpu.roll(tot, N_EXPERTS, axis=1)
    base = cnt_scr[...]
    val = prefix + base + jnp.where(lane[0:1, :] >= N_EXPERTS, tot_sw, 0.0)
    r1 = jnp.sum(jnp.where(sel1, val, 0.0), axis=1, keepdims=True)
    r2 = jnp.sum(jnp.where(sel2, val, 0.0), axis=1, keepdims=True)
    new_cnt = base + tot + tot_sw
    cnt_scr[...] = new_cnt
    cnt_ref[...] = new_cnt
    route = (jnp.where(lane == 0, i1, 0.0) + jnp.where(lane == 1, i2, 0.0)
             + jnp.where(lane == 2, gate1, 0.0) + jnp.where(lane == 3, gate2, 0.0)
             + jnp.where(lane == 4, r1, 0.0) + jnp.where(lane == 5, r2, 0.0))
    route_ref[...] = route
    routet_ref[...] = route.T[0:SUBLANE, :]


def _post_call(lat, ctx, ctx_off, mod3, pre_outs, states, lw, n_post_tiles, n_lat_tiles, tiles_per_sample, n_mod_ctx):
    a_out, qe, oi, sr = pre_outs
    D = lat.shape[1]
    KVR = 2 * CPT * B_DK
    NP = n_post_tiles * TT
    lat_spec, ctx_spec = _stream_specs(D, n_lat_tiles, ctx_off)

    def mod_map(i):
        return (jnp.where(i < n_lat_tiles, i // tiles_per_sample, n_mod_ctx), 0, 0)

    const2 = lambda i: (0, 0)
    tile2 = lambda i: (i, 0)
    return pl.pallas_call(
        functools.partial(_post_kernel, n_lat_tiles=n_lat_tiles),
        grid=(n_post_tiles,),
        in_specs=[
            lat_spec,
            ctx_spec,
            pl.BlockSpec((1, 1, 6 * D), mod_map),
            pl.BlockSpec((TT, A_WIDTH), tile2),
            pl.BlockSpec((TT, 2 * B_KEY_WIDTH), tile2),
            pl.BlockSpec((TT, B_WIDTH), tile2),
            pl.BlockSpec((TT, B_WIDTH), tile2),
            pl.BlockSpec((1, B_HEADS, KVR, B_DV), lambda i: (i, 0, 0, 0)),
            pl.BlockSpec((1, B_WIDTH), const2),
            pl.BlockSpec((D, D), const2),
            pl.BlockSpec((1, D), const2),
            pl.BlockSpec((D, LANE), const2),
            pl.BlockSpec((1, LANE), const2),
            pl.BlockSpec((TT, TT), const2),
        ],
        out_specs=[
            pl.BlockSpec((TT, D), tile2),
            pl.BlockSpec((TT * ROW_SUB, LANE), tile2),
            pl.BlockSpec((TT, LANE), tile2),
            pl.BlockSpec((SUBLANE, TT), lambda i: (0, i)),
            pl.BlockSpec((1, LANE), const2),
        ],
        out_shape=[
            jax.ShapeDtypeStruct((NP, D), F32),
            jax.ShapeDtypeStruct((NP * ROW_SUB, LANE), F32),
            jax.ShapeDtypeStruct((NP, LANE), F32),
            jax.ShapeDtypeStruct((SUBLANE, NP), F32),
            jax.ShapeDtypeStruct((1, LANE), F32),
        ],
        scratch_shapes=[
            pltpu.VMEM((CPT, 2 * B_KEY_WIDTH, B_WIDTH), BF16),
            pltpu.VMEM((1, LANE), F32),
        ],
        compiler_params=_cparams(("arbitrary",)),
        name="mix_post",
    )(lat, ctx, mod3, a_out, qe, oi, sr, states, lw["g_gla"], lw["w_out"], lw["g_norm2"],
      lw["w_router"], lw["b_router"], lw["l_strict"])


def _plan_kernel(rt_ref, ps_ref, idx_ref):
    sub = lax.broadcasted_iota(I32, (N_EXPERTS, LANE), 0).astype(F32)
    ps = ps_ref[...]
    segs = DT // LANE
    for k in range(2):
        for seg in range(segs):
            sl = slice(seg * LANE, (seg + 1) * LANE)
            e = rt_ref[k:k + 1, sl]
            r = rt_ref[4 + k:5 + k, sl]
            base = jnp.sum(jnp.where(sub == e, ps, 0.0), axis=0, keepdims=True)
            idx_ref[0, k * segs + seg:k * segs + seg + 1, :] = ((base + r) * float(ROW_SUB)).astype(I32)


def _plan_call(route_t, pad_starts):
    n_dt = route_t.shape[1] // DT
    return pl.pallas_call(
        _plan_kernel,
        grid=(n_dt,),
        in_specs=[
            pl.BlockSpec((SUBLANE, DT), lambda i: (0, i)),
            pl.BlockSpec((N_EXPERTS, 1), lambda i: (0, 0)),
        ],
        out_specs=pl.BlockSpec((1, 2 * DT // LANE, LANE), lambda i: (i, 0, 0)),
        out_shape=jax.ShapeDtypeStruct((n_dt, 2 * DT // LANE, LANE), I32),
        compiler_params=_cparams(("arbitrary",)),
        name="moe_plan",
    )(route_t, pad_starts.astype(F32)[:, None])


def _row_loop(n_rows, fn):
    unroll = 8

    def body(jo, carry):
        for u in range(unroll):
            fn(jo * unroll + u, jo, u)
        return carry

    lax.fori_loop(0, n_rows // unroll, body, 0)


def _tile_indices(idx_smem, jo, u):
    r = lax.shift_right_logical(jo, 4)
    c = (jo & (LANE // 8 - 1)) * 8 + u
    d0 = pl.multiple_of(idx_smem[r, c], ROW_SUB)
    d1 = pl.multiple_of(idx_smem[r + DT // LANE, c], ROW_SUB)
    return d0, d1


def _dispatch_kernel(pe_ref, idx_hbm, x_ref, xs_out, idx_smem, zbuf, sem_i, sem_o, sem_z):
    i = pl.program_id(0)
    cp = pltpu.make_async_copy(idx_hbm.at[i], idx_smem, sem_i)
    cp.start()
    blk_rows = MOE_BLK * ROW_SUB

    @pl.when(i == 0)
    def _():
        zbuf[...] = jnp.zeros(zbuf.shape, zbuf.dtype)
        for e in range(N_EXPERTS):
            start = pl.multiple_of(jnp.maximum(pe_ref[e] - MOE_BLK, 0) * ROW_SUB, blk_rows)
            pltpu.make_async_copy(zbuf, xs_out.at[pl.ds(start, blk_rows), :], sem_z).start()
        for e in range(N_EXPERTS):
            pltpu.make_async_copy(zbuf, xs_out.at[pl.ds(0, blk_rows), :], sem_z).wait()
        first_free = pe_ref[N_EXPERTS - 1] // MOE_BLK
        n_blocks = xs_out.shape[0] // blk_rows

        def clear(b, carry):
            start = pl.multiple_of(b * blk_rows, blk_rows)
            pltpu.make_async_copy(zbuf, xs_out.at[pl.ds(start, blk_rows), :], sem_z).start()
            return carry

        def clear_wait(b, carry):
            pltpu.make_async_copy(zbuf, xs_out.at[pl.ds(0, blk_rows), :], sem_z).wait()
            return carry

        lax.fori_loop(first_free, n_blocks, clear, 0)
        lax.fori_loop(first_free, n_blocks, clear_wait, 0)

    cp.wait()

    def issue(j, jo, u):
        d0, d1 = _tile_indices(idx_smem, jo, u)
        src = x_ref.at[pl.ds(pl.multiple_of(j * ROW_SUB, ROW_SUB), ROW_SUB), :]
        pltpu.make_async_copy(src, xs_out.at[pl.ds(d0, ROW_SUB), :], sem_o).start(priority=0)
        pltpu.make_async_copy(src, xs_out.at[pl.ds(d1, ROW_SUB), :], sem_o).start(priority=1)

    _row_loop(DT, issue)
    for _ in range(2):
        pltpu.make_async_copy(x_ref, xs_out.at[pl.ds(0, DT * ROW_SUB), :], sem_o).wait()


def _dispatch_call(pad_ends, idx, x_rows, n_buf_rows):
    n_tiles = x_rows.shape[0] // (DT * ROW_SUB)
    grid_spec = pltpu.PrefetchScalarGridSpec(
        num_scalar_prefetch=1,
        grid=(n_tiles,),
        in_specs=[
            pl.BlockSpec(memory_space=pl.ANY),
            pl.BlockSpec((DT * ROW_SUB, LANE), lambda i, pe: (i, 0)),
        ],
        out_specs=pl.BlockSpec(memory_space=pl.ANY),
        scratch_shapes=[
            pltpu.SMEM((2 * DT // LANE, LANE), I32),
            pltpu.VMEM((MOE_BLK * ROW_SUB, LANE), F32),
            pltpu.SemaphoreType.DMA,
            pltpu.SemaphoreType.DMA,
            pltpu.SemaphoreType.DMA,
        ],
    )
    return pl.pallas_call(
        _dispatch_kernel,
        grid_spec=grid_spec,
        out_shape=jax.ShapeDtypeStruct((n_buf_rows * ROW_SUB, LANE), F32),
        compiler_params=_cparams(("arbitrary",)),
        name="moe_dispatch",
    )(pad_ends, idx, x_rows)


def _ffn_kernel(be_ref, nu_ref, x_ref, w1_ref, w3_ref, w2_ref, y_ref):
    i = pl.program_id(0)

    @pl.when(i < nu_ref[0])
    def _():
        x = _load_rows(x_ref, MOE_BLK).astype(BF16)
        h1 = jnp.dot(x, w1_ref[0, 0].astype(BF16), preferred_element_type=F32)
        h3 = jnp.dot(x, w3_ref[0, 0].astype(BF16), preferred_element_type=F32)
        hh = (h1 * jax.nn.sigmoid(h1) * h3).astype(BF16)
        y = jnp.dot(hh, w2_ref[0, 0].astype(BF16), preferred_element_type=F32)
        _store_rows(y_ref, y)

    @pl.when(i >= nu_ref[0])
    def _():
        y_ref[...] = jnp.zeros(y_ref.shape, y_ref.dtype)


def _ffn_call(block_expert, n_used, xs, w1, w3, w2, l):
    blk_rows = MOE_BLK * ROW_SUB
    NB = xs.shape[0] // blk_rows
    _, _, D, DE = w1.shape

    def blk(i, be, nu):
        return jnp.minimum(i, nu[0] - 1)

    grid_spec = pltpu.PrefetchScalarGridSpec(
        num_scalar_prefetch=2,
        grid=(NB,),
        in_specs=[
            pl.BlockSpec((blk_rows, LANE), lambda i, be, nu: (blk(i, be, nu), 0)),
            pl.BlockSpec((1, 1, D, DE), lambda i, be, nu: (l, be[blk(i, be, nu)], 0, 0)),
            pl.BlockSpec((1, 1, D, DE), lambda i, be, nu: (l, be[blk(i, be, nu)], 0, 0)),
            pl.BlockSpec((1, 1, DE, D), lambda i, be, nu: (l, be[blk(i, be, nu)], 0, 0)),
        ],
        out_specs=pl.BlockSpec((blk_rows, LANE), lambda i, be, nu: (i, 0)),
    )
    return pl.pallas_call(
        _ffn_kernel,
        grid_spec=grid_spec,
        out_shape=jax.ShapeDtypeStruct(xs.shape, F32),
        compiler_params=_cparams(("arbitrary",)),
        name="moe_ffn",
    )(block_expert, n_used, xs, w1, w3, w2)


def _combine_kernel(idx_hbm, h1_ref, route_ref, mod_ref, gf_ref, y_hbm, o_ref, idx_smem, rows0, rows1, sem_i, sem_g,
                    *, final):
    D = h1_ref.shape[1]
    i = pl.program_id(0)
    cp = pltpu.make_async_copy(idx_hbm.at[i], idx_smem, sem_i)
    cp.start()
    cp.wait()

    def issue(j, jo, u):
        d0, d1 = _tile_indices(idx_smem, jo, u)
        dst = pl.ds(pl.multiple_of(j * ROW_SUB, ROW_SUB), ROW_SUB)
        pltpu.make_async_copy(y_hbm.at[pl.ds(d0, ROW_SUB), :], rows0.at[dst, :], sem_g).start(priority=0)
        pltpu.make_async_copy(y_hbm.at[pl.ds(d1, ROW_SUB), :], rows1.at[dst, :], sem_g).start(priority=1)

    _row_loop(DT, issue)
    for rows in (rows0, rows1):
        pltpu.make_async_copy(y_hbm.at[pl.ds(0, DT * ROW_SUB), :], rows, sem_g).wait()

    route = route_ref[...]
    gate1 = route[:, 2:3]
    gate2 = route[:, 3:4]
    y = gate1 * _load_rows(rows0, DT) + gate2 * _load_rows(rows1, DT)
    g2 = mod_ref[0][:, 5 * D:6 * D]
    out = h1_ref[...] + g2 * y
    if final:
        msq = jnp.mean(out * out, axis=-1, keepdims=True)
        out = out * lax.rsqrt(msq + EPS) * gf_ref[...]
    o_ref[...] = out


def _combine_call(idx, h1, route, mod3, g_final, yb, n_lat_dt, dt_per_sample, n_mod_ctx, final):
    N, D = h1.shape
    n_tiles = N // DT

    def mod_map(i):
        return (jnp.where(i < n_lat_dt, i // dt_per_sample, n_mod_ctx), 0, 0)

    return pl.pallas_call(
        functools.partial(_combine_kernel, final=final),
        grid=(n_tiles,),
        in_specs=[
            pl.BlockSpec(memory_space=pl.ANY),
            pl.BlockSpec((DT, D), lambda i: (i, 0)),
            pl.BlockSpec((DT, LANE), lambda i: (i, 0)),
            pl.BlockSpec((1, 1, 6 * D), mod_map),
            pl.BlockSpec((1, D), lambda i: (0, 0)),
            pl.BlockSpec(memory_space=pl.ANY),
        ],
        out_specs=pl.BlockSpec((DT, D), lambda i: (i, 0)),
        out_shape=jax.ShapeDtypeStruct((N, D), F32),
        scratch_shapes=[
            pltpu.SMEM((2 * DT // LANE, LANE), I32),
            pltpu.VMEM((DT * ROW_SUB, LANE), F32),
            pltpu.VMEM((DT * ROW_SUB, LANE), F32),
            pltpu.SemaphoreType.DMA,
            pltpu.SemaphoreType.DMA,
        ],
        compiler_params=_cparams(("arbitrary",)),
        name="moe_combine",
    )(idx, h1, route, mod3, g_final, yb)


def _layer_weights(l, w_in, g_norm1, ln_v_g, ln_v_b, w_sp, b_sp, w_gate_up, b_gate, g_gla, w_out, g_norm2,
                   w_router_g, b_router_g, w_router_e, b_router_e):
    D = w_in.shape[1]
    d_in = w_in.shape[2]
    pad = (-d_in) % LANE
    w_in_p = jnp.pad(w_in[l], ((0, 0), (0, pad))).astype(BF16)
    KW = B_KEY_WIDTH
    wg = jnp.zeros((LANE, 2 * KW), F32)
    wg = wg.at[0:GATE_RANK, 0:KW].set(w_gate_up[l, 0])
    wg = wg.at[GATE_RANK:2 * GATE_RANK, KW:2 * KW].set(w_gate_up[l, 1])
    bg = jnp.concatenate([b_gate[l, 0], b_gate[l, 1]])[None, :]
    bsp = jnp.repeat(b_sp[l].T, A_GROUP_DIM, axis=1)
    wr = jnp.zeros((D, LANE), F32)
    wr = wr.at[:, 0:N_EXPERTS].set(w_router_e[l])
    wr = wr.at[:, N_EXPERTS:N_EXPERTS + N_GROUPS].set(w_router_g[l])
    br = jnp.zeros((1, LANE), F32)
    br = br.at[0, 0:N_EXPERTS].set(b_router_e[l])
    br = br.at[0, N_EXPERTS:N_EXPERTS + N_GROUPS].set(b_router_g[l])
    r = jnp.arange(TT, dtype=I32)
    same = (r[:, None] // CHUNK_B) == (r[None, :] // CHUNK_B)
    tri_f = (same & (r[:, None] >= r[None, :])).astype(BF16)
    tri_b = (same & (r[:, None] <= r[None, :])).astype(BF16)
    l_strict = (r[:, None] > r[None, :]).astype(BF16)
    return dict(
        w_in=w_in_p, g_norm1=g_norm1[l][None, :], ln_g=ln_v_g[l][None, :], ln_b=ln_v_b[l][None, :],
        w_sp=w_sp[l].astype(BF16), b_sp=bsp, w_gate=wg.astype(BF16), b_gate=bg,
        g_gla=g_gla[l][None, :], w_out=w_out[l].astype(BF16), g_norm2=g_norm2[l][None, :],
        w_router=wr.astype(BF16), b_router=br, tri_f=tri_f, tri_b=tri_b, l_strict=l_strict)


def _segment_layout(counts_row, n_tokens):
    counts = counts_row[0, :N_EXPERTS].astype(I32)
    padded = (counts + MOE_BLK - 1) // MOE_BLK * MOE_BLK
    pad_ends = jnp.cumsum(padded)
    pad_starts = pad_ends - padded
    n_blocks = (n_tokens * 2 + N_EXPERTS * (MOE_BLK - 1)) // MOE_BLK + 1
    block_start = jnp.arange(n_blocks, dtype=I32) * MOE_BLK
    block_expert = jnp.sum((pad_ends[None, :] <= block_start[:, None]).astype(I32), axis=1)
    block_expert = jnp.minimum(block_expert, N_EXPERTS - 1)
    n_used = (pad_ends[-1] // MOE_BLK).astype(I32)[None]
    return pad_starts, pad_ends, block_expert, n_used, n_blocks * MOE_BLK


def kernel(x, c, ctx, c_ctx, w_mod, b_mod, g_norm1, w_in, ln_v_g, ln_v_b, w_sp, b_sp, w_gate_up, b_gate, g_gla,
           w_out, g_norm2, w_router_g, b_router_g, w_router_e, b_router_e, w1, w3, w2, g_final):
    Bn, S, D = x.shape
    Lc = ctx.shape[1]
    depth = w_mod.shape[0]
    assert S % DT == 0 and Lc % TT == 0 and (Bn * Lc) % DT == 0 and Lc == TT
    n_lat = Bn * S
    n_ctx = Bn * Lc
    n_lat_tiles = n_lat // TT
    tiles_per_sample = S // TT

    cc = jnp.concatenate([c, c_ctx[None, :]], axis=0)
    mod_all = _modulation(cc, w_mod, b_mod)
    gf = g_final[None, :]

    lat, cx, ctx_off = x.reshape(n_lat, D), ctx.reshape(n_ctx, D), 0
    for l in range(depth):
        last = l == depth - 1
        lw = _layer_weights(l, w_in, g_norm1, ln_v_g, ln_v_b, w_sp, b_sp, w_gate_up, b_gate, g_gla, w_out,
                            g_norm2, w_router_g, b_router_g, w_router_e, b_router_e)
        mod3 = mod_all[l].reshape(Bn + 1, 1, 6 * D)
        a_out, qe, oi, sr, kv, dec = _pre_call(lat, cx, ctx_off, n_ctx // TT, mod3, lw, n_lat_tiles,
                                               tiles_per_sample, Bn)
        states = _scan_call(kv, dec, Bn, n_lat_tiles, tiles_per_sample)
        n_tok = n_lat if last else n_lat + n_ctx
        h1, hn2_rows, route, route_t, counts = _post_call(lat, cx, ctx_off, mod3, (a_out, qe, oi, sr), states, lw,
                                                          n_tok // TT, n_lat_tiles, tiles_per_sample, Bn)
        pad_starts, pad_ends, block_expert, n_used, n_buf_rows = _segment_layout(counts, n_tok)
        idx = _plan_call(route_t, pad_starts)
        xs = _dispatch_call(pad_ends, idx, hn2_rows, n_buf_rows)
        yb = _ffn_call(block_expert, n_used, xs, w1, w3, w2, l)
        h_all = _combine_call(idx, h1, route, mod3, gf, yb, n_lat // DT, S // DT, Bn, last)
        lat, cx, ctx_off = h_all, h_all, n_lat_tiles
    return h_all.reshape(Bn, S, D)
```

```python
import functools

import jax
import jax.numpy as jnp
from jax import lax
from jax.experimental import pallas as pl
from jax.experimental.pallas import tpu as pltpu

F32 = jnp.float32
BF16 = jnp.bfloat16
I32 = jnp.int32

EPS = 1e-6
LANE = 128
SUBLANE = 8

A_GROUPS = 4
A_GROUP_DIM = 128
A_WIDTH = A_GROUPS * A_GROUP_DIM
CHUNK_A = 128
B_HEADS = 4
B_DK = 64
B_DV = 128
B_KEY_WIDTH = B_HEADS * B_DK
B_WIDTH = B_HEADS * B_DV
GATE_RANK = 16
GATE_TAU = 16.0
CHUNK_B = 64
N_GROUPS = 8
EXPERTS_PER_GROUP = 8
N_EXPERTS = N_GROUPS * EXPERTS_PER_GROUP

TT = 256
CPT = TT // CHUNK_B
MOE_BLK = 256
DT = 512
ROW_SUB = 8

VMEM_LIMIT = 48 * 1024 * 1024


def _cparams(sem):
    return pltpu.CompilerParams(dimension_semantics=sem, vmem_limit_bytes=VMEM_LIMIT)


def _load_rows(ref, n):
    return jnp.concatenate([ref[pl.ds(s, n, stride=ROW_SUB), :] for s in range(ROW_SUB)], axis=1)


def _store_rows(ref, val, first=0):
    n = val.shape[0]
    for s in range(ROW_SUB):
        ref[pl.ds(first * ROW_SUB + s, n, stride=ROW_SUB), :] = val[:, s * LANE:(s + 1) * LANE]


def _mod_kernel(c_ref, w_ref, b_ref, o_ref):
    c = c_ref[...]
    s = (c * jax.nn.sigmoid(c)).astype(BF16)
    o_ref[0] = jnp.dot(s, w_ref[0].astype(BF16), preferred_element_type=F32) + b_ref[0]


def _modulation(cc, w_mod, b_mod):
    L, D, D6 = w_mod.shape
    R = cc.shape[0]
    tn = 1536
    return pl.pallas_call(
        _mod_kernel,
        grid=(L, D6 // tn),
        in_specs=[
            pl.BlockSpec((R, D), lambda l, j: (0, 0)),
            pl.BlockSpec((1, D, tn), lambda l, j: (l, 0, j)),
            pl.BlockSpec((1, 1, tn), lambda l, j: (l, 0, j)),
        ],
        out_specs=pl.BlockSpec((1, R, tn), lambda l, j: (l, 0, j)),
        out_shape=jax.ShapeDtypeStruct((L, R, D6), F32),
        compiler_params=_cparams(("arbitrary", "arbitrary")),
        name="modulation",
    )(cc, w_mod, b_mod.reshape(L, 1, D6))


def _select_tile(lat_ref, ctx_ref, n_lat_tiles):
    return jnp.where(pl.program_id(0) < n_lat_tiles, lat_ref[...], ctx_ref[...])


def _stream_specs(D, n_lat_tiles, ctx_off):
    lat = pl.BlockSpec((TT, D), lambda i: (jnp.minimum(i, n_lat_tiles - 1), 0))
    ctx = pl.BlockSpec((TT, D), lambda i: (ctx_off + jnp.maximum(i - n_lat_tiles, 0), 0))
    return lat, ctx


def _pre_kernel(lat_ref, ctx_ref, mod_ref, g1_ref, win_ref, lng_ref, lnb_ref, wsp_ref, bsp_ref, wg_ref, bg_ref,
                trif_ref, trib_ref, eye_ref,
                a_ref, qe_ref, oi_ref, sr_ref, kv_ref, dec_ref, *, n_lat_tiles):
    D = lat_ref.shape[1]
    h = _select_tile(lat_ref, ctx_ref, n_lat_tiles)
    mod = mod_ref[0]
    sh1 = mod[:, 0:D]
    sc1 = mod[:, D:2 * D]
    ms = jnp.mean(h * h, axis=-1, keepdims=True)
    hn = h * lax.rsqrt(ms + EPS) * g1_ref[...]
    hn = hn * (1.0 + sc1) + sh1
    z = jnp.dot(hn.astype(BF16), win_ref[...], preferred_element_type=F32)
    o_av = A_WIDTH
    o_q = 2 * A_WIDTH
    o_r = o_q + B_KEY_WIDTH
    o_k = o_r + B_WIDTH
    o_v = o_k + B_KEY_WIDTH
    o_g = o_v + B_WIDTH
    zu = z[:, 0:o_av]
    zv = z[:, o_av:o_q]
    q = z[:, o_q:o_r] * (B_DK ** -0.5)
    zr = z[:, o_r:o_k]
    k = z[:, o_k:o_v]
    vv = z[:, o_v:o_g]
    zg = z[:, o_g:o_g + LANE]

    u = jax.nn.gelu(zu)
    v = jax.nn.gelu(zv)
    s_groups = []
    for g in range(A_GROUPS):
        sl = slice(g * A_GROUP_DIM, (g + 1) * A_GROUP_DIM)
        vg = v[:, sl]
        mu = jnp.mean(vg, axis=-1, keepdims=True)
        dv = vg - mu
        var = jnp.mean(dv * dv, axis=-1, keepdims=True)
        vn = (dv * lax.rsqrt(var + EPS)) * lng_ref[:, sl] + lnb_ref[:, sl]
        vnb = vn.astype(BF16)
        rows = []
        for c in range(TT // CHUNK_A):
            rows.append(jnp.dot(wsp_ref[g], vnb[c * CHUNK_A:(c + 1) * CHUNK_A, :],
                                preferred_element_type=F32) + bsp_ref[:, sl])
        s_groups.append(jnp.concatenate(rows, axis=0))
    a_ref[...] = (u * jnp.concatenate(s_groups, axis=1)).astype(BF16)

    sr_ref[...] = (zr * jax.nn.sigmoid(zr)).astype(BF16)
    lg = jnp.dot(zg.astype(BF16), wg_ref[...], preferred_element_type=F32) + bg_ref[...]
    la = (jnp.minimum(lg, 0.0) - jnp.log1p(jnp.exp(-jnp.abs(lg)))) * (1.0 / GATE_TAU)
    la_hi = la.astype(BF16)
    la_lo = (la - la_hi.astype(F32)).astype(BF16)
    KW = B_KEY_WIDTH
    bf = (jnp.dot(trif_ref[...], la_hi[:, :KW], preferred_element_type=F32)
          + jnp.dot(trif_ref[...], la_lo[:, :KW], preferred_element_type=F32))
    bb = (jnp.dot(trib_ref[...], la_hi[:, KW:], preferred_element_type=F32)
          + jnp.dot(trib_ref[...], la_lo[:, KW:], preferred_element_type=F32))
    blf = jnp.concatenate(
        [jnp.broadcast_to(bf[c * CHUNK_B + CHUNK_B - 1:(c + 1) * CHUNK_B, :], (CHUNK_B, KW)) for c in range(CPT)], axis=0)
    blb = jnp.concatenate(
        [jnp.broadcast_to(bb[c * CHUNK_B:c * CHUNK_B + 1, :], (CHUNK_B, KW)) for c in range(CPT)], axis=0)
    qe_f = q * jnp.exp(bf)
    qe_b = q * jnp.exp(bb)
    ke_f = (k * jnp.exp(-bf)).astype(BF16)
    ke_b = (k * jnp.exp(-bb)).astype(BF16)
    kd_f = k * jnp.exp(blf - bf)
    kd_b = k * jnp.exp(blb - bb)
    qe_ref[...] = jnp.concatenate([qe_f, qe_b], axis=1).astype(BF16)
    nt = (((1,), (1,)), ((), ()))
    dec_rows = ([jnp.exp(bf[c * CHUNK_B + CHUNK_B - 1:(c + 1) * CHUNK_B, :]) for c in range(CPT)]
                + [jnp.exp(bb[c * CHUNK_B:c * CHUNK_B + 1, :]) for c in range(CPT)]
                + [jnp.zeros((LANE - 2 * CPT, KW), F32)])
    dec_pad = jnp.concatenate(dec_rows, axis=0)
    dec_hi = dec_pad.astype(BF16)
    dec_lo = (dec_pad - dec_hi.astype(F32)).astype(BF16)
    dec_ref[0] = (lax.dot_general(eye_ref[...], dec_hi, nt, preferred_element_type=F32)
                  + lax.dot_general(eye_ref[...], dec_lo, nt, preferred_element_type=F32))

    row = lax.broadcasted_iota(I32, (TT, TT), 0)
    col = lax.broadcasted_iota(I32, (TT, TT), 1)
    same = lax.shift_right_logical(row, 6) == lax.shift_right_logical(col, 6)
    m_f = same & (row >= col)
    m_b = same & (row <= col)
    lane_head = lax.shift_right_logical(lax.broadcasted_iota(I32, (1, KW), 1), 6)
    tok_chunk = lax.shift_right_logical(lax.broadcasted_iota(I32, (1, TT), 1), 6)
    vb = vv.astype(BF16)
    kdT_f = kd_f.T
    kdT_b = kd_b.T
    oi_heads = []
    for hd in range(B_HEADS):
        hm = lane_head == hd
        qf = jnp.where(hm, qe_f, 0.0).astype(BF16)
        qb = jnp.where(hm, qe_b, 0.0).astype(BF16)
        att_f = lax.dot_general(qf, ke_f, nt, preferred_element_type=F32)
        att_b = lax.dot_general(qb, ke_b, nt, preferred_element_type=F32)
        att = jnp.where(m_f, att_f, 0.0) + jnp.where(m_b, att_b, 0.0)
        v_h = vb[:, hd * B_DV:(hd + 1) * B_DV]
        oi_heads.append(jnp.dot(att.astype(BF16), v_h, preferred_element_type=F32))
        parts = []
        for kdT in (kdT_f, kdT_b):
            kh = kdT[hd * B_DK:(hd + 1) * B_DK, :]
            for c in range(CPT):
                parts.append(jnp.where(tok_chunk == c, kh, 0.0))
        lhs = jnp.concatenate(parts, axis=0).astype(BF16)
        kv_ref[0, hd] = jnp.dot(lhs, v_h, preferred_element_type=F32)
    oi_ref[...] = jnp.concatenate(oi_heads, axis=1)


def _pre_call(lat, ctx, ctx_off, n_ctx_tiles, mod3, lw, n_lat_tiles, tiles_per_sample, n_mod_ctx):
    D = lat.shape[1]
    n_tiles = n_lat_tiles + n_ctx_tiles
    N = n_tiles * TT
    lat_spec, ctx_spec = _stream_specs(D, n_lat_tiles, ctx_off)
    DINP = lw["w_in"].shape[1]
    KVR = 2 * CPT * B_DK

    def mod_map(i):
        return (jnp.where(i < n_lat_tiles, i // tiles_per_sample, n_mod_ctx), 0, 0)

    const2 = lambda i: (0, 0)
    const3 = lambda i: (0, 0, 0)
    tile2 = lambda i: (i, 0)
    return pl.pallas_call(
        functools.partial(_pre_kernel, n_lat_tiles=n_lat_tiles),
        grid=(n_tiles,),
        in_specs=[
            lat_spec,
            ctx_spec,
            pl.BlockSpec((1, 1, 6 * D), mod_map),
            pl.BlockSpec((1, D), const2),
            pl.BlockSpec((D, DINP), const2),
            pl.BlockSpec((1, A_WIDTH), const2),
            pl.BlockSpec((1, A_WIDTH), const2),
            pl.BlockSpec((A_GROUPS, CHUNK_A, CHUNK_A), const3),
            pl.BlockSpec((CHUNK_A, A_WIDTH), const2),
            pl.BlockSpec((LANE, 2 * B_KEY_WIDTH), const2),
            pl.BlockSpec((1, 2 * B_KEY_WIDTH), const2),
            pl.BlockSpec((TT, TT), const2),
            pl.BlockSpec((TT, TT), const2),
            pl.BlockSpec((B_KEY_WIDTH, B_KEY_WIDTH), const2),
        ],
        out_specs=[
            pl.BlockSpec((TT, A_WIDTH), tile2),
            pl.BlockSpec((TT, 2 * B_KEY_WIDTH), tile2),
            pl.BlockSpec((TT, B_WIDTH), tile2),
            pl.BlockSpec((TT, B_WIDTH), tile2),
            pl.BlockSpec((1, B_HEADS, KVR, B_DV), lambda i: (i, 0, 0, 0)),
            pl.BlockSpec((1, B_KEY_WIDTH, LANE), lambda i: (i, 0, 0)),
        ],
        out_shape=[
            jax.ShapeDtypeStruct((N, A_WIDTH), BF16),
            jax.ShapeDtypeStruct((N, 2 * B_KEY_WIDTH), BF16),
            jax.ShapeDtypeStruct((N, B_WIDTH), F32),
            jax.ShapeDtypeStruct((N, B_WIDTH), BF16),
            jax.ShapeDtypeStruct((n_tiles, B_HEADS, KVR, B_DV), F32),
            jax.ShapeDtypeStruct((n_tiles, B_KEY_WIDTH, LANE), F32),
        ],
        compiler_params=_cparams(("arbitrary",)),
        name="mix_pre",
    )(lat, ctx, mod3, lw["g_norm1"], lw["w_in"], lw["ln_g"], lw["ln_b"], lw["w_sp"], lw["b_sp"],
      lw["w_gate"], lw["b_gate"], lw["tri_f"], lw["tri_b"], lw["eye_k"])


def _scan_kernel(kvc_ref, kvl_ref, dcc_ref, dcl_ref, sc_ref, sl_ref):
    n_lat = kvl_ref.shape[0]
    fwd = [(kvc_ref, dcc_ref, sc_ref, 0, c) for c in range(CPT)]
    fwd += [(kvl_ref, dcl_ref, sl_ref, t, c) for t in range(n_lat) for c in range(CPT)]
    bwd = [(kvc_ref, dcc_ref, sc_ref, 0, c) for c in reversed(range(CPT))]
    bwd += [(kvl_ref, dcl_ref, sl_ref, t, c) for t in reversed(range(n_lat)) for c in reversed(range(CPT))]
    for hd in range(B_HEADS):
        for d, seq in enumerate((fwd, bwd)):
            s = jnp.zeros((B_DK, B_DV), F32)
            for kv_ref, dc_ref, out_ref, t, c in seq:
                r0 = (d * CPT + c) * B_DK
                out_ref[t, hd, r0:r0 + B_DK, :] = s.astype(BF16)
                dcol = dc_ref[t, hd * B_DK:(hd + 1) * B_DK, d * CPT + c:d * CPT + c + 1]
                s = dcol * s + kv_ref[t, hd, r0:r0 + B_DK, :]


def _scan_call(kv, dec, n_samples, n_lat_tiles, tiles_per_sample):
    n_tiles, _, KVR, _ = kv.shape
    kv_c = pl.BlockSpec((1, B_HEADS, KVR, B_DV), lambda b: (n_lat_tiles + b, 0, 0, 0))
    kv_l = pl.BlockSpec((tiles_per_sample, B_HEADS, KVR, B_DV), lambda b: (b, 0, 0, 0))
    dc_c = pl.BlockSpec((1, B_KEY_WIDTH, LANE), lambda b: (n_lat_tiles + b, 0, 0))
    dc_l = pl.BlockSpec((tiles_per_sample, B_KEY_WIDTH, LANE), lambda b: (b, 0, 0))
    s_ctx, s_lat = pl.pallas_call(
        _scan_kernel,
        grid=(n_samples,),
        in_specs=[kv_c, kv_l, dc_c, dc_l],
        out_specs=[
            pl.BlockSpec((1, B_HEADS, KVR, B_DV), lambda b: (b, 0, 0, 0)),
            pl.BlockSpec((tiles_per_sample, B_HEADS, KVR, B_DV), lambda b: (b, 0, 0, 0)),
        ],
        out_shape=[
            jax.ShapeDtypeStruct((n_samples, B_HEADS, KVR, B_DV), BF16),
            jax.ShapeDtypeStruct((n_lat_tiles, B_HEADS, KVR, B_DV), BF16),
        ],
        compiler_params=_cparams(("arbitrary",)),
        name="gla_scan",
    )(kv, kv, dec, dec)
    return jnp.concatenate([s_lat, s_ctx], axis=0)


def _post_kernel(lat_ref, ctx_ref, mod_ref, a_ref, qe_ref, oi_ref, sr_ref, st_ref, ggla_ref, wout_ref, g2_ref,
                 wr_ref, br_ref, ustr_ref,
                 h1_ref, hn2_ref, route_ref, routet_ref, cnt_ref,
                 rhs_scr, cnt_scr, *, n_lat_tiles):
    D = lat_ref.shape[1]
    i = pl.program_id(0)

    @pl.when(i == 0)
    def _():
        rhs_scr[...] = jnp.zeros(rhs_scr.shape, rhs_scr.dtype)
        cnt_scr[...] = jnp.zeros(cnt_scr.shape, cnt_scr.dtype)

    qe = qe_ref[...]
    o_parts = []
    for c in range(CPT):
        for d in range(2):
            for hd in range(B_HEADS):
                r0 = d * B_KEY_WIDTH + hd * B_DK
                s0 = (d * CPT + c) * B_DK
                rhs_scr[c, r0:r0 + B_DK, hd * B_DV:(hd + 1) * B_DV] = st_ref[0, hd, s0:s0 + B_DK, :]
        o_parts.append(jnp.dot(qe[c * CHUNK_B:(c + 1) * CHUNK_B, :], rhs_scr[c], preferred_element_type=F32))
    o = oi_ref[...] + jnp.concatenate(o_parts, axis=0)

    heads = []
    for hd in range(B_HEADS):
        sl = slice(hd * B_DV, (hd + 1) * B_DV)
        oh = o[:, sl]
        msq = jnp.mean(oh * oh, axis=-1, keepdims=True)
        heads.append(oh * lax.rsqrt(msq + EPS) * ggla_ref[:, sl])
    b_out = jnp.concatenate(heads, axis=1) * sr_ref[...].astype(F32)
    mix_in = jnp.concatenate([a_ref[...], b_out.astype(BF16)], axis=1)
    mix = jnp.dot(mix_in, wout_ref[...], preferred_element_type=F32)

    mod = mod_ref[0]
    g1 = mod[:, 2 * D:3 * D]
    sh2 = mod[:, 3 * D:4 * D]
    sc2 = mod[:, 4 * D:5 * D]
    h_in = _select_tile(lat_ref, ctx_ref, n_lat_tiles)
    nt = (((1,), (1,)), ((), ()))
    neg = jnp.float32(-3.0e38)
    big = jnp.float32(1.0e9)
    sub = lax.broadcasted_iota(I32, (EXPERTS_PER_GROUP, LANE), 0).astype(F32)
    route_parts = []
    for hf in range(TT // LANE):
        rs = slice(hf * LANE, (hf + 1) * LANE)
        h1 = h_in[rs, :] + g1 * mix[rs, :]
        h1_ref[rs, :] = h1
        msq = jnp.mean(h1 * h1, axis=-1, keepdims=True)
        hn2 = h1 * lax.rsqrt(msq + EPS) * g2_ref[...]
        hn2 = hn2 * (1.0 + sc2) + sh2
        _store_rows(hn2_ref, hn2, hf * LANE)

        lgt = lax.dot_general(wr_ref[...], hn2.astype(BF16), nt, preferred_element_type=F32) + br_ref[...]
        gl = lgt[N_EXPERTS:N_EXPERTS + N_GROUPS, :]
        gmax = jnp.max(gl, axis=0, keepdims=True)
        p_top = 1.0 / jnp.sum(jnp.exp(gl - gmax), axis=0, keepdims=True)
        gidx = jnp.min(jnp.where(gl == gmax, sub, big), axis=0, keepdims=True)
        el = jnp.zeros((EXPERTS_PER_GROUP, LANE), F32)
        for g in range(N_GROUPS):
            el = el + jnp.where(gidx == float(g), lgt[g * EXPERTS_PER_GROUP:(g + 1) * EXPERTS_PER_GROUP, :], 0.0)
        m1 = jnp.max(el, axis=0, keepdims=True)
        i1 = jnp.min(jnp.where(el == m1, sub, big), axis=0, keepdims=True)
        el2 = jnp.where(sub == i1, neg, el)
        m2 = jnp.max(el2, axis=0, keepdims=True)
        i2 = jnp.min(jnp.where(el2 == m2, sub, big), axis=0, keepdims=True)
        t = jnp.exp(m2 - m1)
        route_parts.append((gidx * float(EXPERTS_PER_GROUP) + i1, gidx * float(EXPERTS_PER_GROUP) + i2,
                            p_top / (1.0 + t), p_top * t / (1.0 + t)))
    e1, e2, gate1, gate2 = (jnp.concatenate(p, axis=1) for p in zip(*route_parts))

    rowf = lax.broadcasted_iota(I32, (LANE, TT), 0).astype(F32)
    sel1 = rowf == e1
    sel2 = rowf == (e2 + float(N_EXPERTS))
    oh = jnp.where(sel1, 1.0, 0.0) + jnp.where(sel2, 1.0, 0.0)
    prefix = jnp.dot(oh.astype(BF16), ustr_ref[...], preferred_element_type=F32)
    tot = jnp.sum(oh, axis=1, keepdims=True)
    tot_sw = jnp.concatenate([tot[N_EXPERTS:, :], tot[:N_EXPERTS, :]], axis=0)
    base = cnt_scr[...]
    second = lax.broadcasted_iota(I32, (LANE, 1), 0) >= N_EXPERTS
    val = prefix + (base + jnp.where(second, tot_sw, 0.0))
    r1 = jnp.sum(jnp.where(sel1, val, 0.0), axis=0, keepdims=True)
    r2 = jnp.sum(jnp.where(sel2, val, 0.0), axis=0, keepdims=True)
    new_cnt = base + tot + tot_sw
    cnt_scr[...] = new_cnt
    cnt_ref[...] = new_cnt
    zero = jnp.zeros_like(e1)
    route_t = jnp.concatenate([e1, e2, gate1, gate2, r1, r2, zero, zero], axis=0)
    routet_ref[...] = route_t
    route_ref[...] = jnp.concatenate([route_t, jnp.zeros((LANE - SUBLANE, TT), F32)], axis=0).T


def _post_call(lat, ctx, ctx_off, mod3, pre_outs, states, lw, n_post_tiles, n_lat_tiles, tiles_per_sample, n_mod_ctx):
    a_out, qe, oi, sr = pre_outs
    D = lat.shape[1]
    KVR = 2 * CPT * B_DK
    NP = n_post_tiles * TT
    lat_spec, ctx_spec = _stream_specs(D, n_lat_tiles, ctx_off)

    def mod_map(i):
        return (jnp.where(i < n_lat_tiles, i // tiles_per_sample, n_mod_ctx), 0, 0)

    const2 = lambda i: (0, 0)
    tile2 = lambda i: (i, 0)
    return pl.pallas_call(
        functools.partial(_post_kernel, n_lat_tiles=n_lat_tiles),
        grid=(n_post_tiles,),
        in_specs=[
            lat_spec,
            ctx_spec,
            pl.BlockSpec((1, 1, 6 * D), mod_map),
            pl.BlockSpec((TT, A_WIDTH), tile2),
            pl.BlockSpec((TT, 2 * B_KEY_WIDTH), tile2),
            pl.BlockSpec((TT, B_WIDTH), tile2),
            pl.BlockSpec((TT, B_WIDTH), tile2),
            pl.BlockSpec((1, B_HEADS, KVR, B_DV), lambda i: (i, 0, 0, 0)),
            pl.BlockSpec((1, B_WIDTH), const2),
            pl.BlockSpec((D, D), const2),
            pl.BlockSpec((1, D), const2),
            pl.BlockSpec((LANE, D), const2),
            pl.BlockSpec((LANE, 1), const2),
            pl.BlockSpec((TT, TT), const2),
        ],
        out_specs=[
            pl.BlockSpec((TT, D), tile2),
            pl.BlockSpec((TT * ROW_SUB, LANE), tile2),
            pl.BlockSpec((TT, LANE), tile2),
            pl.BlockSpec((SUBLANE, TT), lambda i: (0, i)),
            pl.BlockSpec((LANE, 1), const2),
        ],
        out_shape=[
            jax.ShapeDtypeStruct((NP, D), F32),
            jax.ShapeDtypeStruct((NP * ROW_SUB, LANE), F32),
            jax.ShapeDtypeStruct((NP, LANE), F32),
            jax.ShapeDtypeStruct((SUBLANE, NP), F32),
            jax.ShapeDtypeStruct((LANE, 1), F32),
        ],
        scratch_shapes=[
            pltpu.VMEM((CPT, 2 * B_KEY_WIDTH, B_WIDTH), BF16),
            pltpu.VMEM((LANE, 1), F32),
        ],
        compiler_params=_cparams(("arbitrary",)),
        name="mix_post",
    )(lat, ctx, mod3, a_out, qe, oi, sr, states, lw["g_gla"], lw["w_out"], lw["g_norm2"],
      lw["w_router"], lw["b_router"], lw["u_strict"])


def _plan_kernel(rt_ref, ps_ref, idx_ref):
    sub = lax.broadcasted_iota(I32, (N_EXPERTS, LANE), 0).astype(F32)
    ps = ps_ref[...]
    segs = DT // LANE
    for k in range(2):
        for seg in range(segs):
            sl = slice(seg * LANE, (seg + 1) * LANE)
            e = rt_ref[k:k + 1, sl]
            r = rt_ref[4 + k:5 + k, sl]
            base = jnp.sum(jnp.where(sub == e, ps, 0.0), axis=0, keepdims=True)
            idx_ref[0, k * segs + seg:k * segs + seg + 1, :] = ((base + r) * float(ROW_SUB)).astype(I32)


def _plan_call(route_t, pad_starts):
    n_dt = route_t.shape[1] // DT
    return pl.pallas_call(
        _plan_kernel,
        grid=(n_dt,),
        in_specs=[
            pl.BlockSpec((SUBLANE, DT), lambda i: (0, i)),
            pl.BlockSpec((N_EXPERTS, 1), lambda i: (0, 0)),
        ],
        out_specs=pl.BlockSpec((1, 2 * DT // LANE, LANE), lambda i: (i, 0, 0)),
        out_shape=jax.ShapeDtypeStruct((n_dt, 2 * DT // LANE, LANE), I32),
        compiler_params=_cparams(("arbitrary",)),
        name="moe_plan",
    )(route_t, pad_starts.astype(F32)[:, None])


def _row_loop(n_rows, fn):
    unroll = 8

    def body(jo, carry):
        for u in range(unroll):
            fn(jo * unroll + u, jo, u)
        return carry

    lax.fori_loop(0, n_rows // unroll, body, 0)


def _tile_indices(idx_smem, jo, u):
    r = lax.shift_right_logical(jo, 4)
    c = (jo & (LANE // 8 - 1)) * 8 + u
    d0 = pl.multiple_of(idx_smem[r, c], ROW_SUB)
    d1 = pl.multiple_of(idx_smem[r + DT // LANE, c], ROW_SUB)
    return d0, d1


def _dispatch_kernel(pe_ref, idx_hbm, x_ref, xs_out, idx_smem, zbuf, sem_i, sem_o, sem_z):
    i = pl.program_id(0)
    cp = pltpu.make_async_copy(idx_hbm.at[i], idx_smem, sem_i)
    cp.start()
    blk_rows = MOE_BLK * ROW_SUB

    @pl.when(i == 0)
    def _():
        zbuf[...] = jnp.zeros(zbuf.shape, zbuf.dtype)
        for e in range(N_EXPERTS):
            start = pl.multiple_of(jnp.maximum(pe_ref[e] - MOE_BLK, 0) * ROW_SUB, blk_rows)
            pltpu.make_async_copy(zbuf, xs_out.at[pl.ds(start, blk_rows), :], sem_z).start()
        for e in range(N_EXPERTS):
            pltpu.make_async_copy(zbuf, xs_out.at[pl.ds(0, blk_rows), :], sem_z).wait()
        first_free = pe_ref[N_EXPERTS - 1] // MOE_BLK
        n_blocks = xs_out.shape[0] // blk_rows

        def clear(b, carry):
            start = pl.multiple_of(b * blk_rows, blk_rows)
            pltpu.make_async_copy(zbuf, xs_out.at[pl.ds(start, blk_rows), :], sem_z).start()
            return carry

        def clear_wait(b, carry):
            pltpu.make_async_copy(zbuf, xs_out.at[pl.ds(0, blk_rows), :], sem_z).wait()
            return carry

        lax.fori_loop(first_free, n_blocks, clear, 0)
        lax.fori_loop(first_free, n_blocks, clear_wait, 0)

    cp.wait()

    def issue(j, jo, u):
        d0, d1 = _tile_indices(idx_smem, jo, u)
        src = x_ref.at[pl.ds(pl.multiple_of(j * ROW_SUB, ROW_SUB), ROW_SUB), :]
        pltpu.make_async_copy(src, xs_out.at[pl.ds(d0, ROW_SUB), :], sem_o).start(priority=0)
        pltpu.make_async_copy(src, xs_out.at[pl.ds(d1, ROW_SUB), :], sem_o).start(priority=1)

    _row_loop(DT, issue)
    for _ in range(2):
        pltpu.make_async_copy(x_ref, xs_out.at[pl.ds(0, DT * ROW_SUB), :], sem_o).wait()


def _dispatch_call(pad_ends, idx, x_rows, n_buf_rows):
    n_tiles = x_rows.shape[0] // (DT * ROW_SUB)
    grid_spec = pltpu.PrefetchScalarGridSpec(
        num_scalar_prefetch=1,
        grid=(n_tiles,),
        in_specs=[
            pl.BlockSpec(memory_space=pl.ANY),
            pl.BlockSpec((DT * ROW_SUB, LANE), lambda i, pe: (i, 0)),
        ],
        out_specs=pl.BlockSpec(memory_space=pl.ANY),
        scratch_shapes=[
            pltpu.SMEM((2 * DT // LANE, LANE), I32),
            pltpu.VMEM((MOE_BLK * ROW_SUB, LANE), F32),
            pltpu.SemaphoreType.DMA,
            pltpu.SemaphoreType.DMA,
            pltpu.SemaphoreType.DMA,
        ],
    )
    return pl.pallas_call(
        _dispatch_kernel,
        grid_spec=grid_spec,
        out_shape=jax.ShapeDtypeStruct((n_buf_rows * ROW_SUB, LANE), F32),
        compiler_params=_cparams(("arbitrary",)),
        name="moe_dispatch",
    )(pad_ends, idx, x_rows)


def _ffn_kernel(be_ref, nu_ref, x_ref, w1_ref, w3_ref, w2_ref, y_ref):
    i = pl.program_id(0)

    @pl.when(i < nu_ref[0])
    def _():
        x = _load_rows(x_ref, MOE_BLK).astype(BF16)
        h1 = jnp.dot(x, w1_ref[0, 0].astype(BF16), preferred_element_type=F32)
        h3 = jnp.dot(x, w3_ref[0, 0].astype(BF16), preferred_element_type=F32)
        hh = (h1 * jax.nn.sigmoid(h1) * h3).astype(BF16)
        y = jnp.dot(hh, w2_ref[0, 0].astype(BF16), preferred_element_type=F32)
        _store_rows(y_ref, y)

    @pl.when(i >= nu_ref[0])
    def _():
        y_ref[...] = jnp.zeros(y_ref.shape, y_ref.dtype)


def _ffn_call(block_expert, n_used, xs, w1, w3, w2, l):
    blk_rows = MOE_BLK * ROW_SUB
    NB = xs.shape[0] // blk_rows
    _, _, D, DE = w1.shape

    def blk(i, be, nu):
        return jnp.minimum(i, nu[0] - 1)

    grid_spec = pltpu.PrefetchScalarGridSpec(
        num_scalar_prefetch=2,
        grid=(NB,),
        in_specs=[
            pl.BlockSpec((blk_rows, LANE), lambda i, be, nu: (blk(i, be, nu), 0)),
            pl.BlockSpec((1, 1, D, DE), lambda i, be, nu: (l, be[blk(i, be, nu)], 0, 0)),
            pl.BlockSpec((1, 1, D, DE), lambda i, be, nu: (l, be[blk(i, be, nu)], 0, 0)),
            pl.BlockSpec((1, 1, DE, D), lambda i, be, nu: (l, be[blk(i, be, nu)], 0, 0)),
        ],
        out_specs=pl.BlockSpec((blk_rows, LANE), lambda i, be, nu: (i, 0)),
    )
    return pl.pallas_call(
        _ffn_kernel,
        grid_spec=grid_spec,
        out_shape=jax.ShapeDtypeStruct(xs.shape, F32),
        compiler_params=_cparams(("arbitrary",)),
        name="moe_ffn",
    )(block_expert, n_used, xs, w1, w3, w2)


def _combine_kernel(idx_hbm, h1_ref, route_ref, mod_ref, gf_ref, y_hbm, o_ref, idx_smem, rows0, rows1, sem_i, sem_g,
                    *, final):
    D = h1_ref.shape[1]
    i = pl.program_id(0)
    cp = pltpu.make_async_copy(idx_hbm.at[i], idx_smem, sem_i)
    cp.start()
    cp.wait()

    def issue(j, jo, u):
        d0, d1 = _tile_indices(idx_smem, jo, u)
        dst = pl.ds(pl.multiple_of(j * ROW_SUB, ROW_SUB), ROW_SUB)
        pltpu.make_async_copy(y_hbm.at[pl.ds(d0, ROW_SUB), :], rows0.at[dst, :], sem_g).start(priority=0)
        pltpu.make_async_copy(y_hbm.at[pl.ds(d1, ROW_SUB), :], rows1.at[dst, :], sem_g).start(priority=1)

    _row_loop(DT, issue)
    for rows in (rows0, rows1):
        pltpu.make_async_copy(y_hbm.at[pl.ds(0, DT * ROW_SUB), :], rows, sem_g).wait()

    route = route_ref[...]
    gate1 = route[:, 2:3]
    gate2 = route[:, 3:4]
    y = gate1 * _load_rows(rows0, DT) + gate2 * _load_rows(rows1, DT)
    g2 = mod_ref[0][:, 5 * D:6 * D]
    out = h1_ref[...] + g2 * y
    if final:
        msq = jnp.mean(out * out, axis=-1, keepdims=True)
        out = out * lax.rsqrt(msq + EPS) * gf_ref[...]
    o_ref[...] = out


def _combine_call(idx, h1, route, mod3, g_final, yb, n_lat_dt, dt_per_sample, n_mod_ctx, final):
    N, D = h1.shape
    n_tiles = N // DT

    def mod_map(i):
        return (jnp.where(i < n_lat_dt, i // dt_per_sample, n_mod_ctx), 0, 0)

    return pl.pallas_call(
        functools.partial(_combine_kernel, final=final),
        grid=(n_tiles,),
        in_specs=[
            pl.BlockSpec(memory_space=pl.ANY),
            pl.BlockSpec((DT, D), lambda i: (i, 0)),
            pl.BlockSpec((DT, LANE), lambda i: (i, 0)),
            pl.BlockSpec((1, 1, 6 * D), mod_map),
            pl.BlockSpec((1, D), lambda i: (0, 0)),
            pl.BlockSpec(memory_space=pl.ANY),
        ],
        out_specs=pl.BlockSpec((DT, D), lambda i: (i, 0)),
        out_shape=jax.ShapeDtypeStruct((N, D), F32),
        scratch_shapes=[
            pltpu.SMEM((2 * DT // LANE, LANE), I32),
            pltpu.VMEM((DT * ROW_SUB, LANE), F32),
            pltpu.VMEM((DT * ROW_SUB, LANE), F32),
            pltpu.SemaphoreType.DMA,
            pltpu.SemaphoreType.DMA,
        ],
        compiler_params=_cparams(("arbitrary",)),
        name="moe_combine",
    )(idx, h1, route, mod3, g_final, yb)


def _layer_weights(l, w_in, g_norm1, ln_v_g, ln_v_b, w_sp, b_sp, w_gate_up, b_gate, g_gla, w_out, g_norm2,
                   w_router_g, b_router_g, w_router_e, b_router_e):
    D = w_in.shape[1]
    d_in = w_in.shape[2]
    pad = (-d_in) % LANE
    w_in_p = jnp.pad(w_in[l], ((0, 0), (0, pad))).astype(BF16)
    KW = B_KEY_WIDTH
    wg = jnp.zeros((LANE, 2 * KW), F32)
    wg = wg.at[0:GATE_RANK, 0:KW].set(w_gate_up[l, 0])
    wg = wg.at[GATE_RANK:2 * GATE_RANK, KW:2 * KW].set(w_gate_up[l, 1])
    bg = jnp.concatenate([b_gate[l, 0], b_gate[l, 1]])[None, :]
    bsp = jnp.repeat(b_sp[l].T, A_GROUP_DIM, axis=1)
    wr = jnp.zeros((LANE, D), F32)
    wr = wr.at[0:N_EXPERTS, :].set(w_router_e[l].T)
    wr = wr.at[N_EXPERTS:N_EXPERTS + N_GROUPS, :].set(w_router_g[l].T)
    br = jnp.zeros((LANE, 1), F32)
    br = br.at[0:N_EXPERTS, 0].set(b_router_e[l])
    br = br.at[N_EXPERTS:N_EXPERTS + N_GROUPS, 0].set(b_router_g[l])
    r = jnp.arange(TT, dtype=I32)
    same = (r[:, None] // CHUNK_B) == (r[None, :] // CHUNK_B)
    tri_f = (same & (r[:, None] >= r[None, :])).astype(BF16)
    tri_b = (same & (r[:, None] <= r[None, :])).astype(BF16)
    u_strict = (r[:, None] < r[None, :]).astype(BF16)
    eye_k = jnp.eye(B_KEY_WIDTH, dtype=BF16)
    return dict(
        w_in=w_in_p, g_norm1=g_norm1[l][None, :], ln_g=ln_v_g[l][None, :], ln_b=ln_v_b[l][None, :],
        w_sp=w_sp[l].astype(BF16), b_sp=bsp, w_gate=wg.astype(BF16), b_gate=bg,
        g_gla=g_gla[l][None, :], w_out=w_out[l].astype(BF16), g_norm2=g_norm2[l][None, :],
        w_router=wr.astype(BF16), b_router=br, tri_f=tri_f, tri_b=tri_b, u_strict=u_strict, eye_k=eye_k)


def _segment_layout(counts_col, n_tokens):
    counts = counts_col[:N_EXPERTS, 0].astype(I32)
    padded = (counts + MOE_BLK - 1) // MOE_BLK * MOE_BLK
    pad_ends = jnp.cumsum(padded)
    pad_starts = pad_ends - padded
    n_blocks = (n_tokens * 2 + N_EXPERTS * (MOE_BLK - 1)) // MOE_BLK + 1
    block_start = jnp.arange(n_blocks, dtype=I32) * MOE_BLK
    block_expert = jnp.sum((pad_ends[None, :] <= block_start[:, None]).astype(I32), axis=1)
    block_expert = jnp.minimum(block_expert, N_EXPERTS - 1)
    n_used = (pad_ends[-1] // MOE_BLK).astype(I32)[None]
    return pad_starts, pad_ends, block_expert, n_used, n_blocks * MOE_BLK


def kernel(x, c, ctx, c_ctx, w_mod, b_mod, g_norm1, w_in, ln_v_g, ln_v_b, w_sp, b_sp, w_gate_up, b_gate, g_gla,
           w_out, g_norm2, w_router_g, b_router_g, w_router_e, b_router_e, w1, w3, w2, g_final):
    Bn, S, D = x.shape
    Lc = ctx.shape[1]
    depth = w_mod.shape[0]
    assert S % DT == 0 and Lc % TT == 0 and (Bn * Lc) % DT == 0 and Lc == TT
    n_lat = Bn * S
    n_ctx = Bn * Lc
    n_lat_tiles = n_lat // TT
    tiles_per_sample = S // TT

    cc = jnp.concatenate([c, c_ctx[None, :]], axis=0)
    mod_all = _modulation(cc, w_mod, b_mod)
    gf = g_final[None, :]

    lat, cx, ctx_off = x.reshape(n_lat, D), ctx.reshape(n_ctx, D), 0
    for l in range(depth):
        last = l == depth - 1
        lw = _layer_weights(l, w_in, g_norm1, ln_v_g, ln_v_b, w_sp, b_sp, w_gate_up, b_gate, g_gla, w_out,
                            g_norm2, w_router_g, b_router_g, w_router_e, b_router_e)
        mod3 = mod_all[l].reshape(Bn + 1, 1, 6 * D)
        a_out, qe, oi, sr, kv, dec = _pre_call(lat, cx, ctx_off, n_ctx // TT, mod3, lw, n_lat_tiles,
                                               tiles_per_sample, Bn)
        states = _scan_call(kv, dec, Bn, n_lat_tiles, tiles_per_sample)
        n_tok = n_lat if last else n_lat + n_ctx
        h1, hn2_rows, route, route_t, counts = _post_call(lat, cx, ctx_off, mod3, (a_out, qe, oi, sr), states, lw,
                                                          n_tok // TT, n_lat_tiles, tiles_per_sample, Bn)
        pad_starts, pad_ends, block_expert, n_used, n_buf_rows = _segment_layout(counts, n_tok)
        idx = _plan_call(route_t, pad_starts)
        xs = _dispatch_call(pad_ends, idx, hn2_rows, n_buf_rows)
        yb = _ffn_call(block_expert, n_used, xs, w1, w3, w2, l)
        h_all = _combine_call(idx, h1, route, mod3, gf, yb, n_lat // DT, S // DT, Bn, last)
        lat, cx, ctx_off = h_all, h_all, n_lat_tiles
    return h_all.reshape(Bn, S, D)
```

```python
import functools

import jax
import jax.numpy as jnp
from jax import lax
from jax.experimental import pallas as pl
from jax.experimental.pallas import tpu as pltpu

F32 = jnp.float32
BF16 = jnp.bfloat16
I32 = jnp.int32

EPS = 1e-6
LANE = 128
SUBLANE = 8

A_GROUPS = 4
A_GROUP_DIM = 128
A_WIDTH = A_GROUPS * A_GROUP_DIM
CHUNK_A = 128
B_HEADS = 4
B_DK = 64
B_DV = 128
B_KEY_WIDTH = B_HEADS * B_DK
B_WIDTH = B_HEADS * B_DV
GATE_RANK = 16
GATE_TAU = 16.0
CHUNK_B = 64
N_GROUPS = 8
EXPERTS_PER_GROUP = 8
N_EXPERTS = N_GROUPS * EXPERTS_PER_GROUP

TT = 256
GLA_CHUNK = 2 * CHUNK_B
CPT = TT // GLA_CHUNK
MOE_BLK = 256
DT = 512
ROW_SUB = 8

VMEM_LIMIT = 48 * 1024 * 1024


def _cparams(sem):
    return pltpu.CompilerParams(dimension_semantics=sem, vmem_limit_bytes=VMEM_LIMIT)


def _load_rows(ref, n):
    return jnp.concatenate([ref[pl.ds(s, n, stride=ROW_SUB), :] for s in range(ROW_SUB)], axis=1)


def _store_rows(ref, val, first=0):
    n = val.shape[0]
    for s in range(ROW_SUB):
        ref[pl.ds(first * ROW_SUB + s, n, stride=ROW_SUB), :] = val[:, s * LANE:(s + 1) * LANE]


def _mod_kernel(c_ref, w_ref, b_ref, o_ref):
    c = c_ref[...]
    s = (c * jax.nn.sigmoid(c)).astype(BF16)
    o_ref[0] = jnp.dot(s, w_ref[0].astype(BF16), preferred_element_type=F32) + b_ref[0]


def _modulation(cc, w_mod, b_mod):
    L, D, D6 = w_mod.shape
    R = cc.shape[0]
    tn = 1536
    return pl.pallas_call(
        _mod_kernel,
        grid=(L, D6 // tn),
        in_specs=[
            pl.BlockSpec((R, D), lambda l, j: (0, 0)),
            pl.BlockSpec((1, D, tn), lambda l, j: (l, 0, j)),
            pl.BlockSpec((1, 1, tn), lambda l, j: (l, 0, j)),
        ],
        out_specs=pl.BlockSpec((1, R, tn), lambda l, j: (l, 0, j)),
        out_shape=jax.ShapeDtypeStruct((L, R, D6), F32),
        compiler_params=_cparams(("arbitrary", "arbitrary")),
        name="modulation",
    )(cc, w_mod, b_mod.reshape(L, 1, D6))


def _select_tile(lat_ref, ctx_ref, n_lat_tiles):
    return jnp.where(pl.program_id(0) < n_lat_tiles, lat_ref[...], ctx_ref[...])


def _stream_specs(D, n_lat_tiles, ctx_off):
    lat = pl.BlockSpec((TT, D), lambda i: (jnp.minimum(i, n_lat_tiles - 1), 0))
    ctx = pl.BlockSpec((TT, D), lambda i: (ctx_off + jnp.maximum(i - n_lat_tiles, 0), 0))
    return lat, ctx


def _pre_kernel(lat_ref, ctx_ref, mod_ref, g1_ref, win_ref, lng_ref, lnb_ref, wsp_ref, bsp_ref, wg_ref, bg_ref,
                trif_ref, trib_ref, eye_ref,
                a_ref, qe_ref, oi_ref, sr_ref, kv_ref, dec_ref, *, n_lat_tiles):
    D = lat_ref.shape[1]
    h = _select_tile(lat_ref, ctx_ref, n_lat_tiles)
    mod = mod_ref[0]
    sh1 = mod[:, 0:D]
    sc1 = mod[:, D:2 * D]
    ms = jnp.mean(h * h, axis=-1, keepdims=True)
    hn = h * lax.rsqrt(ms + EPS) * g1_ref[...]
    hn = hn * (1.0 + sc1) + sh1
    z = jnp.dot(hn.astype(BF16), win_ref[...], preferred_element_type=F32)
    o_av = A_WIDTH
    o_q = 2 * A_WIDTH
    o_r = o_q + B_KEY_WIDTH
    o_k = o_r + B_WIDTH
    o_v = o_k + B_KEY_WIDTH
    o_g = o_v + B_WIDTH
    zu = z[:, 0:o_av]
    zv = z[:, o_av:o_q]
    q = z[:, o_q:o_r] * (B_DK ** -0.5)
    zr = z[:, o_r:o_k]
    k = z[:, o_k:o_v]
    vv = z[:, o_v:o_g]
    zg = z[:, o_g:o_g + LANE]

    u = jax.nn.gelu(zu)
    v = jax.nn.gelu(zv)
    s_groups = []
    for g in range(A_GROUPS):
        sl = slice(g * A_GROUP_DIM, (g + 1) * A_GROUP_DIM)
        vg = v[:, sl]
        mu = jnp.mean(vg, axis=-1, keepdims=True)
        dv = vg - mu
        var = jnp.mean(dv * dv, axis=-1, keepdims=True)
        vn = (dv * lax.rsqrt(var + EPS)) * lng_ref[:, sl] + lnb_ref[:, sl]
        vnb = vn.astype(BF16)
        rows = []
        for c in range(TT // CHUNK_A):
            rows.append(jnp.dot(wsp_ref[g], vnb[c * CHUNK_A:(c + 1) * CHUNK_A, :],
                                preferred_element_type=F32) + bsp_ref[:, sl])
        s_groups.append(jnp.concatenate(rows, axis=0))
    a_ref[...] = (u * jnp.concatenate(s_groups, axis=1)).astype(BF16)

    sr_ref[...] = (zr * jax.nn.sigmoid(zr)).astype(BF16)
    lg = jnp.dot(zg.astype(BF16), wg_ref[...], preferred_element_type=F32) + bg_ref[...]
    la = (jnp.minimum(lg, 0.0) - jnp.log1p(jnp.exp(-jnp.abs(lg)))) * (1.0 / GATE_TAU)
    la_hi = la.astype(BF16)
    la_lo = (la - la_hi.astype(F32)).astype(BF16)
    KW = B_KEY_WIDTH
    bf = (jnp.dot(trif_ref[...], la_hi[:, :KW], preferred_element_type=F32)
          + jnp.dot(trif_ref[...], la_lo[:, :KW], preferred_element_type=F32))
    bb = (jnp.dot(trib_ref[...], la_hi[:, KW:], preferred_element_type=F32)
          + jnp.dot(trib_ref[...], la_lo[:, KW:], preferred_element_type=F32))
    def chunk_rows(x, r):
        return jnp.concatenate(
            [jnp.broadcast_to(x[c * GLA_CHUNK + r:c * GLA_CHUNK + r + 1, :], (GLA_CHUNK, KW)) for c in range(CPT)], axis=0)

    blf = chunk_rows(bf, GLA_CHUNK - 1)
    blb = chunk_rows(bb, 0)
    bmf = chunk_rows(bf, CHUNK_B - 1)
    bmb = chunk_rows(bb, CHUNK_B)
    qe_ref[...] = jnp.concatenate([q * jnp.exp(bf), q * jnp.exp(bb)], axis=1).astype(BF16)
    kd_f = k * jnp.exp(blf - bf)
    kd_b = k * jnp.exp(blb - bb)
    qe_f = q * jnp.exp(bf - bmf)
    qe_b = q * jnp.exp(bb - bmb)
    ke_f = (k * jnp.exp(bmf - bf)).astype(BF16)
    ke_b = (k * jnp.exp(bmb - bb)).astype(BF16)
    nt = (((1,), (1,)), ((), ()))
    dec_rows = ([jnp.exp(bf[c * GLA_CHUNK + GLA_CHUNK - 1:(c + 1) * GLA_CHUNK, :]) for c in range(CPT)]
                + [jnp.exp(bb[c * GLA_CHUNK:c * GLA_CHUNK + 1, :]) for c in range(CPT)]
                + [jnp.zeros((LANE - 2 * CPT, KW), F32)])
    dec_pad = jnp.concatenate(dec_rows, axis=0)
    dec_hi = dec_pad.astype(BF16)
    dec_lo = (dec_pad - dec_hi.astype(F32)).astype(BF16)
    dec_ref[0] = (lax.dot_general(eye_ref[...], dec_hi, nt, preferred_element_type=F32)
                  + lax.dot_general(eye_ref[...], dec_lo, nt, preferred_element_type=F32))

    row = lax.broadcasted_iota(I32, (TT, TT), 0)
    col = lax.broadcasted_iota(I32, (TT, TT), 1)
    chunk_shift = GLA_CHUNK.bit_length() - 1
    same = lax.shift_right_logical(row, chunk_shift) == lax.shift_right_logical(col, chunk_shift)
    m_f = same & (row >= col)
    m_b = same & (row <= col)
    lane_head = lax.shift_right_logical(lax.broadcasted_iota(I32, (1, KW), 1), B_DK.bit_length() - 1)
    tok_chunk = lax.shift_right_logical(lax.broadcasted_iota(I32, (1, TT), 1), chunk_shift)
    vb = vv.astype(BF16)
    kdT_f = kd_f.T
    kdT_b = kd_b.T
    oi_heads = []
    for hd in range(B_HEADS):
        hm = lane_head == hd
        qf = jnp.where(hm, qe_f, 0.0).astype(BF16)
        qb = jnp.where(hm, qe_b, 0.0).astype(BF16)
        att_f = lax.dot_general(qf, ke_f, nt, preferred_element_type=F32)
        att_b = lax.dot_general(qb, ke_b, nt, preferred_element_type=F32)
        att = jnp.where(m_f, att_f, 0.0) + jnp.where(m_b, att_b, 0.0)
        v_h = vb[:, hd * B_DV:(hd + 1) * B_DV]
        oi_heads.append(jnp.dot(att.astype(BF16), v_h, preferred_element_type=F32))
        parts = []
        for kdT in (kdT_f, kdT_b):
            kh = kdT[hd * B_DK:(hd + 1) * B_DK, :]
            for c in range(CPT):
                parts.append(jnp.where(tok_chunk == c, kh, 0.0))
        lhs = jnp.concatenate(parts, axis=0).astype(BF16)
        kv_ref[0, hd] = jnp.dot(lhs, v_h, preferred_element_type=F32)
    oi_ref[...] = jnp.concatenate(oi_heads, axis=1)


def _pre_call(lat, ctx, ctx_off, n_ctx_tiles, mod3, lw, n_lat_tiles, tiles_per_sample, n_mod_ctx):
    D = lat.shape[1]
    n_tiles = n_lat_tiles + n_ctx_tiles
    N = n_tiles * TT
    lat_spec, ctx_spec = _stream_specs(D, n_lat_tiles, ctx_off)
    DINP = lw["w_in"].shape[1]
    KVR = 2 * CPT * B_DK

    def mod_map(i):
        return (jnp.where(i < n_lat_tiles, i // tiles_per_sample, n_mod_ctx), 0, 0)

    const2 = lambda i: (0, 0)
    const3 = lambda i: (0, 0, 0)
    tile2 = lambda i: (i, 0)
    return pl.pallas_call(
        functools.partial(_pre_kernel, n_lat_tiles=n_lat_tiles),
        grid=(n_tiles,),
        in_specs=[
            lat_spec,
            ctx_spec,
            pl.BlockSpec((1, 1, 6 * D), mod_map),
            pl.BlockSpec((1, D), const2),
            pl.BlockSpec((D, DINP), const2),
            pl.BlockSpec((1, A_WIDTH), const2),
            pl.BlockSpec((1, A_WIDTH), const2),
            pl.BlockSpec((A_GROUPS, CHUNK_A, CHUNK_A), const3),
            pl.BlockSpec((CHUNK_A, A_WIDTH), const2),
            pl.BlockSpec((LANE, 2 * B_KEY_WIDTH), const2),
            pl.BlockSpec((1, 2 * B_KEY_WIDTH), const2),
            pl.BlockSpec((TT, TT), const2),
            pl.BlockSpec((TT, TT), const2),
            pl.BlockSpec((B_KEY_WIDTH, B_KEY_WIDTH), const2),
        ],
        out_specs=[
            pl.BlockSpec((TT, A_WIDTH), tile2),
            pl.BlockSpec((TT, 2 * B_KEY_WIDTH), tile2),
            pl.BlockSpec((TT, B_WIDTH), tile2),
            pl.BlockSpec((TT, B_WIDTH), tile2),
            pl.BlockSpec((1, B_HEADS, KVR, B_DV), lambda i: (i, 0, 0, 0)),
            pl.BlockSpec((1, B_KEY_WIDTH, LANE), lambda i: (i, 0, 0)),
        ],
        out_shape=[
            jax.ShapeDtypeStruct((N, A_WIDTH), BF16),
            jax.ShapeDtypeStruct((N, 2 * B_KEY_WIDTH), BF16),
            jax.ShapeDtypeStruct((N, B_WIDTH), F32),
            jax.ShapeDtypeStruct((N, B_WIDTH), BF16),
            jax.ShapeDtypeStruct((n_tiles, B_HEADS, KVR, B_DV), F32),
            jax.ShapeDtypeStruct((n_tiles, B_KEY_WIDTH, LANE), F32),
        ],
        compiler_params=_cparams(("arbitrary",)),
        name="mix_pre",
    )(lat, ctx, mod3, lw["g_norm1"], lw["w_in"], lw["ln_g"], lw["ln_b"], lw["w_sp"], lw["b_sp"],
      lw["w_gate"], lw["b_gate"], lw["tri_f"], lw["tri_b"], lw["eye_k"])


def _scan_kernel(kvc_ref, kvl_ref, dcc_ref, dcl_ref, sc_ref, sl_ref):
    n_lat = kvl_ref.shape[0]
    fwd = [(kvc_ref, dcc_ref, sc_ref, 0, c) for c in range(CPT)]
    fwd += [(kvl_ref, dcl_ref, sl_ref, t, c) for t in range(n_lat) for c in range(CPT)]
    bwd = [(kvc_ref, dcc_ref, sc_ref, 0, c) for c in reversed(range(CPT))]
    bwd += [(kvl_ref, dcl_ref, sl_ref, t, c) for t in reversed(range(n_lat)) for c in reversed(range(CPT))]
    for hd in range(B_HEADS):
        for d, seq in enumerate((fwd, bwd)):
            s = jnp.zeros((B_DK, B_DV), F32)
            for kv_ref, dc_ref, out_ref, t, c in seq:
                r0 = (d * CPT + c) * B_DK
                out_ref[t, hd, r0:r0 + B_DK, :] = s.astype(BF16)
                dcol = dc_ref[t, hd * B_DK:(hd + 1) * B_DK, d * CPT + c:d * CPT + c + 1]
                s = dcol * s + kv_ref[t, hd, r0:r0 + B_DK, :]


def _scan_call(kv, dec, n_samples, n_lat_tiles, tiles_per_sample):
    n_tiles, _, KVR, _ = kv.shape
    kv_c = pl.BlockSpec((1, B_HEADS, KVR, B_DV), lambda b: (n_lat_tiles + b, 0, 0, 0))
    kv_l = pl.BlockSpec((tiles_per_sample, B_HEADS, KVR, B_DV), lambda b: (b, 0, 0, 0))
    dc_c = pl.BlockSpec((1, B_KEY_WIDTH, LANE), lambda b: (n_lat_tiles + b, 0, 0))
    dc_l = pl.BlockSpec((tiles_per_sample, B_KEY_WIDTH, LANE), lambda b: (b, 0, 0))
    s_ctx, s_lat = pl.pallas_call(
        _scan_kernel,
        grid=(n_samples,),
        in_specs=[kv_c, kv_l, dc_c, dc_l],
        out_specs=[
            pl.BlockSpec((1, B_HEADS, KVR, B_DV), lambda b: (b, 0, 0, 0)),
            pl.BlockSpec((tiles_per_sample, B_HEADS, KVR, B_DV), lambda b: (b, 0, 0, 0)),
        ],
        out_shape=[
            jax.ShapeDtypeStruct((n_samples, B_HEADS, KVR, B_DV), BF16),
            jax.ShapeDtypeStruct((n_lat_tiles, B_HEADS, KVR, B_DV), BF16),
        ],
        compiler_params=_cparams(("arbitrary",)),
        name="gla_scan",
    )(kv, kv, dec, dec)
    return jnp.concatenate([s_lat, s_ctx], axis=0)


def _post_kernel(lat_ref, ctx_ref, mod_ref, a_ref, qe_ref, oi_ref, sr_ref, st_ref, ggla_ref, wout_ref, g2_ref,
                 wr_ref, br_ref, ustr_ref,
                 h1_ref, hn2_ref, route_ref, routet_ref, cnt_ref,
                 rhs_scr, cnt_scr, *, n_lat_tiles):
    D = lat_ref.shape[1]
    i = pl.program_id(0)

    @pl.when(i == 0)
    def _():
        rhs_scr[...] = jnp.zeros(rhs_scr.shape, rhs_scr.dtype)
        cnt_scr[...] = jnp.zeros(cnt_scr.shape, cnt_scr.dtype)

    qe = qe_ref[...]
    o_parts = []
    for c in range(CPT):
        for d in range(2):
            for hd in range(B_HEADS):
                r0 = d * B_KEY_WIDTH + hd * B_DK
                s0 = (d * CPT + c) * B_DK
                rhs_scr[c, r0:r0 + B_DK, hd * B_DV:(hd + 1) * B_DV] = st_ref[0, hd, s0:s0 + B_DK, :]
        o_parts.append(jnp.dot(qe[c * GLA_CHUNK:(c + 1) * GLA_CHUNK, :], rhs_scr[c], preferred_element_type=F32))
    o = oi_ref[...] + jnp.concatenate(o_parts, axis=0)

    heads = []
    for hd in range(B_HEADS):
        sl = slice(hd * B_DV, (hd + 1) * B_DV)
        oh = o[:, sl]
        msq = jnp.mean(oh * oh, axis=-1, keepdims=True)
        heads.append(oh * lax.rsqrt(msq + EPS) * ggla_ref[:, sl])
    b_out = jnp.concatenate(heads, axis=1) * sr_ref[...].astype(F32)
    mix_in = jnp.concatenate([a_ref[...], b_out.astype(BF16)], axis=1)
    mix = jnp.dot(mix_in, wout_ref[...], preferred_element_type=F32)

    mod = mod_ref[0]
    g1 = mod[:, 2 * D:3 * D]
    sh2 = mod[:, 3 * D:4 * D]
    sc2 = mod[:, 4 * D:5 * D]
    h_in = _select_tile(lat_ref, ctx_ref, n_lat_tiles)
    nt = (((1,), (1,)), ((), ()))
    neg = jnp.float32(-3.0e38)
    big = jnp.float32(1.0e9)
    sub = lax.broadcasted_iota(I32, (EXPERTS_PER_GROUP, LANE), 0).astype(F32)
    route_parts = []
    for hf in range(TT // LANE):
        rs = slice(hf * LANE, (hf + 1) * LANE)
        h1 = h_in[rs, :] + g1 * mix[rs, :]
        h1_ref[rs, :] = h1
        msq = jnp.mean(h1 * h1, axis=-1, keepdims=True)
        hn2 = h1 * lax.rsqrt(msq + EPS) * g2_ref[...]
        hn2 = hn2 * (1.0 + sc2) + sh2
        _store_rows(hn2_ref, hn2, hf * LANE)

        lgt = lax.dot_general(wr_ref[...], hn2.astype(BF16), nt, preferred_element_type=F32) + br_ref[...]
        gl = lgt[N_EXPERTS:N_EXPERTS + N_GROUPS, :]
        gmax = jnp.max(gl, axis=0, keepdims=True)
        p_top = 1.0 / jnp.sum(jnp.exp(gl - gmax), axis=0, keepdims=True)
        gidx = jnp.min(jnp.where(gl == gmax, sub, big), axis=0, keepdims=True)
        el = jnp.zeros((EXPERTS_PER_GROUP, LANE), F32)
        for g in range(N_GROUPS):
            el = el + jnp.where(gidx == float(g), lgt[g * EXPERTS_PER_GROUP:(g + 1) * EXPERTS_PER_GROUP, :], 0.0)
        m1 = jnp.max(el, axis=0, keepdims=True)
        i1 = jnp.min(jnp.where(el == m1, sub, big), axis=0, keepdims=True)
        el2 = jnp.where(sub == i1, neg, el)
        m2 = jnp.max(el2, axis=0, keepdims=True)
        i2 = jnp.min(jnp.where(el2 == m2, sub, big), axis=0, keepdims=True)
        t = jnp.exp(m2 - m1)
        route_parts.append((gidx * float(EXPERTS_PER_GROUP) + i1, gidx * float(EXPERTS_PER_GROUP) + i2,
                            p_top / (1.0 + t), p_top * t / (1.0 + t)))
    e1, e2, gate1, gate2 = (jnp.concatenate(p, axis=1) for p in zip(*route_parts))

    rowf = lax.broadcasted_iota(I32, (LANE, TT), 0).astype(F32)
    sel1 = rowf == e1
    sel2 = rowf == (e2 + float(N_EXPERTS))
    oh = jnp.where(sel1, 1.0, 0.0) + jnp.where(sel2, 1.0, 0.0)
    prefix = jnp.dot(oh.astype(BF16), ustr_ref[...], preferred_element_type=F32)
    tot = jnp.sum(oh, axis=1, keepdims=True)
    tot_sw = jnp.concatenate([tot[N_EXPERTS:, :], tot[:N_EXPERTS, :]], axis=0)
    base = cnt_scr[...]
    second = lax.broadcasted_iota(I32, (LANE, 1), 0) >= N_EXPERTS
    val = prefix + (base + jnp.where(second, tot_sw, 0.0))
    r1 = jnp.sum(jnp.where(sel1, val, 0.0), axis=0, keepdims=True)
    r2 = jnp.sum(jnp.where(sel2, val, 0.0), axis=0, keepdims=True)
    new_cnt = base + tot + tot_sw
    cnt_scr[...] = new_cnt
    cnt_ref[...] = new_cnt
    zero = jnp.zeros_like(e1)
    route_t = jnp.concatenate([e1, e2, gate1, gate2, r1, r2, zero, zero], axis=0)
    routet_ref[...] = route_t
    route_ref[...] = jnp.concatenate([route_t, jnp.zeros((LANE - SUBLANE, TT), F32)], axis=0).T


def _post_call(lat, ctx, ctx_off, mod3, pre_outs, states, lw, n_post_tiles, n_lat_tiles, tiles_per_sample, n_mod_ctx):
    a_out, qe, oi, sr = pre_outs
    D = lat.shape[1]
    KVR = 2 * CPT * B_DK
    NP = n_post_tiles * TT
    lat_spec, ctx_spec = _stream_specs(D, n_lat_tiles, ctx_off)

    def mod_map(i):
        return (jnp.where(i < n_lat_tiles, i // tiles_per_sample, n_mod_ctx), 0, 0)

    const2 = lambda i: (0, 0)
    tile2 = lambda i: (i, 0)
    return pl.pallas_call(
        functools.partial(_post_kernel, n_lat_tiles=n_lat_tiles),
        grid=(n_post_tiles,),
        in_specs=[
            lat_spec,
            ctx_spec,
            pl.BlockSpec((1, 1, 6 * D), mod_map),
            pl.BlockSpec((TT, A_WIDTH), tile2),
            pl.BlockSpec((TT, 2 * B_KEY_WIDTH), tile2),
            pl.BlockSpec((TT, B_WIDTH), tile2),
            pl.BlockSpec((TT, B_WIDTH), tile2),
            pl.BlockSpec((1, B_HEADS, KVR, B_DV), lambda i: (i, 0, 0, 0)),
            pl.BlockSpec((1, B_WIDTH), const2),
            pl.BlockSpec((D, D), const2),
            pl.BlockSpec((1, D), const2),
            pl.BlockSpec((LANE, D), const2),
            pl.BlockSpec((LANE, 1), const2),
            pl.BlockSpec((TT, TT), const2),
        ],
        out_specs=[
            pl.BlockSpec((TT, D), tile2),
            pl.BlockSpec((TT * ROW_SUB, LANE), tile2),
            pl.BlockSpec((TT, LANE), tile2),
            pl.BlockSpec((SUBLANE, TT), lambda i: (0, i)),
            pl.BlockSpec((LANE, 1), const2),
        ],
        out_shape=[
            jax.ShapeDtypeStruct((NP, D), F32),
            jax.ShapeDtypeStruct((NP * ROW_SUB, LANE), F32),
            jax.ShapeDtypeStruct((NP, LANE), F32),
            jax.ShapeDtypeStruct((SUBLANE, NP), F32),
            jax.ShapeDtypeStruct((LANE, 1), F32),
        ],
        scratch_shapes=[
            pltpu.VMEM((CPT, 2 * B_KEY_WIDTH, B_WIDTH), BF16),
            pltpu.VMEM((LANE, 1), F32),
        ],
        compiler_params=_cparams(("arbitrary",)),
        name="mix_post",
    )(lat, ctx, mod3, a_out, qe, oi, sr, states, lw["g_gla"], lw["w_out"], lw["g_norm2"],
      lw["w_router"], lw["b_router"], lw["u_strict"])


def _plan_kernel(rt_ref, ps_ref, idx_ref):
    sub = lax.broadcasted_iota(I32, (N_EXPERTS, LANE), 0).astype(F32)
    ps = ps_ref[...]
    segs = DT // LANE
    for k in range(2):
        for seg in range(segs):
            sl = slice(seg * LANE, (seg + 1) * LANE)
            e = rt_ref[k:k + 1, sl]
            r = rt_ref[4 + k:5 + k, sl]
            base = jnp.sum(jnp.where(sub == e, ps, 0.0), axis=0, keepdims=True)
            idx_ref[0, k * segs + seg:k * segs + seg + 1, :] = ((base + r) * float(ROW_SUB)).astype(I32)


def _plan_call(route_t, pad_starts):
    n_dt = route_t.shape[1] // DT
    return pl.pallas_call(
        _plan_kernel,
        grid=(n_dt,),
        in_specs=[
            pl.BlockSpec((SUBLANE, DT), lambda i: (0, i)),
            pl.BlockSpec((N_EXPERTS, 1), lambda i: (0, 0)),
        ],
        out_specs=pl.BlockSpec((1, 2 * DT // LANE, LANE), lambda i: (i, 0, 0)),
        out_shape=jax.ShapeDtypeStruct((n_dt, 2 * DT // LANE, LANE), I32),
        compiler_params=_cparams(("arbitrary",)),
        name="moe_plan",
    )(route_t, pad_starts.astype(F32)[:, None])


def _row_loop(n_rows, fn):
    unroll = 8

    def body(jo, carry):
        for u in range(unroll):
            fn(jo * unroll + u, jo, u)
        return carry

    lax.fori_loop(0, n_rows // unroll, body, 0)


def _tile_indices(idx_smem, jo, u):
    r = lax.shift_right_logical(jo, 4)
    c = (jo & (LANE // 8 - 1)) * 8 + u
    d0 = pl.multiple_of(idx_smem[r, c], ROW_SUB)
    d1 = pl.multiple_of(idx_smem[r + DT // LANE, c], ROW_SUB)
    return d0, d1


def _dispatch_kernel(pe_ref, idx_hbm, x_ref, xs_out, idx_smem, zbuf, sem_i, sem_o, sem_z):
    i = pl.program_id(0)
    cp = pltpu.make_async_copy(idx_hbm.at[i], idx_smem, sem_i)
    cp.start()
    blk_rows = MOE_BLK * ROW_SUB

    @pl.when(i == 0)
    def _():
        zbuf[...] = jnp.zeros(zbuf.shape, zbuf.dtype)
        for e in range(N_EXPERTS):
            start = pl.multiple_of(jnp.maximum(pe_ref[e] - MOE_BLK, 0) * ROW_SUB, blk_rows)
            pltpu.make_async_copy(zbuf, xs_out.at[pl.ds(start, blk_rows), :], sem_z).start()
        for e in range(N_EXPERTS):
            pltpu.make_async_copy(zbuf, xs_out.at[pl.ds(0, blk_rows), :], sem_z).wait()
        first_free = pe_ref[N_EXPERTS - 1] // MOE_BLK
        n_blocks = xs_out.shape[0] // blk_rows

        def clear(b, carry):
            start = pl.multiple_of(b * blk_rows, blk_rows)
            pltpu.make_async_copy(zbuf, xs_out.at[pl.ds(start, blk_rows), :], sem_z).start()
            return carry

        def clear_wait(b, carry):
            pltpu.make_async_copy(zbuf, xs_out.at[pl.ds(0, blk_rows), :], sem_z).wait()
            return carry

        lax.fori_loop(first_free, n_blocks, clear, 0)
        lax.fori_loop(first_free, n_blocks, clear_wait, 0)

    cp.wait()

    def issue(j, jo, u):
        d0, d1 = _tile_indices(idx_smem, jo, u)
        src = x_ref.at[pl.ds(pl.multiple_of(j * ROW_SUB, ROW_SUB), ROW_SUB), :]
        pltpu.make_async_copy(src, xs_out.at[pl.ds(d0, ROW_SUB), :], sem_o).start(priority=0)
        pltpu.make_async_copy(src, xs_out.at[pl.ds(d1, ROW_SUB), :], sem_o).start(priority=1)

    _row_loop(DT, issue)
    for _ in range(2):
        pltpu.make_async_copy(x_ref, xs_out.at[pl.ds(0, DT * ROW_SUB), :], sem_o).wait()


def _dispatch_call(pad_ends, idx, x_rows, n_buf_rows):
    n_tiles = x_rows.shape[0] // (DT * ROW_SUB)
    grid_spec = pltpu.PrefetchScalarGridSpec(
        num_scalar_prefetch=1,
        grid=(n_tiles,),
        in_specs=[
            pl.BlockSpec(memory_space=pl.ANY),
            pl.BlockSpec((DT * ROW_SUB, LANE), lambda i, pe: (i, 0)),
        ],
        out_specs=pl.BlockSpec(memory_space=pl.ANY),
        scratch_shapes=[
            pltpu.SMEM((2 * DT // LANE, LANE), I32),
            pltpu.VMEM((MOE_BLK * ROW_SUB, LANE), F32),
            pltpu.SemaphoreType.DMA,
            pltpu.SemaphoreType.DMA,
            pltpu.SemaphoreType.DMA,
        ],
    )
    return pl.pallas_call(
        _dispatch_kernel,
        grid_spec=grid_spec,
        out_shape=jax.ShapeDtypeStruct((n_buf_rows * ROW_SUB, LANE), F32),
        compiler_params=_cparams(("arbitrary",)),
        name="moe_dispatch",
    )(pad_ends, idx, x_rows)


def _ffn_kernel(be_ref, nu_ref, x_ref, w1_ref, w3_ref, w2_ref, y_ref):
    i = pl.program_id(0)

    @pl.when(i < nu_ref[0])
    def _():
        x = _load_rows(x_ref, MOE_BLK).astype(BF16)
        h1 = jnp.dot(x, w1_ref[0, 0].astype(BF16), preferred_element_type=F32)
        h3 = jnp.dot(x, w3_ref[0, 0].astype(BF16), preferred_element_type=F32)
        hh = (h1 * jax.nn.sigmoid(h1) * h3).astype(BF16)
        y = jnp.dot(hh, w2_ref[0, 0].astype(BF16), preferred_element_type=F32)
        _store_rows(y_ref, y)

    @pl.when(i >= nu_ref[0])
    def _():
        y_ref[...] = jnp.zeros(y_ref.shape, y_ref.dtype)


def _ffn_call(block_expert, n_used, xs, w1, w3, w2, l):
    blk_rows = MOE_BLK * ROW_SUB
    NB = xs.shape[0] // blk_rows
    _, _, D, DE = w1.shape

    def blk(i, be, nu):
        return jnp.minimum(i, nu[0] - 1)

    grid_spec = pltpu.PrefetchScalarGridSpec(
        num_scalar_prefetch=2,
        grid=(NB,),
        in_specs=[
            pl.BlockSpec((blk_rows, LANE), lambda i, be, nu: (blk(i, be, nu), 0)),
            pl.BlockSpec((1, 1, D, DE), lambda i, be, nu: (l, be[blk(i, be, nu)], 0, 0)),
            pl.BlockSpec((1, 1, D, DE), lambda i, be, nu: (l, be[blk(i, be, nu)], 0, 0)),
            pl.BlockSpec((1, 1, DE, D), lambda i, be, nu: (l, be[blk(i, be, nu)], 0, 0)),
        ],
        out_specs=pl.BlockSpec((blk_rows, LANE), lambda i, be, nu: (i, 0)),
    )
    return pl.pallas_call(
        _ffn_kernel,
        grid_spec=grid_spec,
        out_shape=jax.ShapeDtypeStruct(xs.shape, F32),
        compiler_params=_cparams(("arbitrary",)),
        name="moe_ffn",
    )(block_expert, n_used, xs, w1, w3, w2)


def _combine_kernel(idx_hbm, h1_ref, route_ref, mod_ref, gf_ref, y_hbm, o_ref, idx_smem, rows0, rows1, sem_i, sem_g,
                    *, final):
    D = h1_ref.shape[1]
    i = pl.program_id(0)
    cp = pltpu.make_async_copy(idx_hbm.at[i], idx_smem, sem_i)
    cp.start()
    cp.wait()

    def issue(j, jo, u):
        d0, d1 = _tile_indices(idx_smem, jo, u)
        dst = pl.ds(pl.multiple_of(j * ROW_SUB, ROW_SUB), ROW_SUB)
        pltpu.make_async_copy(y_hbm.at[pl.ds(d0, ROW_SUB), :], rows0.at[dst, :], sem_g).start(priority=0)
        pltpu.make_async_copy(y_hbm.at[pl.ds(d1, ROW_SUB), :], rows1.at[dst, :], sem_g).start(priority=1)

    _row_loop(DT, issue)
    for rows in (rows0, rows1):
        pltpu.make_async_copy(y_hbm.at[pl.ds(0, DT * ROW_SUB), :], rows, sem_g).wait()

    route = route_ref[...]
    gate1 = route[:, 2:3]
    gate2 = route[:, 3:4]
    y = gate1 * _load_rows(rows0, DT) + gate2 * _load_rows(rows1, DT)
    g2 = mod_ref[0][:, 5 * D:6 * D]
    out = h1_ref[...] + g2 * y
    if final:
        msq = jnp.mean(out * out, axis=-1, keepdims=True)
        out = out * lax.rsqrt(msq + EPS) * gf_ref[...]
    o_ref[...] = out


def _combine_call(idx, h1, route, mod3, g_final, yb, n_lat_dt, dt_per_sample, n_mod_ctx, final):
    N, D = h1.shape
    n_tiles = N // DT

    def mod_map(i):
        return (jnp.where(i < n_lat_dt, i // dt_per_sample, n_mod_ctx), 0, 0)

    return pl.pallas_call(
        functools.partial(_combine_kernel, final=final),
        grid=(n_tiles,),
        in_specs=[
            pl.BlockSpec(memory_space=pl.ANY),
            pl.BlockSpec((DT, D), lambda i: (i, 0)),
            pl.BlockSpec((DT, LANE), lambda i: (i, 0)),
            pl.BlockSpec((1, 1, 6 * D), mod_map),
            pl.BlockSpec((1, D), lambda i: (0, 0)),
            pl.BlockSpec(memory_space=pl.ANY),
        ],
        out_specs=pl.BlockSpec((DT, D), lambda i: (i, 0)),
        out_shape=jax.ShapeDtypeStruct((N, D), F32),
        scratch_shapes=[
            pltpu.SMEM((2 * DT // LANE, LANE), I32),
            pltpu.VMEM((DT * ROW_SUB, LANE), F32),
            pltpu.VMEM((DT * ROW_SUB, LANE), F32),
            pltpu.SemaphoreType.DMA,
            pltpu.SemaphoreType.DMA,
        ],
        compiler_params=_cparams(("arbitrary",)),
        name="moe_combine",
    )(idx, h1, route, mod3, g_final, yb)


def _layer_weights(l, w_in, g_norm1, ln_v_g, ln_v_b, w_sp, b_sp, w_gate_up, b_gate, g_gla, w_out, g_norm2,
                   w_router_g, b_router_g, w_router_e, b_router_e):
    D = w_in.shape[1]
    d_in = w_in.shape[2]
    pad = (-d_in) % LANE
    w_in_p = jnp.pad(w_in[l], ((0, 0), (0, pad))).astype(BF16)
    KW = B_KEY_WIDTH
    wg = jnp.zeros((LANE, 2 * KW), F32)
    wg = wg.at[0:GATE_RANK, 0:KW].set(w_gate_up[l, 0])
    wg = wg.at[GATE_RANK:2 * GATE_RANK, KW:2 * KW].set(w_gate_up[l, 1])
    bg = jnp.concatenate([b_gate[l, 0], b_gate[l, 1]])[None, :]
    bsp = jnp.repeat(b_sp[l].T, A_GROUP_DIM, axis=1)
    wr = jnp.zeros((LANE, D), F32)
    wr = wr.at[0:N_EXPERTS, :].set(w_router_e[l].T)
    wr = wr.at[N_EXPERTS:N_EXPERTS + N_GROUPS, :].set(w_router_g[l].T)
    br = jnp.zeros((LANE, 1), F32)
    br = br.at[0:N_EXPERTS, 0].set(b_router_e[l])
    br = br.at[N_EXPERTS:N_EXPERTS + N_GROUPS, 0].set(b_router_g[l])
    r = jnp.arange(TT, dtype=I32)
    same = (r[:, None] // GLA_CHUNK) == (r[None, :] // GLA_CHUNK)
    tri_f = (same & (r[:, None] >= r[None, :])).astype(BF16)
    tri_b = (same & (r[:, None] <= r[None, :])).astype(BF16)
    u_strict = (r[:, None] < r[None, :]).astype(BF16)
    eye_k = jnp.eye(B_KEY_WIDTH, dtype=BF16)
    return dict(
        w_in=w_in_p, g_norm1=g_norm1[l][None, :], ln_g=ln_v_g[l][None, :], ln_b=ln_v_b[l][None, :],
        w_sp=w_sp[l].astype(BF16), b_sp=bsp, w_gate=wg.astype(BF16), b_gate=bg,
        g_gla=g_gla[l][None, :], w_out=w_out[l].astype(BF16), g_norm2=g_norm2[l][None, :],
        w_router=wr.astype(BF16), b_router=br, tri_f=tri_f, tri_b=tri_b, u_strict=u_strict, eye_k=eye_k)


def _segment_layout(counts_col, n_tokens):
    counts = counts_col[:N_EXPERTS, 0].astype(I32)
    padded = (counts + MOE_BLK - 1) // MOE_BLK * MOE_BLK
    pad_ends = jnp.cumsum(padded)
    pad_starts = pad_ends - padded
    n_blocks = (n_tokens * 2 + N_EXPERTS * (MOE_BLK - 1)) // MOE_BLK + 1
    block_start = jnp.arange(n_blocks, dtype=I32) * MOE_BLK
    block_expert = jnp.sum((pad_ends[None, :] <= block_start[:, None]).astype(I32), axis=1)
    block_expert = jnp.minimum(block_expert, N_EXPERTS - 1)
    n_used = (pad_ends[-1] // MOE_BLK).astype(I32)[None]
    return pad_starts, pad_ends, block_expert, n_used, n_blocks * MOE_BLK


def kernel(x, c, ctx, c_ctx, w_mod, b_mod, g_norm1, w_in, ln_v_g, ln_v_b, w_sp, b_sp, w_gate_up, b_gate, g_gla,
           w_out, g_norm2, w_router_g, b_router_g, w_router_e, b_router_e, w1, w3, w2, g_final):
    Bn, S, D = x.shape
    Lc = ctx.shape[1]
    depth = w_mod.shape[0]
    assert S % DT == 0 and Lc % TT == 0 and (Bn * Lc) % DT == 0 and Lc == TT
    n_lat = Bn * S
    n_ctx = Bn * Lc
    n_lat_tiles = n_lat // TT
    tiles_per_sample = S // TT

    cc = jnp.concatenate([c, c_ctx[None, :]], axis=0)
    mod_all = _modulation(cc, w_mod, b_mod)
    gf = g_final[None, :]

    lat, cx, ctx_off = x.reshape(n_lat, D), ctx.reshape(n_ctx, D), 0
    for l in range(depth):
        last = l == depth - 1
        lw = _layer_weights(l, w_in, g_norm1, ln_v_g, ln_v_b, w_sp, b_sp, w_gate_up, b_gate, g_gla, w_out,
                            g_norm2, w_router_g, b_router_g, w_router_e, b_router_e)
        mod3 = mod_all[l].reshape(Bn + 1, 1, 6 * D)
        a_out, qe, oi, sr, kv, dec = _pre_call(lat, cx, ctx_off, n_ctx // TT, mod3, lw, n_lat_tiles,
                                               tiles_per_sample, Bn)
        states = _scan_call(kv, dec, Bn, n_lat_tiles, tiles_per_sample)
        n_tok = n_lat if last else n_lat + n_ctx
        h1, hn2_rows, route, route_t, counts = _post_call(lat, cx, ctx_off, mod3, (a_out, qe, oi, sr), states, lw,
                                                          n_tok // TT, n_lat_tiles, tiles_per_sample, Bn)
        pad_starts, pad_ends, block_expert, n_used, n_buf_rows = _segment_layout(counts, n_tok)
        idx = _plan_call(route_t, pad_starts)
        xs = _dispatch_call(pad_ends, idx, hn2_rows, n_buf_rows)
        yb = _ffn_call(block_expert, n_used, xs, w1, w3, w2, l)
        h_all = _combine_call(idx, h1, route, mod3, gf, yb, n_lat // DT, S // DT, Bn, last)
        lat, cx, ctx_off = h_all, h_all, n_lat_tiles
    return h_all.reshape(Bn, S, D)
```

```python
import functools

import jax
import jax.numpy as jnp
from jax import lax
from jax.experimental import pallas as pl
from jax.experimental.pallas import tpu as pltpu

F32 = jnp.float32
BF16 = jnp.bfloat16
I32 = jnp.int32

EPS = 1e-6
LANE = 128
SUBLANE = 8

A_GROUPS = 4
A_GROUP_DIM = 128
A_WIDTH = A_GROUPS * A_GROUP_DIM
CHUNK_A = 128
B_HEADS = 4
B_DK = 64
B_DV = 128
B_KEY_WIDTH = B_HEADS * B_DK
B_WIDTH = B_HEADS * B_DV
GATE_RANK = 16
GATE_TAU = 16.0
CHUNK_B = 64
N_GROUPS = 8
EXPERTS_PER_GROUP = 8
N_EXPERTS = N_GROUPS * EXPERTS_PER_GROUP

TT = 256
GLA_CHUNK = 2 * CHUNK_B
CPT = TT // GLA_CHUNK
MOE_BLK = 256
DT = 512
ROW_SUB = 8

VMEM_LIMIT = 48 * 1024 * 1024


def _cparams(sem):
    return pltpu.CompilerParams(dimension_semantics=sem, vmem_limit_bytes=VMEM_LIMIT)


def _load_rows(ref, n):
    return jnp.concatenate([ref[pl.ds(s, n, stride=ROW_SUB), :] for s in range(ROW_SUB)], axis=1)


def _store_rows(ref, val, first=0):
    n = val.shape[0]
    for s in range(ROW_SUB):
        ref[pl.ds(first * ROW_SUB + s, n, stride=ROW_SUB), :] = val[:, s * LANE:(s + 1) * LANE]


def _mod_kernel(c_ref, w_ref, b_ref, o_ref):
    c = c_ref[...]
    s = (c * jax.nn.sigmoid(c)).astype(BF16)
    o_ref[0] = jnp.dot(s, w_ref[0].astype(BF16), preferred_element_type=F32) + b_ref[0]


def _modulation(cc, w_mod, b_mod):
    L, D, D6 = w_mod.shape
    R = cc.shape[0]
    tn = 1536
    return pl.pallas_call(
        _mod_kernel,
        grid=(L, D6 // tn),
        in_specs=[
            pl.BlockSpec((R, D), lambda l, j: (0, 0)),
            pl.BlockSpec((1, D, tn), lambda l, j: (l, 0, j)),
            pl.BlockSpec((1, 1, tn), lambda l, j: (l, 0, j)),
        ],
        out_specs=pl.BlockSpec((1, R, tn), lambda l, j: (l, 0, j)),
        out_shape=jax.ShapeDtypeStruct((L, R, D6), F32),
        compiler_params=_cparams(("arbitrary", "arbitrary")),
        name="modulation",
    )(cc, w_mod, b_mod.reshape(L, 1, D6))


def _select_tile(lat_ref, ctx_ref, n_lat_tiles):
    return jnp.where(pl.program_id(0) < n_lat_tiles, lat_ref[...], ctx_ref[...])


def _stream_specs(D, n_lat_tiles, ctx_off):
    lat = pl.BlockSpec((TT, D), lambda i: (jnp.minimum(i, n_lat_tiles - 1), 0))
    ctx = pl.BlockSpec((TT, D), lambda i: (ctx_off + jnp.maximum(i - n_lat_tiles, 0), 0))
    return lat, ctx


def _pre_kernel(lat_ref, ctx_ref, mod_ref, g1_ref, win_ref, lng_ref, lnb_ref, wsp_ref, bsp_ref, wg_ref, bg_ref,
                trif_ref, trib_ref, eye_ref,
                a_ref, qe_ref, oi_ref, sr_ref, kv_ref, dec_ref, *, n_lat_tiles):
    D = lat_ref.shape[1]
    h = _select_tile(lat_ref, ctx_ref, n_lat_tiles)
    mod = mod_ref[0]
    sh1 = mod[:, 0:D]
    sc1 = mod[:, D:2 * D]
    ms = jnp.mean(h * h, axis=-1, keepdims=True)
    hn = h * lax.rsqrt(ms + EPS) * g1_ref[...]
    hn = hn * (1.0 + sc1) + sh1
    z = jnp.dot(hn.astype(BF16), win_ref[...], preferred_element_type=F32)
    o_av = A_WIDTH
    o_q = 2 * A_WIDTH
    o_r = o_q + B_KEY_WIDTH
    o_k = o_r + B_WIDTH
    o_v = o_k + B_KEY_WIDTH
    o_g = o_v + B_WIDTH
    zu = z[:, 0:o_av]
    zv = z[:, o_av:o_q]
    q = z[:, o_q:o_r] * (B_DK ** -0.5)
    zr = z[:, o_r:o_k]
    k = z[:, o_k:o_v]
    vv = z[:, o_v:o_g]
    zg = z[:, o_g:o_g + LANE]

    u = jax.nn.gelu(zu)
    v = jax.nn.gelu(zv)
    s_groups = []
    for g in range(A_GROUPS):
        sl = slice(g * A_GROUP_DIM, (g + 1) * A_GROUP_DIM)
        vg = v[:, sl]
        mu = jnp.mean(vg, axis=-1, keepdims=True)
        dv = vg - mu
        var = jnp.mean(dv * dv, axis=-1, keepdims=True)
        vn = (dv * lax.rsqrt(var + EPS)) * lng_ref[:, sl] + lnb_ref[:, sl]
        vnb = vn.astype(BF16)
        rows = []
        for c in range(TT // CHUNK_A):
            rows.append(jnp.dot(wsp_ref[g], vnb[c * CHUNK_A:(c + 1) * CHUNK_A, :],
                                preferred_element_type=F32) + bsp_ref[:, sl])
        s_groups.append(jnp.concatenate(rows, axis=0))
    a_ref[...] = (u * jnp.concatenate(s_groups, axis=1)).astype(BF16)

    sr_ref[...] = (zr * jax.nn.sigmoid(zr)).astype(BF16)
    lg = jnp.dot(zg.astype(BF16), wg_ref[...], preferred_element_type=F32) + bg_ref[...]
    la = (jnp.minimum(lg, 0.0) - jnp.log1p(jnp.exp(-jnp.abs(lg)))) * (1.0 / GATE_TAU)
    la_hi = la.astype(BF16)
    la_lo = (la - la_hi.astype(F32)).astype(BF16)
    KW = B_KEY_WIDTH
    bf = (jnp.dot(trif_ref[...], la_hi[:, :KW], preferred_element_type=F32)
          + jnp.dot(trif_ref[...], la_lo[:, :KW], preferred_element_type=F32))
    bb = (jnp.dot(trib_ref[...], la_hi[:, KW:], preferred_element_type=F32)
          + jnp.dot(trib_ref[...], la_lo[:, KW:], preferred_element_type=F32))
    def chunk_rows(x, r):
        return jnp.concatenate(
            [jnp.broadcast_to(x[c * GLA_CHUNK + r:c * GLA_CHUNK + r + 1, :], (GLA_CHUNK, KW)) for c in range(CPT)], axis=0)

    blf = chunk_rows(bf, GLA_CHUNK - 1)
    blb = chunk_rows(bb, 0)
    bmf = chunk_rows(bf, CHUNK_B - 1)
    bmb = chunk_rows(bb, CHUNK_B)
    qe_ref[...] = jnp.concatenate([q * jnp.exp(bf), q * jnp.exp(bb)], axis=1).astype(BF16)
    kd_f = k * jnp.exp(blf - bf)
    kd_b = k * jnp.exp(blb - bb)
    qe_f = q * jnp.exp(bf - bmf)
    qe_b = q * jnp.exp(bb - bmb)
    ke_f = (k * jnp.exp(bmf - bf)).astype(BF16)
    ke_b = (k * jnp.exp(bmb - bb)).astype(BF16)
    nt = (((1,), (1,)), ((), ()))
    dec_rows = ([jnp.exp(bf[c * GLA_CHUNK + GLA_CHUNK - 1:(c + 1) * GLA_CHUNK, :]) for c in range(CPT)]
                + [jnp.exp(bb[c * GLA_CHUNK:c * GLA_CHUNK + 1, :]) for c in range(CPT)]
                + [jnp.zeros((LANE - 2 * CPT, KW), F32)])
    dec_pad = jnp.concatenate(dec_rows, axis=0)
    dec_hi = dec_pad.astype(BF16)
    dec_lo = (dec_pad - dec_hi.astype(F32)).astype(BF16)
    dec_ref[0] = (lax.dot_general(eye_ref[...], dec_hi, nt, preferred_element_type=F32)
                  + lax.dot_general(eye_ref[...], dec_lo, nt, preferred_element_type=F32))

    row = lax.broadcasted_iota(I32, (TT, TT), 0)
    col = lax.broadcasted_iota(I32, (TT, TT), 1)
    chunk_shift = GLA_CHUNK.bit_length() - 1
    same = lax.shift_right_logical(row, chunk_shift) == lax.shift_right_logical(col, chunk_shift)
    m_f = same & (row >= col)
    m_b = same & (row <= col)
    lane_head = lax.shift_right_logical(lax.broadcasted_iota(I32, (1, KW), 1), B_DK.bit_length() - 1)
    tok_chunk = lax.shift_right_logical(lax.broadcasted_iota(I32, (1, TT), 1), chunk_shift)
    vb = vv.astype(BF16)
    kdT_f = kd_f.T
    kdT_b = kd_b.T
    oi_heads = []
    for hd in range(B_HEADS):
        hm = lane_head == hd
        qf = jnp.where(hm, qe_f, 0.0).astype(BF16)
        qb = jnp.where(hm, qe_b, 0.0).astype(BF16)
        att_f = lax.dot_general(qf, ke_f, nt, preferred_element_type=F32)
        att_b = lax.dot_general(qb, ke_b, nt, preferred_element_type=F32)
        att = jnp.where(m_f, att_f, 0.0) + jnp.where(m_b, att_b, 0.0)
        v_h = vb[:, hd * B_DV:(hd + 1) * B_DV]
        oi_heads.append(jnp.dot(att.astype(BF16), v_h, preferred_element_type=F32))
        parts = []
        for kdT in (kdT_f, kdT_b):
            kh = kdT[hd * B_DK:(hd + 1) * B_DK, :]
            for c in range(CPT):
                parts.append(jnp.where(tok_chunk == c, kh, 0.0))
        lhs = jnp.concatenate(parts, axis=0).astype(BF16)
        kv_ref[0, hd] = jnp.dot(lhs, v_h, preferred_element_type=F32)
    oi_ref[...] = jnp.concatenate(oi_heads, axis=1)


def _pre_call(lat, ctx, ctx_off, n_ctx_tiles, mod3, lw, n_lat_tiles, tiles_per_sample, n_mod_ctx):
    D = lat.shape[1]
    n_tiles = n_lat_tiles + n_ctx_tiles
    N = n_tiles * TT
    lat_spec, ctx_spec = _stream_specs(D, n_lat_tiles, ctx_off)
    DINP = lw["w_in"].shape[1]
    KVR = 2 * CPT * B_DK

    def mod_map(i):
        return (jnp.where(i < n_lat_tiles, i // tiles_per_sample, n_mod_ctx), 0, 0)

    const2 = lambda i: (0, 0)
    const3 = lambda i: (0, 0, 0)
    tile2 = lambda i: (i, 0)
    return pl.pallas_call(
        functools.partial(_pre_kernel, n_lat_tiles=n_lat_tiles),
        grid=(n_tiles,),
        in_specs=[
            lat_spec,
            ctx_spec,
            pl.BlockSpec((1, 1, 6 * D), mod_map),
            pl.BlockSpec((1, D), const2),
            pl.BlockSpec((D, DINP), const2),
            pl.BlockSpec((1, A_WIDTH), const2),
            pl.BlockSpec((1, A_WIDTH), const2),
            pl.BlockSpec((A_GROUPS, CHUNK_A, CHUNK_A), const3),
            pl.BlockSpec((CHUNK_A, A_WIDTH), const2),
            pl.BlockSpec((LANE, 2 * B_KEY_WIDTH), const2),
            pl.BlockSpec((1, 2 * B_KEY_WIDTH), const2),
            pl.BlockSpec((TT, TT), const2),
            pl.BlockSpec((TT, TT), const2),
            pl.BlockSpec((B_KEY_WIDTH, B_KEY_WIDTH), const2),
        ],
        out_specs=[
            pl.BlockSpec((TT, A_WIDTH), tile2),
            pl.BlockSpec((TT, 2 * B_KEY_WIDTH), tile2),
            pl.BlockSpec((TT, B_WIDTH), tile2),
            pl.BlockSpec((TT, B_WIDTH), tile2),
            pl.BlockSpec((1, B_HEADS, KVR, B_DV), lambda i: (i, 0, 0, 0)),
            pl.BlockSpec((1, B_KEY_WIDTH, LANE), lambda i: (i, 0, 0)),
        ],
        out_shape=[
            jax.ShapeDtypeStruct((N, A_WIDTH), BF16),
            jax.ShapeDtypeStruct((N, 2 * B_KEY_WIDTH), BF16),
            jax.ShapeDtypeStruct((N, B_WIDTH), F32),
            jax.ShapeDtypeStruct((N, B_WIDTH), BF16),
            jax.ShapeDtypeStruct((n_tiles, B_HEADS, KVR, B_DV), F32),
            jax.ShapeDtypeStruct((n_tiles, B_KEY_WIDTH, LANE), F32),
        ],
        compiler_params=_cparams(("arbitrary",)),
        name="mix_pre",
    )(lat, ctx, mod3, lw["g_norm1"], lw["w_in"], lw["ln_g"], lw["ln_b"], lw["w_sp"], lw["b_sp"],
      lw["w_gate"], lw["b_gate"], lw["tri_f"], lw["tri_b"], lw["eye_k"])


def _scan_kernel(kvc_ref, kvl_ref, dcc_ref, dcl_ref, sc_ref, sl_ref):
    n_lat = kvl_ref.shape[0]
    fwd = [(kvc_ref, dcc_ref, sc_ref, 0, c) for c in range(CPT)]
    fwd += [(kvl_ref, dcl_ref, sl_ref, t, c) for t in range(n_lat) for c in range(CPT)]
    bwd = [(kvc_ref, dcc_ref, sc_ref, 0, c) for c in reversed(range(CPT))]
    bwd += [(kvl_ref, dcl_ref, sl_ref, t, c) for t in reversed(range(n_lat)) for c in reversed(range(CPT))]
    for hd in range(B_HEADS):
        for d, seq in enumerate((fwd, bwd)):
            s = jnp.zeros((B_DK, B_DV), F32)
            for kv_ref, dc_ref, out_ref, t, c in seq:
                r0 = (d * CPT + c) * B_DK
                out_ref[t, hd, r0:r0 + B_DK, :] = s.astype(BF16)
                dcol = dc_ref[t, hd * B_DK:(hd + 1) * B_DK, d * CPT + c:d * CPT + c + 1]
                s = dcol * s + kv_ref[t, hd, r0:r0 + B_DK, :]


def _scan_call(kv, dec, n_samples, n_lat_tiles, tiles_per_sample):
    n_tiles, _, KVR, _ = kv.shape
    kv_c = pl.BlockSpec((1, B_HEADS, KVR, B_DV), lambda b: (n_lat_tiles + b, 0, 0, 0))
    kv_l = pl.BlockSpec((tiles_per_sample, B_HEADS, KVR, B_DV), lambda b: (b, 0, 0, 0))
    dc_c = pl.BlockSpec((1, B_KEY_WIDTH, LANE), lambda b: (n_lat_tiles + b, 0, 0))
    dc_l = pl.BlockSpec((tiles_per_sample, B_KEY_WIDTH, LANE), lambda b: (b, 0, 0))
    s_ctx, s_lat = pl.pallas_call(
        _scan_kernel,
        grid=(n_samples,),
        in_specs=[kv_c, kv_l, dc_c, dc_l],
        out_specs=[
            pl.BlockSpec((1, B_HEADS, KVR, B_DV), lambda b: (b, 0, 0, 0)),
            pl.BlockSpec((tiles_per_sample, B_HEADS, KVR, B_DV), lambda b: (b, 0, 0, 0)),
        ],
        out_shape=[
            jax.ShapeDtypeStruct((n_samples, B_HEADS, KVR, B_DV), BF16),
            jax.ShapeDtypeStruct((n_lat_tiles, B_HEADS, KVR, B_DV), BF16),
        ],
        compiler_params=_cparams(("arbitrary",)),
        name="gla_scan",
    )(kv, kv, dec, dec)
    return jnp.concatenate([s_lat, s_ctx], axis=0)


def _post_kernel(lat_ref, ctx_ref, mod_ref, a_ref, qe_ref, oi_ref, sr_ref, st_ref, ggla_ref, wout_ref, g2_ref,
                 wr_ref, br_ref, ustr_ref,
                 h1_ref, hn2_ref, route_ref, routet_ref, cnt_ref,
                 rhs_scr, cnt_scr, *, n_lat_tiles):
    D = lat_ref.shape[1]
    i = pl.program_id(0)

    @pl.when(i == 0)
    def _():
        rhs_scr[...] = jnp.zeros(rhs_scr.shape, rhs_scr.dtype)
        cnt_scr[...] = jnp.zeros(cnt_scr.shape, cnt_scr.dtype)

    qe = qe_ref[...]
    o_parts = []
    for c in range(CPT):
        for d in range(2):
            for hd in range(B_HEADS):
                r0 = d * B_KEY_WIDTH + hd * B_DK
                s0 = (d * CPT + c) * B_DK
                rhs_scr[c, r0:r0 + B_DK, hd * B_DV:(hd + 1) * B_DV] = st_ref[0, hd, s0:s0 + B_DK, :]
        o_parts.append(jnp.dot(qe[c * GLA_CHUNK:(c + 1) * GLA_CHUNK, :], rhs_scr[c], preferred_element_type=F32))
    o = oi_ref[...] + jnp.concatenate(o_parts, axis=0)

    heads = []
    for hd in range(B_HEADS):
        sl = slice(hd * B_DV, (hd + 1) * B_DV)
        oh = o[:, sl]
        msq = jnp.mean(oh * oh, axis=-1, keepdims=True)
        heads.append(oh * lax.rsqrt(msq + EPS) * ggla_ref[:, sl])
    b_out = jnp.concatenate(heads, axis=1) * sr_ref[...].astype(F32)
    mix_in = jnp.concatenate([a_ref[...], b_out.astype(BF16)], axis=1)
    mix = jnp.dot(mix_in, wout_ref[...], preferred_element_type=F32)

    mod = mod_ref[0]
    g1 = mod[:, 2 * D:3 * D]
    sh2 = mod[:, 3 * D:4 * D]
    sc2 = mod[:, 4 * D:5 * D]
    h_in = _select_tile(lat_ref, ctx_ref, n_lat_tiles)
    nt = (((1,), (1,)), ((), ()))
    neg = jnp.float32(-3.0e38)
    big = jnp.float32(1.0e9)
    sub = lax.broadcasted_iota(I32, (EXPERTS_PER_GROUP, LANE), 0).astype(F32)
    route_parts = []
    for hf in range(TT // LANE):
        rs = slice(hf * LANE, (hf + 1) * LANE)
        h1 = h_in[rs, :] + g1 * mix[rs, :]
        h1_ref[rs, :] = h1
        msq = jnp.mean(h1 * h1, axis=-1, keepdims=True)
        hn2 = h1 * lax.rsqrt(msq + EPS) * g2_ref[...]
        hn2 = hn2 * (1.0 + sc2) + sh2
        _store_rows(hn2_ref, hn2, hf * LANE)

        lgt = lax.dot_general(wr_ref[...], hn2.astype(BF16), nt, preferred_element_type=F32) + br_ref[...]
        gl = lgt[N_EXPERTS:N_EXPERTS + N_GROUPS, :]
        gmax = jnp.max(gl, axis=0, keepdims=True)
        p_top = 1.0 / jnp.sum(jnp.exp(gl - gmax), axis=0, keepdims=True)
        gidx = jnp.min(jnp.where(gl == gmax, sub, big), axis=0, keepdims=True)
        el = jnp.zeros((EXPERTS_PER_GROUP, LANE), F32)
        for g in range(N_GROUPS):
            el = el + jnp.where(gidx == float(g), lgt[g * EXPERTS_PER_GROUP:(g + 1) * EXPERTS_PER_GROUP, :], 0.0)
        m1 = jnp.max(el, axis=0, keepdims=True)
        i1 = jnp.min(jnp.where(el == m1, sub, big), axis=0, keepdims=True)
        el2 = jnp.where(sub == i1, neg, el)
        m2 = jnp.max(el2, axis=0, keepdims=True)
        i2 = jnp.min(jnp.where(el2 == m2, sub, big), axis=0, keepdims=True)
        t = jnp.exp(m2 - m1)
        route_parts.append((gidx * float(EXPERTS_PER_GROUP) + i1, gidx * float(EXPERTS_PER_GROUP) + i2,
                            p_top / (1.0 + t), p_top * t / (1.0 + t)))
    e1, e2, gate1, gate2 = (jnp.concatenate(p, axis=1) for p in zip(*route_parts))

    rowf = lax.broadcasted_iota(I32, (LANE, TT), 0).astype(F32)
    sel1 = rowf == e1
    sel2 = rowf == (e2 + float(N_EXPERTS))
    oh = jnp.where(sel1, 1.0, 0.0) + jnp.where(sel2, 1.0, 0.0)
    prefix = jnp.dot(oh.astype(BF16), ustr_ref[...], preferred_element_type=F32)
    tot = jnp.sum(oh, axis=1, keepdims=True)
    tot_sw = jnp.concatenate([tot[N_EXPERTS:, :], tot[:N_EXPERTS, :]], axis=0)
    base = cnt_scr[...]
    second = lax.broadcasted_iota(I32, (LANE, 1), 0) >= N_EXPERTS
    val = prefix + (base + jnp.where(second, tot_sw, 0.0))
    r1 = jnp.sum(jnp.where(sel1, val, 0.0), axis=0, keepdims=True)
    r2 = jnp.sum(jnp.where(sel2, val, 0.0), axis=0, keepdims=True)
    new_cnt = base + tot + tot_sw
    cnt_scr[...] = new_cnt
    cnt_ref[...] = new_cnt
    zero = jnp.zeros_like(e1)
    route_t = jnp.concatenate([e1, e2, gate1, gate2, r1, r2, zero, zero], axis=0)
    routet_ref[...] = route_t
    route_ref[...] = jnp.concatenate([route_t, jnp.zeros((LANE - SUBLANE, TT), F32)], axis=0).T


def _post_call(lat, ctx, ctx_off, mod3, pre_outs, states, lw, n_post_tiles, n_lat_tiles, tiles_per_sample, n_mod_ctx):
    a_out, qe, oi, sr = pre_outs
    D = lat.shape[1]
    KVR = 2 * CPT * B_DK
    NP = n_post_tiles * TT
    lat_spec, ctx_spec = _stream_specs(D, n_lat_tiles, ctx_off)

    def mod_map(i):
        return (jnp.where(i < n_lat_tiles, i // tiles_per_sample, n_mod_ctx), 0, 0)

    const2 = lambda i: (0, 0)
    tile2 = lambda i: (i, 0)
    return pl.pallas_call(
        functools.partial(_post_kernel, n_lat_tiles=n_lat_tiles),
        grid=(n_post_tiles,),
        in_specs=[
            lat_spec,
            ctx_spec,
            pl.BlockSpec((1, 1, 6 * D), mod_map),
            pl.BlockSpec((TT, A_WIDTH), tile2),
            pl.BlockSpec((TT, 2 * B_KEY_WIDTH), tile2),
            pl.BlockSpec((TT, B_WIDTH), tile2),
            pl.BlockSpec((TT, B_WIDTH), tile2),
            pl.BlockSpec((1, B_HEADS, KVR, B_DV), lambda i: (i, 0, 0, 0)),
            pl.BlockSpec((1, B_WIDTH), const2),
            pl.BlockSpec((D, D), const2),
            pl.BlockSpec((1, D), const2),
            pl.BlockSpec((LANE, D), const2),
            pl.BlockSpec((LANE, 1), const2),
            pl.BlockSpec((TT, TT), const2),
        ],
        out_specs=[
            pl.BlockSpec((TT, D), tile2),
            pl.BlockSpec((TT * ROW_SUB, LANE), tile2),
            pl.BlockSpec((TT, LANE), tile2),
            pl.BlockSpec((SUBLANE, TT), lambda i: (0, i)),
            pl.BlockSpec((LANE, 1), const2),
        ],
        out_shape=[
            jax.ShapeDtypeStruct((NP, D), F32),
            jax.ShapeDtypeStruct((NP * ROW_SUB, LANE), F32),
            jax.ShapeDtypeStruct((NP, LANE), F32),
            jax.ShapeDtypeStruct((SUBLANE, NP), F32),
            jax.ShapeDtypeStruct((LANE, 1), F32),
        ],
        scratch_shapes=[
            pltpu.VMEM((CPT, 2 * B_KEY_WIDTH, B_WIDTH), BF16),
            pltpu.VMEM((LANE, 1), F32),
        ],
        compiler_params=_cparams(("arbitrary",)),
        name="mix_post",
    )(lat, ctx, mod3, a_out, qe, oi, sr, states, lw["g_gla"], lw["w_out"], lw["g_norm2"],
      lw["w_router"], lw["b_router"], lw["u_strict"])


def _plan_kernel(rt_ref, ps_ref, idx_ref):
    sub = lax.broadcasted_iota(I32, (N_EXPERTS, LANE), 0).astype(F32)
    ps = ps_ref[...]
    segs = DT // LANE
    for k in range(2):
        for seg in range(segs):
            sl = slice(seg * LANE, (seg + 1) * LANE)
            e = rt_ref[k:k + 1, sl]
            r = rt_ref[4 + k:5 + k, sl]
            base = jnp.sum(jnp.where(sub == e, ps, 0.0), axis=0, keepdims=True)
            idx_ref[0, k * segs + seg:k * segs + seg + 1, :] = ((base + r) * float(ROW_SUB)).astype(I32)


def _plan_call(route_t, pad_starts):
    n_dt = route_t.shape[1] // DT
    return pl.pallas_call(
        _plan_kernel,
        grid=(n_dt,),
        in_specs=[
            pl.BlockSpec((SUBLANE, DT), lambda i: (0, i)),
            pl.BlockSpec((N_EXPERTS, 1), lambda i: (0, 0)),
        ],
        out_specs=pl.BlockSpec((1, 2 * DT // LANE, LANE), lambda i: (i, 0, 0)),
        out_shape=jax.ShapeDtypeStruct((n_dt, 2 * DT // LANE, LANE), I32),
        compiler_params=_cparams(("arbitrary",)),
        name="moe_plan",
    )(route_t, pad_starts.astype(F32)[:, None])


def _row_loop(n_rows, fn):
    unroll = 8

    def body(jo, carry):
        for u in range(unroll):
            fn(jo * unroll + u, jo, u)
        return carry

    lax.fori_loop(0, n_rows // unroll, body, 0)


def _tile_indices(idx_smem, jo, u):
    r = lax.shift_right_logical(jo, 4)
    c = (jo & (LANE // 8 - 1)) * 8 + u
    d0 = pl.multiple_of(idx_smem[r, c], ROW_SUB)
    d1 = pl.multiple_of(idx_smem[r + DT // LANE, c], ROW_SUB)
    return d0, d1


def _dispatch_kernel(pe_ref, idx_hbm, x_ref, xs_out, idx_smem, zbuf, sem_i, sem_o, sem_z):
    i = pl.program_id(0)
    cp = pltpu.make_async_copy(idx_hbm.at[i], idx_smem, sem_i)
    cp.start()
    blk_rows = MOE_BLK * ROW_SUB

    @pl.when(i == 0)
    def _():
        zbuf[...] = jnp.zeros(zbuf.shape, zbuf.dtype)
        for e in range(N_EXPERTS):
            start = pl.multiple_of(jnp.maximum(pe_ref[e] - MOE_BLK, 0) * ROW_SUB, blk_rows)
            pltpu.make_async_copy(zbuf, xs_out.at[pl.ds(start, blk_rows), :], sem_z).start()
        for e in range(N_EXPERTS):
            pltpu.make_async_copy(zbuf, xs_out.at[pl.ds(0, blk_rows), :], sem_z).wait()
        first_free = pe_ref[N_EXPERTS - 1] // MOE_BLK
        n_blocks = xs_out.shape[0] // blk_rows

        def clear(b, carry):
            start = pl.multiple_of(b * blk_rows, blk_rows)
            pltpu.make_async_copy(zbuf, xs_out.at[pl.ds(start, blk_rows), :], sem_z).start()
            return carry

        def clear_wait(b, carry):
            pltpu.make_async_copy(zbuf, xs_out.at[pl.ds(0, blk_rows), :], sem_z).wait()
            return carry

        lax.fori_loop(first_free, n_blocks, clear, 0)
        lax.fori_loop(first_free, n_blocks, clear_wait, 0)

    cp.wait()

    def issue(j, jo, u):
        d0, d1 = _tile_indices(idx_smem, jo, u)
        src = x_ref.at[pl.ds(pl.multiple_of(j * ROW_SUB, ROW_SUB), ROW_SUB), :]
        pltpu.make_async_copy(src, xs_out.at[pl.ds(d0, ROW_SUB), :], sem_o).start(priority=0)
        pltpu.make_async_copy(src, xs_out.at[pl.ds(d1, ROW_SUB), :], sem_o).start(priority=1)

    _row_loop(DT, issue)
    for _ in range(2):
        pltpu.make_async_copy(x_ref, xs_out.at[pl.ds(0, DT * ROW_SUB), :], sem_o).wait()


def _dispatch_call(pad_ends, idx, x_rows, n_buf_rows):
    n_tiles = x_rows.shape[0] // (DT * ROW_SUB)
    grid_spec = pltpu.PrefetchScalarGridSpec(
        num_scalar_prefetch=1,
        grid=(n_tiles,),
        in_specs=[
            pl.BlockSpec(memory_space=pl.ANY),
            pl.BlockSpec((DT * ROW_SUB, LANE), lambda i, pe: (i, 0)),
        ],
        out_specs=pl.BlockSpec(memory_space=pl.ANY),
        scratch_shapes=[
            pltpu.SMEM((2 * DT // LANE, LANE), I32),
            pltpu.VMEM((MOE_BLK * ROW_SUB, LANE), F32),
            pltpu.SemaphoreType.DMA,
            pltpu.SemaphoreType.DMA,
            pltpu.SemaphoreType.DMA,
        ],
    )
    return pl.pallas_call(
        _dispatch_kernel,
        grid_spec=grid_spec,
        out_shape=jax.ShapeDtypeStruct((n_buf_rows * ROW_SUB, LANE), F32),
        compiler_params=_cparams(("arbitrary",)),
        name="moe_dispatch",
    )(pad_ends, idx, x_rows)


def _ffn_kernel(be_ref, nu_ref, ws_ref, nx_ref, x_ref, w1_hbm, w3_hbm, w2_hbm, y_ref,
                w1f, w3f, w2f, w1b, w3b, w2b, sem_w, *, layer):
    i = pl.program_id(0)
    nu = nu_ref[0]

    def weight_copies(e, s):
        return (pltpu.make_async_copy(w1_hbm.at[layer, e], w1f.at[s], sem_w.at[s]),
                pltpu.make_async_copy(w3_hbm.at[layer, e], w3f.at[s], sem_w.at[s]),
                pltpu.make_async_copy(w2_hbm.at[layer, e], w2f.at[s], sem_w.at[s]))

    @pl.when(i < nu)
    def _():
        e = be_ref[i]
        s = ws_ref[i]
        first_of_run = (i == 0) | (e != be_ref[jnp.maximum(i - 1, 0)])

        @pl.when(i == 0)
        def _():
            for cp in weight_copies(e, s):
                cp.start()

        @pl.when(first_of_run)
        def _():
            for cp in weight_copies(e, s):
                cp.wait()
            nxt = nx_ref[i]

            @pl.when(nxt >= 0)
            def _():
                for cp in weight_copies(nxt, 1 - s):
                    cp.start(priority=1)

            w1b[...] = w1f[s].astype(BF16)
            w3b[...] = w3f[s].astype(BF16)
            w2b[...] = w2f[s].astype(BF16)

        x = _load_rows(x_ref, MOE_BLK).astype(BF16)
        h1 = jnp.dot(x, w1b[...], preferred_element_type=F32)
        h3 = jnp.dot(x, w3b[...], preferred_element_type=F32)
        hh = (h1 * jax.nn.sigmoid(h1) * h3).astype(BF16)
        y = jnp.dot(hh, w2b[...], preferred_element_type=F32)
        _store_rows(y_ref, y)

    @pl.when(i >= nu)
    def _():
        y_ref[...] = jnp.zeros(y_ref.shape, y_ref.dtype)


def _ffn_call(block_expert, n_used, w_slot, next_expert, xs, w1, w3, w2, l):
    blk_rows = MOE_BLK * ROW_SUB
    NB = xs.shape[0] // blk_rows
    _, _, D, DE = w1.shape
    grid_spec = pltpu.PrefetchScalarGridSpec(
        num_scalar_prefetch=4,
        grid=(NB,),
        in_specs=[
            pl.BlockSpec((blk_rows, LANE), lambda i, be, nu, ws, nx: (jnp.minimum(i, nu[0] - 1), 0)),
            pl.BlockSpec(memory_space=pl.ANY),
            pl.BlockSpec(memory_space=pl.ANY),
            pl.BlockSpec(memory_space=pl.ANY),
        ],
        out_specs=pl.BlockSpec((blk_rows, LANE), lambda i, be, nu, ws, nx: (i, 0)),
        scratch_shapes=[
            pltpu.VMEM((2, D, DE), F32),
            pltpu.VMEM((2, D, DE), F32),
            pltpu.VMEM((2, DE, D), F32),
            pltpu.VMEM((D, DE), BF16),
            pltpu.VMEM((D, DE), BF16),
            pltpu.VMEM((DE, D), BF16),
            pltpu.SemaphoreType.DMA((2,)),
        ],
    )
    return pl.pallas_call(
        functools.partial(_ffn_kernel, layer=l),
        grid_spec=grid_spec,
        out_shape=jax.ShapeDtypeStruct(xs.shape, F32),
        compiler_params=_cparams(("arbitrary",)),
        name="moe_ffn",
    )(block_expert, n_used, w_slot, next_expert, xs, w1, w3, w2)


def _combine_kernel(idx_hbm, h1_ref, route_ref, mod_ref, gf_ref, y_hbm, o_ref, idx_smem, rows0, rows1, sem_i, sem_g,
                    *, final):
    D = h1_ref.shape[1]
    i = pl.program_id(0)
    n = pl.num_programs(0)
    slot = i & 1
    nxt = 1 - slot

    def idx_copy(tile, s):
        return pltpu.make_async_copy(idx_hbm.at[tile], idx_smem.at[s], sem_i.at[s])

    def gather(s):
        def issue(j, jo, u):
            d0, d1 = _tile_indices(idx_smem.at[s], jo, u)
            dst = pl.ds(pl.multiple_of(j * ROW_SUB, ROW_SUB), ROW_SUB)
            pltpu.make_async_copy(y_hbm.at[pl.ds(d0, ROW_SUB), :], rows0.at[s, dst, :], sem_g.at[s]).start(priority=0)
            pltpu.make_async_copy(y_hbm.at[pl.ds(d1, ROW_SUB), :], rows1.at[s, dst, :], sem_g.at[s]).start(priority=1)

        _row_loop(DT, issue)

    @pl.when(i == 0)
    def _():
        cp = idx_copy(0, 0)
        cp.start()
        cp.wait()
        gather(0)

        @pl.when(n > 1)
        def _():
            idx_copy(1, 1).start()

    @pl.when(i + 1 < n)
    def _():
        idx_copy(0, nxt).wait()
        gather(nxt)

    @pl.when(i + 2 < n)
    def _():
        idx_copy(i + 2, slot).start()

    for rows in (rows0, rows1):
        pltpu.make_async_copy(y_hbm.at[pl.ds(0, DT * ROW_SUB), :], rows.at[slot], sem_g.at[slot]).wait()

    route = route_ref[...]
    gate1 = route[:, 2:3]
    gate2 = route[:, 3:4]
    y = gate1 * _load_rows(rows0.at[slot], DT) + gate2 * _load_rows(rows1.at[slot], DT)
    g2 = mod_ref[0][:, 5 * D:6 * D]
    out = h1_ref[...] + g2 * y
    if final:
        msq = jnp.mean(out * out, axis=-1, keepdims=True)
        out = out * lax.rsqrt(msq + EPS) * gf_ref[...]
    o_ref[...] = out


def _combine_call(idx, h1, route, mod3, g_final, yb, n_lat_dt, dt_per_sample, n_mod_ctx, final):
    N, D = h1.shape
    n_tiles = N // DT

    def mod_map(i):
        return (jnp.where(i < n_lat_dt, i // dt_per_sample, n_mod_ctx), 0, 0)

    return pl.pallas_call(
        functools.partial(_combine_kernel, final=final),
        grid=(n_tiles,),
        in_specs=[
            pl.BlockSpec(memory_space=pl.ANY),
            pl.BlockSpec((DT, D), lambda i: (i, 0)),
            pl.BlockSpec((DT, LANE), lambda i: (i, 0)),
            pl.BlockSpec((1, 1, 6 * D), mod_map),
            pl.BlockSpec((1, D), lambda i: (0, 0)),
            pl.BlockSpec(memory_space=pl.ANY),
        ],
        out_specs=pl.BlockSpec((DT, D), lambda i: (i, 0)),
        out_shape=jax.ShapeDtypeStruct((N, D), F32),
        scratch_shapes=[
            pltpu.SMEM((2, 2 * DT // LANE, LANE), I32),
            pltpu.VMEM((2, DT * ROW_SUB, LANE), F32),
            pltpu.VMEM((2, DT * ROW_SUB, LANE), F32),
            pltpu.SemaphoreType.DMA((2,)),
            pltpu.SemaphoreType.DMA((2,)),
        ],
        compiler_params=_cparams(("arbitrary",)),
        name="moe_combine",
    )(idx, h1, route, mod3, g_final, yb)


def _layer_weights(l, w_in, g_norm1, ln_v_g, ln_v_b, w_sp, b_sp, w_gate_up, b_gate, g_gla, w_out, g_norm2,
                   w_router_g, b_router_g, w_router_e, b_router_e):
    D = w_in.shape[1]
    d_in = w_in.shape[2]
    pad = (-d_in) % LANE
    w_in_p = jnp.pad(w_in[l], ((0, 0), (0, pad))).astype(BF16)
    KW = B_KEY_WIDTH
    wg = jnp.zeros((LANE, 2 * KW), F32)
    wg = wg.at[0:GATE_RANK, 0:KW].set(w_gate_up[l, 0])
    wg = wg.at[GATE_RANK:2 * GATE_RANK, KW:2 * KW].set(w_gate_up[l, 1])
    bg = jnp.concatenate([b_gate[l, 0], b_gate[l, 1]])[None, :]
    bsp = jnp.repeat(b_sp[l].T, A_GROUP_DIM, axis=1)
    wr = jnp.zeros((LANE, D), F32)
    wr = wr.at[0:N_EXPERTS, :].set(w_router_e[l].T)
    wr = wr.at[N_EXPERTS:N_EXPERTS + N_GROUPS, :].set(w_router_g[l].T)
    br = jnp.zeros((LANE, 1), F32)
    br = br.at[0:N_EXPERTS, 0].set(b_router_e[l])
    br = br.at[N_EXPERTS:N_EXPERTS + N_GROUPS, 0].set(b_router_g[l])
    r = jnp.arange(TT, dtype=I32)
    same = (r[:, None] // GLA_CHUNK) == (r[None, :] // GLA_CHUNK)
    tri_f = (same & (r[:, None] >= r[None, :])).astype(BF16)
    tri_b = (same & (r[:, None] <= r[None, :])).astype(BF16)
    u_strict = (r[:, None] < r[None, :]).astype(BF16)
    eye_k = jnp.eye(B_KEY_WIDTH, dtype=BF16)
    return dict(
        w_in=w_in_p, g_norm1=g_norm1[l][None, :], ln_g=ln_v_g[l][None, :], ln_b=ln_v_b[l][None, :],
        w_sp=w_sp[l].astype(BF16), b_sp=bsp, w_gate=wg.astype(BF16), b_gate=bg,
        g_gla=g_gla[l][None, :], w_out=w_out[l].astype(BF16), g_norm2=g_norm2[l][None, :],
        w_router=wr.astype(BF16), b_router=br, tri_f=tri_f, tri_b=tri_b, u_strict=u_strict, eye_k=eye_k)


def _segment_layout(counts_col, n_tokens):
    counts = counts_col[:N_EXPERTS, 0].astype(I32)
    padded = (counts + MOE_BLK - 1) // MOE_BLK * MOE_BLK
    pad_ends = jnp.cumsum(padded)
    pad_starts = pad_ends - padded
    n_blocks = (n_tokens * 2 + N_EXPERTS * (MOE_BLK - 1)) // MOE_BLK + 1
    block_start = jnp.arange(n_blocks, dtype=I32) * MOE_BLK
    block_expert = jnp.sum((pad_ends[None, :] <= block_start[:, None]).astype(I32), axis=1)
    block_expert = jnp.minimum(block_expert, N_EXPERTS - 1)
    n_used = (pad_ends[-1] // MOE_BLK).astype(I32)[None]
    blk_id = jnp.arange(n_blocks, dtype=I32)
    prev = jnp.concatenate([block_expert[:1], block_expert[:-1]])
    first = ((blk_id == 0) | (block_expert != prev)) & (blk_id < n_used[0])
    w_slot = (jnp.cumsum(first.astype(I32)) - 1) & 1
    e_id = jnp.arange(N_EXPERTS, dtype=I32)
    later_used = (e_id[None, :] > e_id[:, None]) & (padded[None, :] > 0)
    next_used = jnp.min(jnp.where(later_used, e_id[None, :], N_EXPERTS), axis=1)
    next_used = jnp.where(next_used >= N_EXPERTS, -1, next_used)
    next_expert = jnp.sum(jnp.where(block_expert[:, None] == e_id[None, :], next_used[None, :], 0), axis=1).astype(I32)
    return pad_starts, pad_ends, block_expert, n_used, w_slot.astype(I32), next_expert, n_blocks * MOE_BLK


def kernel(x, c, ctx, c_ctx, w_mod, b_mod, g_norm1, w_in, ln_v_g, ln_v_b, w_sp, b_sp, w_gate_up, b_gate, g_gla,
           w_out, g_norm2, w_router_g, b_router_g, w_router_e, b_router_e, w1, w3, w2, g_final):
    Bn, S, D = x.shape
    Lc = ctx.shape[1]
    depth = w_mod.shape[0]
    assert S % DT == 0 and Lc % TT == 0 and (Bn * Lc) % DT == 0 and Lc == TT
    n_lat = Bn * S
    n_ctx = Bn * Lc
    n_lat_tiles = n_lat // TT
    tiles_per_sample = S // TT

    cc = jnp.concatenate([c, c_ctx[None, :]], axis=0)
    mod_all = _modulation(cc, w_mod, b_mod)
    gf = g_final[None, :]

    lat, cx, ctx_off = x.reshape(n_lat, D), ctx.reshape(n_ctx, D), 0
    for l in range(depth):
        last = l == depth - 1
        lw = _layer_weights(l, w_in, g_norm1, ln_v_g, ln_v_b, w_sp, b_sp, w_gate_up, b_gate, g_gla, w_out,
                            g_norm2, w_router_g, b_router_g, w_router_e, b_router_e)
        mod3 = mod_all[l].reshape(Bn + 1, 1, 6 * D)
        a_out, qe, oi, sr, kv, dec = _pre_call(lat, cx, ctx_off, n_ctx // TT, mod3, lw, n_lat_tiles,
                                               tiles_per_sample, Bn)
        states = _scan_call(kv, dec, Bn, n_lat_tiles, tiles_per_sample)
        n_tok = n_lat if last else n_lat + n_ctx
        h1, hn2_rows, route, route_t, counts = _post_call(lat, cx, ctx_off, mod3, (a_out, qe, oi, sr), states, lw,
                                                          n_tok // TT, n_lat_tiles, tiles_per_sample, Bn)
        pad_starts, pad_ends, block_expert, n_used, w_slot, next_expert, n_buf_rows = _segment_layout(counts, n_tok)
        idx = _plan_call(route_t, pad_starts)
        xs = _dispatch_call(pad_ends, idx, hn2_rows, n_buf_rows)
        yb = _ffn_call(block_expert, n_used, w_slot, next_expert, xs, w1, w3, w2, l)
        h_all = _combine_call(idx, h1, route, mod3, gf, yb, n_lat // DT, S // DT, Bn, last)
        lat, cx, ctx_off = h_all, h_all, n_lat_tiles
    return h_all.reshape(Bn, S, D)
```

```python
import functools

import jax
import jax.numpy as jnp
from jax import lax
from jax.experimental import pallas as pl
from jax.experimental.pallas import tpu as pltpu

F32 = jnp.float32
BF16 = jnp.bfloat16
I32 = jnp.int32

EPS = 1e-6
LANE = 128
SUBLANE = 8

A_GROUPS = 4
A_GROUP_DIM = 128
A_WIDTH = A_GROUPS * A_GROUP_DIM
CHUNK_A = 128
B_HEADS = 4
B_DK = 64
B_DV = 128
B_KEY_WIDTH = B_HEADS * B_DK
B_WIDTH = B_HEADS * B_DV
GATE_RANK = 16
GATE_TAU = 16.0
CHUNK_B = 64
N_GROUPS = 8
EXPERTS_PER_GROUP = 8
N_EXPERTS = N_GROUPS * EXPERTS_PER_GROUP

TT = 256
GLA_CHUNK = 2 * CHUNK_B
CPT = TT // GLA_CHUNK
SUB = 2
ST = SUB * TT
MOE_BLK = 256
DT = 512
ROW_SUB = 8

VMEM_LIMIT = 48 * 1024 * 1024


def _cparams(sem):
    return pltpu.CompilerParams(dimension_semantics=sem, vmem_limit_bytes=VMEM_LIMIT)


def _load_rows(ref, n):
    return jnp.concatenate([ref[pl.ds(s, n, stride=ROW_SUB), :] for s in range(ROW_SUB)], axis=1)


def _store_rows(ref, val, first=0):
    n = val.shape[0]
    for s in range(ROW_SUB):
        ref[pl.ds(first * ROW_SUB + s, n, stride=ROW_SUB), :] = val[:, s * LANE:(s + 1) * LANE]


def _mod_kernel(c_ref, w_ref, b_ref, o_ref):
    c = c_ref[...]
    s = (c * jax.nn.sigmoid(c)).astype(BF16)
    o_ref[0] = jnp.dot(s, w_ref[0].astype(BF16), preferred_element_type=F32) + b_ref[0]


def _modulation(cc, w_mod, b_mod):
    L, D, D6 = w_mod.shape
    R = cc.shape[0]
    tn = 1536
    return pl.pallas_call(
        _mod_kernel,
        grid=(L, D6 // tn),
        in_specs=[
            pl.BlockSpec((R, D), lambda l, j: (0, 0)),
            pl.BlockSpec((1, D, tn), lambda l, j: (l, 0, j)),
            pl.BlockSpec((1, 1, tn), lambda l, j: (l, 0, j)),
        ],
        out_specs=pl.BlockSpec((1, R, tn), lambda l, j: (l, 0, j)),
        out_shape=jax.ShapeDtypeStruct((L, R, D6), F32),
        compiler_params=_cparams(("arbitrary", "arbitrary")),
        name="modulation",
    )(cc, w_mod, b_mod.reshape(L, 1, D6))


def _select_tile(lat_ref, ctx_ref, n_lat_steps):
    return jnp.where(pl.program_id(0) < n_lat_steps, lat_ref[...], ctx_ref[...])


def _stream_specs(D, n_lat_steps, ctx_off):
    lat = pl.BlockSpec((ST, D), lambda i: (jnp.minimum(i, n_lat_steps - 1), 0))
    ctx = pl.BlockSpec((ST, D), lambda i: (ctx_off + jnp.maximum(i - n_lat_steps, 0), 0))
    return lat, ctx


def _pre_kernel(lat_ref, ctx_ref, mod_ref, g1_ref, win_ref, lng_ref, lnb_ref, wsp_ref, bsp_ref, wg_ref, bg_ref,
                trif_ref, trib_ref, eye_ref,
                a_ref, qe_ref, oi_ref, sr_ref, kv_ref, dec_ref, *, n_lat_steps):
    D = lat_ref.shape[1]
    h = _select_tile(lat_ref, ctx_ref, n_lat_steps)
    mod = mod_ref[0]
    sh1 = mod[:, 0:D]
    sc1 = mod[:, D:2 * D]
    ms = jnp.mean(h * h, axis=-1, keepdims=True)
    hn = h * lax.rsqrt(ms + EPS) * g1_ref[...]
    hn = hn * (1.0 + sc1) + sh1
    z = jnp.dot(hn.astype(BF16), win_ref[...], preferred_element_type=F32)
    o_av = A_WIDTH
    o_q = 2 * A_WIDTH
    o_r = o_q + B_KEY_WIDTH
    o_k = o_r + B_WIDTH
    o_v = o_k + B_KEY_WIDTH
    o_g = o_v + B_WIDTH
    zu = z[:, 0:o_av]
    zv = z[:, o_av:o_q]
    q_all = z[:, o_q:o_r] * (B_DK ** -0.5)
    zr = z[:, o_r:o_k]
    k_all = z[:, o_k:o_v]
    vv_all = z[:, o_v:o_g]
    zg = z[:, o_g:o_g + LANE]

    u = jax.nn.gelu(zu)
    v = jax.nn.gelu(zv)
    s_groups = []
    for g in range(A_GROUPS):
        sl = slice(g * A_GROUP_DIM, (g + 1) * A_GROUP_DIM)
        vg = v[:, sl]
        mu = jnp.mean(vg, axis=-1, keepdims=True)
        dv = vg - mu
        var = jnp.mean(dv * dv, axis=-1, keepdims=True)
        vn = (dv * lax.rsqrt(var + EPS)) * lng_ref[:, sl] + lnb_ref[:, sl]
        vnb = vn.astype(BF16)
        rows = []
        for c in range(ST // CHUNK_A):
            rows.append(jnp.dot(wsp_ref[g], vnb[c * CHUNK_A:(c + 1) * CHUNK_A, :],
                                preferred_element_type=F32) + bsp_ref[:, sl])
        s_groups.append(jnp.concatenate(rows, axis=0))
    a_ref[...] = (u * jnp.concatenate(s_groups, axis=1)).astype(BF16)

    sr_ref[...] = (zr * jax.nn.sigmoid(zr)).astype(BF16)
    lg = jnp.dot(zg.astype(BF16), wg_ref[...], preferred_element_type=F32) + bg_ref[...]
    la_all = (jnp.minimum(lg, 0.0) - jnp.log1p(jnp.exp(-jnp.abs(lg)))) * (1.0 / GATE_TAU)
    for t in range(SUB):
        rs = slice(t * TT, (t + 1) * TT)
        _gla_local(t, rs, la_all[rs, :], q_all[rs, :], k_all[rs, :], vv_all[rs, :],
                   trif_ref, trib_ref, eye_ref, qe_ref, oi_ref, kv_ref, dec_ref)


def _gla_local(t, rs, la, q, k, vv, trif_ref, trib_ref, eye_ref, qe_ref, oi_ref, kv_ref, dec_ref):
    la_hi = la.astype(BF16)
    la_lo = (la - la_hi.astype(F32)).astype(BF16)
    KW = B_KEY_WIDTH
    bf = (jnp.dot(trif_ref[...], la_hi[:, :KW], preferred_element_type=F32)
          + jnp.dot(trif_ref[...], la_lo[:, :KW], preferred_element_type=F32))
    bb = (jnp.dot(trib_ref[...], la_hi[:, KW:], preferred_element_type=F32)
          + jnp.dot(trib_ref[...], la_lo[:, KW:], preferred_element_type=F32))

    def chunk_rows(x, r):
        return jnp.concatenate(
            [jnp.broadcast_to(x[c * GLA_CHUNK + r:c * GLA_CHUNK + r + 1, :], (GLA_CHUNK, KW)) for c in range(CPT)], axis=0)

    blf = chunk_rows(bf, GLA_CHUNK - 1)
    blb = chunk_rows(bb, 0)
    bmf = chunk_rows(bf, CHUNK_B - 1)
    bmb = chunk_rows(bb, CHUNK_B)
    qe_ref[rs, :] = jnp.concatenate([q * jnp.exp(bf), q * jnp.exp(bb)], axis=1).astype(BF16)
    kd_f = k * jnp.exp(blf - bf)
    kd_b = k * jnp.exp(blb - bb)
    qe_f = q * jnp.exp(bf - bmf)
    qe_b = q * jnp.exp(bb - bmb)
    ke_f = (k * jnp.exp(bmf - bf)).astype(BF16)
    ke_b = (k * jnp.exp(bmb - bb)).astype(BF16)
    nt = (((1,), (1,)), ((), ()))
    dec_rows = ([jnp.exp(bf[c * GLA_CHUNK + GLA_CHUNK - 1:(c + 1) * GLA_CHUNK, :]) for c in range(CPT)]
                + [jnp.exp(bb[c * GLA_CHUNK:c * GLA_CHUNK + 1, :]) for c in range(CPT)]
                + [jnp.zeros((LANE - 2 * CPT, KW), F32)])
    dec_pad = jnp.concatenate(dec_rows, axis=0)
    dec_hi = dec_pad.astype(BF16)
    dec_lo = (dec_pad - dec_hi.astype(F32)).astype(BF16)
    dec_ref[t] = (lax.dot_general(eye_ref[...], dec_hi, nt, preferred_element_type=F32)
                  + lax.dot_general(eye_ref[...], dec_lo, nt, preferred_element_type=F32))

    row = lax.broadcasted_iota(I32, (TT, TT), 0)
    col = lax.broadcasted_iota(I32, (TT, TT), 1)
    chunk_shift = GLA_CHUNK.bit_length() - 1
    same = lax.shift_right_logical(row, chunk_shift) == lax.shift_right_logical(col, chunk_shift)
    m_f = same & (row >= col)
    m_b = same & (row <= col)
    lane_head = lax.shift_right_logical(lax.broadcasted_iota(I32, (1, KW), 1), B_DK.bit_length() - 1)
    tok_chunk = lax.shift_right_logical(lax.broadcasted_iota(I32, (1, TT), 1), chunk_shift)
    vb = vv.astype(BF16)
    kdT_f = kd_f.T
    kdT_b = kd_b.T
    oi_heads = []
    for hd in range(B_HEADS):
        hm = lane_head == hd
        qf = jnp.where(hm, qe_f, 0.0).astype(BF16)
        qb = jnp.where(hm, qe_b, 0.0).astype(BF16)
        att_f = lax.dot_general(qf, ke_f, nt, preferred_element_type=F32)
        att_b = lax.dot_general(qb, ke_b, nt, preferred_element_type=F32)
        att = jnp.where(m_f, att_f, 0.0) + jnp.where(m_b, att_b, 0.0)
        v_h = vb[:, hd * B_DV:(hd + 1) * B_DV]
        oi_heads.append(jnp.dot(att.astype(BF16), v_h, preferred_element_type=F32))
        parts = []
        for kdT in (kdT_f, kdT_b):
            kh = kdT[hd * B_DK:(hd + 1) * B_DK, :]
            for c in range(CPT):
                parts.append(jnp.where(tok_chunk == c, kh, 0.0))
        lhs = jnp.concatenate(parts, axis=0).astype(BF16)
        kv_ref[t, hd] = jnp.dot(lhs, v_h, preferred_element_type=F32)
    oi_ref[rs, :] = jnp.concatenate(oi_heads, axis=1)


def _pre_call(lat, ctx, ctx_off, n_ctx_steps, mod3, lw, n_lat_steps, steps_per_sample, n_mod_ctx):
    D = lat.shape[1]
    n_steps = n_lat_steps + n_ctx_steps
    n_tiles = n_steps * SUB
    N = n_steps * ST
    lat_spec, ctx_spec = _stream_specs(D, n_lat_steps, ctx_off)
    DINP = lw["w_in"].shape[1]
    KVR = 2 * CPT * B_DK

    def mod_map(i):
        return (jnp.where(i < n_lat_steps, i // steps_per_sample, n_mod_ctx), 0, 0)

    const2 = lambda i: (0, 0)
    const3 = lambda i: (0, 0, 0)
    tile2 = lambda i: (i, 0)
    return pl.pallas_call(
        functools.partial(_pre_kernel, n_lat_steps=n_lat_steps),
        grid=(n_steps,),
        in_specs=[
            lat_spec,
            ctx_spec,
            pl.BlockSpec((1, 1, 6 * D), mod_map),
            pl.BlockSpec((1, D), const2),
            pl.BlockSpec((D, DINP), const2),
            pl.BlockSpec((1, A_WIDTH), const2),
            pl.BlockSpec((1, A_WIDTH), const2),
            pl.BlockSpec((A_GROUPS, CHUNK_A, CHUNK_A), const3),
            pl.BlockSpec((CHUNK_A, A_WIDTH), const2),
            pl.BlockSpec((LANE, 2 * B_KEY_WIDTH), const2),
            pl.BlockSpec((1, 2 * B_KEY_WIDTH), const2),
            pl.BlockSpec((TT, TT), const2),
            pl.BlockSpec((TT, TT), const2),
            pl.BlockSpec((B_KEY_WIDTH, B_KEY_WIDTH), const2),
        ],
        out_specs=[
            pl.BlockSpec((ST, A_WIDTH), tile2),
            pl.BlockSpec((ST, 2 * B_KEY_WIDTH), tile2),
            pl.BlockSpec((ST, B_WIDTH), tile2),
            pl.BlockSpec((ST, B_WIDTH), tile2),
            pl.BlockSpec((SUB, B_HEADS, KVR, B_DV), lambda i: (i, 0, 0, 0)),
            pl.BlockSpec((SUB, B_KEY_WIDTH, LANE), lambda i: (i, 0, 0)),
        ],
        out_shape=[
            jax.ShapeDtypeStruct((N, A_WIDTH), BF16),
            jax.ShapeDtypeStruct((N, 2 * B_KEY_WIDTH), BF16),
            jax.ShapeDtypeStruct((N, B_WIDTH), F32),
            jax.ShapeDtypeStruct((N, B_WIDTH), BF16),
            jax.ShapeDtypeStruct((n_tiles, B_HEADS, KVR, B_DV), F32),
            jax.ShapeDtypeStruct((n_tiles, B_KEY_WIDTH, LANE), F32),
        ],
        compiler_params=_cparams(("arbitrary",)),
        name="mix_pre",
    )(lat, ctx, mod3, lw["g_norm1"], lw["w_in"], lw["ln_g"], lw["ln_b"], lw["w_sp"], lw["b_sp"],
      lw["w_gate"], lw["b_gate"], lw["tri_f"], lw["tri_b"], lw["eye_k"])


def _scan_kernel(kvc_ref, kvl_ref, dcc_ref, dcl_ref, sc_ref, sl_ref):
    n_lat = kvl_ref.shape[0]
    fwd = [(kvc_ref, dcc_ref, sc_ref, 0, c) for c in range(CPT)]
    fwd += [(kvl_ref, dcl_ref, sl_ref, t, c) for t in range(n_lat) for c in range(CPT)]
    bwd = [(kvc_ref, dcc_ref, sc_ref, 0, c) for c in reversed(range(CPT))]
    bwd += [(kvl_ref, dcl_ref, sl_ref, t, c) for t in reversed(range(n_lat)) for c in reversed(range(CPT))]
    for hd in range(B_HEADS):
        for d, seq in enumerate((fwd, bwd)):
            s = jnp.zeros((B_DK, B_DV), F32)
            for kv_ref, dc_ref, out_ref, t, c in seq:
                r0 = (d * CPT + c) * B_DK
                out_ref[t, hd, r0:r0 + B_DK, :] = s.astype(BF16)
                dcol = dc_ref[t, hd * B_DK:(hd + 1) * B_DK, d * CPT + c:d * CPT + c + 1]
                s = dcol * s + kv_ref[t, hd, r0:r0 + B_DK, :]


def _scan_call(kv, dec, n_samples, n_lat_tiles, tiles_per_sample):
    n_tiles, _, KVR, _ = kv.shape
    kv_c = pl.BlockSpec((1, B_HEADS, KVR, B_DV), lambda b: (n_lat_tiles + b, 0, 0, 0))
    kv_l = pl.BlockSpec((tiles_per_sample, B_HEADS, KVR, B_DV), lambda b: (b, 0, 0, 0))
    dc_c = pl.BlockSpec((1, B_KEY_WIDTH, LANE), lambda b: (n_lat_tiles + b, 0, 0))
    dc_l = pl.BlockSpec((tiles_per_sample, B_KEY_WIDTH, LANE), lambda b: (b, 0, 0))
    s_ctx, s_lat = pl.pallas_call(
        _scan_kernel,
        grid=(n_samples,),
        in_specs=[kv_c, kv_l, dc_c, dc_l],
        out_specs=[
            pl.BlockSpec((1, B_HEADS, KVR, B_DV), lambda b: (b, 0, 0, 0)),
            pl.BlockSpec((tiles_per_sample, B_HEADS, KVR, B_DV), lambda b: (b, 0, 0, 0)),
        ],
        out_shape=[
            jax.ShapeDtypeStruct((n_samples, B_HEADS, KVR, B_DV), BF16),
            jax.ShapeDtypeStruct((n_lat_tiles, B_HEADS, KVR, B_DV), BF16),
        ],
        compiler_params=_cparams(("arbitrary",)),
        name="gla_scan",
    )(kv, kv, dec, dec)
    return jnp.concatenate([s_lat, s_ctx], axis=0)


def _post_kernel(lat_ref, ctx_ref, mod_ref, a_ref, qe_ref, oi_ref, sr_ref, st_ref, ggla_ref, wout_ref, g2_ref,
                 wr_ref, br_ref, ustr_ref,
                 h1_ref, hn2_ref, route_ref, routet_ref, cnt_ref,
                 rhs_scr, cnt_scr, *, n_lat_steps):
    D = lat_ref.shape[1]
    i = pl.program_id(0)

    @pl.when(i == 0)
    def _():
        rhs_scr[...] = jnp.zeros(rhs_scr.shape, rhs_scr.dtype)
        cnt_scr[...] = jnp.zeros(cnt_scr.shape, cnt_scr.dtype)

    qe = qe_ref[...]
    o_parts = []
    for c in range(SUB * CPT):
        t, ct = divmod(c, CPT)
        for d in range(2):
            for hd in range(B_HEADS):
                r0 = d * B_KEY_WIDTH + hd * B_DK
                s0 = (d * CPT + ct) * B_DK
                rhs_scr[c, r0:r0 + B_DK, hd * B_DV:(hd + 1) * B_DV] = st_ref[t, hd, s0:s0 + B_DK, :]
        o_parts.append(jnp.dot(qe[c * GLA_CHUNK:(c + 1) * GLA_CHUNK, :], rhs_scr[c], preferred_element_type=F32))
    o = oi_ref[...] + jnp.concatenate(o_parts, axis=0)

    heads = []
    for hd in range(B_HEADS):
        sl = slice(hd * B_DV, (hd + 1) * B_DV)
        oh = o[:, sl]
        msq = jnp.mean(oh * oh, axis=-1, keepdims=True)
        heads.append(oh * lax.rsqrt(msq + EPS) * ggla_ref[:, sl])
    b_out = jnp.concatenate(heads, axis=1) * sr_ref[...].astype(F32)
    mix_in = jnp.concatenate([a_ref[...], b_out.astype(BF16)], axis=1)
    mix = jnp.dot(mix_in, wout_ref[...], preferred_element_type=F32)

    mod = mod_ref[0]
    g1 = mod[:, 2 * D:3 * D]
    sh2 = mod[:, 3 * D:4 * D]
    sc2 = mod[:, 4 * D:5 * D]
    h_in = _select_tile(lat_ref, ctx_ref, n_lat_steps)
    nt = (((1,), (1,)), ((), ()))
    neg = jnp.float32(-3.0e38)
    big = jnp.float32(1.0e9)
    sub = lax.broadcasted_iota(I32, (EXPERTS_PER_GROUP, LANE), 0).astype(F32)
    route_parts = []
    for hf in range(ST // LANE):
        rs = slice(hf * LANE, (hf + 1) * LANE)
        h1 = h_in[rs, :] + g1 * mix[rs, :]
        h1_ref[rs, :] = h1
        msq = jnp.mean(h1 * h1, axis=-1, keepdims=True)
        hn2 = h1 * lax.rsqrt(msq + EPS) * g2_ref[...]
        hn2 = hn2 * (1.0 + sc2) + sh2
        _store_rows(hn2_ref, hn2, hf * LANE)

        lgt = lax.dot_general(wr_ref[...], hn2.astype(BF16), nt, preferred_element_type=F32) + br_ref[...]
        gl = lgt[N_EXPERTS:N_EXPERTS + N_GROUPS, :]
        gmax = jnp.max(gl, axis=0, keepdims=True)
        p_top = 1.0 / jnp.sum(jnp.exp(gl - gmax), axis=0, keepdims=True)
        gidx = jnp.min(jnp.where(gl == gmax, sub, big), axis=0, keepdims=True)
        el = jnp.zeros((EXPERTS_PER_GROUP, LANE), F32)
        for g in range(N_GROUPS):
            el = el + jnp.where(gidx == float(g), lgt[g * EXPERTS_PER_GROUP:(g + 1) * EXPERTS_PER_GROUP, :], 0.0)
        m1 = jnp.max(el, axis=0, keepdims=True)
        i1 = jnp.min(jnp.where(el == m1, sub, big), axis=0, keepdims=True)
        el2 = jnp.where(sub == i1, neg, el)
        m2 = jnp.max(el2, axis=0, keepdims=True)
        i2 = jnp.min(jnp.where(el2 == m2, sub, big), axis=0, keepdims=True)
        t = jnp.exp(m2 - m1)
        route_parts.append((gidx * float(EXPERTS_PER_GROUP) + i1, gidx * float(EXPERTS_PER_GROUP) + i2,
                            p_top / (1.0 + t), p_top * t / (1.0 + t)))
    e1_all, e2_all, gate1_all, gate2_all = (jnp.concatenate(p, axis=1) for p in zip(*route_parts))

    rowf = lax.broadcasted_iota(I32, (LANE, TT), 0).astype(F32)
    second = lax.broadcasted_iota(I32, (LANE, 1), 0) >= N_EXPERTS
    base = cnt_scr[...]
    for t in range(SUB):
        ls = slice(t * TT, (t + 1) * TT)
        e1, e2 = e1_all[:, ls], e2_all[:, ls]
        sel1 = rowf == e1
        sel2 = rowf == (e2 + float(N_EXPERTS))
        oh = jnp.where(sel1, 1.0, 0.0) + jnp.where(sel2, 1.0, 0.0)
        prefix = jnp.dot(oh.astype(BF16), ustr_ref[...], preferred_element_type=F32)
        tot = jnp.sum(oh, axis=1, keepdims=True)
        tot_sw = jnp.concatenate([tot[N_EXPERTS:, :], tot[:N_EXPERTS, :]], axis=0)
        val = prefix + (base + jnp.where(second, tot_sw, 0.0))
        r1 = jnp.sum(jnp.where(sel1, val, 0.0), axis=0, keepdims=True)
        r2 = jnp.sum(jnp.where(sel2, val, 0.0), axis=0, keepdims=True)
        base = base + tot + tot_sw
        zero = jnp.zeros_like(e1)
        route_t = jnp.concatenate([e1, e2, gate1_all[:, ls], gate2_all[:, ls], r1, r2, zero, zero], axis=0)
        routet_ref[:, ls] = route_t
        route_ref[ls, :] = jnp.concatenate([route_t, jnp.zeros((LANE - SUBLANE, TT), F32)], axis=0).T
    cnt_scr[...] = base
    cnt_ref[...] = base


def _post_call(lat, ctx, ctx_off, mod3, pre_outs, states, lw, n_post_steps, n_lat_steps, steps_per_sample, n_mod_ctx):
    a_out, qe, oi, sr = pre_outs
    D = lat.shape[1]
    KVR = 2 * CPT * B_DK
    NP = n_post_steps * ST
    lat_spec, ctx_spec = _stream_specs(D, n_lat_steps, ctx_off)

    def mod_map(i):
        return (jnp.where(i < n_lat_steps, i // steps_per_sample, n_mod_ctx), 0, 0)

    const2 = lambda i: (0, 0)
    tile2 = lambda i: (i, 0)
    return pl.pallas_call(
        functools.partial(_post_kernel, n_lat_steps=n_lat_steps),
        grid=(n_post_steps,),
        in_specs=[
            lat_spec,
            ctx_spec,
            pl.BlockSpec((1, 1, 6 * D), mod_map),
            pl.BlockSpec((ST, A_WIDTH), tile2),
            pl.BlockSpec((ST, 2 * B_KEY_WIDTH), tile2),
            pl.BlockSpec((ST, B_WIDTH), tile2),
            pl.BlockSpec((ST, B_WIDTH), tile2),
            pl.BlockSpec((SUB, B_HEADS, KVR, B_DV), lambda i: (i, 0, 0, 0)),
            pl.BlockSpec((1, B_WIDTH), const2),
            pl.BlockSpec((D, D), const2),
            pl.BlockSpec((1, D), const2),
            pl.BlockSpec((LANE, D), const2),
            pl.BlockSpec((LANE, 1), const2),
            pl.BlockSpec((TT, TT), const2),
        ],
        out_specs=[
            pl.BlockSpec((ST, D), tile2),
            pl.BlockSpec((ST * ROW_SUB, LANE), tile2),
            pl.BlockSpec((ST, LANE), tile2),
            pl.BlockSpec((SUBLANE, ST), lambda i: (0, i)),
            pl.BlockSpec((LANE, 1), const2),
        ],
        out_shape=[
            jax.ShapeDtypeStruct((NP, D), F32),
            jax.ShapeDtypeStruct((NP * ROW_SUB, LANE), F32),
            jax.ShapeDtypeStruct((NP, LANE), F32),
            jax.ShapeDtypeStruct((SUBLANE, NP), F32),
            jax.ShapeDtypeStruct((LANE, 1), F32),
        ],
        scratch_shapes=[
            pltpu.VMEM((SUB * CPT, 2 * B_KEY_WIDTH, B_WIDTH), BF16),
            pltpu.VMEM((LANE, 1), F32),
        ],
        compiler_params=_cparams(("arbitrary",)),
        name="mix_post",
    )(lat, ctx, mod3, a_out, qe, oi, sr, states, lw["g_gla"], lw["w_out"], lw["g_norm2"],
      lw["w_router"], lw["b_router"], lw["u_strict"])


def _plan_kernel(rt_ref, ps_ref, idx_ref):
    sub = lax.broadcasted_iota(I32, (N_EXPERTS, LANE), 0).astype(F32)
    ps = ps_ref[...]
    segs = DT // LANE
    for k in range(2):
        for seg in range(segs):
            sl = slice(seg * LANE, (seg + 1) * LANE)
            e = rt_ref[k:k + 1, sl]
            r = rt_ref[4 + k:5 + k, sl]
            base = jnp.sum(jnp.where(sub == e, ps, 0.0), axis=0, keepdims=True)
            idx_ref[0, k * segs + seg:k * segs + seg + 1, :] = ((base + r) * float(ROW_SUB)).astype(I32)


def _plan_call(route_t, pad_starts):
    n_dt = route_t.shape[1] // DT
    return pl.pallas_call(
        _plan_kernel,
        grid=(n_dt,),
        in_specs=[
            pl.BlockSpec((SUBLANE, DT), lambda i: (0, i)),
            pl.BlockSpec((N_EXPERTS, 1), lambda i: (0, 0)),
        ],
        out_specs=pl.BlockSpec((1, 2 * DT // LANE, LANE), lambda i: (i, 0, 0)),
        out_shape=jax.ShapeDtypeStruct((n_dt, 2 * DT // LANE, LANE), I32),
        compiler_params=_cparams(("arbitrary",)),
        name="moe_plan",
    )(route_t, pad_starts.astype(F32)[:, None])


def _row_loop(n_rows, fn):
    unroll = 8

    def body(jo, carry):
        for u in range(unroll):
            fn(jo * unroll + u, jo, u)
        return carry

    lax.fori_loop(0, n_rows // unroll, body, 0)


def _tile_indices(idx_smem, jo, u):
    r = lax.shift_right_logical(jo, 4)
    c = (jo & (LANE // 8 - 1)) * 8 + u
    d0 = pl.multiple_of(idx_smem[r, c], ROW_SUB)
    d1 = pl.multiple_of(idx_smem[r + DT // LANE, c], ROW_SUB)
    return d0, d1


def _dispatch_kernel(pe_ref, idx_hbm, x_ref, xs_out, idx_smem, zbuf, sem_i, sem_o, sem_z):
    i = pl.program_id(0)
    cp = pltpu.make_async_copy(idx_hbm.at[i], idx_smem, sem_i)
    cp.start()
    blk_rows = MOE_BLK * ROW_SUB

    @pl.when(i == 0)
    def _():
        zbuf[...] = jnp.zeros(zbuf.shape, zbuf.dtype)
        for e in range(N_EXPERTS):
            start = pl.multiple_of(jnp.maximum(pe_ref[e] - MOE_BLK, 0) * ROW_SUB, blk_rows)
            pltpu.make_async_copy(zbuf, xs_out.at[pl.ds(start, blk_rows), :], sem_z).start()
        for e in range(N_EXPERTS):
            pltpu.make_async_copy(zbuf, xs_out.at[pl.ds(0, blk_rows), :], sem_z).wait()
        first_free = pe_ref[N_EXPERTS - 1] // MOE_BLK
        n_blocks = xs_out.shape[0] // blk_rows

        def clear(b, carry):
            start = pl.multiple_of(b * blk_rows, blk_rows)
            pltpu.make_async_copy(zbuf, xs_out.at[pl.ds(start, blk_rows), :], sem_z).start()
            return carry

        def clear_wait(b, carry):
            pltpu.make_async_copy(zbuf, xs_out.at[pl.ds(0, blk_rows), :], sem_z).wait()
            return carry

        lax.fori_loop(first_free, n_blocks, clear, 0)
        lax.fori_loop(first_free, n_blocks, clear_wait, 0)

    cp.wait()

    def issue(j, jo, u):
        d0, d1 = _tile_indices(idx_smem, jo, u)
        src = x_ref.at[pl.ds(pl.multiple_of(j * ROW_SUB, ROW_SUB), ROW_SUB), :]
        pltpu.make_async_copy(src, xs_out.at[pl.ds(d0, ROW_SUB), :], sem_o).start(priority=0)
        pltpu.make_async_copy(src, xs_out.at[pl.ds(d1, ROW_SUB), :], sem_o).start(priority=1)

    _row_loop(DT, issue)
    for _ in range(2):
        pltpu.make_async_copy(x_ref, xs_out.at[pl.ds(0, DT * ROW_SUB), :], sem_o).wait()


def _dispatch_call(pad_ends, idx, x_rows, n_buf_rows):
    n_tiles = x_rows.shape[0] // (DT * ROW_SUB)
    grid_spec = pltpu.PrefetchScalarGridSpec(
        num_scalar_prefetch=1,
        grid=(n_tiles,),
        in_specs=[
            pl.BlockSpec(memory_space=pl.ANY),
            pl.BlockSpec((DT * ROW_SUB, LANE), lambda i, pe: (i, 0)),
        ],
        out_specs=pl.BlockSpec(memory_space=pl.ANY),
        scratch_shapes=[
            pltpu.SMEM((2 * DT // LANE, LANE), I32),
            pltpu.VMEM((MOE_BLK * ROW_SUB, LANE), F32),
            pltpu.SemaphoreType.DMA,
            pltpu.SemaphoreType.DMA,
            pltpu.SemaphoreType.DMA,
        ],
    )
    return pl.pallas_call(
        _dispatch_kernel,
        grid_spec=grid_spec,
        out_shape=jax.ShapeDtypeStruct((n_buf_rows * ROW_SUB, LANE), F32),
        compiler_params=_cparams(("arbitrary",)),
        name="moe_dispatch",
    )(pad_ends, idx, x_rows)


def _ffn_kernel(be_ref, nu_ref, ws_ref, nx_ref, x_ref, w1_hbm, w3_hbm, w2_hbm, y_ref,
                w1f, w3f, w2f, w1b, w3b, w2b, sem_w, *, layer):
    i = pl.program_id(0)
    nu = nu_ref[0]

    def weight_copies(e, s):
        return (pltpu.make_async_copy(w1_hbm.at[layer, e], w1f.at[s], sem_w.at[s]),
                pltpu.make_async_copy(w3_hbm.at[layer, e], w3f.at[s], sem_w.at[s]),
                pltpu.make_async_copy(w2_hbm.at[layer, e], w2f.at[s], sem_w.at[s]))

    @pl.when(i < nu)
    def _():
        e = be_ref[i]
        s = ws_ref[i]
        first_of_run = (i == 0) | (e != be_ref[jnp.maximum(i - 1, 0)])

        @pl.when(i == 0)
        def _():
            for cp in weight_copies(e, s):
                cp.start()

        @pl.when(first_of_run)
        def _():
            for cp in weight_copies(e, s):
                cp.wait()
            nxt = nx_ref[i]

            @pl.when(nxt >= 0)
            def _():
                for cp in weight_copies(nxt, 1 - s):
                    cp.start(priority=1)

            w1b[...] = w1f[s].astype(BF16)
            w3b[...] = w3f[s].astype(BF16)
            w2b[...] = w2f[s].astype(BF16)

        x = _load_rows(x_ref, MOE_BLK).astype(BF16)
        h1 = jnp.dot(x, w1b[...], preferred_element_type=F32)
        h3 = jnp.dot(x, w3b[...], preferred_element_type=F32)
        hh = (h1 * jax.nn.sigmoid(h1) * h3).astype(BF16)
        y = jnp.dot(hh, w2b[...], preferred_element_type=F32)
        _store_rows(y_ref, y)

    @pl.when(i >= nu)
    def _():
        y_ref[...] = jnp.zeros(y_ref.shape, y_ref.dtype)


def _ffn_call(block_expert, n_used, w_slot, next_expert, xs, w1, w3, w2, l):
    blk_rows = MOE_BLK * ROW_SUB
    NB = xs.shape[0] // blk_rows
    _, _, D, DE = w1.shape
    grid_spec = pltpu.PrefetchScalarGridSpec(
        num_scalar_prefetch=4,
        grid=(NB,),
        in_specs=[
            pl.BlockSpec((blk_rows, LANE), lambda i, be, nu, ws, nx: (jnp.minimum(i, nu[0] - 1), 0)),
            pl.BlockSpec(memory_space=pl.ANY),
            pl.BlockSpec(memory_space=pl.ANY),
            pl.BlockSpec(memory_space=pl.ANY),
        ],
        out_specs=pl.BlockSpec((blk_rows, LANE), lambda i, be, nu, ws, nx: (i, 0)),
        scratch_shapes=[
            pltpu.VMEM((2, D, DE), F32),
            pltpu.VMEM((2, D, DE), F32),
            pltpu.VMEM((2, DE, D), F32),
            pltpu.VMEM((D, DE), BF16),
            pltpu.VMEM((D, DE), BF16),
            pltpu.VMEM((DE, D), BF16),
            pltpu.SemaphoreType.DMA((2,)),
        ],
    )
    return pl.pallas_call(
        functools.partial(_ffn_kernel, layer=l),
        grid_spec=grid_spec,
        out_shape=jax.ShapeDtypeStruct(xs.shape, F32),
        compiler_params=_cparams(("arbitrary",)),
        name="moe_ffn",
    )(block_expert, n_used, w_slot, next_expert, xs, w1, w3, w2)


def _combine_kernel(idx_hbm, h1_ref, route_ref, mod_ref, gf_ref, y_hbm, o_ref, idx_smem, rows0, rows1, sem_i, sem_g,
                    *, final):
    D = h1_ref.shape[1]
    i = pl.program_id(0)
    n = pl.num_programs(0)
    slot = i & 1
    nxt = 1 - slot

    def idx_copy(tile, s):
        return pltpu.make_async_copy(idx_hbm.at[tile], idx_smem.at[s], sem_i.at[s])

    def gather(s):
        def issue(j, jo, u):
            d0, d1 = _tile_indices(idx_smem.at[s], jo, u)
            dst = pl.ds(pl.multiple_of(j * ROW_SUB, ROW_SUB), ROW_SUB)
            pltpu.make_async_copy(y_hbm.at[pl.ds(d0, ROW_SUB), :], rows0.at[s, dst, :], sem_g.at[s]).start(priority=0)
            pltpu.make_async_copy(y_hbm.at[pl.ds(d1, ROW_SUB), :], rows1.at[s, dst, :], sem_g.at[s]).start(priority=1)

        _row_loop(DT, issue)

    @pl.when(i == 0)
    def _():
        cp = idx_copy(0, 0)
        cp.start()
        cp.wait()
        gather(0)

        @pl.when(n > 1)
        def _():
            idx_copy(1, 1).start()

    @pl.when(i + 1 < n)
    def _():
        idx_copy(0, nxt).wait()
        gather(nxt)

    @pl.when(i + 2 < n)
    def _():
        idx_copy(i + 2, slot).start()

    for rows in (rows0, rows1):
        pltpu.make_async_copy(y_hbm.at[pl.ds(0, DT * ROW_SUB), :], rows.at[slot], sem_g.at[slot]).wait()

    route = route_ref[...]
    gate1 = route[:, 2:3]
    gate2 = route[:, 3:4]
    y = gate1 * _load_rows(rows0.at[slot], DT) + gate2 * _load_rows(rows1.at[slot], DT)
    g2 = mod_ref[0][:, 5 * D:6 * D]
    out = h1_ref[...] + g2 * y
    if final:
        msq = jnp.mean(out * out, axis=-1, keepdims=True)
        out = out * lax.rsqrt(msq + EPS) * gf_ref[...]
    o_ref[...] = out


def _combine_call(idx, h1, route, mod3, g_final, yb, n_lat_dt, dt_per_sample, n_mod_ctx, final):
    N, D = h1.shape
    n_tiles = N // DT

    def mod_map(i):
        return (jnp.where(i < n_lat_dt, i // dt_per_sample, n_mod_ctx), 0, 0)

    return pl.pallas_call(
        functools.partial(_combine_kernel, final=final),
        grid=(n_tiles,),
        in_specs=[
            pl.BlockSpec(memory_space=pl.ANY),
            pl.BlockSpec((DT, D), lambda i: (i, 0)),
            pl.BlockSpec((DT, LANE), lambda i: (i, 0)),
            pl.BlockSpec((1, 1, 6 * D), mod_map),
            pl.BlockSpec((1, D), lambda i: (0, 0)),
            pl.BlockSpec(memory_space=pl.ANY),
        ],
        out_specs=pl.BlockSpec((DT, D), lambda i: (i, 0)),
        out_shape=jax.ShapeDtypeStruct((N, D), F32),
        scratch_shapes=[
            pltpu.SMEM((2, 2 * DT // LANE, LANE), I32),
            pltpu.VMEM((2, DT * ROW_SUB, LANE), F32),
            pltpu.VMEM((2, DT * ROW_SUB, LANE), F32),
            pltpu.SemaphoreType.DMA((2,)),
            pltpu.SemaphoreType.DMA((2,)),
        ],
        compiler_params=_cparams(("arbitrary",)),
        name="moe_combine",
    )(idx, h1, route, mod3, g_final, yb)


def _layer_weights(l, w_in, g_norm1, ln_v_g, ln_v_b, w_sp, b_sp, w_gate_up, b_gate, g_gla, w_out, g_norm2,
                   w_router_g, b_router_g, w_router_e, b_router_e):
    D = w_in.shape[1]
    d_in = w_in.shape[2]
    pad = (-d_in) % LANE
    w_in_p = jnp.pad(w_in[l], ((0, 0), (0, pad))).astype(BF16)
    KW = B_KEY_WIDTH
    wg = jnp.zeros((LANE, 2 * KW), F32)
    wg = wg.at[0:GATE_RANK, 0:KW].set(w_gate_up[l, 0])
    wg = wg.at[GATE_RANK:2 * GATE_RANK, KW:2 * KW].set(w_gate_up[l, 1])
    bg = jnp.concatenate([b_gate[l, 0], b_gate[l, 1]])[None, :]
    bsp = jnp.repeat(b_sp[l].T, A_GROUP_DIM, axis=1)
    wr = jnp.zeros((LANE, D), F32)
    wr = wr.at[0:N_EXPERTS, :].set(w_router_e[l].T)
    wr = wr.at[N_EXPERTS:N_EXPERTS + N_GROUPS, :].set(w_router_g[l].T)
    br = jnp.zeros((LANE, 1), F32)
    br = br.at[0:N_EXPERTS, 0].set(b_router_e[l])
    br = br.at[N_EXPERTS:N_EXPERTS + N_GROUPS, 0].set(b_router_g[l])
    r = jnp.arange(TT, dtype=I32)
    same = (r[:, None] // GLA_CHUNK) == (r[None, :] // GLA_CHUNK)
    tri_f = (same & (r[:, None] >= r[None, :])).astype(BF16)
    tri_b = (same & (r[:, None] <= r[None, :])).astype(BF16)
    u_strict = (r[:, None] < r[None, :]).astype(BF16)
    eye_k = jnp.eye(B_KEY_WIDTH, dtype=BF16)
    return dict(
        w_in=w_in_p, g_norm1=g_norm1[l][None, :], ln_g=ln_v_g[l][None, :], ln_b=ln_v_b[l][None, :],
        w_sp=w_sp[l].astype(BF16), b_sp=bsp, w_gate=wg.astype(BF16), b_gate=bg,
        g_gla=g_gla[l][None, :], w_out=w_out[l].astype(BF16), g_norm2=g_norm2[l][None, :],
        w_router=wr.astype(BF16), b_router=br, tri_f=tri_f, tri_b=tri_b, u_strict=u_strict, eye_k=eye_k)


def _segment_layout(counts_col, n_tokens):
    counts = counts_col[:N_EXPERTS, 0].astype(I32)
    padded = (counts + MOE_BLK - 1) // MOE_BLK * MOE_BLK
    pad_ends = jnp.cumsum(padded)
    pad_starts = pad_ends - padded
    n_blocks = (n_tokens * 2 + N_EXPERTS * (MOE_BLK - 1)) // MOE_BLK + 1
    block_start = jnp.arange(n_blocks, dtype=I32) * MOE_BLK
    block_expert = jnp.sum((pad_ends[None, :] <= block_start[:, None]).astype(I32), axis=1)
    block_expert = jnp.minimum(block_expert, N_EXPERTS - 1)
    n_used = (pad_ends[-1] // MOE_BLK).astype(I32)[None]
    blk_id = jnp.arange(n_blocks, dtype=I32)
    prev = jnp.concatenate([block_expert[:1], block_expert[:-1]])
    first = ((blk_id == 0) | (block_expert != prev)) & (blk_id < n_used[0])
    w_slot = (jnp.cumsum(first.astype(I32)) - 1) & 1
    e_id = jnp.arange(N_EXPERTS, dtype=I32)
    later_used = (e_id[None, :] > e_id[:, None]) & (padded[None, :] > 0)
    next_used = jnp.min(jnp.where(later_used, e_id[None, :], N_EXPERTS), axis=1)
    next_used = jnp.where(next_used >= N_EXPERTS, -1, next_used)
    next_expert = jnp.sum(jnp.where(block_expert[:, None] == e_id[None, :], next_used[None, :], 0), axis=1).astype(I32)
    return pad_starts, pad_ends, block_expert, n_used, w_slot.astype(I32), next_expert, n_blocks * MOE_BLK


def kernel(x, c, ctx, c_ctx, w_mod, b_mod, g_norm1, w_in, ln_v_g, ln_v_b, w_sp, b_sp, w_gate_up, b_gate, g_gla,
           w_out, g_norm2, w_router_g, b_router_g, w_router_e, b_router_e, w1, w3, w2, g_final):
    Bn, S, D = x.shape
    Lc = ctx.shape[1]
    depth = w_mod.shape[0]
    assert S % DT == 0 and S % ST == 0 and (Bn * Lc) % DT == 0 and (Bn * Lc) % ST == 0 and Lc == TT
    n_lat = Bn * S
    n_ctx = Bn * Lc
    n_lat_tiles = n_lat // TT
    tiles_per_sample = S // TT
    n_lat_steps = n_lat // ST
    steps_per_sample = S // ST

    cc = jnp.concatenate([c, c_ctx[None, :]], axis=0)
    mod_all = _modulation(cc, w_mod, b_mod)
    gf = g_final[None, :]

    lat, cx, ctx_off = x.reshape(n_lat, D), ctx.reshape(n_ctx, D), 0
    for l in range(depth):
        last = l == depth - 1
        lw = _layer_weights(l, w_in, g_norm1, ln_v_g, ln_v_b, w_sp, b_sp, w_gate_up, b_gate, g_gla, w_out,
                            g_norm2, w_router_g, b_router_g, w_router_e, b_router_e)
        mod3 = mod_all[l].reshape(Bn + 1, 1, 6 * D)
        a_out, qe, oi, sr, kv, dec = _pre_call(lat, cx, ctx_off, n_ctx // ST, mod3, lw, n_lat_steps,
                                               steps_per_sample, Bn)
        states = _scan_call(kv, dec, Bn, n_lat_tiles, tiles_per_sample)
        n_tok = n_lat if last else n_lat + n_ctx
        h1, hn2_rows, route, route_t, counts = _post_call(lat, cx, ctx_off, mod3, (a_out, qe, oi, sr), states, lw,
                                                          n_tok // ST, n_lat_steps, steps_per_sample, Bn)
        pad_starts, pad_ends, block_expert, n_used, w_slot, next_expert, n_buf_rows = _segment_layout(counts, n_tok)
        idx = _plan_call(route_t, pad_starts)
        xs = _dispatch_call(pad_ends, idx, hn2_rows, n_buf_rows)
        yb = _ffn_call(block_expert, n_used, w_slot, next_expert, xs, w1, w3, w2, l)
        h_all = _combine_call(idx, h1, route, mod3, gf, yb, n_lat // DT, S // DT, Bn, last)
        lat, cx, ctx_off = h_all, h_all, n_lat_steps
    return h_all.reshape(Bn, S, D)
```

```python
import functools

import jax
import jax.numpy as jnp
from jax import lax
from jax.experimental import pallas as pl
from jax.experimental.pallas import tpu as pltpu

F32 = jnp.float32
BF16 = jnp.bfloat16
I32 = jnp.int32

EPS = 1e-6
LANE = 128
SUBLANE = 8

A_GROUPS = 4
A_GROUP_DIM = 128
A_WIDTH = A_GROUPS * A_GROUP_DIM
CHUNK_A = 128
B_HEADS = 4
B_DK = 64
B_DV = 128
B_KEY_WIDTH = B_HEADS * B_DK
B_WIDTH = B_HEADS * B_DV
GATE_RANK = 16
GATE_TAU = 16.0
CHUNK_B = 64
N_GROUPS = 8
EXPERTS_PER_GROUP = 8
N_EXPERTS = N_GROUPS * EXPERTS_PER_GROUP

TT = 256
GLA_CHUNK = 2 * CHUNK_B
CPT = TT // GLA_CHUNK
SUB = 2
ST = SUB * TT
MOE_BLK = 256
DT = 512
ROW_SUB = 8

VMEM_LIMIT = 48 * 1024 * 1024


def _cparams(sem):
    return pltpu.CompilerParams(dimension_semantics=sem, vmem_limit_bytes=VMEM_LIMIT)


def _load_rows(ref, n):
    return jnp.concatenate([ref[pl.ds(s, n, stride=ROW_SUB), :] for s in range(ROW_SUB)], axis=1)


def _store_rows(ref, val, first=0):
    n = val.shape[0]
    for s in range(ROW_SUB):
        ref[pl.ds(first * ROW_SUB + s, n, stride=ROW_SUB), :] = val[:, s * LANE:(s + 1) * LANE]


def _mod_kernel(c_ref, w_ref, b_ref, o_ref):
    c = c_ref[...]
    s = (c * jax.nn.sigmoid(c)).astype(BF16)
    o_ref[0] = jnp.dot(s, w_ref[0].astype(BF16), preferred_element_type=F32) + b_ref[0]


def _modulation(cc, w_mod, b_mod):
    L, D, D6 = w_mod.shape
    R = cc.shape[0]
    tn = 1536
    return pl.pallas_call(
        _mod_kernel,
        grid=(L, D6 // tn),
        in_specs=[
            pl.BlockSpec((R, D), lambda l, j: (0, 0)),
            pl.BlockSpec((1, D, tn), lambda l, j: (l, 0, j)),
            pl.BlockSpec((1, 1, tn), lambda l, j: (l, 0, j)),
        ],
        out_specs=pl.BlockSpec((1, R, tn), lambda l, j: (l, 0, j)),
        out_shape=jax.ShapeDtypeStruct((L, R, D6), F32),
        compiler_params=_cparams(("arbitrary", "arbitrary")),
        name="modulation",
    )(cc, w_mod, b_mod.reshape(L, 1, D6))


def _select_tile(lat_ref, ctx_ref, n_lat_steps):
    return jnp.where(pl.program_id(0) < n_lat_steps, lat_ref[...], ctx_ref[...])


def _stream_specs(D, n_lat_steps, ctx_off):
    lat = pl.BlockSpec((ST, D), lambda i: (jnp.minimum(i, n_lat_steps - 1), 0))
    ctx = pl.BlockSpec((ST, D), lambda i: (ctx_off + jnp.maximum(i - n_lat_steps, 0), 0))
    return lat, ctx


def _pre_kernel(lat_ref, ctx_ref, mod_ref, g1_ref, win_ref, lng_ref, lnb_ref, wsp_ref, bsp_ref, wg_ref, bg_ref,
                trif_ref, trib_ref, eye_ref,
                a_ref, qe_ref, oi_ref, sr_ref, kv_ref, dec_ref, *, n_lat_steps):
    D = lat_ref.shape[1]
    h = _select_tile(lat_ref, ctx_ref, n_lat_steps)
    mod = mod_ref[0]
    sh1 = mod[:, 0:D]
    sc1 = mod[:, D:2 * D]
    ms = jnp.mean(h * h, axis=-1, keepdims=True)
    hn = h * lax.rsqrt(ms + EPS) * g1_ref[...]
    hn = hn * (1.0 + sc1) + sh1
    z = jnp.dot(hn.astype(BF16), win_ref[...], preferred_element_type=F32)
    o_av = A_WIDTH
    o_q = 2 * A_WIDTH
    o_r = o_q + B_KEY_WIDTH
    o_k = o_r + B_WIDTH
    o_v = o_k + B_KEY_WIDTH
    o_g = o_v + B_WIDTH
    zu = z[:, 0:o_av]
    zv = z[:, o_av:o_q]
    q_all = z[:, o_q:o_r] * (B_DK ** -0.5)
    zr = z[:, o_r:o_k]
    k_all = z[:, o_k:o_v]
    vv_all = z[:, o_v:o_g]
    zg = z[:, o_g:o_g + LANE]

    u = jax.nn.gelu(zu)
    v = jax.nn.gelu(zv)
    s_groups = []
    for g in range(A_GROUPS):
        sl = slice(g * A_GROUP_DIM, (g + 1) * A_GROUP_DIM)
        vg = v[:, sl]
        mu = jnp.mean(vg, axis=-1, keepdims=True)
        dv = vg - mu
        var = jnp.mean(dv * dv, axis=-1, keepdims=True)
        vn = (dv * lax.rsqrt(var + EPS)) * lng_ref[:, sl] + lnb_ref[:, sl]
        vnb = vn.astype(BF16)
        rows = []
        for c in range(ST // CHUNK_A):
            rows.append(jnp.dot(wsp_ref[g], vnb[c * CHUNK_A:(c + 1) * CHUNK_A, :],
                                preferred_element_type=F32) + bsp_ref[:, sl])
        s_groups.append(jnp.concatenate(rows, axis=0))
    a_ref[...] = (u * jnp.concatenate(s_groups, axis=1)).astype(BF16)

    sr_ref[...] = (zr * jax.nn.sigmoid(zr)).astype(BF16)
    lg = jnp.dot(zg.astype(BF16), wg_ref[...], preferred_element_type=F32) + bg_ref[...]
    la_all = (jnp.minimum(lg, 0.0) - jnp.log1p(jnp.exp(-jnp.abs(lg)))) * (1.0 / GATE_TAU)
    for t in range(SUB):
        rs = slice(t * TT, (t + 1) * TT)
        _gla_local(t, rs, la_all[rs, :], q_all[rs, :], k_all[rs, :], vv_all[rs, :],
                   trif_ref, trib_ref, eye_ref, qe_ref, oi_ref, kv_ref, dec_ref)


def _gla_local(t, rs, la, q, k, vv, trif_ref, trib_ref, eye_ref, qe_ref, oi_ref, kv_ref, dec_ref):
    la_hi = la.astype(BF16)
    la_lo = (la - la_hi.astype(F32)).astype(BF16)
    KW = B_KEY_WIDTH
    bf = (jnp.dot(trif_ref[...], la_hi[:, :KW], preferred_element_type=F32)
          + jnp.dot(trif_ref[...], la_lo[:, :KW], preferred_element_type=F32))
    bb = (jnp.dot(trib_ref[...], la_hi[:, KW:], preferred_element_type=F32)
          + jnp.dot(trib_ref[...], la_lo[:, KW:], preferred_element_type=F32))

    def chunk_rows(x, r):
        return jnp.concatenate(
            [jnp.broadcast_to(x[c * GLA_CHUNK + r:c * GLA_CHUNK + r + 1, :], (GLA_CHUNK, KW)) for c in range(CPT)], axis=0)

    blf = chunk_rows(bf, GLA_CHUNK - 1)
    blb = chunk_rows(bb, 0)
    bmf = chunk_rows(bf, CHUNK_B - 1)
    bmb = chunk_rows(bb, CHUNK_B)
    qe_ref[rs, :] = jnp.concatenate([q * jnp.exp(bf), q * jnp.exp(bb)], axis=1).astype(BF16)
    kd_f = k * jnp.exp(blf - bf)
    kd_b = k * jnp.exp(blb - bb)
    qe_f = q * jnp.exp(bf - bmf)
    qe_b = q * jnp.exp(bb - bmb)
    ke_f = (k * jnp.exp(bmf - bf)).astype(BF16)
    ke_b = (k * jnp.exp(bmb - bb)).astype(BF16)
    nt = (((1,), (1,)), ((), ()))
    dec_rows = ([jnp.exp(bf[c * GLA_CHUNK + GLA_CHUNK - 1:(c + 1) * GLA_CHUNK, :]) for c in range(CPT)]
                + [jnp.exp(bb[c * GLA_CHUNK:c * GLA_CHUNK + 1, :]) for c in range(CPT)]
                + [jnp.zeros((LANE - 2 * CPT, KW), F32)])
    dec_pad = jnp.concatenate(dec_rows, axis=0)
    dec_hi = dec_pad.astype(BF16)
    dec_lo = (dec_pad - dec_hi.astype(F32)).astype(BF16)
    dec_ref[t] = (lax.dot_general(eye_ref[...], dec_hi, nt, preferred_element_type=F32)
                  + lax.dot_general(eye_ref[...], dec_lo, nt, preferred_element_type=F32))

    row = lax.broadcasted_iota(I32, (TT, TT), 0)
    col = lax.broadcasted_iota(I32, (TT, TT), 1)
    chunk_shift = GLA_CHUNK.bit_length() - 1
    same = lax.shift_right_logical(row, chunk_shift) == lax.shift_right_logical(col, chunk_shift)
    m_f = same & (row >= col)
    m_b = same & (row <= col)
    lane_head = lax.shift_right_logical(lax.broadcasted_iota(I32, (1, KW), 1), B_DK.bit_length() - 1)
    tok_chunk = lax.shift_right_logical(lax.broadcasted_iota(I32, (1, TT), 1), chunk_shift)
    vb = vv.astype(BF16)
    kdT_f = kd_f.T
    kdT_b = kd_b.T
    oi_heads = []
    for hd in range(B_HEADS):
        hm = lane_head == hd
        qf = jnp.where(hm, qe_f, 0.0).astype(BF16)
        qb = jnp.where(hm, qe_b, 0.0).astype(BF16)
        att_f = lax.dot_general(qf, ke_f, nt, preferred_element_type=F32)
        att_b = lax.dot_general(qb, ke_b, nt, preferred_element_type=F32)
        att = jnp.where(m_f, att_f, 0.0) + jnp.where(m_b, att_b, 0.0)
        v_h = vb[:, hd * B_DV:(hd + 1) * B_DV]
        oi_heads.append(jnp.dot(att.astype(BF16), v_h, preferred_element_type=F32))
        parts = []
        for kdT in (kdT_f, kdT_b):
            kh = kdT[hd * B_DK:(hd + 1) * B_DK, :]
            for c in range(CPT):
                parts.append(jnp.where(tok_chunk == c, kh, 0.0))
        lhs = jnp.concatenate(parts, axis=0).astype(BF16)
        kv_ref[t, hd] = jnp.dot(lhs, v_h, preferred_element_type=F32)
    oi_ref[rs, :] = jnp.concatenate(oi_heads, axis=1)


def _pre_call(lat, ctx, ctx_off, n_ctx_steps, mod3, lw, n_lat_steps, steps_per_sample, n_mod_ctx):
    D = lat.shape[1]
    n_steps = n_lat_steps + n_ctx_steps
    n_tiles = n_steps * SUB
    N = n_steps * ST
    lat_spec, ctx_spec = _stream_specs(D, n_lat_steps, ctx_off)
    DINP = lw["w_in"].shape[1]
    KVR = 2 * CPT * B_DK

    def mod_map(i):
        return (jnp.where(i < n_lat_steps, i // steps_per_sample, n_mod_ctx), 0, 0)

    const2 = lambda i: (0, 0)
    const3 = lambda i: (0, 0, 0)
    tile2 = lambda i: (i, 0)
    return pl.pallas_call(
        functools.partial(_pre_kernel, n_lat_steps=n_lat_steps),
        grid=(n_steps,),
        in_specs=[
            lat_spec,
            ctx_spec,
            pl.BlockSpec((1, 1, 6 * D), mod_map),
            pl.BlockSpec((1, D), const2),
            pl.BlockSpec((D, DINP), const2),
            pl.BlockSpec((1, A_WIDTH), const2),
            pl.BlockSpec((1, A_WIDTH), const2),
            pl.BlockSpec((A_GROUPS, CHUNK_A, CHUNK_A), const3),
            pl.BlockSpec((CHUNK_A, A_WIDTH), const2),
            pl.BlockSpec((LANE, 2 * B_KEY_WIDTH), const2),
            pl.BlockSpec((1, 2 * B_KEY_WIDTH), const2),
            pl.BlockSpec((TT, TT), const2),
            pl.BlockSpec((TT, TT), const2),
            pl.BlockSpec((B_KEY_WIDTH, B_KEY_WIDTH), const2),
        ],
        out_specs=[
            pl.BlockSpec((ST, A_WIDTH), tile2),
            pl.BlockSpec((ST, 2 * B_KEY_WIDTH), tile2),
            pl.BlockSpec((ST, B_WIDTH), tile2),
            pl.BlockSpec((ST, B_WIDTH), tile2),
            pl.BlockSpec((SUB, B_HEADS, KVR, B_DV), lambda i: (i, 0, 0, 0)),
            pl.BlockSpec((SUB, B_KEY_WIDTH, LANE), lambda i: (i, 0, 0)),
        ],
        out_shape=[
            jax.ShapeDtypeStruct((N, A_WIDTH), BF16),
            jax.ShapeDtypeStruct((N, 2 * B_KEY_WIDTH), BF16),
            jax.ShapeDtypeStruct((N, B_WIDTH), F32),
            jax.ShapeDtypeStruct((N, B_WIDTH), BF16),
            jax.ShapeDtypeStruct((n_tiles, B_HEADS, KVR, B_DV), F32),
            jax.ShapeDtypeStruct((n_tiles, B_KEY_WIDTH, LANE), F32),
        ],
        compiler_params=_cparams(("arbitrary",)),
        name="mix_pre",
    )(lat, ctx, mod3, lw["g_norm1"], lw["w_in"], lw["ln_g"], lw["ln_b"], lw["w_sp"], lw["b_sp"],
      lw["w_gate"], lw["b_gate"], lw["tri_f"], lw["tri_b"], lw["eye_k"])


def _scan_kernel(kvc_ref, kvl_ref, dcc_ref, dcl_ref, sc_ref, sl_ref):
    n_lat = kvl_ref.shape[0]
    fwd = [(kvc_ref, dcc_ref, sc_ref, 0, c) for c in range(CPT)]
    fwd += [(kvl_ref, dcl_ref, sl_ref, t, c) for t in range(n_lat) for c in range(CPT)]
    bwd = [(kvc_ref, dcc_ref, sc_ref, 0, c) for c in reversed(range(CPT))]
    bwd += [(kvl_ref, dcl_ref, sl_ref, t, c) for t in reversed(range(n_lat)) for c in reversed(range(CPT))]
    for hd in range(B_HEADS):
        for d, seq in enumerate((fwd, bwd)):
            s = jnp.zeros((B_DK, B_DV), F32)
            for kv_ref, dc_ref, out_ref, t, c in seq:
                r0 = (d * CPT + c) * B_DK
                out_ref[t, hd, r0:r0 + B_DK, :] = s.astype(BF16)
                dcol = dc_ref[t, hd * B_DK:(hd + 1) * B_DK, d * CPT + c:d * CPT + c + 1]
                s = dcol * s + kv_ref[t, hd, r0:r0 + B_DK, :]


def _scan_call(kv, dec, n_samples, n_lat_tiles, tiles_per_sample):
    n_tiles, _, KVR, _ = kv.shape
    kv_c = pl.BlockSpec((1, B_HEADS, KVR, B_DV), lambda b: (n_lat_tiles + b, 0, 0, 0))
    kv_l = pl.BlockSpec((tiles_per_sample, B_HEADS, KVR, B_DV), lambda b: (b, 0, 0, 0))
    dc_c = pl.BlockSpec((1, B_KEY_WIDTH, LANE), lambda b: (n_lat_tiles + b, 0, 0))
    dc_l = pl.BlockSpec((tiles_per_sample, B_KEY_WIDTH, LANE), lambda b: (b, 0, 0))
    s_ctx, s_lat = pl.pallas_call(
        _scan_kernel,
        grid=(n_samples,),
        in_specs=[kv_c, kv_l, dc_c, dc_l],
        out_specs=[
            pl.BlockSpec((1, B_HEADS, KVR, B_DV), lambda b: (b, 0, 0, 0)),
            pl.BlockSpec((tiles_per_sample, B_HEADS, KVR, B_DV), lambda b: (b, 0, 0, 0)),
        ],
        out_shape=[
            jax.ShapeDtypeStruct((n_samples, B_HEADS, KVR, B_DV), BF16),
            jax.ShapeDtypeStruct((n_lat_tiles, B_HEADS, KVR, B_DV), BF16),
        ],
        compiler_params=_cparams(("arbitrary",)),
        name="gla_scan",
    )(kv, kv, dec, dec)
    return jnp.concatenate([s_lat, s_ctx], axis=0)


def _post_kernel(lat_ref, ctx_ref, mod_ref, a_ref, qe_ref, oi_ref, sr_ref, st_ref, ggla_ref, wout_ref, g2_ref,
                 wr_ref, br_ref, ustr_ref,
                 h1_ref, hn2_ref, route_ref, routet_ref, cnt_ref,
                 rhs_scr, cnt_scr, *, n_lat_steps):
    D = lat_ref.shape[1]
    i = pl.program_id(0)

    @pl.when(i == 0)
    def _():
        rhs_scr[...] = jnp.zeros(rhs_scr.shape, rhs_scr.dtype)
        cnt_scr[...] = jnp.zeros(cnt_scr.shape, cnt_scr.dtype)

    qe = qe_ref[...]
    o_parts = []
    for c in range(SUB * CPT):
        t, ct = divmod(c, CPT)
        for d in range(2):
            for hd in range(B_HEADS):
                r0 = d * B_KEY_WIDTH + hd * B_DK
                s0 = (d * CPT + ct) * B_DK
                rhs_scr[c, r0:r0 + B_DK, hd * B_DV:(hd + 1) * B_DV] = st_ref[t, hd, s0:s0 + B_DK, :]
        o_parts.append(jnp.dot(qe[c * GLA_CHUNK:(c + 1) * GLA_CHUNK, :], rhs_scr[c], preferred_element_type=F32))
    o = oi_ref[...] + jnp.concatenate(o_parts, axis=0)

    heads = []
    for hd in range(B_HEADS):
        sl = slice(hd * B_DV, (hd + 1) * B_DV)
        oh = o[:, sl]
        msq = jnp.mean(oh * oh, axis=-1, keepdims=True)
        heads.append(oh * lax.rsqrt(msq + EPS) * ggla_ref[:, sl])
    b_out = jnp.concatenate(heads, axis=1) * sr_ref[...].astype(F32)
    mix_in = jnp.concatenate([a_ref[...], b_out.astype(BF16)], axis=1)
    mix = jnp.dot(mix_in, wout_ref[...], preferred_element_type=F32)

    mod = mod_ref[0]
    g1 = mod[:, 2 * D:3 * D]
    sh2 = mod[:, 3 * D:4 * D]
    sc2 = mod[:, 4 * D:5 * D]
    h_in = _select_tile(lat_ref, ctx_ref, n_lat_steps)
    nt = (((1,), (1,)), ((), ()))
    neg = jnp.float32(-3.0e38)
    big = jnp.float32(1.0e9)
    sub = lax.broadcasted_iota(I32, (EXPERTS_PER_GROUP, LANE), 0).astype(F32)
    route_parts = []
    for hf in range(ST // LANE):
        rs = slice(hf * LANE, (hf + 1) * LANE)
        h1 = h_in[rs, :] + g1 * mix[rs, :]
        h1_ref[rs, :] = h1
        msq = jnp.mean(h1 * h1, axis=-1, keepdims=True)
        hn2 = h1 * lax.rsqrt(msq + EPS) * g2_ref[...]
        hn2 = hn2 * (1.0 + sc2) + sh2
        _store_rows(hn2_ref, hn2, hf * LANE)

        lgt = lax.dot_general(wr_ref[...], hn2.astype(BF16), nt, preferred_element_type=F32) + br_ref[...]
        gl = lgt[N_EXPERTS:N_EXPERTS + N_GROUPS, :]
        gmax = jnp.max(gl, axis=0, keepdims=True)
        p_top = 1.0 / jnp.sum(jnp.exp(gl - gmax), axis=0, keepdims=True)
        gidx = jnp.min(jnp.where(gl == gmax, sub, big), axis=0, keepdims=True)
        el = jnp.zeros((EXPERTS_PER_GROUP, LANE), F32)
        for g in range(N_GROUPS):
            el = el + jnp.where(gidx == float(g), lgt[g * EXPERTS_PER_GROUP:(g + 1) * EXPERTS_PER_GROUP, :], 0.0)
        m1 = jnp.max(el, axis=0, keepdims=True)
        i1 = jnp.min(jnp.where(el == m1, sub, big), axis=0, keepdims=True)
        el2 = jnp.where(sub == i1, neg, el)
        m2 = jnp.max(el2, axis=0, keepdims=True)
        i2 = jnp.min(jnp.where(el2 == m2, sub, big), axis=0, keepdims=True)
        t = jnp.exp(m2 - m1)
        route_parts.append((gidx * float(EXPERTS_PER_GROUP) + i1, gidx * float(EXPERTS_PER_GROUP) + i2,
                            p_top / (1.0 + t), p_top * t / (1.0 + t)))
    e1_all, e2_all, gate1_all, gate2_all = (jnp.concatenate(p, axis=1) for p in zip(*route_parts))

    rowf = lax.broadcasted_iota(I32, (LANE, TT), 0).astype(F32)
    second = lax.broadcasted_iota(I32, (LANE, 1), 0) >= N_EXPERTS
    base = cnt_scr[...]
    for t in range(SUB):
        ls = slice(t * TT, (t + 1) * TT)
        e1, e2 = e1_all[:, ls], e2_all[:, ls]
        sel1 = rowf == e1
        sel2 = rowf == (e2 + float(N_EXPERTS))
        oh = jnp.where(sel1, 1.0, 0.0) + jnp.where(sel2, 1.0, 0.0)
        prefix = jnp.dot(oh.astype(BF16), ustr_ref[...], preferred_element_type=F32)
        tot = jnp.sum(oh, axis=1, keepdims=True)
        tot_sw = jnp.concatenate([tot[N_EXPERTS:, :], tot[:N_EXPERTS, :]], axis=0)
        val = prefix + (base + jnp.where(second, tot_sw, 0.0))
        r1 = jnp.sum(jnp.where(sel1, val, 0.0), axis=0, keepdims=True)
        r2 = jnp.sum(jnp.where(sel2, val, 0.0), axis=0, keepdims=True)
        base = base + tot + tot_sw
        zero = jnp.zeros_like(e1)
        route_t = jnp.concatenate([e1, e2, gate1_all[:, ls], gate2_all[:, ls], r1, r2, zero, zero], axis=0)
        routet_ref[:, ls] = route_t
        route_ref[ls, :] = jnp.concatenate([route_t, jnp.zeros((LANE - SUBLANE, TT), F32)], axis=0).T
    cnt_scr[...] = base
    cnt_ref[...] = base


def _post_call(lat, ctx, ctx_off, mod3, pre_outs, states, lw, n_post_steps, n_lat_steps, steps_per_sample, n_mod_ctx):
    a_out, qe, oi, sr = pre_outs
    D = lat.shape[1]
    KVR = 2 * CPT * B_DK
    NP = n_post_steps * ST
    lat_spec, ctx_spec = _stream_specs(D, n_lat_steps, ctx_off)

    def mod_map(i):
        return (jnp.where(i < n_lat_steps, i // steps_per_sample, n_mod_ctx), 0, 0)

    const2 = lambda i: (0, 0)
    tile2 = lambda i: (i, 0)
    return pl.pallas_call(
        functools.partial(_post_kernel, n_lat_steps=n_lat_steps),
        grid=(n_post_steps,),
        in_specs=[
            lat_spec,
            ctx_spec,
            pl.BlockSpec((1, 1, 6 * D), mod_map),
            pl.BlockSpec((ST, A_WIDTH), tile2),
            pl.BlockSpec((ST, 2 * B_KEY_WIDTH), tile2),
            pl.BlockSpec((ST, B_WIDTH), tile2),
            pl.BlockSpec((ST, B_WIDTH), tile2),
            pl.BlockSpec((SUB, B_HEADS, KVR, B_DV), lambda i: (i, 0, 0, 0)),
            pl.BlockSpec((1, B_WIDTH), const2),
            pl.BlockSpec((D, D), const2),
            pl.BlockSpec((1, D), const2),
            pl.BlockSpec((LANE, D), const2),
            pl.BlockSpec((LANE, 1), const2),
            pl.BlockSpec((TT, TT), const2),
        ],
        out_specs=[
            pl.BlockSpec((ST, D), tile2),
            pl.BlockSpec((ST * ROW_SUB, LANE), tile2),
            pl.BlockSpec((ST, LANE), tile2),
            pl.BlockSpec((SUBLANE, ST), lambda i: (0, i)),
            pl.BlockSpec((LANE, 1), const2),
        ],
        out_shape=[
            jax.ShapeDtypeStruct((NP, D), F32),
            jax.ShapeDtypeStruct((NP * ROW_SUB, LANE), F32),
            jax.ShapeDtypeStruct((NP, LANE), F32),
            jax.ShapeDtypeStruct((SUBLANE, NP), F32),
            jax.ShapeDtypeStruct((LANE, 1), F32),
        ],
        scratch_shapes=[
            pltpu.VMEM((SUB * CPT, 2 * B_KEY_WIDTH, B_WIDTH), BF16),
            pltpu.VMEM((LANE, 1), F32),
        ],
        compiler_params=_cparams(("arbitrary",)),
        name="mix_post",
    )(lat, ctx, mod3, a_out, qe, oi, sr, states, lw["g_gla"], lw["w_out"], lw["g_norm2"],
      lw["w_router"], lw["b_router"], lw["u_strict"])


def _plan_kernel(rt_ref, ps_ref, idx_ref):
    sub = lax.broadcasted_iota(I32, (N_EXPERTS, LANE), 0).astype(F32)
    ps = ps_ref[...]
    segs = DT // LANE
    for k in range(2):
        for seg in range(segs):
            sl = slice(seg * LANE, (seg + 1) * LANE)
            e = rt_ref[k:k + 1, sl]
            r = rt_ref[4 + k:5 + k, sl]
            base = jnp.sum(jnp.where(sub == e, ps, 0.0), axis=0, keepdims=True)
            idx_ref[0, k * segs + seg:k * segs + seg + 1, :] = ((base + r) * float(ROW_SUB)).astype(I32)


def _plan_call(route_t, pad_starts):
    n_dt = route_t.shape[1] // DT
    return pl.pallas_call(
        _plan_kernel,
        grid=(n_dt,),
        in_specs=[
            pl.BlockSpec((SUBLANE, DT), lambda i: (0, i)),
            pl.BlockSpec((N_EXPERTS, 1), lambda i: (0, 0)),
        ],
        out_specs=pl.BlockSpec((1, 2 * DT // LANE, LANE), lambda i: (i, 0, 0)),
        out_shape=jax.ShapeDtypeStruct((n_dt, 2 * DT // LANE, LANE), I32),
        compiler_params=_cparams(("arbitrary",)),
        name="moe_plan",
    )(route_t, pad_starts.astype(F32)[:, None])


IDX_UNROLL = 8


def _row_loop(n_rows, fn):
    def body(jo, carry):
        for u in range(IDX_UNROLL):
            fn(jo * IDX_UNROLL + u, jo, u)
        return carry

    lax.fori_loop(0, n_rows // IDX_UNROLL, body, 0)


def _tile_indices(idx_smem, jo, u):
    d0 = pl.multiple_of(idx_smem[jo, u], ROW_SUB)
    d1 = pl.multiple_of(idx_smem[jo, IDX_UNROLL + u], ROW_SUB)
    return d0, d1


def _index_tiles(idx):
    n = idx.shape[0]
    d = idx.reshape(n, 2, DT // IDX_UNROLL, IDX_UNROLL).transpose(0, 2, 1, 3).reshape(n, DT // IDX_UNROLL, 2 * IDX_UNROLL)
    return jnp.pad(d, ((0, 0), (0, 0), (0, LANE - 2 * IDX_UNROLL)))


def _dispatch_kernel(pe_ref, idx_hbm, x_ref, xs_out, idx_smem, zbuf, sem_i, sem_o, sem_z):
    i = pl.program_id(0)
    cp = pltpu.make_async_copy(idx_hbm.at[i], idx_smem, sem_i)
    cp.start()
    blk_rows = MOE_BLK * ROW_SUB

    @pl.when(i == 0)
    def _():
        zbuf[...] = jnp.zeros(zbuf.shape, zbuf.dtype)
        for e in range(N_EXPERTS):
            start = pl.multiple_of(jnp.maximum(pe_ref[e] - MOE_BLK, 0) * ROW_SUB, blk_rows)
            pltpu.make_async_copy(zbuf, xs_out.at[pl.ds(start, blk_rows), :], sem_z).start()
        for e in range(N_EXPERTS):
            pltpu.make_async_copy(zbuf, xs_out.at[pl.ds(0, blk_rows), :], sem_z).wait()
        first_free = pe_ref[N_EXPERTS - 1] // MOE_BLK
        n_blocks = xs_out.shape[0] // blk_rows

        def clear(b, carry):
            start = pl.multiple_of(b * blk_rows, blk_rows)
            pltpu.make_async_copy(zbuf, xs_out.at[pl.ds(start, blk_rows), :], sem_z).start()
            return carry

        def clear_wait(b, carry):
            pltpu.make_async_copy(zbuf, xs_out.at[pl.ds(0, blk_rows), :], sem_z).wait()
            return carry

        lax.fori_loop(first_free, n_blocks, clear, 0)
        lax.fori_loop(first_free, n_blocks, clear_wait, 0)

    cp.wait()

    def issue(j, jo, u):
        d0, d1 = _tile_indices(idx_smem, jo, u)
        src = x_ref.at[pl.ds(pl.multiple_of(j * ROW_SUB, ROW_SUB), ROW_SUB), :]
        pltpu.make_async_copy(src, xs_out.at[pl.ds(d0, ROW_SUB), :], sem_o).start(priority=0)
        pltpu.make_async_copy(src, xs_out.at[pl.ds(d1, ROW_SUB), :], sem_o).start(priority=1)

    _row_loop(DT, issue)
    for _ in range(2):
        pltpu.make_async_copy(x_ref, xs_out.at[pl.ds(0, DT * ROW_SUB), :], sem_o).wait()


def _dispatch_call(pad_ends, idx, x_rows, n_buf_rows):
    n_tiles = x_rows.shape[0] // (DT * ROW_SUB)
    grid_spec = pltpu.PrefetchScalarGridSpec(
        num_scalar_prefetch=1,
        grid=(n_tiles,),
        in_specs=[
            pl.BlockSpec(memory_space=pl.ANY),
            pl.BlockSpec((DT * ROW_SUB, LANE), lambda i, pe: (i, 0)),
        ],
        out_specs=pl.BlockSpec(memory_space=pl.ANY),
        scratch_shapes=[
            pltpu.SMEM((DT // IDX_UNROLL, LANE), I32),
            pltpu.VMEM((MOE_BLK * ROW_SUB, LANE), F32),
            pltpu.SemaphoreType.DMA,
            pltpu.SemaphoreType.DMA,
            pltpu.SemaphoreType.DMA,
        ],
    )
    return pl.pallas_call(
        _dispatch_kernel,
        grid_spec=grid_spec,
        out_shape=jax.ShapeDtypeStruct((n_buf_rows * ROW_SUB, LANE), F32),
        compiler_params=_cparams(("arbitrary",)),
        name="moe_dispatch",
    )(pad_ends, idx, x_rows)


def _ffn_kernel(be_ref, nu_ref, ws_ref, nx_ref, x_ref, w1_hbm, w3_hbm, w2_hbm, y_ref,
                w1f, w3f, w2f, w1b, w3b, w2b, sem_w, *, layer):
    i = pl.program_id(0)
    nu = nu_ref[0]

    def weight_copies(e, s):
        return (pltpu.make_async_copy(w1_hbm.at[layer, e], w1f.at[s], sem_w.at[s]),
                pltpu.make_async_copy(w3_hbm.at[layer, e], w3f.at[s], sem_w.at[s]),
                pltpu.make_async_copy(w2_hbm.at[layer, e], w2f.at[s], sem_w.at[s]))

    @pl.when(i < nu)
    def _():
        e = be_ref[i]
        s = ws_ref[i]
        first_of_run = (i == 0) | (e != be_ref[jnp.maximum(i - 1, 0)])

        @pl.when(i == 0)
        def _():
            for cp in weight_copies(e, s):
                cp.start()

        @pl.when(first_of_run)
        def _():
            for cp in weight_copies(e, s):
                cp.wait()
            nxt = nx_ref[i]

            @pl.when(nxt >= 0)
            def _():
                for cp in weight_copies(nxt, 1 - s):
                    cp.start(priority=1)

            w1b[...] = w1f[s].astype(BF16)
            w3b[...] = w3f[s].astype(BF16)
            w2b[...] = w2f[s].astype(BF16)

        x = _load_rows(x_ref, MOE_BLK).astype(BF16)
        h1 = jnp.dot(x, w1b[...], preferred_element_type=F32)
        h3 = jnp.dot(x, w3b[...], preferred_element_type=F32)
        hh = (h1 * jax.nn.sigmoid(h1) * h3).astype(BF16)
        y = jnp.dot(hh, w2b[...], preferred_element_type=F32)
        _store_rows(y_ref, y)

    @pl.when(i >= nu)
    def _():
        y_ref[...] = jnp.zeros(y_ref.shape, y_ref.dtype)


def _ffn_call(block_expert, n_used, w_slot, next_expert, xs, w1, w3, w2, l):
    blk_rows = MOE_BLK * ROW_SUB
    NB = xs.shape[0] // blk_rows
    _, _, D, DE = w1.shape
    grid_spec = pltpu.PrefetchScalarGridSpec(
        num_scalar_prefetch=4,
        grid=(NB,),
        in_specs=[
            pl.BlockSpec((blk_rows, LANE), lambda i, be, nu, ws, nx: (jnp.minimum(i, nu[0] - 1), 0)),
            pl.BlockSpec(memory_space=pl.ANY),
            pl.BlockSpec(memory_space=pl.ANY),
            pl.BlockSpec(memory_space=pl.ANY),
        ],
        out_specs=pl.BlockSpec((blk_rows, LANE), lambda i, be, nu, ws, nx: (i, 0)),
        scratch_shapes=[
            pltpu.VMEM((2, D, DE), F32),
            pltpu.VMEM((2, D, DE), F32),
            pltpu.VMEM((2, DE, D), F32),
            pltpu.VMEM((D, DE), BF16),
            pltpu.VMEM((D, DE), BF16),
            pltpu.VMEM((DE, D), BF16),
            pltpu.SemaphoreType.DMA((2,)),
        ],
    )
    return pl.pallas_call(
        functools.partial(_ffn_kernel, layer=l),
        grid_spec=grid_spec,
        out_shape=jax.ShapeDtypeStruct(xs.shape, F32),
        compiler_params=_cparams(("arbitrary",)),
        name="moe_ffn",
    )(block_expert, n_used, w_slot, next_expert, xs, w1, w3, w2)


def _combine_kernel(idx_hbm, h1_ref, route_ref, mod_ref, gf_ref, y_hbm, o_ref, idx_smem, rows0, rows1, sem_i, sem_g,
                    *, final):
    D = h1_ref.shape[1]
    i = pl.program_id(0)
    n = pl.num_programs(0)
    slot = i & 1
    nxt = 1 - slot

    def idx_copy(tile, s):
        return pltpu.make_async_copy(idx_hbm.at[tile], idx_smem.at[s], sem_i.at[s])

    def gather(s):
        def issue(j, jo, u):
            d0, d1 = _tile_indices(idx_smem.at[s], jo, u)
            dst = pl.ds(pl.multiple_of(j * ROW_SUB, ROW_SUB), ROW_SUB)
            pltpu.make_async_copy(y_hbm.at[pl.ds(d0, ROW_SUB), :], rows0.at[s, dst, :], sem_g.at[s]).start(priority=0)
            pltpu.make_async_copy(y_hbm.at[pl.ds(d1, ROW_SUB), :], rows1.at[s, dst, :], sem_g.at[s]).start(priority=1)

        _row_loop(DT, issue)

    @pl.when(i == 0)
    def _():
        cp = idx_copy(0, 0)
        cp.start()
        cp.wait()
        gather(0)

        @pl.when(n > 1)
        def _():
            idx_copy(1, 1).start()

    @pl.when(i + 1 < n)
    def _():
        idx_copy(0, nxt).wait()
        gather(nxt)

    @pl.when(i + 2 < n)
    def _():
        idx_copy(i + 2, slot).start()

    for rows in (rows0, rows1):
        pltpu.make_async_copy(y_hbm.at[pl.ds(0, DT * ROW_SUB), :], rows.at[slot], sem_g.at[slot]).wait()

    route = route_ref[...]
    gate1 = route[:, 2:3]
    gate2 = route[:, 3:4]
    y = gate1 * _load_rows(rows0.at[slot], DT) + gate2 * _load_rows(rows1.at[slot], DT)
    g2 = mod_ref[0][:, 5 * D:6 * D]
    out = h1_ref[...] + g2 * y
    if final:
        msq = jnp.mean(out * out, axis=-1, keepdims=True)
        out = out * lax.rsqrt(msq + EPS) * gf_ref[...]
    o_ref[...] = out


def _combine_call(idx, h1, route, mod3, g_final, yb, n_lat_dt, dt_per_sample, n_mod_ctx, final):
    N, D = h1.shape
    n_tiles = N // DT

    def mod_map(i):
        return (jnp.where(i < n_lat_dt, i // dt_per_sample, n_mod_ctx), 0, 0)

    return pl.pallas_call(
        functools.partial(_combine_kernel, final=final),
        grid=(n_tiles,),
        in_specs=[
            pl.BlockSpec(memory_space=pl.ANY),
            pl.BlockSpec((DT, D), lambda i: (i, 0)),
            pl.BlockSpec((DT, LANE), lambda i: (i, 0)),
            pl.BlockSpec((1, 1, 6 * D), mod_map),
            pl.BlockSpec((1, D), lambda i: (0, 0)),
            pl.BlockSpec(memory_space=pl.ANY),
        ],
        out_specs=pl.BlockSpec((DT, D), lambda i: (i, 0)),
        out_shape=jax.ShapeDtypeStruct((N, D), F32),
        scratch_shapes=[
            pltpu.SMEM((2, DT // IDX_UNROLL, LANE), I32),
            pltpu.VMEM((2, DT * ROW_SUB, LANE), F32),
            pltpu.VMEM((2, DT * ROW_SUB, LANE), F32),
            pltpu.SemaphoreType.DMA((2,)),
            pltpu.SemaphoreType.DMA((2,)),
        ],
        compiler_params=_cparams(("arbitrary",)),
        name="moe_combine",
    )(idx, h1, route, mod3, g_final, yb)


def _layer_weights(l, w_in, g_norm1, ln_v_g, ln_v_b, w_sp, b_sp, w_gate_up, b_gate, g_gla, w_out, g_norm2,
                   w_router_g, b_router_g, w_router_e, b_router_e):
    D = w_in.shape[1]
    d_in = w_in.shape[2]
    pad = (-d_in) % LANE
    w_in_p = jnp.pad(w_in[l], ((0, 0), (0, pad))).astype(BF16)
    KW = B_KEY_WIDTH
    wg = jnp.zeros((LANE, 2 * KW), F32)
    wg = wg.at[0:GATE_RANK, 0:KW].set(w_gate_up[l, 0])
    wg = wg.at[GATE_RANK:2 * GATE_RANK, KW:2 * KW].set(w_gate_up[l, 1])
    bg = jnp.concatenate([b_gate[l, 0], b_gate[l, 1]])[None, :]
    bsp = jnp.repeat(b_sp[l].T, A_GROUP_DIM, axis=1)
    wr = jnp.zeros((LANE, D), F32)
    wr = wr.at[0:N_EXPERTS, :].set(w_router_e[l].T)
    wr = wr.at[N_EXPERTS:N_EXPERTS + N_GROUPS, :].set(w_router_g[l].T)
    br = jnp.zeros((LANE, 1), F32)
    br = br.at[0:N_EXPERTS, 0].set(b_router_e[l])
    br = br.at[N_EXPERTS:N_EXPERTS + N_GROUPS, 0].set(b_router_g[l])
    r = jnp.arange(TT, dtype=I32)
    same = (r[:, None] // GLA_CHUNK) == (r[None, :] // GLA_CHUNK)
    tri_f = (same & (r[:, None] >= r[None, :])).astype(BF16)
    tri_b = (same & (r[:, None] <= r[None, :])).astype(BF16)
    u_strict = (r[:, None] < r[None, :]).astype(BF16)
    eye_k = jnp.eye(B_KEY_WIDTH, dtype=BF16)
    return dict(
        w_in=w_in_p, g_norm1=g_norm1[l][None, :], ln_g=ln_v_g[l][None, :], ln_b=ln_v_b[l][None, :],
        w_sp=w_sp[l].astype(BF16), b_sp=bsp, w_gate=wg.astype(BF16), b_gate=bg,
        g_gla=g_gla[l][None, :], w_out=w_out[l].astype(BF16), g_norm2=g_norm2[l][None, :],
        w_router=wr.astype(BF16), b_router=br, tri_f=tri_f, tri_b=tri_b, u_strict=u_strict, eye_k=eye_k)


def _segment_layout(counts_col, n_tokens):
    counts = counts_col[:N_EXPERTS, 0].astype(I32)
    padded = (counts + MOE_BLK - 1) // MOE_BLK * MOE_BLK
    pad_ends = jnp.cumsum(padded)
    pad_starts = pad_ends - padded
    n_blocks = (n_tokens * 2 + N_EXPERTS * (MOE_BLK - 1)) // MOE_BLK + 1
    block_start = jnp.arange(n_blocks, dtype=I32) * MOE_BLK
    block_expert = jnp.sum((pad_ends[None, :] <= block_start[:, None]).astype(I32), axis=1)
    block_expert = jnp.minimum(block_expert, N_EXPERTS - 1)
    n_used = (pad_ends[-1] // MOE_BLK).astype(I32)[None]
    blk_id = jnp.arange(n_blocks, dtype=I32)
    prev = jnp.concatenate([block_expert[:1], block_expert[:-1]])
    first = ((blk_id == 0) | (block_expert != prev)) & (blk_id < n_used[0])
    w_slot = (jnp.cumsum(first.astype(I32)) - 1) & 1
    e_id = jnp.arange(N_EXPERTS, dtype=I32)
    later_used = (e_id[None, :] > e_id[:, None]) & (padded[None, :] > 0)
    next_used = jnp.min(jnp.where(later_used, e_id[None, :], N_EXPERTS), axis=1)
    next_used = jnp.where(next_used >= N_EXPERTS, -1, next_used)
    next_expert = jnp.sum(jnp.where(block_expert[:, None] == e_id[None, :], next_used[None, :], 0), axis=1).astype(I32)
    return pad_starts, pad_ends, block_expert, n_used, w_slot.astype(I32), next_expert, n_blocks * MOE_BLK


def kernel(x, c, ctx, c_ctx, w_mod, b_mod, g_norm1, w_in, ln_v_g, ln_v_b, w_sp, b_sp, w_gate_up, b_gate, g_gla,
           w_out, g_norm2, w_router_g, b_router_g, w_router_e, b_router_e, w1, w3, w2, g_final):
    Bn, S, D = x.shape
    Lc = ctx.shape[1]
    depth = w_mod.shape[0]
    assert S % DT == 0 and S % ST == 0 and (Bn * Lc) % DT == 0 and (Bn * Lc) % ST == 0 and Lc == TT
    n_lat = Bn * S
    n_ctx = Bn * Lc
    n_lat_tiles = n_lat // TT
    tiles_per_sample = S // TT
    n_lat_steps = n_lat // ST
    steps_per_sample = S // ST

    cc = jnp.concatenate([c, c_ctx[None, :]], axis=0)
    mod_all = _modulation(cc, w_mod, b_mod)
    gf = g_final[None, :]

    lat, cx, ctx_off = x.reshape(n_lat, D), ctx.reshape(n_ctx, D), 0
    for l in range(depth):
        last = l == depth - 1
        lw = _layer_weights(l, w_in, g_norm1, ln_v_g, ln_v_b, w_sp, b_sp, w_gate_up, b_gate, g_gla, w_out,
                            g_norm2, w_router_g, b_router_g, w_router_e, b_router_e)
        mod3 = mod_all[l].reshape(Bn + 1, 1, 6 * D)
        a_out, qe, oi, sr, kv, dec = _pre_call(lat, cx, ctx_off, n_ctx // ST, mod3, lw, n_lat_steps,
                                               steps_per_sample, Bn)
        states = _scan_call(kv, dec, Bn, n_lat_tiles, tiles_per_sample)
        n_tok = n_lat if last else n_lat + n_ctx
        h1, hn2_rows, route, route_t, counts = _post_call(lat, cx, ctx_off, mod3, (a_out, qe, oi, sr), states, lw,
                                                          n_tok // ST, n_lat_steps, steps_per_sample, Bn)
        pad_starts, pad_ends, block_expert, n_used, w_slot, next_expert, n_buf_rows = _segment_layout(counts, n_tok)
        idx = _index_tiles(_plan_call(route_t, pad_starts))
        xs = _dispatch_call(pad_ends, idx, hn2_rows, n_buf_rows)
        yb = _ffn_call(block_expert, n_used, w_slot, next_expert, xs, w1, w3, w2, l)
        h_all = _combine_call(idx, h1, route, mod3, gf, yb, n_lat // DT, S // DT, Bn, last)
        lat, cx, ctx_off = h_all, h_all, n_lat_steps
    return h_all.reshape(Bn, S, D)
```

```python
import functools

import jax
import jax.numpy as jnp
from jax import lax
from jax.experimental import pallas as pl
from jax.experimental.pallas import tpu as pltpu

F32 = jnp.float32
BF16 = jnp.bfloat16
I32 = jnp.int32

EPS = 1e-6
LANE = 128
SUBLANE = 8

A_GROUPS = 4
A_GROUP_DIM = 128
A_WIDTH = A_GROUPS * A_GROUP_DIM
CHUNK_A = 128
B_HEADS = 4
B_DK = 64
B_DV = 128
B_KEY_WIDTH = B_HEADS * B_DK
B_WIDTH = B_HEADS * B_DV
GATE_RANK = 16
GATE_TAU = 16.0
CHUNK_B = 64
N_GROUPS = 8
EXPERTS_PER_GROUP = 8
N_EXPERTS = N_GROUPS * EXPERTS_PER_GROUP

TT = 256
GLA_CHUNK = 2 * CHUNK_B
CPT = TT // GLA_CHUNK
SUB = 4
ST = SUB * TT
MOE_BLK = 256
DT = 512
ROW_SUB = 8

VMEM_LIMIT = 56 * 1024 * 1024


def _cparams(sem):
    return pltpu.CompilerParams(dimension_semantics=sem, vmem_limit_bytes=VMEM_LIMIT)


def _load_rows(ref, n):
    return jnp.concatenate([ref[pl.ds(s, n, stride=ROW_SUB), :] for s in range(ROW_SUB)], axis=1)


def _store_rows(ref, val, first=0):
    n = val.shape[0]
    for s in range(ROW_SUB):
        ref[pl.ds(first * ROW_SUB + s, n, stride=ROW_SUB), :] = val[:, s * LANE:(s + 1) * LANE]


def _mod_kernel(c_ref, w_ref, b_ref, o_ref):
    c = c_ref[...]
    s = (c * jax.nn.sigmoid(c)).astype(BF16)
    o_ref[0] = jnp.dot(s, w_ref[0].astype(BF16), preferred_element_type=F32) + b_ref[0]


def _modulation(cc, w_mod, b_mod):
    L, D, D6 = w_mod.shape
    R = cc.shape[0]
    tn = 1536
    return pl.pallas_call(
        _mod_kernel,
        grid=(L, D6 // tn),
        in_specs=[
            pl.BlockSpec((R, D), lambda l, j: (0, 0)),
            pl.BlockSpec((1, D, tn), lambda l, j: (l, 0, j)),
            pl.BlockSpec((1, 1, tn), lambda l, j: (l, 0, j)),
        ],
        out_specs=pl.BlockSpec((1, R, tn), lambda l, j: (l, 0, j)),
        out_shape=jax.ShapeDtypeStruct((L, R, D6), F32),
        compiler_params=_cparams(("arbitrary", "arbitrary")),
        name="modulation",
    )(cc, w_mod, b_mod.reshape(L, 1, D6))


def _select_tile(lat_ref, ctx_ref, n_lat_steps):
    return jnp.where(pl.program_id(0) < n_lat_steps, lat_ref[...], ctx_ref[...])


def _stream_specs(D, n_lat_steps, ctx_off):
    lat = pl.BlockSpec((ST, D), lambda i: (jnp.minimum(i, n_lat_steps - 1), 0))
    ctx = pl.BlockSpec((ST, D), lambda i: (ctx_off + jnp.maximum(i - n_lat_steps, 0), 0))
    return lat, ctx


def _pre_kernel(lat_ref, ctx_ref, mod_ref, g1_ref, win_ref, lng_ref, lnb_ref, wsp_ref, bsp_ref, wg_ref, bg_ref,
                trif_ref, trib_ref, eye_ref,
                a_ref, qe_ref, oi_ref, sr_ref, kv_ref, dec_ref, *, n_lat_steps):
    D = lat_ref.shape[1]
    h = _select_tile(lat_ref, ctx_ref, n_lat_steps)
    mod = mod_ref[0]
    sh1 = mod[:, 0:D]
    sc1 = mod[:, D:2 * D]
    ms = jnp.mean(h * h, axis=-1, keepdims=True)
    hn = h * lax.rsqrt(ms + EPS) * g1_ref[...]
    hn = hn * (1.0 + sc1) + sh1
    z = jnp.dot(hn.astype(BF16), win_ref[...], preferred_element_type=F32)
    o_av = A_WIDTH
    o_q = 2 * A_WIDTH
    o_r = o_q + B_KEY_WIDTH
    o_k = o_r + B_WIDTH
    o_v = o_k + B_KEY_WIDTH
    o_g = o_v + B_WIDTH
    zu = z[:, 0:o_av]
    zv = z[:, o_av:o_q]
    q_all = z[:, o_q:o_r] * (B_DK ** -0.5)
    zr = z[:, o_r:o_k]
    k_all = z[:, o_k:o_v]
    vv_all = z[:, o_v:o_g]
    zg = z[:, o_g:o_g + LANE]

    u = jax.nn.gelu(zu)
    v = jax.nn.gelu(zv)
    s_groups = []
    for g in range(A_GROUPS):
        sl = slice(g * A_GROUP_DIM, (g + 1) * A_GROUP_DIM)
        vg = v[:, sl]
        mu = jnp.mean(vg, axis=-1, keepdims=True)
        dv = vg - mu
        var = jnp.mean(dv * dv, axis=-1, keepdims=True)
        vn = (dv * lax.rsqrt(var + EPS)) * lng_ref[:, sl] + lnb_ref[:, sl]
        vnb = vn.astype(BF16)
        n_ch = ST // CHUNK_A
        rhs = jnp.concatenate([vnb[c * CHUNK_A:(c + 1) * CHUNK_A, :] for c in range(n_ch)], axis=1)
        mixed = jnp.dot(wsp_ref[g], rhs, preferred_element_type=F32)
        bias = bsp_ref[:, sl]
        s_groups.append(jnp.concatenate(
            [mixed[:, c * A_GROUP_DIM:(c + 1) * A_GROUP_DIM] + bias for c in range(n_ch)], axis=0))
    a_ref[...] = (u * jnp.concatenate(s_groups, axis=1)).astype(BF16)

    sr_ref[...] = (zr * jax.nn.sigmoid(zr)).astype(BF16)
    lg = jnp.dot(zg.astype(BF16), wg_ref[...], preferred_element_type=F32) + bg_ref[...]
    la_all = (jnp.minimum(lg, 0.0) - jnp.log1p(jnp.exp(-jnp.abs(lg)))) * (1.0 / GATE_TAU)
    for t in range(SUB):
        rs = slice(t * TT, (t + 1) * TT)
        _gla_local(t, rs, la_all[rs, :], q_all[rs, :], k_all[rs, :], vv_all[rs, :],
                   trif_ref, trib_ref, eye_ref, qe_ref, oi_ref, kv_ref, dec_ref)


def _gla_local(t, rs, la, q, k, vv, trif_ref, trib_ref, eye_ref, qe_ref, oi_ref, kv_ref, dec_ref):
    la_hi = la.astype(BF16)
    la_lo = (la - la_hi.astype(F32)).astype(BF16)
    KW = B_KEY_WIDTH
    bf = (jnp.dot(trif_ref[...], la_hi[:, :KW], preferred_element_type=F32)
          + jnp.dot(trif_ref[...], la_lo[:, :KW], preferred_element_type=F32))
    bb = (jnp.dot(trib_ref[...], la_hi[:, KW:], preferred_element_type=F32)
          + jnp.dot(trib_ref[...], la_lo[:, KW:], preferred_element_type=F32))

    def chunk_rows(x, r):
        return jnp.concatenate(
            [jnp.broadcast_to(x[c * GLA_CHUNK + r:c * GLA_CHUNK + r + 1, :], (GLA_CHUNK, KW)) for c in range(CPT)], axis=0)

    blf = chunk_rows(bf, GLA_CHUNK - 1)
    blb = chunk_rows(bb, 0)
    bmf = chunk_rows(bf, CHUNK_B - 1)
    bmb = chunk_rows(bb, CHUNK_B)
    qe_ref[rs, :] = jnp.concatenate([q * jnp.exp(bf), q * jnp.exp(bb)], axis=1).astype(BF16)
    kd_f = k * jnp.exp(blf - bf)
    kd_b = k * jnp.exp(blb - bb)
    qe_f = q * jnp.exp(bf - bmf)
    qe_b = q * jnp.exp(bb - bmb)
    ke_f = (k * jnp.exp(bmf - bf)).astype(BF16)
    ke_b = (k * jnp.exp(bmb - bb)).astype(BF16)
    nt = (((1,), (1,)), ((), ()))
    dec_rows = ([jnp.exp(bf[c * GLA_CHUNK + GLA_CHUNK - 1:(c + 1) * GLA_CHUNK, :]) for c in range(CPT)]
                + [jnp.exp(bb[c * GLA_CHUNK:c * GLA_CHUNK + 1, :]) for c in range(CPT)]
                + [jnp.zeros((LANE - 2 * CPT, KW), F32)])
    dec_pad = jnp.concatenate(dec_rows, axis=0)
    dec_hi = dec_pad.astype(BF16)
    dec_lo = (dec_pad - dec_hi.astype(F32)).astype(BF16)
    dec_ref[t] = (lax.dot_general(eye_ref[...], dec_hi, nt, preferred_element_type=F32)
                  + lax.dot_general(eye_ref[...], dec_lo, nt, preferred_element_type=F32))

    row = lax.broadcasted_iota(I32, (TT, TT), 0)
    col = lax.broadcasted_iota(I32, (TT, TT), 1)
    chunk_shift = GLA_CHUNK.bit_length() - 1
    same = lax.shift_right_logical(row, chunk_shift) == lax.shift_right_logical(col, chunk_shift)
    m_f = same & (row >= col)
    m_b = same & (row <= col)
    lane_head = lax.shift_right_logical(lax.broadcasted_iota(I32, (1, KW), 1), B_DK.bit_length() - 1)
    tok_chunk = lax.shift_right_logical(lax.broadcasted_iota(I32, (1, TT), 1), chunk_shift)
    vb = vv.astype(BF16)
    kdT_f = kd_f.T
    kdT_b = kd_b.T
    oi_heads = []
    for hd in range(B_HEADS):
        hm = lane_head == hd
        qf = jnp.where(hm, qe_f, 0.0).astype(BF16)
        qb = jnp.where(hm, qe_b, 0.0).astype(BF16)
        att_f = lax.dot_general(qf, ke_f, nt, preferred_element_type=F32)
        att_b = lax.dot_general(qb, ke_b, nt, preferred_element_type=F32)
        att = jnp.where(m_f, att_f, 0.0) + jnp.where(m_b, att_b, 0.0)
        v_h = vb[:, hd * B_DV:(hd + 1) * B_DV]
        oi_heads.append(jnp.dot(att.astype(BF16), v_h, preferred_element_type=F32))
        parts = []
        for kdT in (kdT_f, kdT_b):
            kh = kdT[hd * B_DK:(hd + 1) * B_DK, :]
            for c in range(CPT):
                parts.append(jnp.where(tok_chunk == c, kh, 0.0))
        lhs = jnp.concatenate(parts, axis=0).astype(BF16)
        kv_ref[t, hd] = jnp.dot(lhs, v_h, preferred_element_type=F32)
    oi_ref[rs, :] = jnp.concatenate(oi_heads, axis=1)


def _pre_call(lat, ctx, ctx_off, n_ctx_steps, mod3, lw, n_lat_steps, steps_per_sample, n_mod_ctx):
    D = lat.shape[1]
    n_steps = n_lat_steps + n_ctx_steps
    n_tiles = n_steps * SUB
    N = n_steps * ST
    lat_spec, ctx_spec = _stream_specs(D, n_lat_steps, ctx_off)
    DINP = lw["w_in"].shape[1]
    KVR = 2 * CPT * B_DK

    def mod_map(i):
        return (jnp.where(i < n_lat_steps, i // steps_per_sample, n_mod_ctx), 0, 0)

    const2 = lambda i: (0, 0)
    const3 = lambda i: (0, 0, 0)
    tile2 = lambda i: (i, 0)
    return pl.pallas_call(
        functools.partial(_pre_kernel, n_lat_steps=n_lat_steps),
        grid=(n_steps,),
        in_specs=[
            lat_spec,
            ctx_spec,
            pl.BlockSpec((1, 1, 6 * D), mod_map),
            pl.BlockSpec((1, D), const2),
            pl.BlockSpec((D, DINP), const2),
            pl.BlockSpec((1, A_WIDTH), const2),
            pl.BlockSpec((1, A_WIDTH), const2),
            pl.BlockSpec((A_GROUPS, CHUNK_A, CHUNK_A), const3),
            pl.BlockSpec((CHUNK_A, A_WIDTH), const2),
            pl.BlockSpec((LANE, 2 * B_KEY_WIDTH), const2),
            pl.BlockSpec((1, 2 * B_KEY_WIDTH), const2),
            pl.BlockSpec((TT, TT), const2),
            pl.BlockSpec((TT, TT), const2),
            pl.BlockSpec((B_KEY_WIDTH, B_KEY_WIDTH), const2),
        ],
        out_specs=[
            pl.BlockSpec((ST, A_WIDTH), tile2),
            pl.BlockSpec((ST, 2 * B_KEY_WIDTH), tile2),
            pl.BlockSpec((ST, B_WIDTH), tile2),
            pl.BlockSpec((ST, B_WIDTH), tile2),
            pl.BlockSpec((SUB, B_HEADS, KVR, B_DV), lambda i: (i, 0, 0, 0)),
            pl.BlockSpec((SUB, B_KEY_WIDTH, LANE), lambda i: (i, 0, 0)),
        ],
        out_shape=[
            jax.ShapeDtypeStruct((N, A_WIDTH), BF16),
            jax.ShapeDtypeStruct((N, 2 * B_KEY_WIDTH), BF16),
            jax.ShapeDtypeStruct((N, B_WIDTH), F32),
            jax.ShapeDtypeStruct((N, B_WIDTH), BF16),
            jax.ShapeDtypeStruct((n_tiles, B_HEADS, KVR, B_DV), F32),
            jax.ShapeDtypeStruct((n_tiles, B_KEY_WIDTH, LANE), F32),
        ],
        compiler_params=_cparams(("arbitrary",)),
        name="mix_pre",
    )(lat, ctx, mod3, lw["g_norm1"], lw["w_in"], lw["ln_g"], lw["ln_b"], lw["w_sp"], lw["b_sp"],
      lw["w_gate"], lw["b_gate"], lw["tri_f"], lw["tri_b"], lw["eye_k"])


def _scan_kernel(kvc_ref, kvl_ref, dcc_ref, dcl_ref, sc_ref, sl_ref):
    n_lat = kvl_ref.shape[0]
    fwd = [(kvc_ref, dcc_ref, sc_ref, 0, c) for c in range(CPT)]
    fwd += [(kvl_ref, dcl_ref, sl_ref, t, c) for t in range(n_lat) for c in range(CPT)]
    bwd = [(kvc_ref, dcc_ref, sc_ref, 0, c) for c in reversed(range(CPT))]
    bwd += [(kvl_ref, dcl_ref, sl_ref, t, c) for t in reversed(range(n_lat)) for c in reversed(range(CPT))]
    for hd in range(B_HEADS):
        for d, seq in enumerate((fwd, bwd)):
            s = jnp.zeros((B_DK, B_DV), F32)
            for kv_ref, dc_ref, out_ref, t, c in seq:
                r0 = (d * CPT + c) * B_DK
                out_ref[t, hd, r0:r0 + B_DK, :] = s.astype(BF16)
                dcol = dc_ref[t, hd * B_DK:(hd + 1) * B_DK, d * CPT + c:d * CPT + c + 1]
                s = dcol * s + kv_ref[t, hd, r0:r0 + B_DK, :]


def _scan_call(kv, dec, n_samples, n_lat_tiles, tiles_per_sample):
    n_tiles, _, KVR, _ = kv.shape
    kv_c = pl.BlockSpec((1, B_HEADS, KVR, B_DV), lambda b: (n_lat_tiles + b, 0, 0, 0))
    kv_l = pl.BlockSpec((tiles_per_sample, B_HEADS, KVR, B_DV), lambda b: (b, 0, 0, 0))
    dc_c = pl.BlockSpec((1, B_KEY_WIDTH, LANE), lambda b: (n_lat_tiles + b, 0, 0))
    dc_l = pl.BlockSpec((tiles_per_sample, B_KEY_WIDTH, LANE), lambda b: (b, 0, 0))
    s_ctx, s_lat = pl.pallas_call(
        _scan_kernel,
        grid=(n_samples,),
        in_specs=[kv_c, kv_l, dc_c, dc_l],
        out_specs=[
            pl.BlockSpec((1, B_HEADS, KVR, B_DV), lambda b: (b, 0, 0, 0)),
            pl.BlockSpec((tiles_per_sample, B_HEADS, KVR, B_DV), lambda b: (b, 0, 0, 0)),
        ],
        out_shape=[
            jax.ShapeDtypeStruct((n_samples, B_HEADS, KVR, B_DV), BF16),
            jax.ShapeDtypeStruct((n_lat_tiles, B_HEADS, KVR, B_DV), BF16),
        ],
        compiler_params=_cparams(("arbitrary",)),
        name="gla_scan",
    )(kv, kv, dec, dec)
    return jnp.concatenate([s_lat, s_ctx], axis=0)


def _post_kernel(lat_ref, ctx_ref, mod_ref, a_ref, qe_ref, oi_ref, sr_ref, st_ref, ggla_ref, wout_ref, g2_ref,
                 wr_ref, br_ref, ustr_ref,
                 h1_ref, hn2_ref, route_ref, routet_ref, cnt_ref,
                 rhs_scr, cnt_scr, *, n_lat_steps):
    D = lat_ref.shape[1]
    i = pl.program_id(0)

    @pl.when(i == 0)
    def _():
        rhs_scr[...] = jnp.zeros(rhs_scr.shape, rhs_scr.dtype)
        cnt_scr[...] = jnp.zeros(cnt_scr.shape, cnt_scr.dtype)

    qe = qe_ref[...]
    o_parts = []
    for c in range(SUB * CPT):
        t, ct = divmod(c, CPT)
        for d in range(2):
            for hd in range(B_HEADS):
                r0 = d * B_KEY_WIDTH + hd * B_DK
                s0 = (d * CPT + ct) * B_DK
                rhs_scr[c, r0:r0 + B_DK, hd * B_DV:(hd + 1) * B_DV] = st_ref[t, hd, s0:s0 + B_DK, :]
        o_parts.append(jnp.dot(qe[c * GLA_CHUNK:(c + 1) * GLA_CHUNK, :], rhs_scr[c], preferred_element_type=F32))
    o = oi_ref[...] + jnp.concatenate(o_parts, axis=0)

    heads = []
    for hd in range(B_HEADS):
        sl = slice(hd * B_DV, (hd + 1) * B_DV)
        oh = o[:, sl]
        msq = jnp.mean(oh * oh, axis=-1, keepdims=True)
        heads.append(oh * lax.rsqrt(msq + EPS) * ggla_ref[:, sl])
    b_out = jnp.concatenate(heads, axis=1) * sr_ref[...].astype(F32)
    mix_in = jnp.concatenate([a_ref[...], b_out.astype(BF16)], axis=1)
    mix = jnp.dot(mix_in, wout_ref[...], preferred_element_type=F32)

    mod = mod_ref[0]
    g1 = mod[:, 2 * D:3 * D]
    sh2 = mod[:, 3 * D:4 * D]
    sc2 = mod[:, 4 * D:5 * D]
    h_in = _select_tile(lat_ref, ctx_ref, n_lat_steps)
    nt = (((1,), (1,)), ((), ()))
    neg = jnp.float32(-3.0e38)
    big = jnp.float32(1.0e9)
    sub = lax.broadcasted_iota(I32, (EXPERTS_PER_GROUP, LANE), 0).astype(F32)
    route_parts = []
    for hf in range(ST // LANE):
        rs = slice(hf * LANE, (hf + 1) * LANE)
        h1 = h_in[rs, :] + g1 * mix[rs, :]
        h1_ref[rs, :] = h1
        msq = jnp.mean(h1 * h1, axis=-1, keepdims=True)
        hn2 = h1 * lax.rsqrt(msq + EPS) * g2_ref[...]
        hn2 = hn2 * (1.0 + sc2) + sh2
        _store_rows(hn2_ref, hn2, hf * LANE)

        lgt = lax.dot_general(wr_ref[...], hn2.astype(BF16), nt, preferred_element_type=F32) + br_ref[...]
        gl = lgt[N_EXPERTS:N_EXPERTS + N_GROUPS, :]
        gmax = jnp.max(gl, axis=0, keepdims=True)
        p_top = 1.0 / jnp.sum(jnp.exp(gl - gmax), axis=0, keepdims=True)
        gidx = jnp.min(jnp.where(gl == gmax, sub, big), axis=0, keepdims=True)
        el = jnp.zeros((EXPERTS_PER_GROUP, LANE), F32)
        for g in range(N_GROUPS):
            el = el + jnp.where(gidx == float(g), lgt[g * EXPERTS_PER_GROUP:(g + 1) * EXPERTS_PER_GROUP, :], 0.0)
        m1 = jnp.max(el, axis=0, keepdims=True)
        i1 = jnp.min(jnp.where(el == m1, sub, big), axis=0, keepdims=True)
        el2 = jnp.where(sub == i1, neg, el)
        m2 = jnp.max(el2, axis=0, keepdims=True)
        i2 = jnp.min(jnp.where(el2 == m2, sub, big), axis=0, keepdims=True)
        t = jnp.exp(m2 - m1)
        route_parts.append((gidx * float(EXPERTS_PER_GROUP) + i1, gidx * float(EXPERTS_PER_GROUP) + i2,
                            p_top / (1.0 + t), p_top * t / (1.0 + t)))
    e1_all, e2_all, gate1_all, gate2_all = (jnp.concatenate(p, axis=1) for p in zip(*route_parts))

    rowf = lax.broadcasted_iota(I32, (LANE, TT), 0).astype(F32)
    second = lax.broadcasted_iota(I32, (LANE, 1), 0) >= N_EXPERTS
    base = cnt_scr[...]
    for t in range(SUB):
        ls = slice(t * TT, (t + 1) * TT)
        e1, e2 = e1_all[:, ls], e2_all[:, ls]
        sel1 = rowf == e1
        sel2 = rowf == (e2 + float(N_EXPERTS))
        oh = jnp.where(sel1, 1.0, 0.0) + jnp.where(sel2, 1.0, 0.0)
        prefix = jnp.dot(oh.astype(BF16), ustr_ref[...], preferred_element_type=F32)
        tot = jnp.sum(oh, axis=1, keepdims=True)
        tot_sw = jnp.concatenate([tot[N_EXPERTS:, :], tot[:N_EXPERTS, :]], axis=0)
        val = prefix + (base + jnp.where(second, tot_sw, 0.0))
        r1 = jnp.sum(jnp.where(sel1, val, 0.0), axis=0, keepdims=True)
        r2 = jnp.sum(jnp.where(sel2, val, 0.0), axis=0, keepdims=True)
        base = base + tot + tot_sw
        zero = jnp.zeros_like(e1)
        route_t = jnp.concatenate([e1, e2, gate1_all[:, ls], gate2_all[:, ls], r1, r2, zero, zero], axis=0)
        routet_ref[:, ls] = route_t
        route_ref[ls, :] = jnp.concatenate([route_t, jnp.zeros((LANE - SUBLANE, TT), F32)], axis=0).T
    cnt_scr[...] = base
    cnt_ref[...] = base


def _post_call(lat, ctx, ctx_off, mod3, pre_outs, states, lw, n_post_steps, n_lat_steps, steps_per_sample, n_mod_ctx):
    a_out, qe, oi, sr = pre_outs
    D = lat.shape[1]
    KVR = 2 * CPT * B_DK
    NP = n_post_steps * ST
    lat_spec, ctx_spec = _stream_specs(D, n_lat_steps, ctx_off)

    def mod_map(i):
        return (jnp.where(i < n_lat_steps, i // steps_per_sample, n_mod_ctx), 0, 0)

    const2 = lambda i: (0, 0)
    tile2 = lambda i: (i, 0)
    return pl.pallas_call(
        functools.partial(_post_kernel, n_lat_steps=n_lat_steps),
        grid=(n_post_steps,),
        in_specs=[
            lat_spec,
            ctx_spec,
            pl.BlockSpec((1, 1, 6 * D), mod_map),
            pl.BlockSpec((ST, A_WIDTH), tile2),
            pl.BlockSpec((ST, 2 * B_KEY_WIDTH), tile2),
            pl.BlockSpec((ST, B_WIDTH), tile2),
            pl.BlockSpec((ST, B_WIDTH), tile2),
            pl.BlockSpec((SUB, B_HEADS, KVR, B_DV), lambda i: (i, 0, 0, 0)),
            pl.BlockSpec((1, B_WIDTH), const2),
            pl.BlockSpec((D, D), const2),
            pl.BlockSpec((1, D), const2),
            pl.BlockSpec((LANE, D), const2),
            pl.BlockSpec((LANE, 1), const2),
            pl.BlockSpec((TT, TT), const2),
        ],
        out_specs=[
            pl.BlockSpec((ST, D), tile2),
            pl.BlockSpec((ST * ROW_SUB, LANE), tile2),
            pl.BlockSpec((ST, LANE), tile2),
            pl.BlockSpec((SUBLANE, ST), lambda i: (0, i)),
            pl.BlockSpec((LANE, 1), const2),
        ],
        out_shape=[
            jax.ShapeDtypeStruct((NP, D), F32),
            jax.ShapeDtypeStruct((NP * ROW_SUB, LANE), F32),
            jax.ShapeDtypeStruct((NP, LANE), F32),
            jax.ShapeDtypeStruct((SUBLANE, NP), F32),
            jax.ShapeDtypeStruct((LANE, 1), F32),
        ],
        scratch_shapes=[
            pltpu.VMEM((SUB * CPT, 2 * B_KEY_WIDTH, B_WIDTH), BF16),
            pltpu.VMEM((LANE, 1), F32),
        ],
        compiler_params=_cparams(("arbitrary",)),
        name="mix_post",
    )(lat, ctx, mod3, a_out, qe, oi, sr, states, lw["g_gla"], lw["w_out"], lw["g_norm2"],
      lw["w_router"], lw["b_router"], lw["u_strict"])


def _plan_kernel(rt_ref, ps_ref, idx_ref):
    sub = lax.broadcasted_iota(I32, (N_EXPERTS, LANE), 0).astype(F32)
    ps = ps_ref[...]
    segs = DT // LANE
    for k in range(2):
        for seg in range(segs):
            sl = slice(seg * LANE, (seg + 1) * LANE)
            e = rt_ref[k:k + 1, sl]
            r = rt_ref[4 + k:5 + k, sl]
            base = jnp.sum(jnp.where(sub == e, ps, 0.0), axis=0, keepdims=True)
            idx_ref[0, k * segs + seg:k * segs + seg + 1, :] = ((base + r) * float(ROW_SUB)).astype(I32)


def _plan_call(route_t, pad_starts):
    n_dt = route_t.shape[1] // DT
    return pl.pallas_call(
        _plan_kernel,
        grid=(n_dt,),
        in_specs=[
            pl.BlockSpec((SUBLANE, DT), lambda i: (0, i)),
            pl.BlockSpec((N_EXPERTS, 1), lambda i: (0, 0)),
        ],
        out_specs=pl.BlockSpec((1, 2 * DT // LANE, LANE), lambda i: (i, 0, 0)),
        out_shape=jax.ShapeDtypeStruct((n_dt, 2 * DT // LANE, LANE), I32),
        compiler_params=_cparams(("arbitrary",)),
        name="moe_plan",
    )(route_t, pad_starts.astype(F32)[:, None])


IDX_UNROLL = 8


def _row_loop(n_rows, fn):
    def body(jo, carry):
        for u in range(IDX_UNROLL):
            fn(jo * IDX_UNROLL + u, jo, u)
        return carry

    lax.fori_loop(0, n_rows // IDX_UNROLL, body, 0)


def _tile_indices(idx_smem, jo, u):
    d0 = pl.multiple_of(idx_smem[jo, u], ROW_SUB)
    d1 = pl.multiple_of(idx_smem[jo, IDX_UNROLL + u], ROW_SUB)
    return d0, d1


def _index_tiles(idx):
    n = idx.shape[0]
    d = idx.reshape(n, 2, DT // IDX_UNROLL, IDX_UNROLL).transpose(0, 2, 1, 3).reshape(n, DT // IDX_UNROLL, 2 * IDX_UNROLL)
    return jnp.pad(d, ((0, 0), (0, 0), (0, LANE - 2 * IDX_UNROLL)))


def _dispatch_kernel(pe_ref, idx_hbm, x_ref, xs_out, idx_smem, zbuf, sem_i, sem_o, sem_z):
    i = pl.program_id(0)
    cp = pltpu.make_async_copy(idx_hbm.at[i], idx_smem, sem_i)
    cp.start()
    blk_rows = MOE_BLK * ROW_SUB

    @pl.when(i == 0)
    def _():
        zbuf[...] = jnp.zeros(zbuf.shape, zbuf.dtype)
        for e in range(N_EXPERTS):
            start = pl.multiple_of(jnp.maximum(pe_ref[e] - MOE_BLK, 0) * ROW_SUB, blk_rows)
            pltpu.make_async_copy(zbuf, xs_out.at[pl.ds(start, blk_rows), :], sem_z).start()
        for e in range(N_EXPERTS):
            pltpu.make_async_copy(zbuf, xs_out.at[pl.ds(0, blk_rows), :], sem_z).wait()
        first_free = pe_ref[N_EXPERTS - 1] // MOE_BLK
        n_blocks = xs_out.shape[0] // blk_rows

        def clear(b, carry):
            start = pl.multiple_of(b * blk_rows, blk_rows)
            pltpu.make_async_copy(zbuf, xs_out.at[pl.ds(start, blk_rows), :], sem_z).start()
            return carry

        def clear_wait(b, carry):
            pltpu.make_async_copy(zbuf, xs_out.at[pl.ds(0, blk_rows), :], sem_z).wait()
            return carry

        lax.fori_loop(first_free, n_blocks, clear, 0)
        lax.fori_loop(first_free, n_blocks, clear_wait, 0)

    cp.wait()

    def issue(j, jo, u):
        d0, d1 = _tile_indices(idx_smem, jo, u)
        src = x_ref.at[pl.ds(pl.multiple_of(j * ROW_SUB, ROW_SUB), ROW_SUB), :]
        pltpu.make_async_copy(src, xs_out.at[pl.ds(d0, ROW_SUB), :], sem_o).start(priority=0)
        pltpu.make_async_copy(src, xs_out.at[pl.ds(d1, ROW_SUB), :], sem_o).start(priority=1)

    _row_loop(DT, issue)
    for _ in range(2):
        pltpu.make_async_copy(x_ref, xs_out.at[pl.ds(0, DT * ROW_SUB), :], sem_o).wait()


def _dispatch_call(pad_ends, idx, x_rows, n_buf_rows):
    n_tiles = x_rows.shape[0] // (DT * ROW_SUB)
    grid_spec = pltpu.PrefetchScalarGridSpec(
        num_scalar_prefetch=1,
        grid=(n_tiles,),
        in_specs=[
            pl.BlockSpec(memory_space=pl.ANY),
            pl.BlockSpec((DT * ROW_SUB, LANE), lambda i, pe: (i, 0)),
        ],
        out_specs=pl.BlockSpec(memory_space=pl.ANY),
        scratch_shapes=[
            pltpu.SMEM((DT // IDX_UNROLL, LANE), I32),
            pltpu.VMEM((MOE_BLK * ROW_SUB, LANE), F32),
            pltpu.SemaphoreType.DMA,
            pltpu.SemaphoreType.DMA,
            pltpu.SemaphoreType.DMA,
        ],
    )
    return pl.pallas_call(
        _dispatch_kernel,
        grid_spec=grid_spec,
        out_shape=jax.ShapeDtypeStruct((n_buf_rows * ROW_SUB, LANE), F32),
        compiler_params=_cparams(("arbitrary",)),
        name="moe_dispatch",
    )(pad_ends, idx, x_rows)


def _ffn_kernel(be_ref, nu_ref, ws_ref, nx_ref, x_ref, w1_hbm, w3_hbm, w2_hbm, y_ref,
                w1f, w3f, w2f, w1b, w3b, w2b, sem_w, *, layer):
    i = pl.program_id(0)
    nu = nu_ref[0]

    def weight_copies(e, s):
        return (pltpu.make_async_copy(w1_hbm.at[layer, e], w1f.at[s], sem_w.at[s]),
                pltpu.make_async_copy(w3_hbm.at[layer, e], w3f.at[s], sem_w.at[s]),
                pltpu.make_async_copy(w2_hbm.at[layer, e], w2f.at[s], sem_w.at[s]))

    @pl.when(i < nu)
    def _():
        e = be_ref[i]
        s = ws_ref[i]
        first_of_run = (i == 0) | (e != be_ref[jnp.maximum(i - 1, 0)])

        @pl.when(i == 0)
        def _():
            for cp in weight_copies(e, s):
                cp.start()

        @pl.when(first_of_run)
        def _():
            for cp in weight_copies(e, s):
                cp.wait()
            nxt = nx_ref[i]

            @pl.when(nxt >= 0)
            def _():
                for cp in weight_copies(nxt, 1 - s):
                    cp.start(priority=1)

            w1b[...] = w1f[s].astype(BF16)
            w3b[...] = w3f[s].astype(BF16)
            w2b[...] = w2f[s].astype(BF16)

        x = _load_rows(x_ref, MOE_BLK).astype(BF16)
        h1 = jnp.dot(x, w1b[...], preferred_element_type=F32)
        h3 = jnp.dot(x, w3b[...], preferred_element_type=F32)
        hh = (h1 * jax.nn.sigmoid(h1) * h3).astype(BF16)
        y = jnp.dot(hh, w2b[...], preferred_element_type=F32)
        _store_rows(y_ref, y)

    @pl.when(i >= nu)
    def _():
        y_ref[...] = jnp.zeros(y_ref.shape, y_ref.dtype)


def _ffn_call(block_expert, n_used, w_slot, next_expert, xs, w1, w3, w2, l):
    blk_rows = MOE_BLK * ROW_SUB
    NB = xs.shape[0] // blk_rows
    _, _, D, DE = w1.shape
    grid_spec = pltpu.PrefetchScalarGridSpec(
        num_scalar_prefetch=4,
        grid=(NB,),
        in_specs=[
            pl.BlockSpec((blk_rows, LANE), lambda i, be, nu, ws, nx: (jnp.minimum(i, nu[0] - 1), 0)),
            pl.BlockSpec(memory_space=pl.ANY),
            pl.BlockSpec(memory_space=pl.ANY),
            pl.BlockSpec(memory_space=pl.ANY),
        ],
        out_specs=pl.BlockSpec((blk_rows, LANE), lambda i, be, nu, ws, nx: (i, 0)),
        scratch_shapes=[
            pltpu.VMEM((2, D, DE), F32),
            pltpu.VMEM((2, D, DE), F32),
            pltpu.VMEM((2, DE, D), F32),
            pltpu.VMEM((D, DE), BF16),
            pltpu.VMEM((D, DE), BF16),
            pltpu.VMEM((DE, D), BF16),
            pltpu.SemaphoreType.DMA((2,)),
        ],
    )
    return pl.pallas_call(
        functools.partial(_ffn_kernel, layer=l),
        grid_spec=grid_spec,
        out_shape=jax.ShapeDtypeStruct(xs.shape, F32),
        compiler_params=_cparams(("arbitrary",)),
        name="moe_ffn",
    )(block_expert, n_used, w_slot, next_expert, xs, w1, w3, w2)


def _combine_kernel(idx_hbm, h1_ref, route_ref, mod_ref, gf_ref, y_hbm, o_ref, idx_smem, rows0, rows1, sem_i, sem_g,
                    *, final):
    D = h1_ref.shape[1]
    i = pl.program_id(0)
    n = pl.num_programs(0)
    slot = i & 1
    nxt = 1 - slot

    def idx_copy(tile, s):
        return pltpu.make_async_copy(idx_hbm.at[tile], idx_smem.at[s], sem_i.at[s])

    def gather(s):
        def issue(j, jo, u):
            d0, d1 = _tile_indices(idx_smem.at[s], jo, u)
            dst = pl.ds(pl.multiple_of(j * ROW_SUB, ROW_SUB), ROW_SUB)
            pltpu.make_async_copy(y_hbm.at[pl.ds(d0, ROW_SUB), :], rows0.at[s, dst, :], sem_g.at[s]).start(priority=0)
            pltpu.make_async_copy(y_hbm.at[pl.ds(d1, ROW_SUB), :], rows1.at[s, dst, :], sem_g.at[s]).start(priority=1)

        _row_loop(DT, issue)

    @pl.when(i == 0)
    def _():
        cp = idx_copy(0, 0)
        cp.start()
        cp.wait()
        gather(0)

        @pl.when(n > 1)
        def _():
            idx_copy(1, 1).start()

    @pl.when(i + 1 < n)
    def _():
        idx_copy(0, nxt).wait()
        gather(nxt)

    @pl.when(i + 2 < n)
    def _():
        idx_copy(i + 2, slot).start()

    for rows in (rows0, rows1):
        pltpu.make_async_copy(y_hbm.at[pl.ds(0, DT * ROW_SUB), :], rows.at[slot], sem_g.at[slot]).wait()

    route = route_ref[...]
    gate1 = route[:, 2:3]
    gate2 = route[:, 3:4]
    y = gate1 * _load_rows(rows0.at[slot], DT) + gate2 * _load_rows(rows1.at[slot], DT)
    g2 = mod_ref[0][:, 5 * D:6 * D]
    out = h1_ref[...] + g2 * y
    if final:
        msq = jnp.mean(out * out, axis=-1, keepdims=True)
        out = out * lax.rsqrt(msq + EPS) * gf_ref[...]
    o_ref[...] = out


def _combine_call(idx, h1, route, mod3, g_final, yb, n_lat_dt, dt_per_sample, n_mod_ctx, final):
    N, D = h1.shape
    n_tiles = N // DT

    def mod_map(i):
        return (jnp.where(i < n_lat_dt, i // dt_per_sample, n_mod_ctx), 0, 0)

    return pl.pallas_call(
        functools.partial(_combine_kernel, final=final),
        grid=(n_tiles,),
        in_specs=[
            pl.BlockSpec(memory_space=pl.ANY),
            pl.BlockSpec((DT, D), lambda i: (i, 0)),
            pl.BlockSpec((DT, LANE), lambda i: (i, 0)),
            pl.BlockSpec((1, 1, 6 * D), mod_map),
            pl.BlockSpec((1, D), lambda i: (0, 0)),
            pl.BlockSpec(memory_space=pl.ANY),
        ],
        out_specs=pl.BlockSpec((DT, D), lambda i: (i, 0)),
        out_shape=jax.ShapeDtypeStruct((N, D), F32),
        scratch_shapes=[
            pltpu.SMEM((2, DT // IDX_UNROLL, LANE), I32),
            pltpu.VMEM((2, DT * ROW_SUB, LANE), F32),
            pltpu.VMEM((2, DT * ROW_SUB, LANE), F32),
            pltpu.SemaphoreType.DMA((2,)),
            pltpu.SemaphoreType.DMA((2,)),
        ],
        compiler_params=_cparams(("arbitrary",)),
        name="moe_combine",
    )(idx, h1, route, mod3, g_final, yb)


def _layer_weights(l, w_in, g_norm1, ln_v_g, ln_v_b, w_sp, b_sp, w_gate_up, b_gate, g_gla, w_out, g_norm2,
                   w_router_g, b_router_g, w_router_e, b_router_e):
    D = w_in.shape[1]
    d_in = w_in.shape[2]
    pad = (-d_in) % LANE
    w_in_p = jnp.pad(w_in[l], ((0, 0), (0, pad))).astype(BF16)
    KW = B_KEY_WIDTH
    wg = jnp.zeros((LANE, 2 * KW), F32)
    wg = wg.at[0:GATE_RANK, 0:KW].set(w_gate_up[l, 0])
    wg = wg.at[GATE_RANK:2 * GATE_RANK, KW:2 * KW].set(w_gate_up[l, 1])
    bg = jnp.concatenate([b_gate[l, 0], b_gate[l, 1]])[None, :]
    bsp = jnp.repeat(b_sp[l].T, A_GROUP_DIM, axis=1)
    wr = jnp.zeros((LANE, D), F32)
    wr = wr.at[0:N_EXPERTS, :].set(w_router_e[l].T)
    wr = wr.at[N_EXPERTS:N_EXPERTS + N_GROUPS, :].set(w_router_g[l].T)
    br = jnp.zeros((LANE, 1), F32)
    br = br.at[0:N_EXPERTS, 0].set(b_router_e[l])
    br = br.at[N_EXPERTS:N_EXPERTS + N_GROUPS, 0].set(b_router_g[l])
    r = jnp.arange(TT, dtype=I32)
    same = (r[:, None] // GLA_CHUNK) == (r[None, :] // GLA_CHUNK)
    tri_f = (same & (r[:, None] >= r[None, :])).astype(BF16)
    tri_b = (same & (r[:, None] <= r[None, :])).astype(BF16)
    u_strict = (r[:, None] < r[None, :]).astype(BF16)
    eye_k = jnp.eye(B_KEY_WIDTH, dtype=BF16)
    return dict(
        w_in=w_in_p, g_norm1=g_norm1[l][None, :], ln_g=ln_v_g[l][None, :], ln_b=ln_v_b[l][None, :],
        w_sp=w_sp[l].astype(BF16), b_sp=bsp, w_gate=wg.astype(BF16), b_gate=bg,
        g_gla=g_gla[l][None, :], w_out=w_out[l].astype(BF16), g_norm2=g_norm2[l][None, :],
        w_router=wr.astype(BF16), b_router=br, tri_f=tri_f, tri_b=tri_b, u_strict=u_strict, eye_k=eye_k)


def _segment_layout(counts_col, n_tokens):
    counts = counts_col[:N_EXPERTS, 0].astype(I32)
    padded = (counts + MOE_BLK - 1) // MOE_BLK * MOE_BLK
    pad_ends = jnp.cumsum(padded)
    pad_starts = pad_ends - padded
    n_blocks = (n_tokens * 2 + N_EXPERTS * (MOE_BLK - 1)) // MOE_BLK + 1
    block_start = jnp.arange(n_blocks, dtype=I32) * MOE_BLK
    block_expert = jnp.sum((pad_ends[None, :] <= block_start[:, None]).astype(I32), axis=1)
    block_expert = jnp.minimum(block_expert, N_EXPERTS - 1)
    n_used = (pad_ends[-1] // MOE_BLK).astype(I32)[None]
    blk_id = jnp.arange(n_blocks, dtype=I32)
    prev = jnp.concatenate([block_expert[:1], block_expert[:-1]])
    first = ((blk_id == 0) | (block_expert != prev)) & (blk_id < n_used[0])
    w_slot = (jnp.cumsum(first.astype(I32)) - 1) & 1
    e_id = jnp.arange(N_EXPERTS, dtype=I32)
    later_used = (e_id[None, :] > e_id[:, None]) & (padded[None, :] > 0)
    next_used = jnp.min(jnp.where(later_used, e_id[None, :], N_EXPERTS), axis=1)
    next_used = jnp.where(next_used >= N_EXPERTS, -1, next_used)
    next_expert = jnp.sum(jnp.where(block_expert[:, None] == e_id[None, :], next_used[None, :], 0), axis=1).astype(I32)
    return pad_starts, pad_ends, block_expert, n_used, w_slot.astype(I32), next_expert, n_blocks * MOE_BLK


def kernel(x, c, ctx, c_ctx, w_mod, b_mod, g_norm1, w_in, ln_v_g, ln_v_b, w_sp, b_sp, w_gate_up, b_gate, g_gla,
           w_out, g_norm2, w_router_g, b_router_g, w_router_e, b_router_e, w1, w3, w2, g_final):
    Bn, S, D = x.shape
    Lc = ctx.shape[1]
    depth = w_mod.shape[0]
    assert S % DT == 0 and S % ST == 0 and (Bn * Lc) % DT == 0 and (Bn * Lc) % ST == 0 and Lc == TT
    n_lat = Bn * S
    n_ctx = Bn * Lc
    n_lat_tiles = n_lat // TT
    tiles_per_sample = S // TT
    n_lat_steps = n_lat // ST
    steps_per_sample = S // ST

    cc = jnp.concatenate([c, c_ctx[None, :]], axis=0)
    mod_all = _modulation(cc, w_mod, b_mod)
    gf = g_final[None, :]

    lat, cx, ctx_off = x.reshape(n_lat, D), ctx.reshape(n_ctx, D), 0
    for l in range(depth):
        last = l == depth - 1
        lw = _layer_weights(l, w_in, g_norm1, ln_v_g, ln_v_b, w_sp, b_sp, w_gate_up, b_gate, g_gla, w_out,
                            g_norm2, w_router_g, b_router_g, w_router_e, b_router_e)
        mod3 = mod_all[l].reshape(Bn + 1, 1, 6 * D)
        a_out, qe, oi, sr, kv, dec = _pre_call(lat, cx, ctx_off, n_ctx // ST, mod3, lw, n_lat_steps,
                                               steps_per_sample, Bn)
        states = _scan_call(kv, dec, Bn, n_lat_tiles, tiles_per_sample)
        n_tok = n_lat if last else n_lat + n_ctx
        h1, hn2_rows, route, route_t, counts = _post_call(lat, cx, ctx_off, mod3, (a_out, qe, oi, sr), states, lw,
                                                          n_tok // ST, n_lat_steps, steps_per_sample, Bn)
        pad_starts, pad_ends, block_expert, n_used, w_slot, next_expert, n_buf_rows = _segment_layout(counts, n_tok)
        idx = _index_tiles(_plan_call(route_t, pad_starts))
        xs = _dispatch_call(pad_ends, idx, hn2_rows, n_buf_rows)
        yb = _ffn_call(block_expert, n_used, w_slot, next_expert, xs, w1, w3, w2, l)
        h_all = _combine_call(idx, h1, route, mod3, gf, yb, n_lat // DT, S // DT, Bn, last)
        lat, cx, ctx_off = h_all, h_all, n_lat_steps
    return h_all.reshape(Bn, S, D)
```

```python
import functools

import jax
import jax.numpy as jnp
from jax import lax
from jax.experimental import pallas as pl
from jax.experimental.pallas import tpu as pltpu

F32 = jnp.float32
BF16 = jnp.bfloat16
I32 = jnp.int32

EPS = 1e-6
LANE = 128
SUBLANE = 8

A_GROUPS = 4
A_GROUP_DIM = 128
A_WIDTH = A_GROUPS * A_GROUP_DIM
CHUNK_A = 128
B_HEADS = 4
B_DK = 64
B_DV = 128
B_KEY_WIDTH = B_HEADS * B_DK
B_WIDTH = B_HEADS * B_DV
GATE_RANK = 16
GATE_TAU = 16.0
CHUNK_B = 64
N_GROUPS = 8
EXPERTS_PER_GROUP = 8
N_EXPERTS = N_GROUPS * EXPERTS_PER_GROUP

TT = 256
GLA_CHUNK = 2 * CHUNK_B
CPT = TT // GLA_CHUNK
SUB = 4
ST = SUB * TT
MOE_BLK = 256
DT = 512
ROW_SUB = 8

VMEM_LIMIT = 48 * 1024 * 1024
VMEM_LIMIT_TOKEN_LOCAL = 56 * 1024 * 1024


def _cparams(sem, vmem_limit=VMEM_LIMIT):
    return pltpu.CompilerParams(dimension_semantics=sem, vmem_limit_bytes=vmem_limit)


def _load_rows(ref, n):
    return jnp.concatenate([ref[pl.ds(s, n, stride=ROW_SUB), :] for s in range(ROW_SUB)], axis=1)


def _store_rows(ref, val, first=0):
    n = val.shape[0]
    for s in range(ROW_SUB):
        ref[pl.ds(first * ROW_SUB + s, n, stride=ROW_SUB), :] = val[:, s * LANE:(s + 1) * LANE]


def _mod_kernel(c_ref, w_ref, b_ref, o_ref):
    c = c_ref[...]
    s = (c * jax.nn.sigmoid(c)).astype(BF16)
    o_ref[0] = jnp.dot(s, w_ref[0].astype(BF16), preferred_element_type=F32) + b_ref[0]


def _modulation(cc, w_mod, b_mod):
    L, D, D6 = w_mod.shape
    R = cc.shape[0]
    tn = 1536
    return pl.pallas_call(
        _mod_kernel,
        grid=(L, D6 // tn),
        in_specs=[
            pl.BlockSpec((R, D), lambda l, j: (0, 0)),
            pl.BlockSpec((1, D, tn), lambda l, j: (l, 0, j)),
            pl.BlockSpec((1, 1, tn), lambda l, j: (l, 0, j)),
        ],
        out_specs=pl.BlockSpec((1, R, tn), lambda l, j: (l, 0, j)),
        out_shape=jax.ShapeDtypeStruct((L, R, D6), F32),
        compiler_params=_cparams(("arbitrary", "arbitrary")),
        name="modulation",
    )(cc, w_mod, b_mod.reshape(L, 1, D6))


def _select_tile(lat_ref, ctx_ref, n_lat_steps):
    return jnp.where(pl.program_id(0) < n_lat_steps, lat_ref[...], ctx_ref[...])


def _stream_specs(D, n_lat_steps, ctx_off):
    lat = pl.BlockSpec((ST, D), lambda i: (jnp.minimum(i, n_lat_steps - 1), 0))
    ctx = pl.BlockSpec((ST, D), lambda i: (ctx_off + jnp.maximum(i - n_lat_steps, 0), 0))
    return lat, ctx


def _pre_kernel(lat_ref, ctx_ref, mod_ref, g1_ref, win_ref, lng_ref, lnb_ref, wsp_ref, bsp_ref, wg_ref, bg_ref,
                trif_ref, trib_ref, eye_ref,
                a_ref, qe_ref, oi_ref, sr_ref, kv_ref, dec_ref, *, n_lat_steps):
    D = lat_ref.shape[1]
    h = _select_tile(lat_ref, ctx_ref, n_lat_steps)
    mod = mod_ref[0]
    sh1 = mod[:, 0:D]
    sc1 = mod[:, D:2 * D]
    ms = jnp.mean(h * h, axis=-1, keepdims=True)
    hn = h * lax.rsqrt(ms + EPS) * g1_ref[...]
    hn = hn * (1.0 + sc1) + sh1
    z = jnp.dot(hn.astype(BF16), win_ref[...], preferred_element_type=F32)
    o_av = A_WIDTH
    o_q = 2 * A_WIDTH
    o_r = o_q + B_KEY_WIDTH
    o_k = o_r + B_WIDTH
    o_v = o_k + B_KEY_WIDTH
    o_g = o_v + B_WIDTH
    zu = z[:, 0:o_av]
    zv = z[:, o_av:o_q]
    q_all = z[:, o_q:o_r] * (B_DK ** -0.5)
    zr = z[:, o_r:o_k]
    k_all = z[:, o_k:o_v]
    vv_all = z[:, o_v:o_g]
    zg = z[:, o_g:o_g + LANE]

    u = jax.nn.gelu(zu)
    v = jax.nn.gelu(zv)
    s_groups = []
    for g in range(A_GROUPS):
        sl = slice(g * A_GROUP_DIM, (g + 1) * A_GROUP_DIM)
        vg = v[:, sl]
        mu = jnp.mean(vg, axis=-1, keepdims=True)
        dv = vg - mu
        var = jnp.mean(dv * dv, axis=-1, keepdims=True)
        vn = (dv * lax.rsqrt(var + EPS)) * lng_ref[:, sl] + lnb_ref[:, sl]
        vnb = vn.astype(BF16)
        n_ch = ST // CHUNK_A
        rhs = jnp.concatenate([vnb[c * CHUNK_A:(c + 1) * CHUNK_A, :] for c in range(n_ch)], axis=1)
        mixed = jnp.dot(wsp_ref[g], rhs, preferred_element_type=F32)
        bias = bsp_ref[:, sl]
        s_groups.append(jnp.concatenate(
            [mixed[:, c * A_GROUP_DIM:(c + 1) * A_GROUP_DIM] + bias for c in range(n_ch)], axis=0))
    a_ref[...] = (u * jnp.concatenate(s_groups, axis=1)).astype(BF16)

    sr_ref[...] = (zr * jax.nn.sigmoid(zr)).astype(BF16)
    lg = jnp.dot(zg.astype(BF16), wg_ref[...], preferred_element_type=F32) + bg_ref[...]
    la_all = (jnp.minimum(lg, 0.0) - jnp.log1p(jnp.exp(-jnp.abs(lg)))) * (1.0 / GATE_TAU)
    for t in range(SUB):
        rs = slice(t * TT, (t + 1) * TT)
        _gla_local(t, rs, la_all[rs, :], q_all[rs, :], k_all[rs, :], vv_all[rs, :],
                   trif_ref, trib_ref, eye_ref, qe_ref, oi_ref, kv_ref, dec_ref)


def _gla_local(t, rs, la, q, k, vv, trif_ref, trib_ref, eye_ref, qe_ref, oi_ref, kv_ref, dec_ref):
    la_hi = la.astype(BF16)
    la_lo = (la - la_hi.astype(F32)).astype(BF16)
    KW = B_KEY_WIDTH
    bf = (jnp.dot(trif_ref[...], la_hi[:, :KW], preferred_element_type=F32)
          + jnp.dot(trif_ref[...], la_lo[:, :KW], preferred_element_type=F32))
    bb = (jnp.dot(trib_ref[...], la_hi[:, KW:], preferred_element_type=F32)
          + jnp.dot(trib_ref[...], la_lo[:, KW:], preferred_element_type=F32))

    def chunk_rows(x, r):
        return jnp.concatenate(
            [jnp.broadcast_to(x[c * GLA_CHUNK + r:c * GLA_CHUNK + r + 1, :], (GLA_CHUNK, KW)) for c in range(CPT)], axis=0)

    blf = chunk_rows(bf, GLA_CHUNK - 1)
    blb = chunk_rows(bb, 0)
    bmf = chunk_rows(bf, CHUNK_B - 1)
    bmb = chunk_rows(bb, CHUNK_B)
    qe_ref[rs, :] = jnp.concatenate([q * jnp.exp(bf), q * jnp.exp(bb)], axis=1).astype(BF16)
    kd_f = k * jnp.exp(blf - bf)
    kd_b = k * jnp.exp(blb - bb)
    qe_f = q * jnp.exp(bf - bmf)
    qe_b = q * jnp.exp(bb - bmb)
    ke_f = (k * jnp.exp(bmf - bf)).astype(BF16)
    ke_b = (k * jnp.exp(bmb - bb)).astype(BF16)
    nt = (((1,), (1,)), ((), ()))
    dec_rows = ([jnp.exp(bf[c * GLA_CHUNK + GLA_CHUNK - 1:(c + 1) * GLA_CHUNK, :]) for c in range(CPT)]
                + [jnp.exp(bb[c * GLA_CHUNK:c * GLA_CHUNK + 1, :]) for c in range(CPT)]
                + [jnp.zeros((LANE - 2 * CPT, KW), F32)])
    dec_pad = jnp.concatenate(dec_rows, axis=0)
    dec_hi = dec_pad.astype(BF16)
    dec_lo = (dec_pad - dec_hi.astype(F32)).astype(BF16)
    dec_ref[t] = (lax.dot_general(eye_ref[...], dec_hi, nt, preferred_element_type=F32)
                  + lax.dot_general(eye_ref[...], dec_lo, nt, preferred_element_type=F32))

    row = lax.broadcasted_iota(I32, (TT, TT), 0)
    col = lax.broadcasted_iota(I32, (TT, TT), 1)
    chunk_shift = GLA_CHUNK.bit_length() - 1
    same = lax.shift_right_logical(row, chunk_shift) == lax.shift_right_logical(col, chunk_shift)
    m_f = same & (row >= col)
    m_b = same & (row <= col)
    lane_head = lax.shift_right_logical(lax.broadcasted_iota(I32, (1, KW), 1), B_DK.bit_length() - 1)
    tok_chunk = lax.shift_right_logical(lax.broadcasted_iota(I32, (1, TT), 1), chunk_shift)
    vb = vv.astype(BF16)
    kdT_f = kd_f.T
    kdT_b = kd_b.T
    oi_heads = []
    for hd in range(B_HEADS):
        hm = lane_head == hd
        qf = jnp.where(hm, qe_f, 0.0).astype(BF16)
        qb = jnp.where(hm, qe_b, 0.0).astype(BF16)
        att_f = lax.dot_general(qf, ke_f, nt, preferred_element_type=F32)
        att_b = lax.dot_general(qb, ke_b, nt, preferred_element_type=F32)
        att = jnp.where(m_f, att_f, 0.0) + jnp.where(m_b, att_b, 0.0)
        v_h = vb[:, hd * B_DV:(hd + 1) * B_DV]
        oi_heads.append(jnp.dot(att.astype(BF16), v_h, preferred_element_type=F32))
        parts = []
        for kdT in (kdT_f, kdT_b):
            kh = kdT[hd * B_DK:(hd + 1) * B_DK, :]
            for c in range(CPT):
                parts.append(jnp.where(tok_chunk == c, kh, 0.0))
        lhs = jnp.concatenate(parts, axis=0).astype(BF16)
        kv_ref[t, hd] = jnp.dot(lhs, v_h, preferred_element_type=F32)
    oi_ref[rs, :] = jnp.concatenate(oi_heads, axis=1)


def _pre_call(lat, ctx, ctx_off, n_ctx_steps, mod3, lw, n_lat_steps, steps_per_sample, n_mod_ctx):
    D = lat.shape[1]
    n_steps = n_lat_steps + n_ctx_steps
    n_tiles = n_steps * SUB
    N = n_steps * ST
    lat_spec, ctx_spec = _stream_specs(D, n_lat_steps, ctx_off)
    DINP = lw["w_in"].shape[1]
    KVR = 2 * CPT * B_DK

    def mod_map(i):
        return (jnp.where(i < n_lat_steps, i // steps_per_sample, n_mod_ctx), 0, 0)

    const2 = lambda i: (0, 0)
    const3 = lambda i: (0, 0, 0)
    tile2 = lambda i: (i, 0)
    return pl.pallas_call(
        functools.partial(_pre_kernel, n_lat_steps=n_lat_steps),
        grid=(n_steps,),
        in_specs=[
            lat_spec,
            ctx_spec,
            pl.BlockSpec((1, 1, 6 * D), mod_map),
            pl.BlockSpec((1, D), const2),
            pl.BlockSpec((D, DINP), const2),
            pl.BlockSpec((1, A_WIDTH), const2),
            pl.BlockSpec((1, A_WIDTH), const2),
            pl.BlockSpec((A_GROUPS, CHUNK_A, CHUNK_A), const3),
            pl.BlockSpec((CHUNK_A, A_WIDTH), const2),
            pl.BlockSpec((LANE, 2 * B_KEY_WIDTH), const2),
            pl.BlockSpec((1, 2 * B_KEY_WIDTH), const2),
            pl.BlockSpec((TT, TT), const2),
            pl.BlockSpec((TT, TT), const2),
            pl.BlockSpec((B_KEY_WIDTH, B_KEY_WIDTH), const2),
        ],
        out_specs=[
            pl.BlockSpec((ST, A_WIDTH), tile2),
            pl.BlockSpec((ST, 2 * B_KEY_WIDTH), tile2),
            pl.BlockSpec((ST, B_WIDTH), tile2),
            pl.BlockSpec((ST, B_WIDTH), tile2),
            pl.BlockSpec((SUB, B_HEADS, KVR, B_DV), lambda i: (i, 0, 0, 0)),
            pl.BlockSpec((SUB, B_KEY_WIDTH, LANE), lambda i: (i, 0, 0)),
        ],
        out_shape=[
            jax.ShapeDtypeStruct((N, A_WIDTH), BF16),
            jax.ShapeDtypeStruct((N, 2 * B_KEY_WIDTH), BF16),
            jax.ShapeDtypeStruct((N, B_WIDTH), F32),
            jax.ShapeDtypeStruct((N, B_WIDTH), BF16),
            jax.ShapeDtypeStruct((n_tiles, B_HEADS, KVR, B_DV), F32),
            jax.ShapeDtypeStruct((n_tiles, B_KEY_WIDTH, LANE), F32),
        ],
        compiler_params=_cparams(("arbitrary",), VMEM_LIMIT_TOKEN_LOCAL),
        name="mix_pre",
    )(lat, ctx, mod3, lw["g_norm1"], lw["w_in"], lw["ln_g"], lw["ln_b"], lw["w_sp"], lw["b_sp"],
      lw["w_gate"], lw["b_gate"], lw["tri_f"], lw["tri_b"], lw["eye_k"])


def _scan_kernel(kvc_ref, kvl_ref, dcc_ref, dcl_ref, sc_ref, sl_ref):
    n_lat = kvl_ref.shape[0]
    fwd = [(kvc_ref, dcc_ref, sc_ref, 0, c) for c in range(CPT)]
    fwd += [(kvl_ref, dcl_ref, sl_ref, t, c) for t in range(n_lat) for c in range(CPT)]
    bwd = [(kvc_ref, dcc_ref, sc_ref, 0, c) for c in reversed(range(CPT))]
    bwd += [(kvl_ref, dcl_ref, sl_ref, t, c) for t in reversed(range(n_lat)) for c in reversed(range(CPT))]
    for hd in range(B_HEADS):
        for d, seq in enumerate((fwd, bwd)):
            s = jnp.zeros((B_DK, B_DV), F32)
            for kv_ref, dc_ref, out_ref, t, c in seq:
                r0 = (d * CPT + c) * B_DK
                out_ref[t, hd, r0:r0 + B_DK, :] = s.astype(BF16)
                dcol = dc_ref[t, hd * B_DK:(hd + 1) * B_DK, d * CPT + c:d * CPT + c + 1]
                s = dcol * s + kv_ref[t, hd, r0:r0 + B_DK, :]


def _scan_call(kv, dec, n_samples, n_lat_tiles, tiles_per_sample):
    n_tiles, _, KVR, _ = kv.shape
    kv_c = pl.BlockSpec((1, B_HEADS, KVR, B_DV), lambda b: (n_lat_tiles + b, 0, 0, 0))
    kv_l = pl.BlockSpec((tiles_per_sample, B_HEADS, KVR, B_DV), lambda b: (b, 0, 0, 0))
    dc_c = pl.BlockSpec((1, B_KEY_WIDTH, LANE), lambda b: (n_lat_tiles + b, 0, 0))
    dc_l = pl.BlockSpec((tiles_per_sample, B_KEY_WIDTH, LANE), lambda b: (b, 0, 0))
    s_ctx, s_lat = pl.pallas_call(
        _scan_kernel,
        grid=(n_samples,),
        in_specs=[kv_c, kv_l, dc_c, dc_l],
        out_specs=[
            pl.BlockSpec((1, B_HEADS, KVR, B_DV), lambda b: (b, 0, 0, 0)),
            pl.BlockSpec((tiles_per_sample, B_HEADS, KVR, B_DV), lambda b: (b, 0, 0, 0)),
        ],
        out_shape=[
            jax.ShapeDtypeStruct((n_samples, B_HEADS, KVR, B_DV), BF16),
            jax.ShapeDtypeStruct((n_lat_tiles, B_HEADS, KVR, B_DV), BF16),
        ],
        compiler_params=_cparams(("arbitrary",)),
        name="gla_scan",
    )(kv, kv, dec, dec)
    return jnp.concatenate([s_lat, s_ctx], axis=0)


def _post_kernel(lat_ref, ctx_ref, mod_ref, a_ref, qe_ref, oi_ref, sr_ref, st_ref, ggla_ref, wout_ref, g2_ref,
                 wr_ref, br_ref, ustr_ref,
                 h1_ref, hn2_ref, route_ref, routet_ref, cnt_ref,
                 rhs_scr, cnt_scr, *, n_lat_steps):
    D = lat_ref.shape[1]
    i = pl.program_id(0)

    @pl.when(i == 0)
    def _():
        rhs_scr[...] = jnp.zeros(rhs_scr.shape, rhs_scr.dtype)
        cnt_scr[...] = jnp.zeros(cnt_scr.shape, cnt_scr.dtype)

    qe = qe_ref[...]
    o_parts = []
    for c in range(SUB * CPT):
        t, ct = divmod(c, CPT)
        for d in range(2):
            for hd in range(B_HEADS):
                r0 = d * B_KEY_WIDTH + hd * B_DK
                s0 = (d * CPT + ct) * B_DK
                rhs_scr[c, r0:r0 + B_DK, hd * B_DV:(hd + 1) * B_DV] = st_ref[t, hd, s0:s0 + B_DK, :]
        o_parts.append(jnp.dot(qe[c * GLA_CHUNK:(c + 1) * GLA_CHUNK, :], rhs_scr[c], preferred_element_type=F32))
    o = oi_ref[...] + jnp.concatenate(o_parts, axis=0)

    heads = []
    for hd in range(B_HEADS):
        sl = slice(hd * B_DV, (hd + 1) * B_DV)
        oh = o[:, sl]
        msq = jnp.mean(oh * oh, axis=-1, keepdims=True)
        heads.append(oh * lax.rsqrt(msq + EPS) * ggla_ref[:, sl])
    b_out = jnp.concatenate(heads, axis=1) * sr_ref[...].astype(F32)
    mix_in = jnp.concatenate([a_ref[...], b_out.astype(BF16)], axis=1)
    mix = jnp.dot(mix_in, wout_ref[...], preferred_element_type=F32)

    mod = mod_ref[0]
    g1 = mod[:, 2 * D:3 * D]
    sh2 = mod[:, 3 * D:4 * D]
    sc2 = mod[:, 4 * D:5 * D]
    h_in = _select_tile(lat_ref, ctx_ref, n_lat_steps)
    nt = (((1,), (1,)), ((), ()))
    neg = jnp.float32(-3.0e38)
    big = jnp.float32(1.0e9)
    sub = lax.broadcasted_iota(I32, (EXPERTS_PER_GROUP, LANE), 0).astype(F32)
    route_parts = []
    for hf in range(ST // LANE):
        rs = slice(hf * LANE, (hf + 1) * LANE)
        h1 = h_in[rs, :] + g1 * mix[rs, :]
        h1_ref[rs, :] = h1
        msq = jnp.mean(h1 * h1, axis=-1, keepdims=True)
        hn2 = h1 * lax.rsqrt(msq + EPS) * g2_ref[...]
        hn2 = hn2 * (1.0 + sc2) + sh2
        _store_rows(hn2_ref, hn2, hf * LANE)

        lgt = lax.dot_general(wr_ref[...], hn2.astype(BF16), nt, preferred_element_type=F32) + br_ref[...]
        gl = lgt[N_EXPERTS:N_EXPERTS + N_GROUPS, :]
        gmax = jnp.max(gl, axis=0, keepdims=True)
        p_top = 1.0 / jnp.sum(jnp.exp(gl - gmax), axis=0, keepdims=True)
        gidx = jnp.min(jnp.where(gl == gmax, sub, big), axis=0, keepdims=True)
        el = jnp.zeros((EXPERTS_PER_GROUP, LANE), F32)
        for g in range(N_GROUPS):
            el = el + jnp.where(gidx == float(g), lgt[g * EXPERTS_PER_GROUP:(g + 1) * EXPERTS_PER_GROUP, :], 0.0)
        m1 = jnp.max(el, axis=0, keepdims=True)
        i1 = jnp.min(jnp.where(el == m1, sub, big), axis=0, keepdims=True)
        el2 = jnp.where(sub == i1, neg, el)
        m2 = jnp.max(el2, axis=0, keepdims=True)
        i2 = jnp.min(jnp.where(el2 == m2, sub, big), axis=0, keepdims=True)
        t = jnp.exp(m2 - m1)
        route_parts.append((gidx * float(EXPERTS_PER_GROUP) + i1, gidx * float(EXPERTS_PER_GROUP) + i2,
                            p_top / (1.0 + t), p_top * t / (1.0 + t)))
    e1_all, e2_all, gate1_all, gate2_all = (jnp.concatenate(p, axis=1) for p in zip(*route_parts))

    rowf = lax.broadcasted_iota(I32, (LANE, TT), 0).astype(F32)
    second = lax.broadcasted_iota(I32, (LANE, 1), 0) >= N_EXPERTS
    base = cnt_scr[...]
    for t in range(SUB):
        ls = slice(t * TT, (t + 1) * TT)
        e1, e2 = e1_all[:, ls], e2_all[:, ls]
        sel1 = rowf == e1
        sel2 = rowf == (e2 + float(N_EXPERTS))
        oh = jnp.where(sel1, 1.0, 0.0) + jnp.where(sel2, 1.0, 0.0)
        prefix = jnp.dot(oh.astype(BF16), ustr_ref[...], preferred_element_type=F32)
        tot = jnp.sum(oh, axis=1, keepdims=True)
        tot_sw = jnp.concatenate([tot[N_EXPERTS:, :], tot[:N_EXPERTS, :]], axis=0)
        val = prefix + (base + jnp.where(second, tot_sw, 0.0))
        r1 = jnp.sum(jnp.where(sel1, val, 0.0), axis=0, keepdims=True)
        r2 = jnp.sum(jnp.where(sel2, val, 0.0), axis=0, keepdims=True)
        base = base + tot + tot_sw
        zero = jnp.zeros_like(e1)
        route_t = jnp.concatenate([e1, e2, gate1_all[:, ls], gate2_all[:, ls], r1, r2, zero, zero], axis=0)
        routet_ref[:, ls] = route_t
        route_ref[ls, :] = jnp.concatenate([route_t, jnp.zeros((LANE - SUBLANE, TT), F32)], axis=0).T
    cnt_scr[...] = base
    cnt_ref[...] = base


def _post_call(lat, ctx, ctx_off, mod3, pre_outs, states, lw, n_post_steps, n_lat_steps, steps_per_sample, n_mod_ctx):
    a_out, qe, oi, sr = pre_outs
    D = lat.shape[1]
    KVR = 2 * CPT * B_DK
    NP = n_post_steps * ST
    lat_spec, ctx_spec = _stream_specs(D, n_lat_steps, ctx_off)

    def mod_map(i):
        return (jnp.where(i < n_lat_steps, i // steps_per_sample, n_mod_ctx), 0, 0)

    const2 = lambda i: (0, 0)
    tile2 = lambda i: (i, 0)
    return pl.pallas_call(
        functools.partial(_post_kernel, n_lat_steps=n_lat_steps),
        grid=(n_post_steps,),
        in_specs=[
            lat_spec,
            ctx_spec,
            pl.BlockSpec((1, 1, 6 * D), mod_map),
            pl.BlockSpec((ST, A_WIDTH), tile2),
            pl.BlockSpec((ST, 2 * B_KEY_WIDTH), tile2),
            pl.BlockSpec((ST, B_WIDTH), tile2),
            pl.BlockSpec((ST, B_WIDTH), tile2),
            pl.BlockSpec((SUB, B_HEADS, KVR, B_DV), lambda i: (i, 0, 0, 0)),
            pl.BlockSpec((1, B_WIDTH), const2),
            pl.BlockSpec((D, D), const2),
            pl.BlockSpec((1, D), const2),
            pl.BlockSpec((LANE, D), const2),
            pl.BlockSpec((LANE, 1), const2),
            pl.BlockSpec((TT, TT), const2),
        ],
        out_specs=[
            pl.BlockSpec((ST, D), tile2),
            pl.BlockSpec((ST * ROW_SUB, LANE), tile2),
            pl.BlockSpec((ST, LANE), tile2),
            pl.BlockSpec((SUBLANE, ST), lambda i: (0, i)),
            pl.BlockSpec((LANE, 1), const2),
        ],
        out_shape=[
            jax.ShapeDtypeStruct((NP, D), F32),
            jax.ShapeDtypeStruct((NP * ROW_SUB, LANE), F32),
            jax.ShapeDtypeStruct((NP, LANE), F32),
            jax.ShapeDtypeStruct((SUBLANE, NP), F32),
            jax.ShapeDtypeStruct((LANE, 1), F32),
        ],
        scratch_shapes=[
            pltpu.VMEM((SUB * CPT, 2 * B_KEY_WIDTH, B_WIDTH), BF16),
            pltpu.VMEM((LANE, 1), F32),
        ],
        compiler_params=_cparams(("arbitrary",), VMEM_LIMIT_TOKEN_LOCAL),
        name="mix_post",
    )(lat, ctx, mod3, a_out, qe, oi, sr, states, lw["g_gla"], lw["w_out"], lw["g_norm2"],
      lw["w_router"], lw["b_router"], lw["u_strict"])


def _plan_kernel(rt_ref, ps_ref, idx_ref):
    sub = lax.broadcasted_iota(I32, (N_EXPERTS, LANE), 0).astype(F32)
    ps = ps_ref[...]
    segs = DT // LANE
    for k in range(2):
        for seg in range(segs):
            sl = slice(seg * LANE, (seg + 1) * LANE)
            e = rt_ref[k:k + 1, sl]
            r = rt_ref[4 + k:5 + k, sl]
            base = jnp.sum(jnp.where(sub == e, ps, 0.0), axis=0, keepdims=True)
            idx_ref[0, k * segs + seg:k * segs + seg + 1, :] = ((base + r) * float(ROW_SUB)).astype(I32)


def _plan_call(route_t, pad_starts):
    n_dt = route_t.shape[1] // DT
    return pl.pallas_call(
        _plan_kernel,
        grid=(n_dt,),
        in_specs=[
            pl.BlockSpec((SUBLANE, DT), lambda i: (0, i)),
            pl.BlockSpec((N_EXPERTS, 1), lambda i: (0, 0)),
        ],
        out_specs=pl.BlockSpec((1, 2 * DT // LANE, LANE), lambda i: (i, 0, 0)),
        out_shape=jax.ShapeDtypeStruct((n_dt, 2 * DT // LANE, LANE), I32),
        compiler_params=_cparams(("arbitrary",)),
        name="moe_plan",
    )(route_t, pad_starts.astype(F32)[:, None])


IDX_UNROLL = 8


def _row_loop(n_rows, fn):
    def body(jo, carry):
        for u in range(IDX_UNROLL):
            fn(jo * IDX_UNROLL + u, jo, u)
        return carry

    lax.fori_loop(0, n_rows // IDX_UNROLL, body, 0)


def _tile_indices(idx_smem, jo, u):
    d0 = pl.multiple_of(idx_smem[jo, u], ROW_SUB)
    d1 = pl.multiple_of(idx_smem[jo, IDX_UNROLL + u], ROW_SUB)
    return d0, d1


def _index_tiles(idx):
    n = idx.shape[0]
    d = idx.reshape(n, 2, DT // IDX_UNROLL, IDX_UNROLL).transpose(0, 2, 1, 3).reshape(n, DT // IDX_UNROLL, 2 * IDX_UNROLL)
    return jnp.pad(d, ((0, 0), (0, 0), (0, LANE - 2 * IDX_UNROLL)))


def _dispatch_kernel(pe_ref, idx_hbm, x_ref, xs_out, idx_smem, zbuf, sem_i, sem_o, sem_z):
    i = pl.program_id(0)
    cp = pltpu.make_async_copy(idx_hbm.at[i], idx_smem, sem_i)
    cp.start()
    blk_rows = MOE_BLK * ROW_SUB

    @pl.when(i == 0)
    def _():
        zbuf[...] = jnp.zeros(zbuf.shape, zbuf.dtype)
        for e in range(N_EXPERTS):
            start = pl.multiple_of(jnp.maximum(pe_ref[e] - MOE_BLK, 0) * ROW_SUB, blk_rows)
            pltpu.make_async_copy(zbuf, xs_out.at[pl.ds(start, blk_rows), :], sem_z).start()
        for e in range(N_EXPERTS):
            pltpu.make_async_copy(zbuf, xs_out.at[pl.ds(0, blk_rows), :], sem_z).wait()
        first_free = pe_ref[N_EXPERTS - 1] // MOE_BLK
        n_blocks = xs_out.shape[0] // blk_rows

        def clear(b, carry):
            start = pl.multiple_of(b * blk_rows, blk_rows)
            pltpu.make_async_copy(zbuf, xs_out.at[pl.ds(start, blk_rows), :], sem_z).start()
            return carry

        def clear_wait(b, carry):
            pltpu.make_async_copy(zbuf, xs_out.at[pl.ds(0, blk_rows), :], sem_z).wait()
            return carry

        lax.fori_loop(first_free, n_blocks, clear, 0)
        lax.fori_loop(first_free, n_blocks, clear_wait, 0)

    cp.wait()

    def issue(j, jo, u):
        d0, d1 = _tile_indices(idx_smem, jo, u)
        src = x_ref.at[pl.ds(pl.multiple_of(j * ROW_SUB, ROW_SUB), ROW_SUB), :]
        pltpu.make_async_copy(src, xs_out.at[pl.ds(d0, ROW_SUB), :], sem_o).start(priority=0)
        pltpu.make_async_copy(src, xs_out.at[pl.ds(d1, ROW_SUB), :], sem_o).start(priority=1)

    _row_loop(DT, issue)
    for _ in range(2):
        pltpu.make_async_copy(x_ref, xs_out.at[pl.ds(0, DT * ROW_SUB), :], sem_o).wait()


def _dispatch_call(pad_ends, idx, x_rows, n_buf_rows):
    n_tiles = x_rows.shape[0] // (DT * ROW_SUB)
    grid_spec = pltpu.PrefetchScalarGridSpec(
        num_scalar_prefetch=1,
        grid=(n_tiles,),
        in_specs=[
            pl.BlockSpec(memory_space=pl.ANY),
            pl.BlockSpec((DT * ROW_SUB, LANE), lambda i, pe: (i, 0)),
        ],
        out_specs=pl.BlockSpec(memory_space=pl.ANY),
        scratch_shapes=[
            pltpu.SMEM((DT // IDX_UNROLL, LANE), I32),
            pltpu.VMEM((MOE_BLK * ROW_SUB, LANE), F32),
            pltpu.SemaphoreType.DMA,
            pltpu.SemaphoreType.DMA,
            pltpu.SemaphoreType.DMA,
        ],
    )
    return pl.pallas_call(
        _dispatch_kernel,
        grid_spec=grid_spec,
        out_shape=jax.ShapeDtypeStruct((n_buf_rows * ROW_SUB, LANE), F32),
        compiler_params=_cparams(("arbitrary",)),
        name="moe_dispatch",
    )(pad_ends, idx, x_rows)


def _ffn_kernel(be_ref, nu_ref, ws_ref, nx_ref, x_ref, w1_hbm, w3_hbm, w2_hbm, y_ref,
                w1f, w3f, w2f, w1b, w3b, w2b, sem_w, *, layer):
    i = pl.program_id(0)
    nu = nu_ref[0]

    def weight_copies(e, s):
        return (pltpu.make_async_copy(w1_hbm.at[layer, e], w1f.at[s], sem_w.at[s]),
                pltpu.make_async_copy(w3_hbm.at[layer, e], w3f.at[s], sem_w.at[s]),
                pltpu.make_async_copy(w2_hbm.at[layer, e], w2f.at[s], sem_w.at[s]))

    @pl.when(i < nu)
    def _():
        e = be_ref[i]
        s = ws_ref[i]
        first_of_run = (i == 0) | (e != be_ref[jnp.maximum(i - 1, 0)])

        @pl.when(i == 0)
        def _():
            for cp in weight_copies(e, s):
                cp.start()

        @pl.when(first_of_run)
        def _():
            for cp in weight_copies(e, s):
                cp.wait()
            nxt = nx_ref[i]

            @pl.when(nxt >= 0)
            def _():
                for cp in weight_copies(nxt, 1 - s):
                    cp.start(priority=1)

            w1b[...] = w1f[s].astype(BF16)
            w3b[...] = w3f[s].astype(BF16)
            w2b[...] = w2f[s].astype(BF16)

        x = _load_rows(x_ref, MOE_BLK).astype(BF16)
        h1 = jnp.dot(x, w1b[...], preferred_element_type=F32)
        h3 = jnp.dot(x, w3b[...], preferred_element_type=F32)
        hh = (h1 * jax.nn.sigmoid(h1) * h3).astype(BF16)
        y = jnp.dot(hh, w2b[...], preferred_element_type=F32)
        _store_rows(y_ref, y)

    @pl.when(i >= nu)
    def _():
        y_ref[...] = jnp.zeros(y_ref.shape, y_ref.dtype)


def _ffn_call(block_expert, n_used, w_slot, next_expert, xs, w1, w3, w2, l):
    blk_rows = MOE_BLK * ROW_SUB
    NB = xs.shape[0] // blk_rows
    _, _, D, DE = w1.shape
    grid_spec = pltpu.PrefetchScalarGridSpec(
        num_scalar_prefetch=4,
        grid=(NB,),
        in_specs=[
            pl.BlockSpec((blk_rows, LANE), lambda i, be, nu, ws, nx: (jnp.minimum(i, nu[0] - 1), 0)),
            pl.BlockSpec(memory_space=pl.ANY),
            pl.BlockSpec(memory_space=pl.ANY),
            pl.BlockSpec(memory_space=pl.ANY),
        ],
        out_specs=pl.BlockSpec((blk_rows, LANE), lambda i, be, nu, ws, nx: (i, 0)),
        scratch_shapes=[
            pltpu.VMEM((2, D, DE), F32),
            pltpu.VMEM((2, D, DE), F32),
            pltpu.VMEM((2, DE, D), F32),
            pltpu.VMEM((D, DE), BF16),
            pltpu.VMEM((D, DE), BF16),
            pltpu.VMEM((DE, D), BF16),
            pltpu.SemaphoreType.DMA((2,)),
        ],
    )
    return pl.pallas_call(
        functools.partial(_ffn_kernel, layer=l),
        grid_spec=grid_spec,
        out_shape=jax.ShapeDtypeStruct(xs.shape, F32),
        compiler_params=_cparams(("arbitrary",)),
        name="moe_ffn",
    )(block_expert, n_used, w_slot, next_expert, xs, w1, w3, w2)


def _combine_kernel(idx_hbm, h1_ref, route_ref, mod_ref, gf_ref, y_hbm, o_ref, idx_smem, rows0, rows1, sem_i, sem_g,
                    *, final):
    D = h1_ref.shape[1]
    i = pl.program_id(0)
    n = pl.num_programs(0)
    slot = i & 1
    nxt = 1 - slot

    def idx_copy(tile, s):
        return pltpu.make_async_copy(idx_hbm.at[tile], idx_smem.at[s], sem_i.at[s])

    def gather(s):
        def issue(j, jo, u):
            d0, d1 = _tile_indices(idx_smem.at[s], jo, u)
            dst = pl.ds(pl.multiple_of(j * ROW_SUB, ROW_SUB), ROW_SUB)
            pltpu.make_async_copy(y_hbm.at[pl.ds(d0, ROW_SUB), :], rows0.at[s, dst, :], sem_g.at[s]).start(priority=0)
            pltpu.make_async_copy(y_hbm.at[pl.ds(d1, ROW_SUB), :], rows1.at[s, dst, :], sem_g.at[s]).start(priority=1)

        _row_loop(DT, issue)

    @pl.when(i == 0)
    def _():
        cp = idx_copy(0, 0)
        cp.start()
        cp.wait()
        gather(0)

        @pl.when(n > 1)
        def _():
            idx_copy(1, 1).start()

    @pl.when(i + 1 < n)
    def _():
        idx_copy(0, nxt).wait()
        gather(nxt)

    @pl.when(i + 2 < n)
    def _():
        idx_copy(i + 2, slot).start()

    for rows in (rows0, rows1):
        pltpu.make_async_copy(y_hbm.at[pl.ds(0, DT * ROW_SUB), :], rows.at[slot], sem_g.at[slot]).wait()

    route = route_ref[...]
    gate1 = route[:, 2:3]
    gate2 = route[:, 3:4]
    y = gate1 * _load_rows(rows0.at[slot], DT) + gate2 * _load_rows(rows1.at[slot], DT)
    g2 = mod_ref[0][:, 5 * D:6 * D]
    out = h1_ref[...] + g2 * y
    if final:
        msq = jnp.mean(out * out, axis=-1, keepdims=True)
        out = out * lax.rsqrt(msq + EPS) * gf_ref[...]
    o_ref[...] = out


def _combine_call(idx, h1, route, mod3, g_final, yb, n_lat_dt, dt_per_sample, n_mod_ctx, final):
    N, D = h1.shape
    n_tiles = N // DT

    def mod_map(i):
        return (jnp.where(i < n_lat_dt, i // dt_per_sample, n_mod_ctx), 0, 0)

    return pl.pallas_call(
        functools.partial(_combine_kernel, final=final),
        grid=(n_tiles,),
        in_specs=[
            pl.BlockSpec(memory_space=pl.ANY),
            pl.BlockSpec((DT, D), lambda i: (i, 0)),
            pl.BlockSpec((DT, LANE), lambda i: (i, 0)),
            pl.BlockSpec((1, 1, 6 * D), mod_map),
            pl.BlockSpec((1, D), lambda i: (0, 0)),
            pl.BlockSpec(memory_space=pl.ANY),
        ],
        out_specs=pl.BlockSpec((DT, D), lambda i: (i, 0)),
        out_shape=jax.ShapeDtypeStruct((N, D), F32),
        scratch_shapes=[
            pltpu.SMEM((2, DT // IDX_UNROLL, LANE), I32),
            pltpu.VMEM((2, DT * ROW_SUB, LANE), F32),
            pltpu.VMEM((2, DT * ROW_SUB, LANE), F32),
            pltpu.SemaphoreType.DMA((2,)),
            pltpu.SemaphoreType.DMA((2,)),
        ],
        compiler_params=_cparams(("arbitrary",)),
        name="moe_combine",
    )(idx, h1, route, mod3, g_final, yb)


def _layer_weights(l, w_in, g_norm1, ln_v_g, ln_v_b, w_sp, b_sp, w_gate_up, b_gate, g_gla, w_out, g_norm2,
                   w_router_g, b_router_g, w_router_e, b_router_e):
    D = w_in.shape[1]
    d_in = w_in.shape[2]
    pad = (-d_in) % LANE
    w_in_p = jnp.pad(w_in[l], ((0, 0), (0, pad))).astype(BF16)
    KW = B_KEY_WIDTH
    wg = jnp.zeros((LANE, 2 * KW), F32)
    wg = wg.at[0:GATE_RANK, 0:KW].set(w_gate_up[l, 0])
    wg = wg.at[GATE_RANK:2 * GATE_RANK, KW:2 * KW].set(w_gate_up[l, 1])
    bg = jnp.concatenate([b_gate[l, 0], b_gate[l, 1]])[None, :]
    bsp = jnp.repeat(b_sp[l].T, A_GROUP_DIM, axis=1)
    wr = jnp.zeros((LANE, D), F32)
    wr = wr.at[0:N_EXPERTS, :].set(w_router_e[l].T)
    wr = wr.at[N_EXPERTS:N_EXPERTS + N_GROUPS, :].set(w_router_g[l].T)
    br = jnp.zeros((LANE, 1), F32)
    br = br.at[0:N_EXPERTS, 0].set(b_router_e[l])
    br = br.at[N_EXPERTS:N_EXPERTS + N_GROUPS, 0].set(b_router_g[l])
    r = jnp.arange(TT, dtype=I32)
    same = (r[:, None] // GLA_CHUNK) == (r[None, :] // GLA_CHUNK)
    tri_f = (same & (r[:, None] >= r[None, :])).astype(BF16)
    tri_b = (same & (r[:, None] <= r[None, :])).astype(BF16)
    u_strict = (r[:, None] < r[None, :]).astype(BF16)
    eye_k = jnp.eye(B_KEY_WIDTH, dtype=BF16)
    return dict(
        w_in=w_in_p, g_norm1=g_norm1[l][None, :], ln_g=ln_v_g[l][None, :], ln_b=ln_v_b[l][None, :],
        w_sp=w_sp[l].astype(BF16), b_sp=bsp, w_gate=wg.astype(BF16), b_gate=bg,
        g_gla=g_gla[l][None, :], w_out=w_out[l].astype(BF16), g_norm2=g_norm2[l][None, :],
        w_router=wr.astype(BF16), b_router=br, tri_f=tri_f, tri_b=tri_b, u_strict=u_strict, eye_k=eye_k)


def _segment_layout(counts_col, n_tokens):
    counts = counts_col[:N_EXPERTS, 0].astype(I32)
    padded = (counts + MOE_BLK - 1) // MOE_BLK * MOE_BLK
    pad_ends = jnp.cumsum(padded)
    pad_starts = pad_ends - padded
    n_blocks = (n_tokens * 2 + N_EXPERTS * (MOE_BLK - 1)) // MOE_BLK + 1
    block_start = jnp.arange(n_blocks, dtype=I32) * MOE_BLK
    block_expert = jnp.sum((pad_ends[None, :] <= block_start[:, None]).astype(I32), axis=1)
    block_expert = jnp.minimum(block_expert, N_EXPERTS - 1)
    n_used = (pad_ends[-1] // MOE_BLK).astype(I32)[None]
    blk_id = jnp.arange(n_blocks, dtype=I32)
    prev = jnp.concatenate([block_expert[:1], block_expert[:-1]])
    first = ((blk_id == 0) | (block_expert != prev)) & (blk_id < n_used[0])
    w_slot = (jnp.cumsum(first.astype(I32)) - 1) & 1
    e_id = jnp.arange(N_EXPERTS, dtype=I32)
    later_used = (e_id[None, :] > e_id[:, None]) & (padded[None, :] > 0)
    next_used = jnp.min(jnp.where(later_used, e_id[None, :], N_EXPERTS), axis=1)
    next_used = jnp.where(next_used >= N_EXPERTS, -1, next_used)
    next_expert = jnp.sum(jnp.where(block_expert[:, None] == e_id[None, :], next_used[None, :], 0), axis=1).astype(I32)
    return pad_starts, pad_ends, block_expert, n_used, w_slot.astype(I32), next_expert, n_blocks * MOE_BLK


def kernel(x, c, ctx, c_ctx, w_mod, b_mod, g_norm1, w_in, ln_v_g, ln_v_b, w_sp, b_sp, w_gate_up, b_gate, g_gla,
           w_out, g_norm2, w_router_g, b_router_g, w_router_e, b_router_e, w1, w3, w2, g_final):
    Bn, S, D = x.shape
    Lc = ctx.shape[1]
    depth = w_mod.shape[0]
    assert S % DT == 0 and S % ST == 0 and (Bn * Lc) % DT == 0 and (Bn * Lc) % ST == 0 and Lc == TT
    n_lat = Bn * S
    n_ctx = Bn * Lc
    n_lat_tiles = n_lat // TT
    tiles_per_sample = S // TT
    n_lat_steps = n_lat // ST
    steps_per_sample = S // ST

    cc = jnp.concatenate([c, c_ctx[None, :]], axis=0)
    mod_all = _modulation(cc, w_mod, b_mod)
    gf = g_final[None, :]

    lat, cx, ctx_off = x.reshape(n_lat, D), ctx.reshape(n_ctx, D), 0
    for l in range(depth):
        last = l == depth - 1
        lw = _layer_weights(l, w_in, g_norm1, ln_v_g, ln_v_b, w_sp, b_sp, w_gate_up, b_gate, g_gla, w_out,
                            g_norm2, w_router_g, b_router_g, w_router_e, b_router_e)
        mod3 = mod_all[l].reshape(Bn + 1, 1, 6 * D)
        a_out, qe, oi, sr, kv, dec = _pre_call(lat, cx, ctx_off, n_ctx // ST, mod3, lw, n_lat_steps,
                                               steps_per_sample, Bn)
        states = _scan_call(kv, dec, Bn, n_lat_tiles, tiles_per_sample)
        n_tok = n_lat if last else n_lat + n_ctx
        h1, hn2_rows, route, route_t, counts = _post_call(lat, cx, ctx_off, mod3, (a_out, qe, oi, sr), states, lw,
                                                          n_tok // ST, n_lat_steps, steps_per_sample, Bn)
        pad_starts, pad_ends, block_expert, n_used, w_slot, next_expert, n_buf_rows = _segment_layout(counts, n_tok)
        idx = _index_tiles(_plan_call(route_t, pad_starts))
        xs = _dispatch_call(pad_ends, idx, hn2_rows, n_buf_rows)
        yb = _ffn_call(block_expert, n_used, w_slot, next_expert, xs, w1, w3, w2, l)
        h_all = _combine_call(idx, h1, route, mod3, gf, yb, n_lat // DT, S // DT, Bn, last)
        lat, cx, ctx_off = h_all, h_all, n_lat_steps
    return h_all.reshape(Bn, S, D)
```

```python
import functools

import jax
import jax.numpy as jnp
from jax import lax
from jax.experimental import pallas as pl
from jax.experimental.pallas import tpu as pltpu

F32 = jnp.float32
BF16 = jnp.bfloat16
I32 = jnp.int32

EPS = 1e-6
LANE = 128
SUBLANE = 8

A_GROUPS = 4
A_GROUP_DIM = 128
A_WIDTH = A_GROUPS * A_GROUP_DIM
CHUNK_A = 128
B_HEADS = 4
B_DK = 64
B_DV = 128
B_KEY_WIDTH = B_HEADS * B_DK
B_WIDTH = B_HEADS * B_DV
GATE_RANK = 16
GATE_TAU = 16.0
CHUNK_B = 64
N_GROUPS = 8
EXPERTS_PER_GROUP = 8
N_EXPERTS = N_GROUPS * EXPERTS_PER_GROUP

TT = 256
GLA_CHUNK = 2 * CHUNK_B
CPT = TT // GLA_CHUNK
SUB = 4
ST = SUB * TT
MOE_BLK = 256
DT = 512
ROW_SUB = 8

VMEM_LIMIT = 48 * 1024 * 1024
VMEM_LIMIT_TOKEN_LOCAL = 56 * 1024 * 1024
VMEM_LIMIT_ROW_MOVES = 24 * 1024 * 1024


def _cparams(sem, vmem_limit=VMEM_LIMIT):
    return pltpu.CompilerParams(dimension_semantics=sem, vmem_limit_bytes=vmem_limit)


def _load_rows(ref, n):
    return jnp.concatenate([ref[pl.ds(s, n, stride=ROW_SUB), :] for s in range(ROW_SUB)], axis=1)


def _store_rows(ref, val, first=0):
    n = val.shape[0]
    for s in range(ROW_SUB):
        ref[pl.ds(first * ROW_SUB + s, n, stride=ROW_SUB), :] = val[:, s * LANE:(s + 1) * LANE]


def _mod_kernel(c_ref, w_ref, b_ref, o_ref):
    c = c_ref[...]
    s = (c * jax.nn.sigmoid(c)).astype(BF16)
    o_ref[0] = jnp.dot(s, w_ref[0].astype(BF16), preferred_element_type=F32) + b_ref[0]


def _modulation(cc, w_mod, b_mod):
    L, D, D6 = w_mod.shape
    R = cc.shape[0]
    tn = 1536
    return pl.pallas_call(
        _mod_kernel,
        grid=(L, D6 // tn),
        in_specs=[
            pl.BlockSpec((R, D), lambda l, j: (0, 0)),
            pl.BlockSpec((1, D, tn), lambda l, j: (l, 0, j)),
            pl.BlockSpec((1, 1, tn), lambda l, j: (l, 0, j)),
        ],
        out_specs=pl.BlockSpec((1, R, tn), lambda l, j: (l, 0, j)),
        out_shape=jax.ShapeDtypeStruct((L, R, D6), F32),
        compiler_params=_cparams(("arbitrary", "arbitrary")),
        name="modulation",
    )(cc, w_mod, b_mod.reshape(L, 1, D6))


def _select_tile(lat_ref, ctx_ref, n_lat_steps):
    return jnp.where(pl.program_id(0) < n_lat_steps, lat_ref[...], ctx_ref[...])


def _stream_specs(D, n_lat_steps, ctx_off):
    lat = pl.BlockSpec((ST, D), lambda i: (jnp.minimum(i, n_lat_steps - 1), 0))
    ctx = pl.BlockSpec((ST, D), lambda i: (ctx_off + jnp.maximum(i - n_lat_steps, 0), 0))
    return lat, ctx


def _pre_kernel(lat_ref, ctx_ref, mod_ref, g1_ref, win_ref, lng_ref, lnb_ref, wsp_ref, bsp_ref, wg_ref, bg_ref,
                trif_ref, trib_ref, eye_ref,
                a_ref, qe_ref, oi_ref, sr_ref, kv_ref, dec_ref, *, n_lat_steps):
    D = lat_ref.shape[1]
    h = _select_tile(lat_ref, ctx_ref, n_lat_steps)
    mod = mod_ref[0]
    sh1 = mod[:, 0:D]
    sc1 = mod[:, D:2 * D]
    ms = jnp.mean(h * h, axis=-1, keepdims=True)
    hn = h * lax.rsqrt(ms + EPS) * g1_ref[...]
    hn = hn * (1.0 + sc1) + sh1
    z = jnp.dot(hn.astype(BF16), win_ref[...], preferred_element_type=F32)
    o_av = A_WIDTH
    o_q = 2 * A_WIDTH
    o_r = o_q + B_KEY_WIDTH
    o_k = o_r + B_WIDTH
    o_v = o_k + B_KEY_WIDTH
    o_g = o_v + B_WIDTH
    zu = z[:, 0:o_av]
    zv = z[:, o_av:o_q]
    q_all = z[:, o_q:o_r] * (B_DK ** -0.5)
    zr = z[:, o_r:o_k]
    k_all = z[:, o_k:o_v]
    vv_all = z[:, o_v:o_g]
    zg = z[:, o_g:o_g + LANE]

    u = jax.nn.gelu(zu)
    v = jax.nn.gelu(zv)
    s_groups = []
    for g in range(A_GROUPS):
        sl = slice(g * A_GROUP_DIM, (g + 1) * A_GROUP_DIM)
        vg = v[:, sl]
        mu = jnp.mean(vg, axis=-1, keepdims=True)
        dv = vg - mu
        var = jnp.mean(dv * dv, axis=-1, keepdims=True)
        vn = (dv * lax.rsqrt(var + EPS)) * lng_ref[:, sl] + lnb_ref[:, sl]
        vnb = vn.astype(BF16)
        n_ch = ST // CHUNK_A
        rhs = jnp.concatenate([vnb[c * CHUNK_A:(c + 1) * CHUNK_A, :] for c in range(n_ch)], axis=1)
        mixed = jnp.dot(wsp_ref[g], rhs, preferred_element_type=F32)
        bias = bsp_ref[:, sl]
        s_groups.append(jnp.concatenate(
            [mixed[:, c * A_GROUP_DIM:(c + 1) * A_GROUP_DIM] + bias for c in range(n_ch)], axis=0))
    a_ref[...] = (u * jnp.concatenate(s_groups, axis=1)).astype(BF16)

    sr_ref[...] = (zr * jax.nn.sigmoid(zr)).astype(BF16)
    lg = jnp.dot(zg.astype(BF16), wg_ref[...], preferred_element_type=F32) + bg_ref[...]
    la_all = (jnp.minimum(lg, 0.0) - jnp.log1p(jnp.exp(-jnp.abs(lg)))) * (1.0 / GATE_TAU)
    for t in range(SUB):
        rs = slice(t * TT, (t + 1) * TT)
        _gla_local(t, rs, la_all[rs, :], q_all[rs, :], k_all[rs, :], vv_all[rs, :],
                   trif_ref, trib_ref, eye_ref, qe_ref, oi_ref, kv_ref, dec_ref)


def _gla_local(t, rs, la, q, k, vv, trif_ref, trib_ref, eye_ref, qe_ref, oi_ref, kv_ref, dec_ref):
    la_hi = la.astype(BF16)
    la_lo = (la - la_hi.astype(F32)).astype(BF16)
    KW = B_KEY_WIDTH
    bf = (jnp.dot(trif_ref[...], la_hi[:, :KW], preferred_element_type=F32)
          + jnp.dot(trif_ref[...], la_lo[:, :KW], preferred_element_type=F32))
    bb = (jnp.dot(trib_ref[...], la_hi[:, KW:], preferred_element_type=F32)
          + jnp.dot(trib_ref[...], la_lo[:, KW:], preferred_element_type=F32))

    def chunk_rows(x, r):
        return jnp.concatenate(
            [jnp.broadcast_to(x[c * GLA_CHUNK + r:c * GLA_CHUNK + r + 1, :], (GLA_CHUNK, KW)) for c in range(CPT)], axis=0)

    blf = chunk_rows(bf, GLA_CHUNK - 1)
    blb = chunk_rows(bb, 0)
    bmf = chunk_rows(bf, CHUNK_B - 1)
    bmb = chunk_rows(bb, CHUNK_B)
    qe_ref[rs, :] = jnp.concatenate([q * jnp.exp(bf), q * jnp.exp(bb)], axis=1).astype(BF16)
    kd_f = k * jnp.exp(blf - bf)
    kd_b = k * jnp.exp(blb - bb)
    qe_f = q * jnp.exp(bf - bmf)
    qe_b = q * jnp.exp(bb - bmb)
    ke_f = (k * jnp.exp(bmf - bf)).astype(BF16)
    ke_b = (k * jnp.exp(bmb - bb)).astype(BF16)
    nt = (((1,), (1,)), ((), ()))
    dec_rows = ([jnp.exp(bf[c * GLA_CHUNK + GLA_CHUNK - 1:(c + 1) * GLA_CHUNK, :]) for c in range(CPT)]
                + [jnp.exp(bb[c * GLA_CHUNK:c * GLA_CHUNK + 1, :]) for c in range(CPT)]
                + [jnp.zeros((LANE - 2 * CPT, KW), F32)])
    dec_pad = jnp.concatenate(dec_rows, axis=0)
    dec_hi = dec_pad.astype(BF16)
    dec_lo = (dec_pad - dec_hi.astype(F32)).astype(BF16)
    dec_ref[t] = (lax.dot_general(eye_ref[...], dec_hi, nt, preferred_element_type=F32)
                  + lax.dot_general(eye_ref[...], dec_lo, nt, preferred_element_type=F32))

    row = lax.broadcasted_iota(I32, (TT, TT), 0)
    col = lax.broadcasted_iota(I32, (TT, TT), 1)
    chunk_shift = GLA_CHUNK.bit_length() - 1
    same = lax.shift_right_logical(row, chunk_shift) == lax.shift_right_logical(col, chunk_shift)
    m_f = same & (row >= col)
    m_b = same & (row <= col)
    lane_head = lax.shift_right_logical(lax.broadcasted_iota(I32, (1, KW), 1), B_DK.bit_length() - 1)
    tok_chunk = lax.shift_right_logical(lax.broadcasted_iota(I32, (1, TT), 1), chunk_shift)
    vb = vv.astype(BF16)
    kdT_f = kd_f.T
    kdT_b = kd_b.T
    oi_heads = []
    for hd in range(B_HEADS):
        hm = lane_head == hd
        qf = jnp.where(hm, qe_f, 0.0).astype(BF16)
        qb = jnp.where(hm, qe_b, 0.0).astype(BF16)
        att_f = lax.dot_general(qf, ke_f, nt, preferred_element_type=F32)
        att_b = lax.dot_general(qb, ke_b, nt, preferred_element_type=F32)
        att = jnp.where(m_f, att_f, 0.0) + jnp.where(m_b, att_b, 0.0)
        v_h = vb[:, hd * B_DV:(hd + 1) * B_DV]
        oi_heads.append(jnp.dot(att.astype(BF16), v_h, preferred_element_type=F32))
        parts = []
        for kdT in (kdT_f, kdT_b):
            kh = kdT[hd * B_DK:(hd + 1) * B_DK, :]
            for c in range(CPT):
                parts.append(jnp.where(tok_chunk == c, kh, 0.0))
        lhs = jnp.concatenate(parts, axis=0).astype(BF16)
        kv_ref[t, hd] = jnp.dot(lhs, v_h, preferred_element_type=F32)
    oi_ref[rs, :] = jnp.concatenate(oi_heads, axis=1)


def _pre_call(lat, ctx, ctx_off, n_ctx_steps, mod3, lw, n_lat_steps, steps_per_sample, n_mod_ctx):
    D = lat.shape[1]
    n_steps = n_lat_steps + n_ctx_steps
    n_tiles = n_steps * SUB
    N = n_steps * ST
    lat_spec, ctx_spec = _stream_specs(D, n_lat_steps, ctx_off)
    DINP = lw["w_in"].shape[1]
    KVR = 2 * CPT * B_DK

    def mod_map(i):
        return (jnp.where(i < n_lat_steps, i // steps_per_sample, n_mod_ctx), 0, 0)

    const2 = lambda i: (0, 0)
    const3 = lambda i: (0, 0, 0)
    tile2 = lambda i: (i, 0)
    return pl.pallas_call(
        functools.partial(_pre_kernel, n_lat_steps=n_lat_steps),
        grid=(n_steps,),
        in_specs=[
            lat_spec,
            ctx_spec,
            pl.BlockSpec((1, 1, 6 * D), mod_map),
            pl.BlockSpec((1, D), const2),
            pl.BlockSpec((D, DINP), const2),
            pl.BlockSpec((1, A_WIDTH), const2),
            pl.BlockSpec((1, A_WIDTH), const2),
            pl.BlockSpec((A_GROUPS, CHUNK_A, CHUNK_A), const3),
            pl.BlockSpec((CHUNK_A, A_WIDTH), const2),
            pl.BlockSpec((LANE, 2 * B_KEY_WIDTH), const2),
            pl.BlockSpec((1, 2 * B_KEY_WIDTH), const2),
            pl.BlockSpec((TT, TT), const2),
            pl.BlockSpec((TT, TT), const2),
            pl.BlockSpec((B_KEY_WIDTH, B_KEY_WIDTH), const2),
        ],
        out_specs=[
            pl.BlockSpec((ST, A_WIDTH), tile2),
            pl.BlockSpec((ST, 2 * B_KEY_WIDTH), tile2),
            pl.BlockSpec((ST, B_WIDTH), tile2),
            pl.BlockSpec((ST, B_WIDTH), tile2),
            pl.BlockSpec((SUB, B_HEADS, KVR, B_DV), lambda i: (i, 0, 0, 0)),
            pl.BlockSpec((SUB, B_KEY_WIDTH, LANE), lambda i: (i, 0, 0)),
        ],
        out_shape=[
            jax.ShapeDtypeStruct((N, A_WIDTH), BF16),
            jax.ShapeDtypeStruct((N, 2 * B_KEY_WIDTH), BF16),
            jax.ShapeDtypeStruct((N, B_WIDTH), F32),
            jax.ShapeDtypeStruct((N, B_WIDTH), BF16),
            jax.ShapeDtypeStruct((n_tiles, B_HEADS, KVR, B_DV), F32),
            jax.ShapeDtypeStruct((n_tiles, B_KEY_WIDTH, LANE), F32),
        ],
        compiler_params=_cparams(("arbitrary",), VMEM_LIMIT_TOKEN_LOCAL),
        name="mix_pre",
    )(lat, ctx, mod3, lw["g_norm1"], lw["w_in"], lw["ln_g"], lw["ln_b"], lw["w_sp"], lw["b_sp"],
      lw["w_gate"], lw["b_gate"], lw["tri_f"], lw["tri_b"], lw["eye_k"])


def _scan_kernel(kvc_ref, kvl_ref, dcc_ref, dcl_ref, sc_ref, sl_ref):
    n_lat = kvl_ref.shape[0]
    fwd = [(kvc_ref, dcc_ref, sc_ref, 0, c) for c in range(CPT)]
    fwd += [(kvl_ref, dcl_ref, sl_ref, t, c) for t in range(n_lat) for c in range(CPT)]
    bwd = [(kvc_ref, dcc_ref, sc_ref, 0, c) for c in reversed(range(CPT))]
    bwd += [(kvl_ref, dcl_ref, sl_ref, t, c) for t in reversed(range(n_lat)) for c in reversed(range(CPT))]
    for hd in range(B_HEADS):
        for d, seq in enumerate((fwd, bwd)):
            s = jnp.zeros((B_DK, B_DV), F32)
            for kv_ref, dc_ref, out_ref, t, c in seq:
                r0 = (d * CPT + c) * B_DK
                out_ref[t, hd, r0:r0 + B_DK, :] = s.astype(BF16)
                dcol = dc_ref[t, hd * B_DK:(hd + 1) * B_DK, d * CPT + c:d * CPT + c + 1]
                s = dcol * s + kv_ref[t, hd, r0:r0 + B_DK, :]


def _scan_call(kv, dec, n_samples, n_lat_tiles, tiles_per_sample):
    n_tiles, _, KVR, _ = kv.shape
    kv_c = pl.BlockSpec((1, B_HEADS, KVR, B_DV), lambda b: (n_lat_tiles + b, 0, 0, 0))
    kv_l = pl.BlockSpec((tiles_per_sample, B_HEADS, KVR, B_DV), lambda b: (b, 0, 0, 0))
    dc_c = pl.BlockSpec((1, B_KEY_WIDTH, LANE), lambda b: (n_lat_tiles + b, 0, 0))
    dc_l = pl.BlockSpec((tiles_per_sample, B_KEY_WIDTH, LANE), lambda b: (b, 0, 0))
    s_ctx, s_lat = pl.pallas_call(
        _scan_kernel,
        grid=(n_samples,),
        in_specs=[kv_c, kv_l, dc_c, dc_l],
        out_specs=[
            pl.BlockSpec((1, B_HEADS, KVR, B_DV), lambda b: (b, 0, 0, 0)),
            pl.BlockSpec((tiles_per_sample, B_HEADS, KVR, B_DV), lambda b: (b, 0, 0, 0)),
        ],
        out_shape=[
            jax.ShapeDtypeStruct((n_samples, B_HEADS, KVR, B_DV), BF16),
            jax.ShapeDtypeStruct((n_lat_tiles, B_HEADS, KVR, B_DV), BF16),
        ],
        compiler_params=_cparams(("arbitrary",)),
        name="gla_scan",
    )(kv, kv, dec, dec)
    return jnp.concatenate([s_lat, s_ctx], axis=0)


def _post_kernel(lat_ref, ctx_ref, mod_ref, a_ref, qe_ref, oi_ref, sr_ref, st_ref, ggla_ref, wout_ref, g2_ref,
                 wr_ref, br_ref, ustr_ref,
                 h1_ref, hn2_ref, route_ref, routet_ref, cnt_ref,
                 rhs_scr, cnt_scr, *, n_lat_steps):
    D = lat_ref.shape[1]
    i = pl.program_id(0)

    @pl.when(i == 0)
    def _():
        rhs_scr[...] = jnp.zeros(rhs_scr.shape, rhs_scr.dtype)
        cnt_scr[...] = jnp.zeros(cnt_scr.shape, cnt_scr.dtype)

    qe = qe_ref[...]
    o_parts = []
    for c in range(SUB * CPT):
        t, ct = divmod(c, CPT)
        for d in range(2):
            for hd in range(B_HEADS):
                r0 = d * B_KEY_WIDTH + hd * B_DK
                s0 = (d * CPT + ct) * B_DK
                rhs_scr[c, r0:r0 + B_DK, hd * B_DV:(hd + 1) * B_DV] = st_ref[t, hd, s0:s0 + B_DK, :]
        o_parts.append(jnp.dot(qe[c * GLA_CHUNK:(c + 1) * GLA_CHUNK, :], rhs_scr[c], preferred_element_type=F32))
    o = oi_ref[...] + jnp.concatenate(o_parts, axis=0)

    heads = []
    for hd in range(B_HEADS):
        sl = slice(hd * B_DV, (hd + 1) * B_DV)
        oh = o[:, sl]
        msq = jnp.mean(oh * oh, axis=-1, keepdims=True)
        heads.append(oh * lax.rsqrt(msq + EPS) * ggla_ref[:, sl])
    b_out = jnp.concatenate(heads, axis=1) * sr_ref[...].astype(F32)
    mix_in = jnp.concatenate([a_ref[...], b_out.astype(BF16)], axis=1)
    mix = jnp.dot(mix_in, wout_ref[...], preferred_element_type=F32)

    mod = mod_ref[0]
    g1 = mod[:, 2 * D:3 * D]
    sh2 = mod[:, 3 * D:4 * D]
    sc2 = mod[:, 4 * D:5 * D]
    h_in = _select_tile(lat_ref, ctx_ref, n_lat_steps)
    nt = (((1,), (1,)), ((), ()))
    neg = jnp.float32(-3.0e38)
    big = jnp.float32(1.0e9)
    sub = lax.broadcasted_iota(I32, (EXPERTS_PER_GROUP, LANE), 0).astype(F32)
    route_parts = []
    for hf in range(ST // LANE):
        rs = slice(hf * LANE, (hf + 1) * LANE)
        h1 = h_in[rs, :] + g1 * mix[rs, :]
        h1_ref[rs, :] = h1
        msq = jnp.mean(h1 * h1, axis=-1, keepdims=True)
        hn2 = h1 * lax.rsqrt(msq + EPS) * g2_ref[...]
        hn2 = hn2 * (1.0 + sc2) + sh2
        _store_rows(hn2_ref, hn2, hf * LANE)

        lgt = lax.dot_general(wr_ref[...], hn2.astype(BF16), nt, preferred_element_type=F32) + br_ref[...]
        gl = lgt[N_EXPERTS:N_EXPERTS + N_GROUPS, :]
        gmax = jnp.max(gl, axis=0, keepdims=True)
        p_top = 1.0 / jnp.sum(jnp.exp(gl - gmax), axis=0, keepdims=True)
        gidx = jnp.min(jnp.where(gl == gmax, sub, big), axis=0, keepdims=True)
        el = jnp.zeros((EXPERTS_PER_GROUP, LANE), F32)
        for g in range(N_GROUPS):
            el = el + jnp.where(gidx == float(g), lgt[g * EXPERTS_PER_GROUP:(g + 1) * EXPERTS_PER_GROUP, :], 0.0)
        m1 = jnp.max(el, axis=0, keepdims=True)
        i1 = jnp.min(jnp.where(el == m1, sub, big), axis=0, keepdims=True)
        el2 = jnp.where(sub == i1, neg, el)
        m2 = jnp.max(el2, axis=0, keepdims=True)
        i2 = jnp.min(jnp.where(el2 == m2, sub, big), axis=0, keepdims=True)
        t = jnp.exp(m2 - m1)
        route_parts.append((gidx * float(EXPERTS_PER_GROUP) + i1, gidx * float(EXPERTS_PER_GROUP) + i2,
                            p_top / (1.0 + t), p_top * t / (1.0 + t)))
    e1_all, e2_all, gate1_all, gate2_all = (jnp.concatenate(p, axis=1) for p in zip(*route_parts))

    rowf = lax.broadcasted_iota(I32, (LANE, TT), 0).astype(F32)
    second = lax.broadcasted_iota(I32, (LANE, 1), 0) >= N_EXPERTS
    base = cnt_scr[...]
    for t in range(SUB):
        ls = slice(t * TT, (t + 1) * TT)
        e1, e2 = e1_all[:, ls], e2_all[:, ls]
        sel1 = rowf == e1
        sel2 = rowf == (e2 + float(N_EXPERTS))
        oh = jnp.where(sel1, 1.0, 0.0) + jnp.where(sel2, 1.0, 0.0)
        prefix = jnp.dot(oh.astype(BF16), ustr_ref[...], preferred_element_type=F32)
        tot = jnp.sum(oh, axis=1, keepdims=True)
        tot_sw = jnp.concatenate([tot[N_EXPERTS:, :], tot[:N_EXPERTS, :]], axis=0)
        val = prefix + (base + jnp.where(second, tot_sw, 0.0))
        r1 = jnp.sum(jnp.where(sel1, val, 0.0), axis=0, keepdims=True)
        r2 = jnp.sum(jnp.where(sel2, val, 0.0), axis=0, keepdims=True)
        base = base + tot + tot_sw
        zero = jnp.zeros_like(e1)
        route_t = jnp.concatenate([e1, e2, gate1_all[:, ls], gate2_all[:, ls], r1, r2, zero, zero], axis=0)
        routet_ref[:, ls] = route_t
        route_ref[ls, :] = jnp.concatenate([route_t, jnp.zeros((LANE - SUBLANE, TT), F32)], axis=0).T
    cnt_scr[...] = base
    cnt_ref[...] = base


def _post_call(lat, ctx, ctx_off, mod3, pre_outs, states, lw, n_post_steps, n_lat_steps, steps_per_sample, n_mod_ctx):
    a_out, qe, oi, sr = pre_outs
    D = lat.shape[1]
    KVR = 2 * CPT * B_DK
    NP = n_post_steps * ST
    lat_spec, ctx_spec = _stream_specs(D, n_lat_steps, ctx_off)

    def mod_map(i):
        return (jnp.where(i < n_lat_steps, i // steps_per_sample, n_mod_ctx), 0, 0)

    const2 = lambda i: (0, 0)
    tile2 = lambda i: (i, 0)
    return pl.pallas_call(
        functools.partial(_post_kernel, n_lat_steps=n_lat_steps),
        grid=(n_post_steps,),
        in_specs=[
            lat_spec,
            ctx_spec,
            pl.BlockSpec((1, 1, 6 * D), mod_map),
            pl.BlockSpec((ST, A_WIDTH), tile2),
            pl.BlockSpec((ST, 2 * B_KEY_WIDTH), tile2),
            pl.BlockSpec((ST, B_WIDTH), tile2),
            pl.BlockSpec((ST, B_WIDTH), tile2),
            pl.BlockSpec((SUB, B_HEADS, KVR, B_DV), lambda i: (i, 0, 0, 0)),
            pl.BlockSpec((1, B_WIDTH), const2),
            pl.BlockSpec((D, D), const2),
            pl.BlockSpec((1, D), const2),
            pl.BlockSpec((LANE, D), const2),
            pl.BlockSpec((LANE, 1), const2),
            pl.BlockSpec((TT, TT), const2),
        ],
        out_specs=[
            pl.BlockSpec((ST, D), tile2),
            pl.BlockSpec((ST * ROW_SUB, LANE), tile2),
            pl.BlockSpec((ST, LANE), tile2),
            pl.BlockSpec((SUBLANE, ST), lambda i: (0, i)),
            pl.BlockSpec((LANE, 1), const2),
        ],
        out_shape=[
            jax.ShapeDtypeStruct((NP, D), F32),
            jax.ShapeDtypeStruct((NP * ROW_SUB, LANE), F32),
            jax.ShapeDtypeStruct((NP, LANE), F32),
            jax.ShapeDtypeStruct((SUBLANE, NP), F32),
            jax.ShapeDtypeStruct((LANE, 1), F32),
        ],
        scratch_shapes=[
            pltpu.VMEM((SUB * CPT, 2 * B_KEY_WIDTH, B_WIDTH), BF16),
            pltpu.VMEM((LANE, 1), F32),
        ],
        compiler_params=_cparams(("arbitrary",), VMEM_LIMIT_TOKEN_LOCAL),
        name="mix_post",
    )(lat, ctx, mod3, a_out, qe, oi, sr, states, lw["g_gla"], lw["w_out"], lw["g_norm2"],
      lw["w_router"], lw["b_router"], lw["u_strict"])


def _plan_kernel(rt_ref, ps_ref, idx_ref):
    sub = lax.broadcasted_iota(I32, (N_EXPERTS, LANE), 0).astype(F32)
    ps = ps_ref[...]
    segs = DT // LANE
    for k in range(2):
        for seg in range(segs):
            sl = slice(seg * LANE, (seg + 1) * LANE)
            e = rt_ref[k:k + 1, sl]
            r = rt_ref[4 + k:5 + k, sl]
            base = jnp.sum(jnp.where(sub == e, ps, 0.0), axis=0, keepdims=True)
            idx_ref[0, k * segs + seg:k * segs + seg + 1, :] = ((base + r) * float(ROW_SUB)).astype(I32)


def _plan_call(route_t, pad_starts):
    n_dt = route_t.shape[1] // DT
    return pl.pallas_call(
        _plan_kernel,
        grid=(n_dt,),
        in_specs=[
            pl.BlockSpec((SUBLANE, DT), lambda i: (0, i)),
            pl.BlockSpec((N_EXPERTS, 1), lambda i: (0, 0)),
        ],
        out_specs=pl.BlockSpec((1, 2 * DT // LANE, LANE), lambda i: (i, 0, 0)),
        out_shape=jax.ShapeDtypeStruct((n_dt, 2 * DT // LANE, LANE), I32),
        compiler_params=_cparams(("arbitrary",)),
        name="moe_plan",
    )(route_t, pad_starts.astype(F32)[:, None])


IDX_UNROLL = 8


def _row_loop(n_rows, fn):
    def body(jo, carry):
        for u in range(IDX_UNROLL):
            fn(jo * IDX_UNROLL + u, jo, u)
        return carry

    lax.fori_loop(0, n_rows // IDX_UNROLL, body, 0)


def _tile_indices(idx_smem, jo, u):
    d0 = pl.multiple_of(idx_smem[jo, u], ROW_SUB)
    d1 = pl.multiple_of(idx_smem[jo, IDX_UNROLL + u], ROW_SUB)
    return d0, d1


def _index_tiles(idx):
    n = idx.shape[0]
    d = idx.reshape(n, 2, DT // IDX_UNROLL, IDX_UNROLL).transpose(0, 2, 1, 3).reshape(n, DT // IDX_UNROLL, 2 * IDX_UNROLL)
    return jnp.pad(d, ((0, 0), (0, 0), (0, LANE - 2 * IDX_UNROLL)))


def _dispatch_kernel(pe_ref, idx_hbm, x_ref, xs_out, idx_smem, zbuf, sem_i, sem_o, sem_z):
    i = pl.program_id(0)
    cp = pltpu.make_async_copy(idx_hbm.at[i], idx_smem, sem_i)
    cp.start()
    blk_rows = MOE_BLK * ROW_SUB

    @pl.when(i == 0)
    def _():
        zbuf[...] = jnp.zeros(zbuf.shape, zbuf.dtype)
        for e in range(N_EXPERTS):
            start = pl.multiple_of(jnp.maximum(pe_ref[e] - MOE_BLK, 0) * ROW_SUB, blk_rows)
            pltpu.make_async_copy(zbuf, xs_out.at[pl.ds(start, blk_rows), :], sem_z).start()
        for e in range(N_EXPERTS):
            pltpu.make_async_copy(zbuf, xs_out.at[pl.ds(0, blk_rows), :], sem_z).wait()
        first_free = pe_ref[N_EXPERTS - 1] // MOE_BLK
        n_blocks = xs_out.shape[0] // blk_rows

        def clear(b, carry):
            start = pl.multiple_of(b * blk_rows, blk_rows)
            pltpu.make_async_copy(zbuf, xs_out.at[pl.ds(start, blk_rows), :], sem_z).start()
            return carry

        def clear_wait(b, carry):
            pltpu.make_async_copy(zbuf, xs_out.at[pl.ds(0, blk_rows), :], sem_z).wait()
            return carry

        lax.fori_loop(first_free, n_blocks, clear, 0)
        lax.fori_loop(first_free, n_blocks, clear_wait, 0)

    cp.wait()

    def issue(j, jo, u):
        d0, d1 = _tile_indices(idx_smem, jo, u)
        src = x_ref.at[pl.ds(pl.multiple_of(j * ROW_SUB, ROW_SUB), ROW_SUB), :]
        pltpu.make_async_copy(src, xs_out.at[pl.ds(d0, ROW_SUB), :], sem_o).start(priority=0)
        pltpu.make_async_copy(src, xs_out.at[pl.ds(d1, ROW_SUB), :], sem_o).start(priority=1)

    _row_loop(DT, issue)
    for _ in range(2):
        pltpu.make_async_copy(x_ref, xs_out.at[pl.ds(0, DT * ROW_SUB), :], sem_o).wait()


def _dispatch_call(pad_ends, idx, x_rows, n_buf_rows):
    n_tiles = x_rows.shape[0] // (DT * ROW_SUB)
    grid_spec = pltpu.PrefetchScalarGridSpec(
        num_scalar_prefetch=1,
        grid=(n_tiles,),
        in_specs=[
            pl.BlockSpec(memory_space=pl.ANY),
            pl.BlockSpec((DT * ROW_SUB, LANE), lambda i, pe: (i, 0)),
        ],
        out_specs=pl.BlockSpec(memory_space=pl.ANY),
        scratch_shapes=[
            pltpu.SMEM((DT // IDX_UNROLL, LANE), I32),
            pltpu.VMEM((MOE_BLK * ROW_SUB, LANE), F32),
            pltpu.SemaphoreType.DMA,
            pltpu.SemaphoreType.DMA,
            pltpu.SemaphoreType.DMA,
        ],
    )
    return pl.pallas_call(
        _dispatch_kernel,
        grid_spec=grid_spec,
        out_shape=jax.ShapeDtypeStruct((n_buf_rows * ROW_SUB, LANE), F32),
        compiler_params=_cparams(("arbitrary",), VMEM_LIMIT_ROW_MOVES),
        name="moe_dispatch",
    )(pad_ends, idx, x_rows)


def _ffn_kernel(be_ref, nu_ref, ws_ref, nx_ref, x_ref, w1_hbm, w3_hbm, w2_hbm, y_ref,
                w1f, w3f, w2f, w1b, w3b, w2b, sem_w, *, layer):
    i = pl.program_id(0)
    nu = nu_ref[0]

    def weight_copies(e, s):
        return (pltpu.make_async_copy(w1_hbm.at[layer, e], w1f.at[s], sem_w.at[s]),
                pltpu.make_async_copy(w3_hbm.at[layer, e], w3f.at[s], sem_w.at[s]),
                pltpu.make_async_copy(w2_hbm.at[layer, e], w2f.at[s], sem_w.at[s]))

    @pl.when(i < nu)
    def _():
        e = be_ref[i]
        s = ws_ref[i]
        first_of_run = (i == 0) | (e != be_ref[jnp.maximum(i - 1, 0)])

        @pl.when(i == 0)
        def _():
            for cp in weight_copies(e, s):
                cp.start()

        @pl.when(first_of_run)
        def _():
            for cp in weight_copies(e, s):
                cp.wait()
            nxt = nx_ref[i]

            @pl.when(nxt >= 0)
            def _():
                for cp in weight_copies(nxt, 1 - s):
                    cp.start(priority=1)

            w1b[...] = w1f[s].astype(BF16)
            w3b[...] = w3f[s].astype(BF16)
            w2b[...] = w2f[s].astype(BF16)

        x = _load_rows(x_ref, MOE_BLK).astype(BF16)
        h1 = jnp.dot(x, w1b[...], preferred_element_type=F32)
        h3 = jnp.dot(x, w3b[...], preferred_element_type=F32)
        hh = (h1 * jax.nn.sigmoid(h1) * h3).astype(BF16)
        y = jnp.dot(hh, w2b[...], preferred_element_type=F32)
        _store_rows(y_ref, y)

    @pl.when(i >= nu)
    def _():
        y_ref[...] = jnp.zeros(y_ref.shape, y_ref.dtype)


def _ffn_call(block_expert, n_used, w_slot, next_expert, xs, w1, w3, w2, l):
    blk_rows = MOE_BLK * ROW_SUB
    NB = xs.shape[0] // blk_rows
    _, _, D, DE = w1.shape
    grid_spec = pltpu.PrefetchScalarGridSpec(
        num_scalar_prefetch=4,
        grid=(NB,),
        in_specs=[
            pl.BlockSpec((blk_rows, LANE), lambda i, be, nu, ws, nx: (jnp.minimum(i, nu[0] - 1), 0)),
            pl.BlockSpec(memory_space=pl.ANY),
            pl.BlockSpec(memory_space=pl.ANY),
            pl.BlockSpec(memory_space=pl.ANY),
        ],
        out_specs=pl.BlockSpec((blk_rows, LANE), lambda i, be, nu, ws, nx: (i, 0)),
        scratch_shapes=[
            pltpu.VMEM((2, D, DE), F32),
            pltpu.VMEM((2, D, DE), F32),
            pltpu.VMEM((2, DE, D), F32),
            pltpu.VMEM((D, DE), BF16),
            pltpu.VMEM((D, DE), BF16),
            pltpu.VMEM((DE, D), BF16),
            pltpu.SemaphoreType.DMA((2,)),
        ],
    )
    return pl.pallas_call(
        functools.partial(_ffn_kernel, layer=l),
        grid_spec=grid_spec,
        out_shape=jax.ShapeDtypeStruct(xs.shape, F32),
        compiler_params=_cparams(("arbitrary",)),
        name="moe_ffn",
    )(block_expert, n_used, w_slot, next_expert, xs, w1, w3, w2)


def _combine_kernel(idx_hbm, h1_ref, route_ref, mod_ref, gf_ref, y_hbm, o_ref, idx_smem, rows0, rows1, sem_i, sem_g,
                    *, final):
    D = h1_ref.shape[1]
    i = pl.program_id(0)
    n = pl.num_programs(0)
    slot = i & 1
    nxt = 1 - slot

    def idx_copy(tile, s):
        return pltpu.make_async_copy(idx_hbm.at[tile], idx_smem.at[s], sem_i.at[s])

    def gather(s):
        def issue(j, jo, u):
            d0, d1 = _tile_indices(idx_smem.at[s], jo, u)
            dst = pl.ds(pl.multiple_of(j * ROW_SUB, ROW_SUB), ROW_SUB)
            pltpu.make_async_copy(y_hbm.at[pl.ds(d0, ROW_SUB), :], rows0.at[s, dst, :], sem_g.at[s]).start(priority=0)
            pltpu.make_async_copy(y_hbm.at[pl.ds(d1, ROW_SUB), :], rows1.at[s, dst, :], sem_g.at[s]).start(priority=1)

        _row_loop(DT, issue)

    @pl.when(i == 0)
    def _():
        cp = idx_copy(0, 0)
        cp.start()
        cp.wait()
        gather(0)

        @pl.when(n > 1)
        def _():
            idx_copy(1, 1).start()

    @pl.when(i + 1 < n)
    def _():
        idx_copy(0, nxt).wait()
        gather(nxt)

    @pl.when(i + 2 < n)
    def _():
        idx_copy(i + 2, slot).start()

    for rows in (rows0, rows1):
        pltpu.make_async_copy(y_hbm.at[pl.ds(0, DT * ROW_SUB), :], rows.at[slot], sem_g.at[slot]).wait()

    route = route_ref[...]
    gate1 = route[:, 2:3]
    gate2 = route[:, 3:4]
    y = gate1 * _load_rows(rows0.at[slot], DT) + gate2 * _load_rows(rows1.at[slot], DT)
    g2 = mod_ref[0][:, 5 * D:6 * D]
    out = h1_ref[...] + g2 * y
    if final:
        msq = jnp.mean(out * out, axis=-1, keepdims=True)
        out = out * lax.rsqrt(msq + EPS) * gf_ref[...]
    o_ref[...] = out


def _combine_call(idx, h1, route, mod3, g_final, yb, n_lat_dt, dt_per_sample, n_mod_ctx, final):
    N, D = h1.shape
    n_tiles = N // DT

    def mod_map(i):
        return (jnp.where(i < n_lat_dt, i // dt_per_sample, n_mod_ctx), 0, 0)

    return pl.pallas_call(
        functools.partial(_combine_kernel, final=final),
        grid=(n_tiles,),
        in_specs=[
            pl.BlockSpec(memory_space=pl.ANY),
            pl.BlockSpec((DT, D), lambda i: (i, 0)),
            pl.BlockSpec((DT, LANE), lambda i: (i, 0)),
            pl.BlockSpec((1, 1, 6 * D), mod_map),
            pl.BlockSpec((1, D), lambda i: (0, 0)),
            pl.BlockSpec(memory_space=pl.ANY),
        ],
        out_specs=pl.BlockSpec((DT, D), lambda i: (i, 0)),
        out_shape=jax.ShapeDtypeStruct((N, D), F32),
        scratch_shapes=[
            pltpu.SMEM((2, DT // IDX_UNROLL, LANE), I32),
            pltpu.VMEM((2, DT * ROW_SUB, LANE), F32),
            pltpu.VMEM((2, DT * ROW_SUB, LANE), F32),
            pltpu.SemaphoreType.DMA((2,)),
            pltpu.SemaphoreType.DMA((2,)),
        ],
        compiler_params=_cparams(("arbitrary",), VMEM_LIMIT_ROW_MOVES),
        name="moe_combine",
    )(idx, h1, route, mod3, g_final, yb)


def _layer_weights(l, w_in, g_norm1, ln_v_g, ln_v_b, w_sp, b_sp, w_gate_up, b_gate, g_gla, w_out, g_norm2,
                   w_router_g, b_router_g, w_router_e, b_router_e):
    D = w_in.shape[1]
    d_in = w_in.shape[2]
    pad = (-d_in) % LANE
    w_in_p = jnp.pad(w_in[l], ((0, 0), (0, pad))).astype(BF16)
    KW = B_KEY_WIDTH
    wg = jnp.zeros((LANE, 2 * KW), F32)
    wg = wg.at[0:GATE_RANK, 0:KW].set(w_gate_up[l, 0])
    wg = wg.at[GATE_RANK:2 * GATE_RANK, KW:2 * KW].set(w_gate_up[l, 1])
    bg = jnp.concatenate([b_gate[l, 0], b_gate[l, 1]])[None, :]
    bsp = jnp.repeat(b_sp[l].T, A_GROUP_DIM, axis=1)
    wr = jnp.zeros((LANE, D), F32)
    wr = wr.at[0:N_EXPERTS, :].set(w_router_e[l].T)
    wr = wr.at[N_EXPERTS:N_EXPERTS + N_GROUPS, :].set(w_router_g[l].T)
    br = jnp.zeros((LANE, 1), F32)
    br = br.at[0:N_EXPERTS, 0].set(b_router_e[l])
    br = br.at[N_EXPERTS:N_EXPERTS + N_GROUPS, 0].set(b_router_g[l])
    r = jnp.arange(TT, dtype=I32)
    same = (r[:, None] // GLA_CHUNK) == (r[None, :] // GLA_CHUNK)
    tri_f = (same & (r[:, None] >= r[None, :])).astype(BF16)
    tri_b = (same & (r[:, None] <= r[None, :])).astype(BF16)
    u_strict = (r[:, None] < r[None, :]).astype(BF16)
    eye_k = jnp.eye(B_KEY_WIDTH, dtype=BF16)
    return dict(
        w_in=w_in_p, g_norm1=g_norm1[l][None, :], ln_g=ln_v_g[l][None, :], ln_b=ln_v_b[l][None, :],
        w_sp=w_sp[l].astype(BF16), b_sp=bsp, w_gate=wg.astype(BF16), b_gate=bg,
        g_gla=g_gla[l][None, :], w_out=w_out[l].astype(BF16), g_norm2=g_norm2[l][None, :],
        w_router=wr.astype(BF16), b_router=br, tri_f=tri_f, tri_b=tri_b, u_strict=u_strict, eye_k=eye_k)


def _segment_layout(counts_col, n_tokens):
    counts = counts_col[:N_EXPERTS, 0].astype(I32)
    padded = (counts + MOE_BLK - 1) // MOE_BLK * MOE_BLK
    pad_ends = jnp.cumsum(padded)
    pad_starts = pad_ends - padded
    n_blocks = (n_tokens * 2 + N_EXPERTS * (MOE_BLK - 1)) // MOE_BLK + 1
    block_start = jnp.arange(n_blocks, dtype=I32) * MOE_BLK
    block_expert = jnp.sum((pad_ends[None, :] <= block_start[:, None]).astype(I32), axis=1)
    block_expert = jnp.minimum(block_expert, N_EXPERTS - 1)
    n_used = (pad_ends[-1] // MOE_BLK).astype(I32)[None]
    blk_id = jnp.arange(n_blocks, dtype=I32)
    prev = jnp.concatenate([block_expert[:1], block_expert[:-1]])
    first = ((blk_id == 0) | (block_expert != prev)) & (blk_id < n_used[0])
    w_slot = (jnp.cumsum(first.astype(I32)) - 1) & 1
    e_id = jnp.arange(N_EXPERTS, dtype=I32)
    later_used = (e_id[None, :] > e_id[:, None]) & (padded[None, :] > 0)
    next_used = jnp.min(jnp.where(later_used, e_id[None, :], N_EXPERTS), axis=1)
    next_used = jnp.where(next_used >= N_EXPERTS, -1, next_used)
    next_expert = jnp.sum(jnp.where(block_expert[:, None] == e_id[None, :], next_used[None, :], 0), axis=1).astype(I32)
    return pad_starts, pad_ends, block_expert, n_used, w_slot.astype(I32), next_expert, n_blocks * MOE_BLK


def kernel(x, c, ctx, c_ctx, w_mod, b_mod, g_norm1, w_in, ln_v_g, ln_v_b, w_sp, b_sp, w_gate_up, b_gate, g_gla,
           w_out, g_norm2, w_router_g, b_router_g, w_router_e, b_router_e, w1, w3, w2, g_final):
    Bn, S, D = x.shape
    Lc = ctx.shape[1]
    depth = w_mod.shape[0]
    assert S % DT == 0 and S % ST == 0 and (Bn * Lc) % DT == 0 and (Bn * Lc) % ST == 0 and Lc == TT
    n_lat = Bn * S
    n_ctx = Bn * Lc
    n_lat_tiles = n_lat // TT
    tiles_per_sample = S // TT
    n_lat_steps = n_lat // ST
    steps_per_sample = S // ST

    cc = jnp.concatenate([c, c_ctx[None, :]], axis=0)
    mod_all = _modulation(cc, w_mod, b_mod)
    gf = g_final[None, :]

    lat, cx, ctx_off = x.reshape(n_lat, D), ctx.reshape(n_ctx, D), 0
    for l in range(depth):
        last = l == depth - 1
        lw = _layer_weights(l, w_in, g_norm1, ln_v_g, ln_v_b, w_sp, b_sp, w_gate_up, b_gate, g_gla, w_out,
                            g_norm2, w_router_g, b_router_g, w_router_e, b_router_e)
        mod3 = mod_all[l].reshape(Bn + 1, 1, 6 * D)
        a_out, qe, oi, sr, kv, dec = _pre_call(lat, cx, ctx_off, n_ctx // ST, mod3, lw, n_lat_steps,
                                               steps_per_sample, Bn)
        states = _scan_call(kv, dec, Bn, n_lat_tiles, tiles_per_sample)
        n_tok = n_lat if last else n_lat + n_ctx
        h1, hn2_rows, route, route_t, counts = _post_call(lat, cx, ctx_off, mod3, (a_out, qe, oi, sr), states, lw,
                                                          n_tok // ST, n_lat_steps, steps_per_sample, Bn)
        pad_starts, pad_ends, block_expert, n_used, w_slot, next_expert, n_buf_rows = _segment_layout(counts, n_tok)
        idx = _index_tiles(_plan_call(route_t, pad_starts))
        xs = _dispatch_call(pad_ends, idx, hn2_rows, n_buf_rows)
        yb = _ffn_call(block_expert, n_used, w_slot, next_expert, xs, w1, w3, w2, l)
        h_all = _combine_call(idx, h1, route, mod3, gf, yb, n_lat // DT, S // DT, Bn, last)
        lat, cx, ctx_off = h_all, h_all, n_lat_steps
    return h_all.reshape(Bn, S, D)
```

```python
import functools

import jax
import jax.numpy as jnp
from jax import lax
from jax.experimental import pallas as pl
from jax.experimental.pallas import tpu as pltpu

F32 = jnp.float32
BF16 = jnp.bfloat16
I32 = jnp.int32

EPS = 1e-6
LANE = 128
SUBLANE = 8

A_GROUPS = 4
A_GROUP_DIM = 128
A_WIDTH = A_GROUPS * A_GROUP_DIM
CHUNK_A = 128
B_HEADS = 4
B_DK = 64
B_DV = 128
B_KEY_WIDTH = B_HEADS * B_DK
B_WIDTH = B_HEADS * B_DV
GATE_RANK = 16
GATE_TAU = 16.0
CHUNK_B = 64
N_GROUPS = 8
EXPERTS_PER_GROUP = 8
N_EXPERTS = N_GROUPS * EXPERTS_PER_GROUP

TT = 256
GLA_CHUNK = 2 * CHUNK_B
CPT = TT // GLA_CHUNK
SUB = 4
ST = SUB * TT
MOE_BLK = 256
DT = 1024
ROW_SUB = 8

VMEM_LIMIT = 48 * 1024 * 1024
VMEM_LIMIT_TOKEN_LOCAL = 56 * 1024 * 1024


def _cparams(sem, vmem_limit=VMEM_LIMIT):
    return pltpu.CompilerParams(dimension_semantics=sem, vmem_limit_bytes=vmem_limit)


def _load_rows(ref, n):
    return jnp.concatenate([ref[pl.ds(s, n, stride=ROW_SUB), :] for s in range(ROW_SUB)], axis=1)


def _store_rows(ref, val, first=0):
    n = val.shape[0]
    for s in range(ROW_SUB):
        ref[pl.ds(first * ROW_SUB + s, n, stride=ROW_SUB), :] = val[:, s * LANE:(s + 1) * LANE]


def _mod_kernel(c_ref, w_ref, b_ref, o_ref):
    c = c_ref[...]
    s = (c * jax.nn.sigmoid(c)).astype(BF16)
    o_ref[0] = jnp.dot(s, w_ref[0].astype(BF16), preferred_element_type=F32) + b_ref[0]


def _modulation(cc, w_mod, b_mod):
    L, D, D6 = w_mod.shape
    R = cc.shape[0]
    tn = 1536
    return pl.pallas_call(
        _mod_kernel,
        grid=(L, D6 // tn),
        in_specs=[
            pl.BlockSpec((R, D), lambda l, j: (0, 0)),
            pl.BlockSpec((1, D, tn), lambda l, j: (l, 0, j)),
            pl.BlockSpec((1, 1, tn), lambda l, j: (l, 0, j)),
        ],
        out_specs=pl.BlockSpec((1, R, tn), lambda l, j: (l, 0, j)),
        out_shape=jax.ShapeDtypeStruct((L, R, D6), F32),
        compiler_params=_cparams(("arbitrary", "arbitrary")),
        name="modulation",
    )(cc, w_mod, b_mod.reshape(L, 1, D6))


def _select_tile(lat_ref, ctx_ref, n_lat_steps):
    return jnp.where(pl.program_id(0) < n_lat_steps, lat_ref[...], ctx_ref[...])


def _stream_specs(D, n_lat_steps, ctx_off):
    lat = pl.BlockSpec((ST, D), lambda i: (jnp.minimum(i, n_lat_steps - 1), 0))
    ctx = pl.BlockSpec((ST, D), lambda i: (ctx_off + jnp.maximum(i - n_lat_steps, 0), 0))
    return lat, ctx


def _pre_kernel(lat_ref, ctx_ref, mod_ref, g1_ref, win_ref, lng_ref, lnb_ref, wsp_ref, bsp_ref, wg_ref, bg_ref,
                trif_ref, trib_ref, eye_ref,
                a_ref, qe_ref, oi_ref, sr_ref, kv_ref, dec_ref, *, n_lat_steps):
    D = lat_ref.shape[1]
    h = _select_tile(lat_ref, ctx_ref, n_lat_steps)
    mod = mod_ref[0]
    sh1 = mod[:, 0:D]
    sc1 = mod[:, D:2 * D]
    ms = jnp.mean(h * h, axis=-1, keepdims=True)
    hn = h * lax.rsqrt(ms + EPS) * g1_ref[...]
    hn = hn * (1.0 + sc1) + sh1
    z = jnp.dot(hn.astype(BF16), win_ref[...], preferred_element_type=F32)
    o_av = A_WIDTH
    o_q = 2 * A_WIDTH
    o_r = o_q + B_KEY_WIDTH
    o_k = o_r + B_WIDTH
    o_v = o_k + B_KEY_WIDTH
    o_g = o_v + B_WIDTH
    zu = z[:, 0:o_av]
    zv = z[:, o_av:o_q]
    q_all = z[:, o_q:o_r] * (B_DK ** -0.5)
    zr = z[:, o_r:o_k]
    k_all = z[:, o_k:o_v]
    vv_all = z[:, o_v:o_g]
    zg = z[:, o_g:o_g + LANE]

    u = jax.nn.gelu(zu)
    v = jax.nn.gelu(zv)
    s_groups = []
    for g in range(A_GROUPS):
        sl = slice(g * A_GROUP_DIM, (g + 1) * A_GROUP_DIM)
        vg = v[:, sl]
        mu = jnp.mean(vg, axis=-1, keepdims=True)
        dv = vg - mu
        var = jnp.mean(dv * dv, axis=-1, keepdims=True)
        vn = (dv * lax.rsqrt(var + EPS)) * lng_ref[:, sl] + lnb_ref[:, sl]
        vnb = vn.astype(BF16)
        n_ch = ST // CHUNK_A
        rhs = jnp.concatenate([vnb[c * CHUNK_A:(c + 1) * CHUNK_A, :] for c in range(n_ch)], axis=1)
        mixed = jnp.dot(wsp_ref[g], rhs, preferred_element_type=F32)
        bias = bsp_ref[:, sl]
        s_groups.append(jnp.concatenate(
            [mixed[:, c * A_GROUP_DIM:(c + 1) * A_GROUP_DIM] + bias for c in range(n_ch)], axis=0))
    a_ref[...] = (u * jnp.concatenate(s_groups, axis=1)).astype(BF16)

    sr_ref[...] = (zr * jax.nn.sigmoid(zr)).astype(BF16)
    lg = jnp.dot(zg.astype(BF16), wg_ref[...], preferred_element_type=F32) + bg_ref[...]
    la_all = (jnp.minimum(lg, 0.0) - jnp.log1p(jnp.exp(-jnp.abs(lg)))) * (1.0 / GATE_TAU)
    for t in range(SUB):
        rs = slice(t * TT, (t + 1) * TT)
        _gla_local(t, rs, la_all[rs, :], q_all[rs, :], k_all[rs, :], vv_all[rs, :],
                   trif_ref, trib_ref, eye_ref, qe_ref, oi_ref, kv_ref, dec_ref)


def _gla_local(t, rs, la, q, k, vv, trif_ref, trib_ref, eye_ref, qe_ref, oi_ref, kv_ref, dec_ref):
    la_hi = la.astype(BF16)
    la_lo = (la - la_hi.astype(F32)).astype(BF16)
    KW = B_KEY_WIDTH
    bf = (jnp.dot(trif_ref[...], la_hi[:, :KW], preferred_element_type=F32)
          + jnp.dot(trif_ref[...], la_lo[:, :KW], preferred_element_type=F32))
    bb = (jnp.dot(trib_ref[...], la_hi[:, KW:], preferred_element_type=F32)
          + jnp.dot(trib_ref[...], la_lo[:, KW:], preferred_element_type=F32))

    def chunk_rows(x, r):
        return jnp.concatenate(
            [jnp.broadcast_to(x[c * GLA_CHUNK + r:c * GLA_CHUNK + r + 1, :], (GLA_CHUNK, KW)) for c in range(CPT)], axis=0)

    blf = chunk_rows(bf, GLA_CHUNK - 1)
    blb = chunk_rows(bb, 0)
    bmf = chunk_rows(bf, CHUNK_B - 1)
    bmb = chunk_rows(bb, CHUNK_B)
    qe_ref[rs, :] = jnp.concatenate([q * jnp.exp(bf), q * jnp.exp(bb)], axis=1).astype(BF16)
    kd_f = k * jnp.exp(blf - bf)
    kd_b = k * jnp.exp(blb - bb)
    qe_f = q * jnp.exp(bf - bmf)
    qe_b = q * jnp.exp(bb - bmb)
    ke_f = (k * jnp.exp(bmf - bf)).astype(BF16)
    ke_b = (k * jnp.exp(bmb - bb)).astype(BF16)
    nt = (((1,), (1,)), ((), ()))
    dec_rows = ([jnp.exp(bf[c * GLA_CHUNK + GLA_CHUNK - 1:(c + 1) * GLA_CHUNK, :]) for c in range(CPT)]
                + [jnp.exp(bb[c * GLA_CHUNK:c * GLA_CHUNK + 1, :]) for c in range(CPT)]
                + [jnp.zeros((LANE - 2 * CPT, KW), F32)])
    dec_pad = jnp.concatenate(dec_rows, axis=0)
    dec_hi = dec_pad.astype(BF16)
    dec_lo = (dec_pad - dec_hi.astype(F32)).astype(BF16)
    dec_ref[t] = (lax.dot_general(eye_ref[...], dec_hi, nt, preferred_element_type=F32)
                  + lax.dot_general(eye_ref[...], dec_lo, nt, preferred_element_type=F32))

    row = lax.broadcasted_iota(I32, (TT, TT), 0)
    col = lax.broadcasted_iota(I32, (TT, TT), 1)
    chunk_shift = GLA_CHUNK.bit_length() - 1
    same = lax.shift_right_logical(row, chunk_shift) == lax.shift_right_logical(col, chunk_shift)
    m_f = same & (row >= col)
    m_b = same & (row <= col)
    lane_head = lax.shift_right_logical(lax.broadcasted_iota(I32, (1, KW), 1), B_DK.bit_length() - 1)
    tok_chunk = lax.shift_right_logical(lax.broadcasted_iota(I32, (1, TT), 1), chunk_shift)
    vb = vv.astype(BF16)
    kdT_f = kd_f.T
    kdT_b = kd_b.T
    oi_heads = []
    for hd in range(B_HEADS):
        hm = lane_head == hd
        qf = jnp.where(hm, qe_f, 0.0).astype(BF16)
        qb = jnp.where(hm, qe_b, 0.0).astype(BF16)
        att_f = lax.dot_general(qf, ke_f, nt, preferred_element_type=F32)
        att_b = lax.dot_general(qb, ke_b, nt, preferred_element_type=F32)
        att = jnp.where(m_f, att_f, 0.0) + jnp.where(m_b, att_b, 0.0)
        v_h = vb[:, hd * B_DV:(hd + 1) * B_DV]
        oi_heads.append(jnp.dot(att.astype(BF16), v_h, preferred_element_type=F32))
        parts = []
        for kdT in (kdT_f, kdT_b):
            kh = kdT[hd * B_DK:(hd + 1) * B_DK, :]
            for c in range(CPT):
                parts.append(jnp.where(tok_chunk == c, kh, 0.0))
        lhs = jnp.concatenate(parts, axis=0).astype(BF16)
        kv_ref[t, hd] = jnp.dot(lhs, v_h, preferred_element_type=F32)
    oi_ref[rs, :] = jnp.concatenate(oi_heads, axis=1)


def _pre_call(lat, ctx, ctx_off, n_ctx_steps, mod3, lw, n_lat_steps, steps_per_sample, n_mod_ctx):
    D = lat.shape[1]
    n_steps = n_lat_steps + n_ctx_steps
    n_tiles = n_steps * SUB
    N = n_steps * ST
    lat_spec, ctx_spec = _stream_specs(D, n_lat_steps, ctx_off)
    DINP = lw["w_in"].shape[1]
    KVR = 2 * CPT * B_DK

    def mod_map(i):
        return (jnp.where(i < n_lat_steps, i // steps_per_sample, n_mod_ctx), 0, 0)

    const2 = lambda i: (0, 0)
    const3 = lambda i: (0, 0, 0)
    tile2 = lambda i: (i, 0)
    return pl.pallas_call(
        functools.partial(_pre_kernel, n_lat_steps=n_lat_steps),
        grid=(n_steps,),
        in_specs=[
            lat_spec,
            ctx_spec,
            pl.BlockSpec((1, 1, 6 * D), mod_map),
            pl.BlockSpec((1, D), const2),
            pl.BlockSpec((D, DINP), const2),
            pl.BlockSpec((1, A_WIDTH), const2),
            pl.BlockSpec((1, A_WIDTH), const2),
            pl.BlockSpec((A_GROUPS, CHUNK_A, CHUNK_A), const3),
            pl.BlockSpec((CHUNK_A, A_WIDTH), const2),
            pl.BlockSpec((LANE, 2 * B_KEY_WIDTH), const2),
            pl.BlockSpec((1, 2 * B_KEY_WIDTH), const2),
            pl.BlockSpec((TT, TT), const2),
            pl.BlockSpec((TT, TT), const2),
            pl.BlockSpec((B_KEY_WIDTH, B_KEY_WIDTH), const2),
        ],
        out_specs=[
            pl.BlockSpec((ST, A_WIDTH), tile2),
            pl.BlockSpec((ST, 2 * B_KEY_WIDTH), tile2),
            pl.BlockSpec((ST, B_WIDTH), tile2),
            pl.BlockSpec((ST, B_WIDTH), tile2),
            pl.BlockSpec((SUB, B_HEADS, KVR, B_DV), lambda i: (i, 0, 0, 0)),
            pl.BlockSpec((SUB, B_KEY_WIDTH, LANE), lambda i: (i, 0, 0)),
        ],
        out_shape=[
            jax.ShapeDtypeStruct((N, A_WIDTH), BF16),
            jax.ShapeDtypeStruct((N, 2 * B_KEY_WIDTH), BF16),
            jax.ShapeDtypeStruct((N, B_WIDTH), F32),
            jax.ShapeDtypeStruct((N, B_WIDTH), BF16),
            jax.ShapeDtypeStruct((n_tiles, B_HEADS, KVR, B_DV), F32),
            jax.ShapeDtypeStruct((n_tiles, B_KEY_WIDTH, LANE), F32),
        ],
        compiler_params=_cparams(("arbitrary",), VMEM_LIMIT_TOKEN_LOCAL),
        name="mix_pre",
    )(lat, ctx, mod3, lw["g_norm1"], lw["w_in"], lw["ln_g"], lw["ln_b"], lw["w_sp"], lw["b_sp"],
      lw["w_gate"], lw["b_gate"], lw["tri_f"], lw["tri_b"], lw["eye_k"])


def _scan_kernel(kvc_ref, kvl_ref, dcc_ref, dcl_ref, sc_ref, sl_ref):
    n_lat = kvl_ref.shape[0]
    fwd = [(kvc_ref, dcc_ref, sc_ref, 0, c) for c in range(CPT)]
    fwd += [(kvl_ref, dcl_ref, sl_ref, t, c) for t in range(n_lat) for c in range(CPT)]
    bwd = [(kvc_ref, dcc_ref, sc_ref, 0, c) for c in reversed(range(CPT))]
    bwd += [(kvl_ref, dcl_ref, sl_ref, t, c) for t in reversed(range(n_lat)) for c in reversed(range(CPT))]
    for hd in range(B_HEADS):
        for d, seq in enumerate((fwd, bwd)):
            s = jnp.zeros((B_DK, B_DV), F32)
            for kv_ref, dc_ref, out_ref, t, c in seq:
                r0 = (d * CPT + c) * B_DK
                out_ref[t, hd, r0:r0 + B_DK, :] = s.astype(BF16)
                dcol = dc_ref[t, hd * B_DK:(hd + 1) * B_DK, d * CPT + c:d * CPT + c + 1]
                s = dcol * s + kv_ref[t, hd, r0:r0 + B_DK, :]


def _scan_call(kv, dec, n_samples, n_lat_tiles, tiles_per_sample):
    n_tiles, _, KVR, _ = kv.shape
    kv_c = pl.BlockSpec((1, B_HEADS, KVR, B_DV), lambda b: (n_lat_tiles + b, 0, 0, 0))
    kv_l = pl.BlockSpec((tiles_per_sample, B_HEADS, KVR, B_DV), lambda b: (b, 0, 0, 0))
    dc_c = pl.BlockSpec((1, B_KEY_WIDTH, LANE), lambda b: (n_lat_tiles + b, 0, 0))
    dc_l = pl.BlockSpec((tiles_per_sample, B_KEY_WIDTH, LANE), lambda b: (b, 0, 0))
    s_ctx, s_lat = pl.pallas_call(
        _scan_kernel,
        grid=(n_samples,),
        in_specs=[kv_c, kv_l, dc_c, dc_l],
        out_specs=[
            pl.BlockSpec((1, B_HEADS, KVR, B_DV), lambda b: (b, 0, 0, 0)),
            pl.BlockSpec((tiles_per_sample, B_HEADS, KVR, B_DV), lambda b: (b, 0, 0, 0)),
        ],
        out_shape=[
            jax.ShapeDtypeStruct((n_samples, B_HEADS, KVR, B_DV), BF16),
            jax.ShapeDtypeStruct((n_lat_tiles, B_HEADS, KVR, B_DV), BF16),
        ],
        compiler_params=_cparams(("arbitrary",)),
        name="gla_scan",
    )(kv, kv, dec, dec)
    return jnp.concatenate([s_lat, s_ctx], axis=0)


def _post_kernel(lat_ref, ctx_ref, mod_ref, a_ref, qe_ref, oi_ref, sr_ref, st_ref, ggla_ref, wout_ref, g2_ref,
                 wr_ref, br_ref, ustr_ref,
                 h1_ref, hn2_ref, route_ref, routet_ref, cnt_ref,
                 rhs_scr, cnt_scr, *, n_lat_steps):
    D = lat_ref.shape[1]
    i = pl.program_id(0)

    @pl.when(i == 0)
    def _():
        rhs_scr[...] = jnp.zeros(rhs_scr.shape, rhs_scr.dtype)
        cnt_scr[...] = jnp.zeros(cnt_scr.shape, cnt_scr.dtype)

    qe = qe_ref[...]
    o_parts = []
    for c in range(SUB * CPT):
        t, ct = divmod(c, CPT)
        for d in range(2):
            for hd in range(B_HEADS):
                r0 = d * B_KEY_WIDTH + hd * B_DK
                s0 = (d * CPT + ct) * B_DK
                rhs_scr[c, r0:r0 + B_DK, hd * B_DV:(hd + 1) * B_DV] = st_ref[t, hd, s0:s0 + B_DK, :]
        o_parts.append(jnp.dot(qe[c * GLA_CHUNK:(c + 1) * GLA_CHUNK, :], rhs_scr[c], preferred_element_type=F32))
    o = oi_ref[...] + jnp.concatenate(o_parts, axis=0)

    heads = []
    for hd in range(B_HEADS):
        sl = slice(hd * B_DV, (hd + 1) * B_DV)
        oh = o[:, sl]
        msq = jnp.mean(oh * oh, axis=-1, keepdims=True)
        heads.append(oh * lax.rsqrt(msq + EPS) * ggla_ref[:, sl])
    b_out = jnp.concatenate(heads, axis=1) * sr_ref[...].astype(F32)
    mix_in = jnp.concatenate([a_ref[...], b_out.astype(BF16)], axis=1)
    mix = jnp.dot(mix_in, wout_ref[...], preferred_element_type=F32)

    mod = mod_ref[0]
    g1 = mod[:, 2 * D:3 * D]
    sh2 = mod[:, 3 * D:4 * D]
    sc2 = mod[:, 4 * D:5 * D]
    h_in = _select_tile(lat_ref, ctx_ref, n_lat_steps)
    nt = (((1,), (1,)), ((), ()))
    neg = jnp.float32(-3.0e38)
    big = jnp.float32(1.0e9)
    sub = lax.broadcasted_iota(I32, (EXPERTS_PER_GROUP, LANE), 0).astype(F32)
    route_parts = []
    for hf in range(ST // LANE):
        rs = slice(hf * LANE, (hf + 1) * LANE)
        h1 = h_in[rs, :] + g1 * mix[rs, :]
        h1_ref[rs, :] = h1
        msq = jnp.mean(h1 * h1, axis=-1, keepdims=True)
        hn2 = h1 * lax.rsqrt(msq + EPS) * g2_ref[...]
        hn2 = hn2 * (1.0 + sc2) + sh2
        _store_rows(hn2_ref, hn2, hf * LANE)

        lgt = lax.dot_general(wr_ref[...], hn2.astype(BF16), nt, preferred_element_type=F32) + br_ref[...]
        gl = lgt[N_EXPERTS:N_EXPERTS + N_GROUPS, :]
        gmax = jnp.max(gl, axis=0, keepdims=True)
        p_top = 1.0 / jnp.sum(jnp.exp(gl - gmax), axis=0, keepdims=True)
        gidx = jnp.min(jnp.where(gl == gmax, sub, big), axis=0, keepdims=True)
        el = jnp.zeros((EXPERTS_PER_GROUP, LANE), F32)
        for g in range(N_GROUPS):
            el = el + jnp.where(gidx == float(g), lgt[g * EXPERTS_PER_GROUP:(g + 1) * EXPERTS_PER_GROUP, :], 0.0)
        m1 = jnp.max(el, axis=0, keepdims=True)
        i1 = jnp.min(jnp.where(el == m1, sub, big), axis=0, keepdims=True)
        el2 = jnp.where(sub == i1, neg, el)
        m2 = jnp.max(el2, axis=0, keepdims=True)
        i2 = jnp.min(jnp.where(el2 == m2, sub, big), axis=0, keepdims=True)
        t = jnp.exp(m2 - m1)
        route_parts.append((gidx * float(EXPERTS_PER_GROUP) + i1, gidx * float(EXPERTS_PER_GROUP) + i2,
                            p_top / (1.0 + t), p_top * t / (1.0 + t)))
    e1_all, e2_all, gate1_all, gate2_all = (jnp.concatenate(p, axis=1) for p in zip(*route_parts))

    rowf = lax.broadcasted_iota(I32, (LANE, TT), 0).astype(F32)
    second = lax.broadcasted_iota(I32, (LANE, 1), 0) >= N_EXPERTS
    base = cnt_scr[...]
    for t in range(SUB):
        ls = slice(t * TT, (t + 1) * TT)
        e1, e2 = e1_all[:, ls], e2_all[:, ls]
        sel1 = rowf == e1
        sel2 = rowf == (e2 + float(N_EXPERTS))
        oh = jnp.where(sel1, 1.0, 0.0) + jnp.where(sel2, 1.0, 0.0)
        prefix = jnp.dot(oh.astype(BF16), ustr_ref[...], preferred_element_type=F32)
        tot = jnp.sum(oh, axis=1, keepdims=True)
        tot_sw = jnp.concatenate([tot[N_EXPERTS:, :], tot[:N_EXPERTS, :]], axis=0)
        val = prefix + (base + jnp.where(second, tot_sw, 0.0))
        r1 = jnp.sum(jnp.where(sel1, val, 0.0), axis=0, keepdims=True)
        r2 = jnp.sum(jnp.where(sel2, val, 0.0), axis=0, keepdims=True)
        base = base + tot + tot_sw
        zero = jnp.zeros_like(e1)
        route_t = jnp.concatenate([e1, e2, gate1_all[:, ls], gate2_all[:, ls], r1, r2, zero, zero], axis=0)
        routet_ref[:, ls] = route_t
        route_ref[ls, :] = jnp.concatenate([route_t, jnp.zeros((LANE - SUBLANE, TT), F32)], axis=0).T
    cnt_scr[...] = base
    cnt_ref[...] = base


def _post_call(lat, ctx, ctx_off, mod3, pre_outs, states, lw, n_post_steps, n_lat_steps, steps_per_sample, n_mod_ctx):
    a_out, qe, oi, sr = pre_outs
    D = lat.shape[1]
    KVR = 2 * CPT * B_DK
    NP = n_post_steps * ST
    lat_spec, ctx_spec = _stream_specs(D, n_lat_steps, ctx_off)

    def mod_map(i):
        return (jnp.where(i < n_lat_steps, i // steps_per_sample, n_mod_ctx), 0, 0)

    const2 = lambda i: (0, 0)
    tile2 = lambda i: (i, 0)
    return pl.pallas_call(
        functools.partial(_post_kernel, n_lat_steps=n_lat_steps),
        grid=(n_post_steps,),
        in_specs=[
            lat_spec,
            ctx_spec,
            pl.BlockSpec((1, 1, 6 * D), mod_map),
            pl.BlockSpec((ST, A_WIDTH), tile2),
            pl.BlockSpec((ST, 2 * B_KEY_WIDTH), tile2),
            pl.BlockSpec((ST, B_WIDTH), tile2),
            pl.BlockSpec((ST, B_WIDTH), tile2),
            pl.BlockSpec((SUB, B_HEADS, KVR, B_DV), lambda i: (i, 0, 0, 0)),
            pl.BlockSpec((1, B_WIDTH), const2),
            pl.BlockSpec((D, D), const2),
            pl.BlockSpec((1, D), const2),
            pl.BlockSpec((LANE, D), const2),
            pl.BlockSpec((LANE, 1), const2),
            pl.BlockSpec((TT, TT), const2),
        ],
        out_specs=[
            pl.BlockSpec((ST, D), tile2),
            pl.BlockSpec((ST * ROW_SUB, LANE), tile2),
            pl.BlockSpec((ST, LANE), tile2),
            pl.BlockSpec((SUBLANE, ST), lambda i: (0, i)),
            pl.BlockSpec((LANE, 1), const2),
        ],
        out_shape=[
            jax.ShapeDtypeStruct((NP, D), F32),
            jax.ShapeDtypeStruct((NP * ROW_SUB, LANE), F32),
            jax.ShapeDtypeStruct((NP, LANE), F32),
            jax.ShapeDtypeStruct((SUBLANE, NP), F32),
            jax.ShapeDtypeStruct((LANE, 1), F32),
        ],
        scratch_shapes=[
            pltpu.VMEM((SUB * CPT, 2 * B_KEY_WIDTH, B_WIDTH), BF16),
            pltpu.VMEM((LANE, 1), F32),
        ],
        compiler_params=_cparams(("arbitrary",), VMEM_LIMIT_TOKEN_LOCAL),
        name="mix_post",
    )(lat, ctx, mod3, a_out, qe, oi, sr, states, lw["g_gla"], lw["w_out"], lw["g_norm2"],
      lw["w_router"], lw["b_router"], lw["u_strict"])


def _plan_kernel(rt_ref, ps_ref, idx_ref):
    sub = lax.broadcasted_iota(I32, (N_EXPERTS, LANE), 0).astype(F32)
    ps = ps_ref[...]
    segs = DT // LANE
    for k in range(2):
        for seg in range(segs):
            sl = slice(seg * LANE, (seg + 1) * LANE)
            e = rt_ref[k:k + 1, sl]
            r = rt_ref[4 + k:5 + k, sl]
            base = jnp.sum(jnp.where(sub == e, ps, 0.0), axis=0, keepdims=True)
            idx_ref[0, k * segs + seg:k * segs + seg + 1, :] = ((base + r) * float(ROW_SUB)).astype(I32)


def _plan_call(route_t, pad_starts):
    n_dt = route_t.shape[1] // DT
    return pl.pallas_call(
        _plan_kernel,
        grid=(n_dt,),
        in_specs=[
            pl.BlockSpec((SUBLANE, DT), lambda i: (0, i)),
            pl.BlockSpec((N_EXPERTS, 1), lambda i: (0, 0)),
        ],
        out_specs=pl.BlockSpec((1, 2 * DT // LANE, LANE), lambda i: (i, 0, 0)),
        out_shape=jax.ShapeDtypeStruct((n_dt, 2 * DT // LANE, LANE), I32),
        compiler_params=_cparams(("arbitrary",)),
        name="moe_plan",
    )(route_t, pad_starts.astype(F32)[:, None])


IDX_UNROLL = 8


def _row_loop(n_rows, fn):
    def body(jo, carry):
        for u in range(IDX_UNROLL):
            fn(jo * IDX_UNROLL + u, jo, u)
        return carry

    lax.fori_loop(0, n_rows // IDX_UNROLL, body, 0)


def _tile_indices(idx_smem, jo, u):
    d0 = pl.multiple_of(idx_smem[jo, u], ROW_SUB)
    d1 = pl.multiple_of(idx_smem[jo, IDX_UNROLL + u], ROW_SUB)
    return d0, d1


def _index_tiles(idx):
    n = idx.shape[0]
    d = idx.reshape(n, 2, DT // IDX_UNROLL, IDX_UNROLL).transpose(0, 2, 1, 3).reshape(n, DT // IDX_UNROLL, 2 * IDX_UNROLL)
    return jnp.pad(d, ((0, 0), (0, 0), (0, LANE - 2 * IDX_UNROLL)))


def _dispatch_kernel(pe_ref, idx_hbm, x_ref, xs_out, idx_smem, zbuf, sem_i, sem_o, sem_z):
    i = pl.program_id(0)
    cp = pltpu.make_async_copy(idx_hbm.at[i], idx_smem, sem_i)
    cp.start()
    blk_rows = MOE_BLK * ROW_SUB

    @pl.when(i == 0)
    def _():
        zbuf[...] = jnp.zeros(zbuf.shape, zbuf.dtype)
        for e in range(N_EXPERTS):
            start = pl.multiple_of(jnp.maximum(pe_ref[e] - MOE_BLK, 0) * ROW_SUB, blk_rows)
            pltpu.make_async_copy(zbuf, xs_out.at[pl.ds(start, blk_rows), :], sem_z).start()
        for e in range(N_EXPERTS):
            pltpu.make_async_copy(zbuf, xs_out.at[pl.ds(0, blk_rows), :], sem_z).wait()
        first_free = pe_ref[N_EXPERTS - 1] // MOE_BLK
        n_blocks = xs_out.shape[0] // blk_rows

        def clear(b, carry):
            start = pl.multiple_of(b * blk_rows, blk_rows)
            pltpu.make_async_copy(zbuf, xs_out.at[pl.ds(start, blk_rows), :], sem_z).start()
            return carry

        def clear_wait(b, carry):
            pltpu.make_async_copy(zbuf, xs_out.at[pl.ds(0, blk_rows), :], sem_z).wait()
            return carry

        lax.fori_loop(first_free, n_blocks, clear, 0)
        lax.fori_loop(first_free, n_blocks, clear_wait, 0)

    cp.wait()

    def issue(j, jo, u):
        d0, d1 = _tile_indices(idx_smem, jo, u)
        src = x_ref.at[pl.ds(pl.multiple_of(j * ROW_SUB, ROW_SUB), ROW_SUB), :]
        pltpu.make_async_copy(src, xs_out.at[pl.ds(d0, ROW_SUB), :], sem_o).start(priority=0)
        pltpu.make_async_copy(src, xs_out.at[pl.ds(d1, ROW_SUB), :], sem_o).start(priority=1)

    _row_loop(DT, issue)
    for _ in range(2):
        pltpu.make_async_copy(x_ref, xs_out.at[pl.ds(0, DT * ROW_SUB), :], sem_o).wait()


def _dispatch_call(pad_ends, idx, x_rows, n_buf_rows):
    n_tiles = x_rows.shape[0] // (DT * ROW_SUB)
    grid_spec = pltpu.PrefetchScalarGridSpec(
        num_scalar_prefetch=1,
        grid=(n_tiles,),
        in_specs=[
            pl.BlockSpec(memory_space=pl.ANY),
            pl.BlockSpec((DT * ROW_SUB, LANE), lambda i, pe: (i, 0)),
        ],
        out_specs=pl.BlockSpec(memory_space=pl.ANY),
        scratch_shapes=[
            pltpu.SMEM((DT // IDX_UNROLL, LANE), I32),
            pltpu.VMEM((MOE_BLK * ROW_SUB, LANE), F32),
            pltpu.SemaphoreType.DMA,
            pltpu.SemaphoreType.DMA,
            pltpu.SemaphoreType.DMA,
        ],
    )
    return pl.pallas_call(
        _dispatch_kernel,
        grid_spec=grid_spec,
        out_shape=jax.ShapeDtypeStruct((n_buf_rows * ROW_SUB, LANE), F32),
        compiler_params=_cparams(("arbitrary",)),
        name="moe_dispatch",
    )(pad_ends, idx, x_rows)


def _ffn_kernel(be_ref, nu_ref, ws_ref, nx_ref, x_ref, w1_hbm, w3_hbm, w2_hbm, y_ref,
                w1f, w3f, w2f, w1b, w3b, w2b, sem_w, *, layer):
    i = pl.program_id(0)
    nu = nu_ref[0]

    def weight_copies(e, s):
        return (pltpu.make_async_copy(w1_hbm.at[layer, e], w1f.at[s], sem_w.at[s]),
                pltpu.make_async_copy(w3_hbm.at[layer, e], w3f.at[s], sem_w.at[s]),
                pltpu.make_async_copy(w2_hbm.at[layer, e], w2f.at[s], sem_w.at[s]))

    @pl.when(i < nu)
    def _():
        e = be_ref[i]
        s = ws_ref[i]
        first_of_run = (i == 0) | (e != be_ref[jnp.maximum(i - 1, 0)])

        @pl.when(i == 0)
        def _():
            for cp in weight_copies(e, s):
                cp.start()

        @pl.when(first_of_run)
        def _():
            for cp in weight_copies(e, s):
                cp.wait()
            nxt = nx_ref[i]

            @pl.when(nxt >= 0)
            def _():
                for cp in weight_copies(nxt, 1 - s):
                    cp.start(priority=1)

            w1b[...] = w1f[s].astype(BF16)
            w3b[...] = w3f[s].astype(BF16)
            w2b[...] = w2f[s].astype(BF16)

        x = _load_rows(x_ref, MOE_BLK).astype(BF16)
        h1 = jnp.dot(x, w1b[...], preferred_element_type=F32)
        h3 = jnp.dot(x, w3b[...], preferred_element_type=F32)
        hh = (h1 * jax.nn.sigmoid(h1) * h3).astype(BF16)
        y = jnp.dot(hh, w2b[...], preferred_element_type=F32)
        _store_rows(y_ref, y)

    @pl.when(i >= nu)
    def _():
        y_ref[...] = jnp.zeros(y_ref.shape, y_ref.dtype)


def _ffn_call(block_expert, n_used, w_slot, next_expert, xs, w1, w3, w2, l):
    blk_rows = MOE_BLK * ROW_SUB
    NB = xs.shape[0] // blk_rows
    _, _, D, DE = w1.shape
    grid_spec = pltpu.PrefetchScalarGridSpec(
        num_scalar_prefetch=4,
        grid=(NB,),
        in_specs=[
            pl.BlockSpec((blk_rows, LANE), lambda i, be, nu, ws, nx: (jnp.minimum(i, nu[0] - 1), 0)),
            pl.BlockSpec(memory_space=pl.ANY),
            pl.BlockSpec(memory_space=pl.ANY),
            pl.BlockSpec(memory_space=pl.ANY),
        ],
        out_specs=pl.BlockSpec((blk_rows, LANE), lambda i, be, nu, ws, nx: (i, 0)),
        scratch_shapes=[
            pltpu.VMEM((2, D, DE), F32),
            pltpu.VMEM((2, D, DE), F32),
            pltpu.VMEM((2, DE, D), F32),
            pltpu.VMEM((D, DE), BF16),
            pltpu.VMEM((D, DE), BF16),
            pltpu.VMEM((DE, D), BF16),
            pltpu.SemaphoreType.DMA((2,)),
        ],
    )
    return pl.pallas_call(
        functools.partial(_ffn_kernel, layer=l),
        grid_spec=grid_spec,
        out_shape=jax.ShapeDtypeStruct(xs.shape, F32),
        compiler_params=_cparams(("arbitrary",)),
        name="moe_ffn",
    )(block_expert, n_used, w_slot, next_expert, xs, w1, w3, w2)


def _combine_kernel(idx_hbm, h1_ref, route_ref, mod_ref, gf_ref, y_hbm, o_ref, idx_smem, rows0, rows1, sem_i, sem_g,
                    *, final):
    D = h1_ref.shape[1]
    i = pl.program_id(0)
    n = pl.num_programs(0)
    slot = i & 1
    nxt = 1 - slot

    def idx_copy(tile, s):
        return pltpu.make_async_copy(idx_hbm.at[tile], idx_smem.at[s], sem_i.at[s])

    def gather(s):
        def issue(j, jo, u):
            d0, d1 = _tile_indices(idx_smem.at[s], jo, u)
            dst = pl.ds(pl.multiple_of(j * ROW_SUB, ROW_SUB), ROW_SUB)
            pltpu.make_async_copy(y_hbm.at[pl.ds(d0, ROW_SUB), :], rows0.at[s, dst, :], sem_g.at[s]).start(priority=0)
            pltpu.make_async_copy(y_hbm.at[pl.ds(d1, ROW_SUB), :], rows1.at[s, dst, :], sem_g.at[s]).start(priority=1)

        _row_loop(DT, issue)

    @pl.when(i == 0)
    def _():
        cp = idx_copy(0, 0)
        cp.start()
        cp.wait()
        gather(0)

        @pl.when(n > 1)
        def _():
            idx_copy(1, 1).start()

    @pl.when(i + 1 < n)
    def _():
        idx_copy(0, nxt).wait()
        gather(nxt)

    @pl.when(i + 2 < n)
    def _():
        idx_copy(i + 2, slot).start()

    for rows in (rows0, rows1):
        pltpu.make_async_copy(y_hbm.at[pl.ds(0, DT * ROW_SUB), :], rows.at[slot], sem_g.at[slot]).wait()

    route = route_ref[...]
    gate1 = route[:, 2:3]
    gate2 = route[:, 3:4]
    y = gate1 * _load_rows(rows0.at[slot], DT) + gate2 * _load_rows(rows1.at[slot], DT)
    g2 = mod_ref[0][:, 5 * D:6 * D]
    out = h1_ref[...] + g2 * y
    if final:
        msq = jnp.mean(out * out, axis=-1, keepdims=True)
        out = out * lax.rsqrt(msq + EPS) * gf_ref[...]
    o_ref[...] = out


def _combine_call(idx, h1, route, mod3, g_final, yb, n_lat_dt, dt_per_sample, n_mod_ctx, final):
    N, D = h1.shape
    n_tiles = N // DT

    def mod_map(i):
        return (jnp.where(i < n_lat_dt, i // dt_per_sample, n_mod_ctx), 0, 0)

    return pl.pallas_call(
        functools.partial(_combine_kernel, final=final),
        grid=(n_tiles,),
        in_specs=[
            pl.BlockSpec(memory_space=pl.ANY),
            pl.BlockSpec((DT, D), lambda i: (i, 0)),
            pl.BlockSpec((DT, LANE), lambda i: (i, 0)),
            pl.BlockSpec((1, 1, 6 * D), mod_map),
            pl.BlockSpec((1, D), lambda i: (0, 0)),
            pl.BlockSpec(memory_space=pl.ANY),
        ],
        out_specs=pl.BlockSpec((DT, D), lambda i: (i, 0)),
        out_shape=jax.ShapeDtypeStruct((N, D), F32),
        scratch_shapes=[
            pltpu.SMEM((2, DT // IDX_UNROLL, LANE), I32),
            pltpu.VMEM((2, DT * ROW_SUB, LANE), F32),
            pltpu.VMEM((2, DT * ROW_SUB, LANE), F32),
            pltpu.SemaphoreType.DMA((2,)),
            pltpu.SemaphoreType.DMA((2,)),
        ],
        compiler_params=_cparams(("arbitrary",)),
        name="moe_combine",
    )(idx, h1, route, mod3, g_final, yb)


def _layer_weights(l, w_in, g_norm1, ln_v_g, ln_v_b, w_sp, b_sp, w_gate_up, b_gate, g_gla, w_out, g_norm2,
                   w_router_g, b_router_g, w_router_e, b_router_e):
    D = w_in.shape[1]
    d_in = w_in.shape[2]
    pad = (-d_in) % LANE
    w_in_p = jnp.pad(w_in[l], ((0, 0), (0, pad))).astype(BF16)
    KW = B_KEY_WIDTH
    wg = jnp.zeros((LANE, 2 * KW), F32)
    wg = wg.at[0:GATE_RANK, 0:KW].set(w_gate_up[l, 0])
    wg = wg.at[GATE_RANK:2 * GATE_RANK, KW:2 * KW].set(w_gate_up[l, 1])
    bg = jnp.concatenate([b_gate[l, 0], b_gate[l, 1]])[None, :]
    bsp = jnp.repeat(b_sp[l].T, A_GROUP_DIM, axis=1)
    wr = jnp.zeros((LANE, D), F32)
    wr = wr.at[0:N_EXPERTS, :].set(w_router_e[l].T)
    wr = wr.at[N_EXPERTS:N_EXPERTS + N_GROUPS, :].set(w_router_g[l].T)
    br = jnp.zeros((LANE, 1), F32)
    br = br.at[0:N_EXPERTS, 0].set(b_router_e[l])
    br = br.at[N_EXPERTS:N_EXPERTS + N_GROUPS, 0].set(b_router_g[l])
    r = jnp.arange(TT, dtype=I32)
    same = (r[:, None] // GLA_CHUNK) == (r[None, :] // GLA_CHUNK)
    tri_f = (same & (r[:, None] >= r[None, :])).astype(BF16)
    tri_b = (same & (r[:, None] <= r[None, :])).astype(BF16)
    u_strict = (r[:, None] < r[None, :]).astype(BF16)
    eye_k = jnp.eye(B_KEY_WIDTH, dtype=BF16)
    return dict(
        w_in=w_in_p, g_norm1=g_norm1[l][None, :], ln_g=ln_v_g[l][None, :], ln_b=ln_v_b[l][None, :],
        w_sp=w_sp[l].astype(BF16), b_sp=bsp, w_gate=wg.astype(BF16), b_gate=bg,
        g_gla=g_gla[l][None, :], w_out=w_out[l].astype(BF16), g_norm2=g_norm2[l][None, :],
        w_router=wr.astype(BF16), b_router=br, tri_f=tri_f, tri_b=tri_b, u_strict=u_strict, eye_k=eye_k)


def _segment_layout(counts_col, n_tokens):
    counts = counts_col[:N_EXPERTS, 0].astype(I32)
    padded = (counts + MOE_BLK - 1) // MOE_BLK * MOE_BLK
    pad_ends = jnp.cumsum(padded)
    pad_starts = pad_ends - padded
    n_blocks = (n_tokens * 2 + N_EXPERTS * (MOE_BLK - 1)) // MOE_BLK + 1
    block_start = jnp.arange(n_blocks, dtype=I32) * MOE_BLK
    block_expert = jnp.sum((pad_ends[None, :] <= block_start[:, None]).astype(I32), axis=1)
    block_expert = jnp.minimum(block_expert, N_EXPERTS - 1)
    n_used = (pad_ends[-1] // MOE_BLK).astype(I32)[None]
    blk_id = jnp.arange(n_blocks, dtype=I32)
    prev = jnp.concatenate([block_expert[:1], block_expert[:-1]])
    first = ((blk_id == 0) | (block_expert != prev)) & (blk_id < n_used[0])
    w_slot = (jnp.cumsum(first.astype(I32)) - 1) & 1
    e_id = jnp.arange(N_EXPERTS, dtype=I32)
    later_used = (e_id[None, :] > e_id[:, None]) & (padded[None, :] > 0)
    next_used = jnp.min(jnp.where(later_used, e_id[None, :], N_EXPERTS), axis=1)
    next_used = jnp.where(next_used >= N_EXPERTS, -1, next_used)
    next_expert = jnp.sum(jnp.where(block_expert[:, None] == e_id[None, :], next_used[None, :], 0), axis=1).astype(I32)
    return pad_starts, pad_ends, block_expert, n_used, w_slot.astype(I32), next_expert, n_blocks * MOE_BLK


def kernel(x, c, ctx, c_ctx, w_mod, b_mod, g_norm1, w_in, ln_v_g, ln_v_b, w_sp, b_sp, w_gate_up, b_gate, g_gla,
           w_out, g_norm2, w_router_g, b_router_g, w_router_e, b_router_e, w1, w3, w2, g_final):
    Bn, S, D = x.shape
    Lc = ctx.shape[1]
    depth = w_mod.shape[0]
    assert S % DT == 0 and S % ST == 0 and (Bn * Lc) % DT == 0 and (Bn * Lc) % ST == 0 and Lc == TT
    n_lat = Bn * S
    n_ctx = Bn * Lc
    n_lat_tiles = n_lat // TT
    tiles_per_sample = S // TT
    n_lat_steps = n_lat // ST
    steps_per_sample = S // ST

    cc = jnp.concatenate([c, c_ctx[None, :]], axis=0)
    mod_all = _modulation(cc, w_mod, b_mod)
    gf = g_final[None, :]

    lat, cx, ctx_off = x.reshape(n_lat, D), ctx.reshape(n_ctx, D), 0
    for l in range(depth):
        last = l == depth - 1
        lw = _layer_weights(l, w_in, g_norm1, ln_v_g, ln_v_b, w_sp, b_sp, w_gate_up, b_gate, g_gla, w_out,
                            g_norm2, w_router_g, b_router_g, w_router_e, b_router_e)
        mod3 = mod_all[l].reshape(Bn + 1, 1, 6 * D)
        a_out, qe, oi, sr, kv, dec = _pre_call(lat, cx, ctx_off, n_ctx // ST, mod3, lw, n_lat_steps,
                                               steps_per_sample, Bn)
        states = _scan_call(kv, dec, Bn, n_lat_tiles, tiles_per_sample)
        n_tok = n_lat if last else n_lat + n_ctx
        h1, hn2_rows, route, route_t, counts = _post_call(lat, cx, ctx_off, mod3, (a_out, qe, oi, sr), states, lw,
                                                          n_tok // ST, n_lat_steps, steps_per_sample, Bn)
        pad_starts, pad_ends, block_expert, n_used, w_slot, next_expert, n_buf_rows = _segment_layout(counts, n_tok)
        idx = _index_tiles(_plan_call(route_t, pad_starts))
        xs = _dispatch_call(pad_ends, idx, hn2_rows, n_buf_rows)
        yb = _ffn_call(block_expert, n_used, w_slot, next_expert, xs, w1, w3, w2, l)
        h_all = _combine_call(idx, h1, route, mod3, gf, yb, n_lat // DT, S // DT, Bn, last)
        lat, cx, ctx_off = h_all, h_all, n_lat_steps
    return h_all.reshape(Bn, S, D)
```

```python
import functools

import jax
import jax.numpy as jnp
from jax import lax
from jax.experimental import pallas as pl
from jax.experimental.pallas import tpu as pltpu

F32 = jnp.float32
BF16 = jnp.bfloat16
I32 = jnp.int32

EPS = 1e-6
LANE = 128
SUBLANE = 8

A_GROUPS = 4
A_GROUP_DIM = 128
A_WIDTH = A_GROUPS * A_GROUP_DIM
CHUNK_A = 128
B_HEADS = 4
B_DK = 64
B_DV = 128
B_KEY_WIDTH = B_HEADS * B_DK
B_WIDTH = B_HEADS * B_DV
GATE_RANK = 16
GATE_TAU = 16.0
CHUNK_B = 64
N_GROUPS = 8
EXPERTS_PER_GROUP = 8
N_EXPERTS = N_GROUPS * EXPERTS_PER_GROUP

TT = 256
GLA_CHUNK = 2 * CHUNK_B
CPT = TT // GLA_CHUNK
SUB = 4
ST = SUB * TT
MOE_BLK = 256
DT = 1024
CT = 512
ROW_SUB = 8

VMEM_LIMIT = 48 * 1024 * 1024
VMEM_LIMIT_TOKEN_LOCAL = 56 * 1024 * 1024


def _cparams(sem, vmem_limit=VMEM_LIMIT):
    return pltpu.CompilerParams(dimension_semantics=sem, vmem_limit_bytes=vmem_limit)


def _load_rows(ref, n):
    return jnp.concatenate([ref[pl.ds(s, n, stride=ROW_SUB), :] for s in range(ROW_SUB)], axis=1)


def _store_rows(ref, val, first=0):
    n = val.shape[0]
    for s in range(ROW_SUB):
        ref[pl.ds(first * ROW_SUB + s, n, stride=ROW_SUB), :] = val[:, s * LANE:(s + 1) * LANE]


def _mod_kernel(c_ref, w_ref, b_ref, o_ref):
    c = c_ref[...]
    s = (c * jax.nn.sigmoid(c)).astype(BF16)
    o_ref[0] = jnp.dot(s, w_ref[0].astype(BF16), preferred_element_type=F32) + b_ref[0]


def _modulation(cc, w_mod, b_mod):
    L, D, D6 = w_mod.shape
    R = cc.shape[0]
    tn = 1536
    return pl.pallas_call(
        _mod_kernel,
        grid=(L, D6 // tn),
        in_specs=[
            pl.BlockSpec((R, D), lambda l, j: (0, 0)),
            pl.BlockSpec((1, D, tn), lambda l, j: (l, 0, j)),
            pl.BlockSpec((1, 1, tn), lambda l, j: (l, 0, j)),
        ],
        out_specs=pl.BlockSpec((1, R, tn), lambda l, j: (l, 0, j)),
        out_shape=jax.ShapeDtypeStruct((L, R, D6), F32),
        compiler_params=_cparams(("arbitrary", "arbitrary")),
        name="modulation",
    )(cc, w_mod, b_mod.reshape(L, 1, D6))


def _select_tile(lat_ref, ctx_ref, n_lat_steps):
    return jnp.where(pl.program_id(0) < n_lat_steps, lat_ref[...], ctx_ref[...])


def _stream_specs(D, n_lat_steps, ctx_off):
    lat = pl.BlockSpec((ST, D), lambda i: (jnp.minimum(i, n_lat_steps - 1), 0))
    ctx = pl.BlockSpec((ST, D), lambda i: (ctx_off + jnp.maximum(i - n_lat_steps, 0), 0))
    return lat, ctx


def _pre_kernel(lat_ref, ctx_ref, mod_ref, g1_ref, win_ref, lng_ref, lnb_ref, wsp_ref, bsp_ref, wg_ref, bg_ref,
                trif_ref, trib_ref, eye_ref,
                a_ref, qe_ref, oi_ref, sr_ref, kv_ref, dec_ref, *, n_lat_steps):
    D = lat_ref.shape[1]
    h = _select_tile(lat_ref, ctx_ref, n_lat_steps)
    mod = mod_ref[0]
    sh1 = mod[:, 0:D]
    sc1 = mod[:, D:2 * D]
    ms = jnp.mean(h * h, axis=-1, keepdims=True)
    hn = h * lax.rsqrt(ms + EPS) * g1_ref[...]
    hn = hn * (1.0 + sc1) + sh1
    z = jnp.dot(hn.astype(BF16), win_ref[...], preferred_element_type=F32)
    o_av = A_WIDTH
    o_q = 2 * A_WIDTH
    o_r = o_q + B_KEY_WIDTH
    o_k = o_r + B_WIDTH
    o_v = o_k + B_KEY_WIDTH
    o_g = o_v + B_WIDTH
    zu = z[:, 0:o_av]
    zv = z[:, o_av:o_q]
    q_all = z[:, o_q:o_r] * (B_DK ** -0.5)
    zr = z[:, o_r:o_k]
    k_all = z[:, o_k:o_v]
    vv_all = z[:, o_v:o_g]
    zg = z[:, o_g:o_g + LANE]

    u = jax.nn.gelu(zu)
    v = jax.nn.gelu(zv)
    s_groups = []
    for g in range(A_GROUPS):
        sl = slice(g * A_GROUP_DIM, (g + 1) * A_GROUP_DIM)
        vg = v[:, sl]
        mu = jnp.mean(vg, axis=-1, keepdims=True)
        dv = vg - mu
        var = jnp.mean(dv * dv, axis=-1, keepdims=True)
        vn = (dv * lax.rsqrt(var + EPS)) * lng_ref[:, sl] + lnb_ref[:, sl]
        vnb = vn.astype(BF16)
        n_ch = ST // CHUNK_A
        rhs = jnp.concatenate([vnb[c * CHUNK_A:(c + 1) * CHUNK_A, :] for c in range(n_ch)], axis=1)
        mixed = jnp.dot(wsp_ref[g], rhs, preferred_element_type=F32)
        bias = bsp_ref[:, sl]
        s_groups.append(jnp.concatenate(
            [mixed[:, c * A_GROUP_DIM:(c + 1) * A_GROUP_DIM] + bias for c in range(n_ch)], axis=0))
    a_ref[...] = (u * jnp.concatenate(s_groups, axis=1)).astype(BF16)

    sr_ref[...] = (zr * jax.nn.sigmoid(zr)).astype(BF16)
    lg = jnp.dot(zg.astype(BF16), wg_ref[...], preferred_element_type=F32) + bg_ref[...]
    la_all = (jnp.minimum(lg, 0.0) - jnp.log1p(jnp.exp(-jnp.abs(lg)))) * (1.0 / GATE_TAU)
    for t in range(SUB):
        rs = slice(t * TT, (t + 1) * TT)
        _gla_local(t, rs, la_all[rs, :], q_all[rs, :], k_all[rs, :], vv_all[rs, :],
                   trif_ref, trib_ref, eye_ref, qe_ref, oi_ref, kv_ref, dec_ref)


def _gla_local(t, rs, la, q, k, vv, trif_ref, trib_ref, eye_ref, qe_ref, oi_ref, kv_ref, dec_ref):
    la_hi = la.astype(BF16)
    la_lo = (la - la_hi.astype(F32)).astype(BF16)
    KW = B_KEY_WIDTH
    bf = (jnp.dot(trif_ref[...], la_hi[:, :KW], preferred_element_type=F32)
          + jnp.dot(trif_ref[...], la_lo[:, :KW], preferred_element_type=F32))
    bb = (jnp.dot(trib_ref[...], la_hi[:, KW:], preferred_element_type=F32)
          + jnp.dot(trib_ref[...], la_lo[:, KW:], preferred_element_type=F32))

    def chunk_rows(x, r):
        return jnp.concatenate(
            [jnp.broadcast_to(x[c * GLA_CHUNK + r:c * GLA_CHUNK + r + 1, :], (GLA_CHUNK, KW)) for c in range(CPT)], axis=0)

    blf = chunk_rows(bf, GLA_CHUNK - 1)
    blb = chunk_rows(bb, 0)
    bmf = chunk_rows(bf, CHUNK_B - 1)
    bmb = chunk_rows(bb, CHUNK_B)
    qe_ref[rs, :] = jnp.concatenate([q * jnp.exp(bf), q * jnp.exp(bb)], axis=1).astype(BF16)
    kd_f = k * jnp.exp(blf - bf)
    kd_b = k * jnp.exp(blb - bb)
    qe_f = q * jnp.exp(bf - bmf)
    qe_b = q * jnp.exp(bb - bmb)
    ke_f = (k * jnp.exp(bmf - bf)).astype(BF16)
    ke_b = (k * jnp.exp(bmb - bb)).astype(BF16)
    nt = (((1,), (1,)), ((), ()))
    dec_rows = ([jnp.exp(bf[c * GLA_CHUNK + GLA_CHUNK - 1:(c + 1) * GLA_CHUNK, :]) for c in range(CPT)]
                + [jnp.exp(bb[c * GLA_CHUNK:c * GLA_CHUNK + 1, :]) for c in range(CPT)]
                + [jnp.zeros((LANE - 2 * CPT, KW), F32)])
    dec_pad = jnp.concatenate(dec_rows, axis=0)
    dec_hi = dec_pad.astype(BF16)
    dec_lo = (dec_pad - dec_hi.astype(F32)).astype(BF16)
    dec_ref[t] = (lax.dot_general(eye_ref[...], dec_hi, nt, preferred_element_type=F32)
                  + lax.dot_general(eye_ref[...], dec_lo, nt, preferred_element_type=F32))

    row = lax.broadcasted_iota(I32, (TT, TT), 0)
    col = lax.broadcasted_iota(I32, (TT, TT), 1)
    chunk_shift = GLA_CHUNK.bit_length() - 1
    same = lax.shift_right_logical(row, chunk_shift) == lax.shift_right_logical(col, chunk_shift)
    m_f = same & (row >= col)
    m_b = same & (row <= col)
    lane_head = lax.shift_right_logical(lax.broadcasted_iota(I32, (1, KW), 1), B_DK.bit_length() - 1)
    tok_chunk = lax.shift_right_logical(lax.broadcasted_iota(I32, (1, TT), 1), chunk_shift)
    vb = vv.astype(BF16)
    kdT_f = kd_f.T
    kdT_b = kd_b.T
    oi_heads = []
    for hd in range(B_HEADS):
        hm = lane_head == hd
        qf = jnp.where(hm, qe_f, 0.0).astype(BF16)
        qb = jnp.where(hm, qe_b, 0.0).astype(BF16)
        att_f = lax.dot_general(qf, ke_f, nt, preferred_element_type=F32)
        att_b = lax.dot_general(qb, ke_b, nt, preferred_element_type=F32)
        att = jnp.where(m_f, att_f, 0.0) + jnp.where(m_b, att_b, 0.0)
        v_h = vb[:, hd * B_DV:(hd + 1) * B_DV]
        oi_heads.append(jnp.dot(att.astype(BF16), v_h, preferred_element_type=F32))
        parts = []
        for kdT in (kdT_f, kdT_b):
            kh = kdT[hd * B_DK:(hd + 1) * B_DK, :]
            for c in range(CPT):
                parts.append(jnp.where(tok_chunk == c, kh, 0.0))
        lhs = jnp.concatenate(parts, axis=0).astype(BF16)
        kv_ref[t, hd] = jnp.dot(lhs, v_h, preferred_element_type=F32)
    oi_ref[rs, :] = jnp.concatenate(oi_heads, axis=1)


def _pre_call(lat, ctx, ctx_off, n_ctx_steps, mod3, lw, n_lat_steps, steps_per_sample, n_mod_ctx):
    D = lat.shape[1]
    n_steps = n_lat_steps + n_ctx_steps
    n_tiles = n_steps * SUB
    N = n_steps * ST
    lat_spec, ctx_spec = _stream_specs(D, n_lat_steps, ctx_off)
    DINP = lw["w_in"].shape[1]
    KVR = 2 * CPT * B_DK

    def mod_map(i):
        return (jnp.where(i < n_lat_steps, i // steps_per_sample, n_mod_ctx), 0, 0)

    const2 = lambda i: (0, 0)
    const3 = lambda i: (0, 0, 0)
    tile2 = lambda i: (i, 0)
    return pl.pallas_call(
        functools.partial(_pre_kernel, n_lat_steps=n_lat_steps),
        grid=(n_steps,),
        in_specs=[
            lat_spec,
            ctx_spec,
            pl.BlockSpec((1, 1, 6 * D), mod_map),
            pl.BlockSpec((1, D), const2),
            pl.BlockSpec((D, DINP), const2),
            pl.BlockSpec((1, A_WIDTH), const2),
            pl.BlockSpec((1, A_WIDTH), const2),
            pl.BlockSpec((A_GROUPS, CHUNK_A, CHUNK_A), const3),
            pl.BlockSpec((CHUNK_A, A_WIDTH), const2),
            pl.BlockSpec((LANE, 2 * B_KEY_WIDTH), const2),
            pl.BlockSpec((1, 2 * B_KEY_WIDTH), const2),
            pl.BlockSpec((TT, TT), const2),
            pl.BlockSpec((TT, TT), const2),
            pl.BlockSpec((B_KEY_WIDTH, B_KEY_WIDTH), const2),
        ],
        out_specs=[
            pl.BlockSpec((ST, A_WIDTH), tile2),
            pl.BlockSpec((ST, 2 * B_KEY_WIDTH), tile2),
            pl.BlockSpec((ST, B_WIDTH), tile2),
            pl.BlockSpec((ST, B_WIDTH), tile2),
            pl.BlockSpec((SUB, B_HEADS, KVR, B_DV), lambda i: (i, 0, 0, 0)),
            pl.BlockSpec((SUB, B_KEY_WIDTH, LANE), lambda i: (i, 0, 0)),
        ],
        out_shape=[
            jax.ShapeDtypeStruct((N, A_WIDTH), BF16),
            jax.ShapeDtypeStruct((N, 2 * B_KEY_WIDTH), BF16),
            jax.ShapeDtypeStruct((N, B_WIDTH), F32),
            jax.ShapeDtypeStruct((N, B_WIDTH), BF16),
            jax.ShapeDtypeStruct((n_tiles, B_HEADS, KVR, B_DV), F32),
            jax.ShapeDtypeStruct((n_tiles, B_KEY_WIDTH, LANE), F32),
        ],
        compiler_params=_cparams(("arbitrary",), VMEM_LIMIT_TOKEN_LOCAL),
        name="mix_pre",
    )(lat, ctx, mod3, lw["g_norm1"], lw["w_in"], lw["ln_g"], lw["ln_b"], lw["w_sp"], lw["b_sp"],
      lw["w_gate"], lw["b_gate"], lw["tri_f"], lw["tri_b"], lw["eye_k"])


def _scan_kernel(kvc_ref, kvl_ref, dcc_ref, dcl_ref, sc_ref, sl_ref):
    n_lat = kvl_ref.shape[0]
    fwd = [(kvc_ref, dcc_ref, sc_ref, 0, c) for c in range(CPT)]
    fwd += [(kvl_ref, dcl_ref, sl_ref, t, c) for t in range(n_lat) for c in range(CPT)]
    bwd = [(kvc_ref, dcc_ref, sc_ref, 0, c) for c in reversed(range(CPT))]
    bwd += [(kvl_ref, dcl_ref, sl_ref, t, c) for t in reversed(range(n_lat)) for c in reversed(range(CPT))]
    for hd in range(B_HEADS):
        for d, seq in enumerate((fwd, bwd)):
            s = jnp.zeros((B_DK, B_DV), F32)
            for kv_ref, dc_ref, out_ref, t, c in seq:
                r0 = (d * CPT + c) * B_DK
                out_ref[t, hd, r0:r0 + B_DK, :] = s.astype(BF16)
                dcol = dc_ref[t, hd * B_DK:(hd + 1) * B_DK, d * CPT + c:d * CPT + c + 1]
                s = dcol * s + kv_ref[t, hd, r0:r0 + B_DK, :]


def _scan_call(kv, dec, n_samples, n_lat_tiles, tiles_per_sample):
    n_tiles, _, KVR, _ = kv.shape
    kv_c = pl.BlockSpec((1, B_HEADS, KVR, B_DV), lambda b: (n_lat_tiles + b, 0, 0, 0))
    kv_l = pl.BlockSpec((tiles_per_sample, B_HEADS, KVR, B_DV), lambda b: (b, 0, 0, 0))
    dc_c = pl.BlockSpec((1, B_KEY_WIDTH, LANE), lambda b: (n_lat_tiles + b, 0, 0))
    dc_l = pl.BlockSpec((tiles_per_sample, B_KEY_WIDTH, LANE), lambda b: (b, 0, 0))
    s_ctx, s_lat = pl.pallas_call(
        _scan_kernel,
        grid=(n_samples,),
        in_specs=[kv_c, kv_l, dc_c, dc_l],
        out_specs=[
            pl.BlockSpec((1, B_HEADS, KVR, B_DV), lambda b: (b, 0, 0, 0)),
            pl.BlockSpec((tiles_per_sample, B_HEADS, KVR, B_DV), lambda b: (b, 0, 0, 0)),
        ],
        out_shape=[
            jax.ShapeDtypeStruct((n_samples, B_HEADS, KVR, B_DV), BF16),
            jax.ShapeDtypeStruct((n_lat_tiles, B_HEADS, KVR, B_DV), BF16),
        ],
        compiler_params=_cparams(("arbitrary",)),
        name="gla_scan",
    )(kv, kv, dec, dec)
    return jnp.concatenate([s_lat, s_ctx], axis=0)


def _post_kernel(lat_ref, ctx_ref, mod_ref, a_ref, qe_ref, oi_ref, sr_ref, st_ref, ggla_ref, wout_ref, g2_ref,
                 wr_ref, br_ref, ustr_ref,
                 h1_ref, hn2_ref, route_ref, routet_ref, cnt_ref,
                 rhs_scr, cnt_scr, *, n_lat_steps):
    D = lat_ref.shape[1]
    i = pl.program_id(0)

    @pl.when(i == 0)
    def _():
        rhs_scr[...] = jnp.zeros(rhs_scr.shape, rhs_scr.dtype)
        cnt_scr[...] = jnp.zeros(cnt_scr.shape, cnt_scr.dtype)

    qe = qe_ref[...]
    o_parts = []
    for c in range(SUB * CPT):
        t, ct = divmod(c, CPT)
        for d in range(2):
            for hd in range(B_HEADS):
                r0 = d * B_KEY_WIDTH + hd * B_DK
                s0 = (d * CPT + ct) * B_DK
                rhs_scr[c, r0:r0 + B_DK, hd * B_DV:(hd + 1) * B_DV] = st_ref[t, hd, s0:s0 + B_DK, :]
        o_parts.append(jnp.dot(qe[c * GLA_CHUNK:(c + 1) * GLA_CHUNK, :], rhs_scr[c], preferred_element_type=F32))
    o = oi_ref[...] + jnp.concatenate(o_parts, axis=0)

    heads = []
    for hd in range(B_HEADS):
        sl = slice(hd * B_DV, (hd + 1) * B_DV)
        oh = o[:, sl]
        msq = jnp.mean(oh * oh, axis=-1, keepdims=True)
        heads.append(oh * lax.rsqrt(msq + EPS) * ggla_ref[:, sl])
    b_out = jnp.concatenate(heads, axis=1) * sr_ref[...].astype(F32)
    mix_in = jnp.concatenate([a_ref[...], b_out.astype(BF16)], axis=1)
    mix = jnp.dot(mix_in, wout_ref[...], preferred_element_type=F32)

    mod = mod_ref[0]
    g1 = mod[:, 2 * D:3 * D]
    sh2 = mod[:, 3 * D:4 * D]
    sc2 = mod[:, 4 * D:5 * D]
    h_in = _select_tile(lat_ref, ctx_ref, n_lat_steps)
    nt = (((1,), (1,)), ((), ()))
    neg = jnp.float32(-3.0e38)
    big = jnp.float32(1.0e9)
    sub = lax.broadcasted_iota(I32, (EXPERTS_PER_GROUP, LANE), 0).astype(F32)
    route_parts = []
    for hf in range(ST // LANE):
        rs = slice(hf * LANE, (hf + 1) * LANE)
        h1 = h_in[rs, :] + g1 * mix[rs, :]
        h1_ref[rs, :] = h1
        msq = jnp.mean(h1 * h1, axis=-1, keepdims=True)
        hn2 = h1 * lax.rsqrt(msq + EPS) * g2_ref[...]
        hn2 = hn2 * (1.0 + sc2) + sh2
        _store_rows(hn2_ref, hn2, hf * LANE)

        lgt = lax.dot_general(wr_ref[...], hn2.astype(BF16), nt, preferred_element_type=F32) + br_ref[...]
        gl = lgt[N_EXPERTS:N_EXPERTS + N_GROUPS, :]
        gmax = jnp.max(gl, axis=0, keepdims=True)
        p_top = 1.0 / jnp.sum(jnp.exp(gl - gmax), axis=0, keepdims=True)
        gidx = jnp.min(jnp.where(gl == gmax, sub, big), axis=0, keepdims=True)
        el = jnp.zeros((EXPERTS_PER_GROUP, LANE), F32)
        for g in range(N_GROUPS):
            el = el + jnp.where(gidx == float(g), lgt[g * EXPERTS_PER_GROUP:(g + 1) * EXPERTS_PER_GROUP, :], 0.0)
        m1 = jnp.max(el, axis=0, keepdims=True)
        i1 = jnp.min(jnp.where(el == m1, sub, big), axis=0, keepdims=True)
        el2 = jnp.where(sub == i1, neg, el)
        m2 = jnp.max(el2, axis=0, keepdims=True)
        i2 = jnp.min(jnp.where(el2 == m2, sub, big), axis=0, keepdims=True)
        t = jnp.exp(m2 - m1)
        route_parts.append((gidx * float(EXPERTS_PER_GROUP) + i1, gidx * float(EXPERTS_PER_GROUP) + i2,
                            p_top / (1.0 + t), p_top * t / (1.0 + t)))
    e1_all, e2_all, gate1_all, gate2_all = (jnp.concatenate(p, axis=1) for p in zip(*route_parts))

    rowf = lax.broadcasted_iota(I32, (LANE, TT), 0).astype(F32)
    second = lax.broadcasted_iota(I32, (LANE, 1), 0) >= N_EXPERTS
    base = cnt_scr[...]
    for t in range(SUB):
        ls = slice(t * TT, (t + 1) * TT)
        e1, e2 = e1_all[:, ls], e2_all[:, ls]
        sel1 = rowf == e1
        sel2 = rowf == (e2 + float(N_EXPERTS))
        oh = jnp.where(sel1, 1.0, 0.0) + jnp.where(sel2, 1.0, 0.0)
        prefix = jnp.dot(oh.astype(BF16), ustr_ref[...], preferred_element_type=F32)
        tot = jnp.sum(oh, axis=1, keepdims=True)
        tot_sw = jnp.concatenate([tot[N_EXPERTS:, :], tot[:N_EXPERTS, :]], axis=0)
        val = prefix + (base + jnp.where(second, tot_sw, 0.0))
        r1 = jnp.sum(jnp.where(sel1, val, 0.0), axis=0, keepdims=True)
        r2 = jnp.sum(jnp.where(sel2, val, 0.0), axis=0, keepdims=True)
        base = base + tot + tot_sw
        zero = jnp.zeros_like(e1)
        route_t = jnp.concatenate([e1, e2, gate1_all[:, ls], gate2_all[:, ls], r1, r2, zero, zero], axis=0)
        routet_ref[:, ls] = route_t
        route_ref[ls, :] = jnp.concatenate([route_t, jnp.zeros((LANE - SUBLANE, TT), F32)], axis=0).T
    cnt_scr[...] = base
    cnt_ref[...] = base


def _post_call(lat, ctx, ctx_off, mod3, pre_outs, states, lw, n_post_steps, n_lat_steps, steps_per_sample, n_mod_ctx):
    a_out, qe, oi, sr = pre_outs
    D = lat.shape[1]
    KVR = 2 * CPT * B_DK
    NP = n_post_steps * ST
    lat_spec, ctx_spec = _stream_specs(D, n_lat_steps, ctx_off)

    def mod_map(i):
        return (jnp.where(i < n_lat_steps, i // steps_per_sample, n_mod_ctx), 0, 0)

    const2 = lambda i: (0, 0)
    tile2 = lambda i: (i, 0)
    return pl.pallas_call(
        functools.partial(_post_kernel, n_lat_steps=n_lat_steps),
        grid=(n_post_steps,),
        in_specs=[
            lat_spec,
            ctx_spec,
            pl.BlockSpec((1, 1, 6 * D), mod_map),
            pl.BlockSpec((ST, A_WIDTH), tile2),
            pl.BlockSpec((ST, 2 * B_KEY_WIDTH), tile2),
            pl.BlockSpec((ST, B_WIDTH), tile2),
            pl.BlockSpec((ST, B_WIDTH), tile2),
            pl.BlockSpec((SUB, B_HEADS, KVR, B_DV), lambda i: (i, 0, 0, 0)),
            pl.BlockSpec((1, B_WIDTH), const2),
            pl.BlockSpec((D, D), const2),
            pl.BlockSpec((1, D), const2),
            pl.BlockSpec((LANE, D), const2),
            pl.BlockSpec((LANE, 1), const2),
            pl.BlockSpec((TT, TT), const2),
        ],
        out_specs=[
            pl.BlockSpec((ST, D), tile2),
            pl.BlockSpec((ST * ROW_SUB, LANE), tile2),
            pl.BlockSpec((ST, LANE), tile2),
            pl.BlockSpec((SUBLANE, ST), lambda i: (0, i)),
            pl.BlockSpec((LANE, 1), const2),
        ],
        out_shape=[
            jax.ShapeDtypeStruct((NP, D), F32),
            jax.ShapeDtypeStruct((NP * ROW_SUB, LANE), F32),
            jax.ShapeDtypeStruct((NP, LANE), F32),
            jax.ShapeDtypeStruct((SUBLANE, NP), F32),
            jax.ShapeDtypeStruct((LANE, 1), F32),
        ],
        scratch_shapes=[
            pltpu.VMEM((SUB * CPT, 2 * B_KEY_WIDTH, B_WIDTH), BF16),
            pltpu.VMEM((LANE, 1), F32),
        ],
        compiler_params=_cparams(("arbitrary",), VMEM_LIMIT_TOKEN_LOCAL),
        name="mix_post",
    )(lat, ctx, mod3, a_out, qe, oi, sr, states, lw["g_gla"], lw["w_out"], lw["g_norm2"],
      lw["w_router"], lw["b_router"], lw["u_strict"])


def _plan_kernel(rt_ref, ps_ref, idx_ref):
    sub = lax.broadcasted_iota(I32, (N_EXPERTS, LANE), 0).astype(F32)
    ps = ps_ref[...]
    segs = DT // LANE
    for k in range(2):
        for seg in range(segs):
            sl = slice(seg * LANE, (seg + 1) * LANE)
            e = rt_ref[k:k + 1, sl]
            r = rt_ref[4 + k:5 + k, sl]
            base = jnp.sum(jnp.where(sub == e, ps, 0.0), axis=0, keepdims=True)
            idx_ref[0, k * segs + seg:k * segs + seg + 1, :] = ((base + r) * float(ROW_SUB)).astype(I32)


def _plan_call(route_t, pad_starts):
    n_dt = route_t.shape[1] // DT
    return pl.pallas_call(
        _plan_kernel,
        grid=(n_dt,),
        in_specs=[
            pl.BlockSpec((SUBLANE, DT), lambda i: (0, i)),
            pl.BlockSpec((N_EXPERTS, 1), lambda i: (0, 0)),
        ],
        out_specs=pl.BlockSpec((1, 2 * DT // LANE, LANE), lambda i: (i, 0, 0)),
        out_shape=jax.ShapeDtypeStruct((n_dt, 2 * DT // LANE, LANE), I32),
        compiler_params=_cparams(("arbitrary",)),
        name="moe_plan",
    )(route_t, pad_starts.astype(F32)[:, None])


IDX_UNROLL = 8


def _row_loop(n_rows, fn):
    def body(jo, carry):
        for u in range(IDX_UNROLL):
            fn(jo * IDX_UNROLL + u, jo, u)
        return carry

    lax.fori_loop(0, n_rows // IDX_UNROLL, body, 0)


def _tile_indices(idx_smem, jo, u):
    d0 = pl.multiple_of(idx_smem[jo, u], ROW_SUB)
    d1 = pl.multiple_of(idx_smem[jo, IDX_UNROLL + u], ROW_SUB)
    return d0, d1


def _index_tiles(idx):
    n = idx.shape[0]
    d = idx.reshape(n, 2, DT // IDX_UNROLL, IDX_UNROLL).transpose(0, 2, 1, 3).reshape(n, DT // IDX_UNROLL, 2 * IDX_UNROLL)
    return jnp.pad(d, ((0, 0), (0, 0), (0, LANE - 2 * IDX_UNROLL)))


def _dispatch_kernel(pe_ref, idx_hbm, x_ref, xs_out, idx_smem, zbuf, sem_i, sem_o, sem_z):
    i = pl.program_id(0)
    cp = pltpu.make_async_copy(idx_hbm.at[i], idx_smem, sem_i)
    cp.start()
    blk_rows = MOE_BLK * ROW_SUB

    @pl.when(i == 0)
    def _():
        zbuf[...] = jnp.zeros(zbuf.shape, zbuf.dtype)
        for e in range(N_EXPERTS):
            start = pl.multiple_of(jnp.maximum(pe_ref[e] - MOE_BLK, 0) * ROW_SUB, blk_rows)
            pltpu.make_async_copy(zbuf, xs_out.at[pl.ds(start, blk_rows), :], sem_z).start()
        for e in range(N_EXPERTS):
            pltpu.make_async_copy(zbuf, xs_out.at[pl.ds(0, blk_rows), :], sem_z).wait()
        first_free = pe_ref[N_EXPERTS - 1] // MOE_BLK
        n_blocks = xs_out.shape[0] // blk_rows

        def clear(b, carry):
            start = pl.multiple_of(b * blk_rows, blk_rows)
            pltpu.make_async_copy(zbuf, xs_out.at[pl.ds(start, blk_rows), :], sem_z).start()
            return carry

        def clear_wait(b, carry):
            pltpu.make_async_copy(zbuf, xs_out.at[pl.ds(0, blk_rows), :], sem_z).wait()
            return carry

        lax.fori_loop(first_free, n_blocks, clear, 0)
        lax.fori_loop(first_free, n_blocks, clear_wait, 0)

    cp.wait()

    def issue(j, jo, u):
        d0, d1 = _tile_indices(idx_smem, jo, u)
        src = x_ref.at[pl.ds(pl.multiple_of(j * ROW_SUB, ROW_SUB), ROW_SUB), :]
        pltpu.make_async_copy(src, xs_out.at[pl.ds(d0, ROW_SUB), :], sem_o).start(priority=0)
        pltpu.make_async_copy(src, xs_out.at[pl.ds(d1, ROW_SUB), :], sem_o).start(priority=1)

    _row_loop(DT, issue)
    for _ in range(2):
        pltpu.make_async_copy(x_ref, xs_out.at[pl.ds(0, DT * ROW_SUB), :], sem_o).wait()


def _dispatch_call(pad_ends, idx, x_rows, n_buf_rows):
    n_tiles = x_rows.shape[0] // (DT * ROW_SUB)
    grid_spec = pltpu.PrefetchScalarGridSpec(
        num_scalar_prefetch=1,
        grid=(n_tiles,),
        in_specs=[
            pl.BlockSpec(memory_space=pl.ANY),
            pl.BlockSpec((DT * ROW_SUB, LANE), lambda i, pe: (i, 0)),
        ],
        out_specs=pl.BlockSpec(memory_space=pl.ANY),
        scratch_shapes=[
            pltpu.SMEM((DT // IDX_UNROLL, LANE), I32),
            pltpu.VMEM((MOE_BLK * ROW_SUB, LANE), F32),
            pltpu.SemaphoreType.DMA,
            pltpu.SemaphoreType.DMA,
            pltpu.SemaphoreType.DMA,
        ],
    )
    return pl.pallas_call(
        _dispatch_kernel,
        grid_spec=grid_spec,
        out_shape=jax.ShapeDtypeStruct((n_buf_rows * ROW_SUB, LANE), F32),
        compiler_params=_cparams(("arbitrary",)),
        name="moe_dispatch",
    )(pad_ends, idx, x_rows)


def _ffn_kernel(be_ref, nu_ref, ws_ref, nx_ref, x_ref, w1_hbm, w3_hbm, w2_hbm, y_ref,
                w1f, w3f, w2f, w1b, w3b, w2b, sem_w, *, layer):
    i = pl.program_id(0)
    nu = nu_ref[0]

    def weight_copies(e, s):
        return (pltpu.make_async_copy(w1_hbm.at[layer, e], w1f.at[s], sem_w.at[s]),
                pltpu.make_async_copy(w3_hbm.at[layer, e], w3f.at[s], sem_w.at[s]),
                pltpu.make_async_copy(w2_hbm.at[layer, e], w2f.at[s], sem_w.at[s]))

    @pl.when(i < nu)
    def _():
        e = be_ref[i]
        s = ws_ref[i]
        first_of_run = (i == 0) | (e != be_ref[jnp.maximum(i - 1, 0)])

        @pl.when(i == 0)
        def _():
            for cp in weight_copies(e, s):
                cp.start()

        @pl.when(first_of_run)
        def _():
            for cp in weight_copies(e, s):
                cp.wait()
            nxt = nx_ref[i]

            @pl.when(nxt >= 0)
            def _():
                for cp in weight_copies(nxt, 1 - s):
                    cp.start(priority=1)

            w1b[...] = w1f[s].astype(BF16)
            w3b[...] = w3f[s].astype(BF16)
            w2b[...] = w2f[s].astype(BF16)

        x = _load_rows(x_ref, MOE_BLK).astype(BF16)
        h1 = jnp.dot(x, w1b[...], preferred_element_type=F32)
        h3 = jnp.dot(x, w3b[...], preferred_element_type=F32)
        hh = (h1 * jax.nn.sigmoid(h1) * h3).astype(BF16)
        y = jnp.dot(hh, w2b[...], preferred_element_type=F32)
        _store_rows(y_ref, y)

    @pl.when(i >= nu)
    def _():
        y_ref[...] = jnp.zeros(y_ref.shape, y_ref.dtype)


def _ffn_call(block_expert, n_used, w_slot, next_expert, xs, w1, w3, w2, l):
    blk_rows = MOE_BLK * ROW_SUB
    NB = xs.shape[0] // blk_rows
    _, _, D, DE = w1.shape
    grid_spec = pltpu.PrefetchScalarGridSpec(
        num_scalar_prefetch=4,
        grid=(NB,),
        in_specs=[
            pl.BlockSpec((blk_rows, LANE), lambda i, be, nu, ws, nx: (jnp.minimum(i, nu[0] - 1), 0)),
            pl.BlockSpec(memory_space=pl.ANY),
            pl.BlockSpec(memory_space=pl.ANY),
            pl.BlockSpec(memory_space=pl.ANY),
        ],
        out_specs=pl.BlockSpec((blk_rows, LANE), lambda i, be, nu, ws, nx: (i, 0)),
        scratch_shapes=[
            pltpu.VMEM((2, D, DE), F32),
            pltpu.VMEM((2, D, DE), F32),
            pltpu.VMEM((2, DE, D), F32),
            pltpu.VMEM((D, DE), BF16),
            pltpu.VMEM((D, DE), BF16),
            pltpu.VMEM((DE, D), BF16),
            pltpu.SemaphoreType.DMA((2,)),
        ],
    )
    return pl.pallas_call(
        functools.partial(_ffn_kernel, layer=l),
        grid_spec=grid_spec,
        out_shape=jax.ShapeDtypeStruct(xs.shape, F32),
        compiler_params=_cparams(("arbitrary",)),
        name="moe_ffn",
    )(block_expert, n_used, w_slot, next_expert, xs, w1, w3, w2)


def _combine_kernel(idx_hbm, h1_ref, route_ref, mod_ref, gf_ref, y_hbm, o_ref, idx_smem, rows0, rows1, sem_i, sem_g,
                    *, final):
    D = h1_ref.shape[1]
    i = pl.program_id(0)
    n = pl.num_programs(0)
    slot = i & 1
    nxt = 1 - slot

    def idx_copy(tile, s):
        return pltpu.make_async_copy(idx_hbm.at[tile], idx_smem.at[s], sem_i.at[s])

    def gather(s):
        def issue(j, jo, u):
            d0, d1 = _tile_indices(idx_smem.at[s], jo, u)
            dst = pl.ds(pl.multiple_of(j * ROW_SUB, ROW_SUB), ROW_SUB)
            pltpu.make_async_copy(y_hbm.at[pl.ds(d0, ROW_SUB), :], rows0.at[s, dst, :], sem_g.at[s]).start(priority=0)
            pltpu.make_async_copy(y_hbm.at[pl.ds(d1, ROW_SUB), :], rows1.at[s, dst, :], sem_g.at[s]).start(priority=1)

        _row_loop(CT, issue)

    @pl.when(i == 0)
    def _():
        cp = idx_copy(0, 0)
        cp.start()
        cp.wait()
        gather(0)

        @pl.when(n > 1)
        def _():
            idx_copy(1, 1).start()

    @pl.when(i + 1 < n)
    def _():
        idx_copy(0, nxt).wait()
        gather(nxt)

    @pl.when(i + 2 < n)
    def _():
        idx_copy(i + 2, slot).start()

    for rows in (rows0, rows1):
        pltpu.make_async_copy(y_hbm.at[pl.ds(0, CT * ROW_SUB), :], rows.at[slot], sem_g.at[slot]).wait()

    route = route_ref[...]
    gate1 = route[:, 2:3]
    gate2 = route[:, 3:4]
    y = gate1 * _load_rows(rows0.at[slot], CT) + gate2 * _load_rows(rows1.at[slot], CT)
    g2 = mod_ref[0][:, 5 * D:6 * D]
    out = h1_ref[...] + g2 * y
    if final:
        msq = jnp.mean(out * out, axis=-1, keepdims=True)
        out = out * lax.rsqrt(msq + EPS) * gf_ref[...]
    o_ref[...] = out


def _combine_call(idx, h1, route, mod3, g_final, yb, n_lat_dt, dt_per_sample, n_mod_ctx, final):
    N, D = h1.shape
    n_tiles = N // CT
    idx = idx.reshape(n_tiles, CT // IDX_UNROLL, LANE)

    def mod_map(i):
        return (jnp.where(i < n_lat_dt, i // dt_per_sample, n_mod_ctx), 0, 0)

    return pl.pallas_call(
        functools.partial(_combine_kernel, final=final),
        grid=(n_tiles,),
        in_specs=[
            pl.BlockSpec(memory_space=pl.ANY),
            pl.BlockSpec((CT, D), lambda i: (i, 0)),
            pl.BlockSpec((CT, LANE), lambda i: (i, 0)),
            pl.BlockSpec((1, 1, 6 * D), mod_map),
            pl.BlockSpec((1, D), lambda i: (0, 0)),
            pl.BlockSpec(memory_space=pl.ANY),
        ],
        out_specs=pl.BlockSpec((CT, D), lambda i: (i, 0)),
        out_shape=jax.ShapeDtypeStruct((N, D), F32),
        scratch_shapes=[
            pltpu.SMEM((2, CT // IDX_UNROLL, LANE), I32),
            pltpu.VMEM((2, CT * ROW_SUB, LANE), F32),
            pltpu.VMEM((2, CT * ROW_SUB, LANE), F32),
            pltpu.SemaphoreType.DMA((2,)),
            pltpu.SemaphoreType.DMA((2,)),
        ],
        compiler_params=_cparams(("arbitrary",)),
        name="moe_combine",
    )(idx, h1, route, mod3, g_final, yb)


def _layer_weights(l, w_in, g_norm1, ln_v_g, ln_v_b, w_sp, b_sp, w_gate_up, b_gate, g_gla, w_out, g_norm2,
                   w_router_g, b_router_g, w_router_e, b_router_e):
    D = w_in.shape[1]
    d_in = w_in.shape[2]
    pad = (-d_in) % LANE
    w_in_p = jnp.pad(w_in[l], ((0, 0), (0, pad))).astype(BF16)
    KW = B_KEY_WIDTH
    wg = jnp.zeros((LANE, 2 * KW), F32)
    wg = wg.at[0:GATE_RANK, 0:KW].set(w_gate_up[l, 0])
    wg = wg.at[GATE_RANK:2 * GATE_RANK, KW:2 * KW].set(w_gate_up[l, 1])
    bg = jnp.concatenate([b_gate[l, 0], b_gate[l, 1]])[None, :]
    bsp = jnp.repeat(b_sp[l].T, A_GROUP_DIM, axis=1)
    wr = jnp.zeros((LANE, D), F32)
    wr = wr.at[0:N_EXPERTS, :].set(w_router_e[l].T)
    wr = wr.at[N_EXPERTS:N_EXPERTS + N_GROUPS, :].set(w_router_g[l].T)
    br = jnp.zeros((LANE, 1), F32)
    br = br.at[0:N_EXPERTS, 0].set(b_router_e[l])
    br = br.at[N_EXPERTS:N_EXPERTS + N_GROUPS, 0].set(b_router_g[l])
    r = jnp.arange(TT, dtype=I32)
    same = (r[:, None] // GLA_CHUNK) == (r[None, :] // GLA_CHUNK)
    tri_f = (same & (r[:, None] >= r[None, :])).astype(BF16)
    tri_b = (same & (r[:, None] <= r[None, :])).astype(BF16)
    u_strict = (r[:, None] < r[None, :]).astype(BF16)
    eye_k = jnp.eye(B_KEY_WIDTH, dtype=BF16)
    return dict(
        w_in=w_in_p, g_norm1=g_norm1[l][None, :], ln_g=ln_v_g[l][None, :], ln_b=ln_v_b[l][None, :],
        w_sp=w_sp[l].astype(BF16), b_sp=bsp, w_gate=wg.astype(BF16), b_gate=bg,
        g_gla=g_gla[l][None, :], w_out=w_out[l].astype(BF16), g_norm2=g_norm2[l][None, :],
        w_router=wr.astype(BF16), b_router=br, tri_f=tri_f, tri_b=tri_b, u_strict=u_strict, eye_k=eye_k)


def _segment_layout(counts_col, n_tokens):
    counts = counts_col[:N_EXPERTS, 0].astype(I32)
    padded = (counts + MOE_BLK - 1) // MOE_BLK * MOE_BLK
    pad_ends = jnp.cumsum(padded)
    pad_starts = pad_ends - padded
    n_blocks = (n_tokens * 2 + N_EXPERTS * (MOE_BLK - 1)) // MOE_BLK + 1
    block_start = jnp.arange(n_blocks, dtype=I32) * MOE_BLK
    block_expert = jnp.sum((pad_ends[None, :] <= block_start[:, None]).astype(I32), axis=1)
    block_expert = jnp.minimum(block_expert, N_EXPERTS - 1)
    n_used = (pad_ends[-1] // MOE_BLK).astype(I32)[None]
    blk_id = jnp.arange(n_blocks, dtype=I32)
    prev = jnp.concatenate([block_expert[:1], block_expert[:-1]])
    first = ((blk_id == 0) | (block_expert != prev)) & (blk_id < n_used[0])
    w_slot = (jnp.cumsum(first.astype(I32)) - 1) & 1
    e_id = jnp.arange(N_EXPERTS, dtype=I32)
    later_used = (e_id[None, :] > e_id[:, None]) & (padded[None, :] > 0)
    next_used = jnp.min(jnp.where(later_used, e_id[None, :], N_EXPERTS), axis=1)
    next_used = jnp.where(next_used >= N_EXPERTS, -1, next_used)
    next_expert = jnp.sum(jnp.where(block_expert[:, None] == e_id[None, :], next_used[None, :], 0), axis=1).astype(I32)
    return pad_starts, pad_ends, block_expert, n_used, w_slot.astype(I32), next_expert, n_blocks * MOE_BLK


def kernel(x, c, ctx, c_ctx, w_mod, b_mod, g_norm1, w_in, ln_v_g, ln_v_b, w_sp, b_sp, w_gate_up, b_gate, g_gla,
           w_out, g_norm2, w_router_g, b_router_g, w_router_e, b_router_e, w1, w3, w2, g_final):
    Bn, S, D = x.shape
    Lc = ctx.shape[1]
    depth = w_mod.shape[0]
    assert S % DT == 0 and S % ST == 0 and S % CT == 0 and Lc == TT
    assert (Bn * Lc) % DT == 0 and (Bn * Lc) % ST == 0 and (Bn * Lc) % CT == 0
    n_lat = Bn * S
    n_ctx = Bn * Lc
    n_lat_tiles = n_lat // TT
    tiles_per_sample = S // TT
    n_lat_steps = n_lat // ST
    steps_per_sample = S // ST

    cc = jnp.concatenate([c, c_ctx[None, :]], axis=0)
    mod_all = _modulation(cc, w_mod, b_mod)
    gf = g_final[None, :]

    lat, cx, ctx_off = x.reshape(n_lat, D), ctx.reshape(n_ctx, D), 0
    for l in range(depth):
        last = l == depth - 1
        lw = _layer_weights(l, w_in, g_norm1, ln_v_g, ln_v_b, w_sp, b_sp, w_gate_up, b_gate, g_gla, w_out,
                            g_norm2, w_router_g, b_router_g, w_router_e, b_router_e)
        mod3 = mod_all[l].reshape(Bn + 1, 1, 6 * D)
        a_out, qe, oi, sr, kv, dec = _pre_call(lat, cx, ctx_off, n_ctx // ST, mod3, lw, n_lat_steps,
                                               steps_per_sample, Bn)
        states = _scan_call(kv, dec, Bn, n_lat_tiles, tiles_per_sample)
        n_tok = n_lat if last else n_lat + n_ctx
        h1, hn2_rows, route, route_t, counts = _post_call(lat, cx, ctx_off, mod3, (a_out, qe, oi, sr), states, lw,
                                                          n_tok // ST, n_lat_steps, steps_per_sample, Bn)
        pad_starts, pad_ends, block_expert, n_used, w_slot, next_expert, n_buf_rows = _segment_layout(counts, n_tok)
        idx = _index_tiles(_plan_call(route_t, pad_starts))
        xs = _dispatch_call(pad_ends, idx, hn2_rows, n_buf_rows)
        yb = _ffn_call(block_expert, n_used, w_slot, next_expert, xs, w1, w3, w2, l)
        h_all = _combine_call(idx, h1, route, mod3, gf, yb, n_lat // CT, S // CT, Bn, last)
        lat, cx, ctx_off = h_all, h_all, n_lat_steps
    return h_all.reshape(Bn, S, D)
```

```python
import functools

import jax
import jax.numpy as jnp
from jax import lax
from jax.experimental import pallas as pl
from jax.experimental.pallas import tpu as pltpu

F32 = jnp.float32
BF16 = jnp.bfloat16
I32 = jnp.int32

EPS = 1e-6
LANE = 128
SUBLANE = 8

A_GROUPS = 4
A_GROUP_DIM = 128
A_WIDTH = A_GROUPS * A_GROUP_DIM
CHUNK_A = 128
B_HEADS = 4
B_DK = 64
B_DV = 128
B_KEY_WIDTH = B_HEADS * B_DK
B_WIDTH = B_HEADS * B_DV
GATE_RANK = 16
GATE_TAU = 16.0
CHUNK_B = 64
N_GROUPS = 8
EXPERTS_PER_GROUP = 8
N_EXPERTS = N_GROUPS * EXPERTS_PER_GROUP

TT = 256
GLA_CHUNK = 2 * CHUNK_B
CPT = TT // GLA_CHUNK
SUB = 4
ST = SUB * TT
MOE_BLK = 256
DT = 2048
CT = 512
ROW_SUB = 8

VMEM_LIMIT = 48 * 1024 * 1024
VMEM_LIMIT_TOKEN_LOCAL = 56 * 1024 * 1024


def _cparams(sem, vmem_limit=VMEM_LIMIT):
    return pltpu.CompilerParams(dimension_semantics=sem, vmem_limit_bytes=vmem_limit)


def _load_rows(ref, n):
    return jnp.concatenate([ref[pl.ds(s, n, stride=ROW_SUB), :] for s in range(ROW_SUB)], axis=1)


def _store_rows(ref, val, first=0):
    n = val.shape[0]
    for s in range(ROW_SUB):
        ref[pl.ds(first * ROW_SUB + s, n, stride=ROW_SUB), :] = val[:, s * LANE:(s + 1) * LANE]


def _mod_kernel(c_ref, w_ref, b_ref, o_ref):
    c = c_ref[...]
    s = (c * jax.nn.sigmoid(c)).astype(BF16)
    o_ref[0] = jnp.dot(s, w_ref[0].astype(BF16), preferred_element_type=F32) + b_ref[0]


def _modulation(cc, w_mod, b_mod):
    L, D, D6 = w_mod.shape
    R = cc.shape[0]
    tn = 1536
    return pl.pallas_call(
        _mod_kernel,
        grid=(L, D6 // tn),
        in_specs=[
            pl.BlockSpec((R, D), lambda l, j: (0, 0)),
            pl.BlockSpec((1, D, tn), lambda l, j: (l, 0, j)),
            pl.BlockSpec((1, 1, tn), lambda l, j: (l, 0, j)),
        ],
        out_specs=pl.BlockSpec((1, R, tn), lambda l, j: (l, 0, j)),
        out_shape=jax.ShapeDtypeStruct((L, R, D6), F32),
        compiler_params=_cparams(("arbitrary", "arbitrary")),
        name="modulation",
    )(cc, w_mod, b_mod.reshape(L, 1, D6))


def _select_tile(lat_ref, ctx_ref, n_lat_steps):
    return jnp.where(pl.program_id(0) < n_lat_steps, lat_ref[...], ctx_ref[...])


def _stream_specs(D, n_lat_steps, ctx_off):
    lat = pl.BlockSpec((ST, D), lambda i: (jnp.minimum(i, n_lat_steps - 1), 0))
    ctx = pl.BlockSpec((ST, D), lambda i: (ctx_off + jnp.maximum(i - n_lat_steps, 0), 0))
    return lat, ctx


def _pre_kernel(lat_ref, ctx_ref, mod_ref, g1_ref, win_ref, lng_ref, lnb_ref, wsp_ref, bsp_ref, wg_ref, bg_ref,
                trif_ref, trib_ref, eye_ref,
                a_ref, qe_ref, oi_ref, sr_ref, kv_ref, dec_ref, *, n_lat_steps):
    D = lat_ref.shape[1]
    h = _select_tile(lat_ref, ctx_ref, n_lat_steps)
    mod = mod_ref[0]
    sh1 = mod[:, 0:D]
    sc1 = mod[:, D:2 * D]
    ms = jnp.mean(h * h, axis=-1, keepdims=True)
    hn = h * lax.rsqrt(ms + EPS) * g1_ref[...]
    hn = hn * (1.0 + sc1) + sh1
    z = jnp.dot(hn.astype(BF16), win_ref[...], preferred_element_type=F32)
    o_av = A_WIDTH
    o_q = 2 * A_WIDTH
    o_r = o_q + B_KEY_WIDTH
    o_k = o_r + B_WIDTH
    o_v = o_k + B_KEY_WIDTH
    o_g = o_v + B_WIDTH
    zu = z[:, 0:o_av]
    zv = z[:, o_av:o_q]
    q_all = z[:, o_q:o_r] * (B_DK ** -0.5)
    zr = z[:, o_r:o_k]
    k_all = z[:, o_k:o_v]
    vv_all = z[:, o_v:o_g]
    zg = z[:, o_g:o_g + LANE]

    u = jax.nn.gelu(zu)
    v = jax.nn.gelu(zv)
    s_groups = []
    for g in range(A_GROUPS):
        sl = slice(g * A_GROUP_DIM, (g + 1) * A_GROUP_DIM)
        vg = v[:, sl]
        mu = jnp.mean(vg, axis=-1, keepdims=True)
        dv = vg - mu
        var = jnp.mean(dv * dv, axis=-1, keepdims=True)
        vn = (dv * lax.rsqrt(var + EPS)) * lng_ref[:, sl] + lnb_ref[:, sl]
        vnb = vn.astype(BF16)
        n_ch = ST // CHUNK_A
        rhs = jnp.concatenate([vnb[c * CHUNK_A:(c + 1) * CHUNK_A, :] for c in range(n_ch)], axis=1)
        mixed = jnp.dot(wsp_ref[g], rhs, preferred_element_type=F32)
        bias = bsp_ref[:, sl]
        s_groups.append(jnp.concatenate(
            [mixed[:, c * A_GROUP_DIM:(c + 1) * A_GROUP_DIM] + bias for c in range(n_ch)], axis=0))
    a_ref[...] = (u * jnp.concatenate(s_groups, axis=1)).astype(BF16)

    sr_ref[...] = (zr * jax.nn.sigmoid(zr)).astype(BF16)
    lg = jnp.dot(zg.astype(BF16), wg_ref[...], preferred_element_type=F32) + bg_ref[...]
    la_all = (jnp.minimum(lg, 0.0) - jnp.log1p(jnp.exp(-jnp.abs(lg)))) * (1.0 / GATE_TAU)
    for t in range(SUB):
        rs = slice(t * TT, (t + 1) * TT)
        _gla_local(t, rs, la_all[rs, :], q_all[rs, :], k_all[rs, :], vv_all[rs, :],
                   trif_ref, trib_ref, eye_ref, qe_ref, oi_ref, kv_ref, dec_ref)


def _gla_local(t, rs, la, q, k, vv, trif_ref, trib_ref, eye_ref, qe_ref, oi_ref, kv_ref, dec_ref):
    la_hi = la.astype(BF16)
    la_lo = (la - la_hi.astype(F32)).astype(BF16)
    KW = B_KEY_WIDTH
    bf = (jnp.dot(trif_ref[...], la_hi[:, :KW], preferred_element_type=F32)
          + jnp.dot(trif_ref[...], la_lo[:, :KW], preferred_element_type=F32))
    bb = (jnp.dot(trib_ref[...], la_hi[:, KW:], preferred_element_type=F32)
          + jnp.dot(trib_ref[...], la_lo[:, KW:], preferred_element_type=F32))

    def chunk_rows(x, r):
        return jnp.concatenate(
            [jnp.broadcast_to(x[c * GLA_CHUNK + r:c * GLA_CHUNK + r + 1, :], (GLA_CHUNK, KW)) for c in range(CPT)], axis=0)

    blf = chunk_rows(bf, GLA_CHUNK - 1)
    blb = chunk_rows(bb, 0)
    bmf = chunk_rows(bf, CHUNK_B - 1)
    bmb = chunk_rows(bb, CHUNK_B)
    qe_ref[rs, :] = jnp.concatenate([q * jnp.exp(bf), q * jnp.exp(bb)], axis=1).astype(BF16)
    kd_f = k * jnp.exp(blf - bf)
    kd_b = k * jnp.exp(blb - bb)
    qe_f = q * jnp.exp(bf - bmf)
    qe_b = q * jnp.exp(bb - bmb)
    ke_f = (k * jnp.exp(bmf - bf)).astype(BF16)
    ke_b = (k * jnp.exp(bmb - bb)).astype(BF16)
    nt = (((1,), (1,)), ((), ()))
    dec_rows = ([jnp.exp(bf[c * GLA_CHUNK + GLA_CHUNK - 1:(c + 1) * GLA_CHUNK, :]) for c in range(CPT)]
                + [jnp.exp(bb[c * GLA_CHUNK:c * GLA_CHUNK + 1, :]) for c in range(CPT)]
                + [jnp.zeros((LANE - 2 * CPT, KW), F32)])
    dec_pad = jnp.concatenate(dec_rows, axis=0)
    dec_hi = dec_pad.astype(BF16)
    dec_lo = (dec_pad - dec_hi.astype(F32)).astype(BF16)
    dec_ref[t] = (lax.dot_general(eye_ref[...], dec_hi, nt, preferred_element_type=F32)
                  + lax.dot_general(eye_ref[...], dec_lo, nt, preferred_element_type=F32))

    row = lax.broadcasted_iota(I32, (TT, TT), 0)
    col = lax.broadcasted_iota(I32, (TT, TT), 1)
    chunk_shift = GLA_CHUNK.bit_length() - 1
    same = lax.shift_right_logical(row, chunk_shift) == lax.shift_right_logical(col, chunk_shift)
    m_f = same & (row >= col)
    m_b = same & (row <= col)
    lane_head = lax.shift_right_logical(lax.broadcasted_iota(I32, (1, KW), 1), B_DK.bit_length() - 1)
    tok_chunk = lax.shift_right_logical(lax.broadcasted_iota(I32, (1, TT), 1), chunk_shift)
    vb = vv.astype(BF16)
    kdT_f = kd_f.T
    kdT_b = kd_b.T
    oi_heads = []
    for hd in range(B_HEADS):
        hm = lane_head == hd
        qf = jnp.where(hm, qe_f, 0.0).astype(BF16)
        qb = jnp.where(hm, qe_b, 0.0).astype(BF16)
        att_f = lax.dot_general(qf, ke_f, nt, preferred_element_type=F32)
        att_b = lax.dot_general(qb, ke_b, nt, preferred_element_type=F32)
        att = jnp.where(m_f, att_f, 0.0) + jnp.where(m_b, att_b, 0.0)
        v_h = vb[:, hd * B_DV:(hd + 1) * B_DV]
        oi_heads.append(jnp.dot(att.astype(BF16), v_h, preferred_element_type=F32))
        parts = []
        for kdT in (kdT_f, kdT_b):
            kh = kdT[hd * B_DK:(hd + 1) * B_DK, :]
            for c in range(CPT):
                parts.append(jnp.where(tok_chunk == c, kh, 0.0))
        lhs = jnp.concatenate(parts, axis=0).astype(BF16)
        kv_ref[t, hd] = jnp.dot(lhs, v_h, preferred_element_type=F32)
    oi_ref[rs, :] = jnp.concatenate(oi_heads, axis=1)


def _pre_call(lat, ctx, ctx_off, n_ctx_steps, mod3, lw, n_lat_steps, steps_per_sample, n_mod_ctx):
    D = lat.shape[1]
    n_steps = n_lat_steps + n_ctx_steps
    n_tiles = n_steps * SUB
    N = n_steps * ST
    lat_spec, ctx_spec = _stream_specs(D, n_lat_steps, ctx_off)
    DINP = lw["w_in"].shape[1]
    KVR = 2 * CPT * B_DK

    def mod_map(i):
        return (jnp.where(i < n_lat_steps, i // steps_per_sample, n_mod_ctx), 0, 0)

    const2 = lambda i: (0, 0)
    const3 = lambda i: (0, 0, 0)
    tile2 = lambda i: (i, 0)
    return pl.pallas_call(
        functools.partial(_pre_kernel, n_lat_steps=n_lat_steps),
        grid=(n_steps,),
        in_specs=[
            lat_spec,
            ctx_spec,
            pl.BlockSpec((1, 1, 6 * D), mod_map),
            pl.BlockSpec((1, D), const2),
            pl.BlockSpec((D, DINP), const2),
            pl.BlockSpec((1, A_WIDTH), const2),
            pl.BlockSpec((1, A_WIDTH), const2),
            pl.BlockSpec((A_GROUPS, CHUNK_A, CHUNK_A), const3),
            pl.BlockSpec((CHUNK_A, A_WIDTH), const2),
            pl.BlockSpec((LANE, 2 * B_KEY_WIDTH), const2),
            pl.BlockSpec((1, 2 * B_KEY_WIDTH), const2),
            pl.BlockSpec((TT, TT), const2),
            pl.BlockSpec((TT, TT), const2),
            pl.BlockSpec((B_KEY_WIDTH, B_KEY_WIDTH), const2),
        ],
        out_specs=[
            pl.BlockSpec((ST, A_WIDTH), tile2),
            pl.BlockSpec((ST, 2 * B_KEY_WIDTH), tile2),
            pl.BlockSpec((ST, B_WIDTH), tile2),
            pl.BlockSpec((ST, B_WIDTH), tile2),
            pl.BlockSpec((SUB, B_HEADS, KVR, B_DV), lambda i: (i, 0, 0, 0)),
            pl.BlockSpec((SUB, B_KEY_WIDTH, LANE), lambda i: (i, 0, 0)),
        ],
        out_shape=[
            jax.ShapeDtypeStruct((N, A_WIDTH), BF16),
            jax.ShapeDtypeStruct((N, 2 * B_KEY_WIDTH), BF16),
            jax.ShapeDtypeStruct((N, B_WIDTH), F32),
            jax.ShapeDtypeStruct((N, B_WIDTH), BF16),
            jax.ShapeDtypeStruct((n_tiles, B_HEADS, KVR, B_DV), F32),
            jax.ShapeDtypeStruct((n_tiles, B_KEY_WIDTH, LANE), F32),
        ],
        compiler_params=_cparams(("arbitrary",), VMEM_LIMIT_TOKEN_LOCAL),
        name="mix_pre",
    )(lat, ctx, mod3, lw["g_norm1"], lw["w_in"], lw["ln_g"], lw["ln_b"], lw["w_sp"], lw["b_sp"],
      lw["w_gate"], lw["b_gate"], lw["tri_f"], lw["tri_b"], lw["eye_k"])


def _scan_kernel(kvc_ref, kvl_ref, dcc_ref, dcl_ref, sc_ref, sl_ref):
    n_lat = kvl_ref.shape[0]
    fwd = [(kvc_ref, dcc_ref, sc_ref, 0, c) for c in range(CPT)]
    fwd += [(kvl_ref, dcl_ref, sl_ref, t, c) for t in range(n_lat) for c in range(CPT)]
    bwd = [(kvc_ref, dcc_ref, sc_ref, 0, c) for c in reversed(range(CPT))]
    bwd += [(kvl_ref, dcl_ref, sl_ref, t, c) for t in reversed(range(n_lat)) for c in reversed(range(CPT))]
    for hd in range(B_HEADS):
        for d, seq in enumerate((fwd, bwd)):
            s = jnp.zeros((B_DK, B_DV), F32)
            for kv_ref, dc_ref, out_ref, t, c in seq:
                r0 = (d * CPT + c) * B_DK
                out_ref[t, hd, r0:r0 + B_DK, :] = s.astype(BF16)
                dcol = dc_ref[t, hd * B_DK:(hd + 1) * B_DK, d * CPT + c:d * CPT + c + 1]
                s = dcol * s + kv_ref[t, hd, r0:r0 + B_DK, :]


def _scan_call(kv, dec, n_samples, n_lat_tiles, tiles_per_sample):
    n_tiles, _, KVR, _ = kv.shape
    kv_c = pl.BlockSpec((1, B_HEADS, KVR, B_DV), lambda b: (n_lat_tiles + b, 0, 0, 0))
    kv_l = pl.BlockSpec((tiles_per_sample, B_HEADS, KVR, B_DV), lambda b: (b, 0, 0, 0))
    dc_c = pl.BlockSpec((1, B_KEY_WIDTH, LANE), lambda b: (n_lat_tiles + b, 0, 0))
    dc_l = pl.BlockSpec((tiles_per_sample, B_KEY_WIDTH, LANE), lambda b: (b, 0, 0))
    s_ctx, s_lat = pl.pallas_call(
        _scan_kernel,
        grid=(n_samples,),
        in_specs=[kv_c, kv_l, dc_c, dc_l],
        out_specs=[
            pl.BlockSpec((1, B_HEADS, KVR, B_DV), lambda b: (b, 0, 0, 0)),
            pl.BlockSpec((tiles_per_sample, B_HEADS, KVR, B_DV), lambda b: (b, 0, 0, 0)),
        ],
        out_shape=[
            jax.ShapeDtypeStruct((n_samples, B_HEADS, KVR, B_DV), BF16),
            jax.ShapeDtypeStruct((n_lat_tiles, B_HEADS, KVR, B_DV), BF16),
        ],
        compiler_params=_cparams(("arbitrary",)),
        name="gla_scan",
    )(kv, kv, dec, dec)
    return jnp.concatenate([s_lat, s_ctx], axis=0)


def _post_kernel(lat_ref, ctx_ref, mod_ref, a_ref, qe_ref, oi_ref, sr_ref, st_ref, ggla_ref, wout_ref, g2_ref,
                 wr_ref, br_ref, ustr_ref,
                 h1_ref, hn2_ref, route_ref, routet_ref, cnt_ref,
                 rhs_scr, cnt_scr, *, n_lat_steps):
    D = lat_ref.shape[1]
    i = pl.program_id(0)

    @pl.when(i == 0)
    def _():
        rhs_scr[...] = jnp.zeros(rhs_scr.shape, rhs_scr.dtype)
        cnt_scr[...] = jnp.zeros(cnt_scr.shape, cnt_scr.dtype)

    qe = qe_ref[...]
    o_parts = []
    for c in range(SUB * CPT):
        t, ct = divmod(c, CPT)
        for d in range(2):
            for hd in range(B_HEADS):
                r0 = d * B_KEY_WIDTH + hd * B_DK
                s0 = (d * CPT + ct) * B_DK
                rhs_scr[c, r0:r0 + B_DK, hd * B_DV:(hd + 1) * B_DV] = st_ref[t, hd, s0:s0 + B_DK, :]
        o_parts.append(jnp.dot(qe[c * GLA_CHUNK:(c + 1) * GLA_CHUNK, :], rhs_scr[c], preferred_element_type=F32))
    o = oi_ref[...] + jnp.concatenate(o_parts, axis=0)

    heads = []
    for hd in range(B_HEADS):
        sl = slice(hd * B_DV, (hd + 1) * B_DV)
        oh = o[:, sl]
        msq = jnp.mean(oh * oh, axis=-1, keepdims=True)
        heads.append(oh * lax.rsqrt(msq + EPS) * ggla_ref[:, sl])
    b_out = jnp.concatenate(heads, axis=1) * sr_ref[...].astype(F32)
    mix_in = jnp.concatenate([a_ref[...], b_out.astype(BF16)], axis=1)
    mix = jnp.dot(mix_in, wout_ref[...], preferred_element_type=F32)

    mod = mod_ref[0]
    g1 = mod[:, 2 * D:3 * D]
    sh2 = mod[:, 3 * D:4 * D]
    sc2 = mod[:, 4 * D:5 * D]
    h_in = _select_tile(lat_ref, ctx_ref, n_lat_steps)
    nt = (((1,), (1,)), ((), ()))
    neg = jnp.float32(-3.0e38)
    big = jnp.float32(1.0e9)
    sub = lax.broadcasted_iota(I32, (EXPERTS_PER_GROUP, LANE), 0).astype(F32)
    route_parts = []
    for hf in range(ST // LANE):
        rs = slice(hf * LANE, (hf + 1) * LANE)
        h1 = h_in[rs, :] + g1 * mix[rs, :]
        h1_ref[rs, :] = h1
        msq = jnp.mean(h1 * h1, axis=-1, keepdims=True)
        hn2 = h1 * lax.rsqrt(msq + EPS) * g2_ref[...]
        hn2 = hn2 * (1.0 + sc2) + sh2
        _store_rows(hn2_ref, hn2, hf * LANE)

        lgt = lax.dot_general(wr_ref[...], hn2.astype(BF16), nt, preferred_element_type=F32) + br_ref[...]
        gl = lgt[N_EXPERTS:N_EXPERTS + N_GROUPS, :]
        gmax = jnp.max(gl, axis=0, keepdims=True)
        p_top = 1.0 / jnp.sum(jnp.exp(gl - gmax), axis=0, keepdims=True)
        gidx = jnp.min(jnp.where(gl == gmax, sub, big), axis=0, keepdims=True)
        el = jnp.zeros((EXPERTS_PER_GROUP, LANE), F32)
        for g in range(N_GROUPS):
            el = el + jnp.where(gidx == float(g), lgt[g * EXPERTS_PER_GROUP:(g + 1) * EXPERTS_PER_GROUP, :], 0.0)
        m1 = jnp.max(el, axis=0, keepdims=True)
        i1 = jnp.min(jnp.where(el == m1, sub, big), axis=0, keepdims=True)
        el2 = jnp.where(sub == i1, neg, el)
        m2 = jnp.max(el2, axis=0, keepdims=True)
        i2 = jnp.min(jnp.where(el2 == m2, sub, big), axis=0, keepdims=True)
        t = jnp.exp(m2 - m1)
        route_parts.append((gidx * float(EXPERTS_PER_GROUP) + i1, gidx * float(EXPERTS_PER_GROUP) + i2,
                            p_top / (1.0 + t), p_top * t / (1.0 + t)))
    e1_all, e2_all, gate1_all, gate2_all = (jnp.concatenate(p, axis=1) for p in zip(*route_parts))

    rowf = lax.broadcasted_iota(I32, (LANE, TT), 0).astype(F32)
    second = lax.broadcasted_iota(I32, (LANE, 1), 0) >= N_EXPERTS
    base = cnt_scr[...]
    for t in range(SUB):
        ls = slice(t * TT, (t + 1) * TT)
        e1, e2 = e1_all[:, ls], e2_all[:, ls]
        sel1 = rowf == e1
        sel2 = rowf == (e2 + float(N_EXPERTS))
        oh = jnp.where(sel1, 1.0, 0.0) + jnp.where(sel2, 1.0, 0.0)
        prefix = jnp.dot(oh.astype(BF16), ustr_ref[...], preferred_element_type=F32)
        tot = jnp.sum(oh, axis=1, keepdims=True)
        tot_sw = jnp.concatenate([tot[N_EXPERTS:, :], tot[:N_EXPERTS, :]], axis=0)
        val = prefix + (base + jnp.where(second, tot_sw, 0.0))
        r1 = jnp.sum(jnp.where(sel1, val, 0.0), axis=0, keepdims=True)
        r2 = jnp.sum(jnp.where(sel2, val, 0.0), axis=0, keepdims=True)
        base = base + tot + tot_sw
        zero = jnp.zeros_like(e1)
        route_t = jnp.concatenate([e1, e2, gate1_all[:, ls], gate2_all[:, ls], r1, r2, zero, zero], axis=0)
        routet_ref[:, ls] = route_t
        route_ref[ls, :] = jnp.concatenate([route_t, jnp.zeros((LANE - SUBLANE, TT), F32)], axis=0).T
    cnt_scr[...] = base
    cnt_ref[...] = base


def _post_call(lat, ctx, ctx_off, mod3, pre_outs, states, lw, n_post_steps, n_lat_steps, steps_per_sample, n_mod_ctx):
    a_out, qe, oi, sr = pre_outs
    D = lat.shape[1]
    KVR = 2 * CPT * B_DK
    NP = n_post_steps * ST
    lat_spec, ctx_spec = _stream_specs(D, n_lat_steps, ctx_off)

    def mod_map(i):
        return (jnp.where(i < n_lat_steps, i // steps_per_sample, n_mod_ctx), 0, 0)

    const2 = lambda i: (0, 0)
    tile2 = lambda i: (i, 0)
    return pl.pallas_call(
        functools.partial(_post_kernel, n_lat_steps=n_lat_steps),
        grid=(n_post_steps,),
        in_specs=[
            lat_spec,
            ctx_spec,
            pl.BlockSpec((1, 1, 6 * D), mod_map),
            pl.BlockSpec((ST, A_WIDTH), tile2),
            pl.BlockSpec((ST, 2 * B_KEY_WIDTH), tile2),
            pl.BlockSpec((ST, B_WIDTH), tile2),
            pl.BlockSpec((ST, B_WIDTH), tile2),
            pl.BlockSpec((SUB, B_HEADS, KVR, B_DV), lambda i: (i, 0, 0, 0)),
            pl.BlockSpec((1, B_WIDTH), const2),
            pl.BlockSpec((D, D), const2),
            pl.BlockSpec((1, D), const2),
            pl.BlockSpec((LANE, D), const2),
            pl.BlockSpec((LANE, 1), const2),
            pl.BlockSpec((TT, TT), const2),
        ],
        out_specs=[
            pl.BlockSpec((ST, D), tile2),
            pl.BlockSpec((ST * ROW_SUB, LANE), tile2),
            pl.BlockSpec((ST, LANE), tile2),
            pl.BlockSpec((SUBLANE, ST), lambda i: (0, i)),
            pl.BlockSpec((LANE, 1), const2),
        ],
        out_shape=[
            jax.ShapeDtypeStruct((NP, D), F32),
            jax.ShapeDtypeStruct((NP * ROW_SUB, LANE), F32),
            jax.ShapeDtypeStruct((NP, LANE), F32),
            jax.ShapeDtypeStruct((SUBLANE, NP), F32),
            jax.ShapeDtypeStruct((LANE, 1), F32),
        ],
        scratch_shapes=[
            pltpu.VMEM((SUB * CPT, 2 * B_KEY_WIDTH, B_WIDTH), BF16),
            pltpu.VMEM((LANE, 1), F32),
        ],
        compiler_params=_cparams(("arbitrary",), VMEM_LIMIT_TOKEN_LOCAL),
        name="mix_post",
    )(lat, ctx, mod3, a_out, qe, oi, sr, states, lw["g_gla"], lw["w_out"], lw["g_norm2"],
      lw["w_router"], lw["b_router"], lw["u_strict"])


def _plan_kernel(rt_ref, ps_ref, idx_ref):
    sub = lax.broadcasted_iota(I32, (N_EXPERTS, LANE), 0).astype(F32)
    ps = ps_ref[...]
    segs = DT // LANE
    for k in range(2):
        for seg in range(segs):
            sl = slice(seg * LANE, (seg + 1) * LANE)
            e = rt_ref[k:k + 1, sl]
            r = rt_ref[4 + k:5 + k, sl]
            base = jnp.sum(jnp.where(sub == e, ps, 0.0), axis=0, keepdims=True)
            idx_ref[0, k * segs + seg:k * segs + seg + 1, :] = ((base + r) * float(ROW_SUB)).astype(I32)


def _plan_call(route_t, pad_starts):
    n_dt = route_t.shape[1] // DT
    return pl.pallas_call(
        _plan_kernel,
        grid=(n_dt,),
        in_specs=[
            pl.BlockSpec((SUBLANE, DT), lambda i: (0, i)),
            pl.BlockSpec((N_EXPERTS, 1), lambda i: (0, 0)),
        ],
        out_specs=pl.BlockSpec((1, 2 * DT // LANE, LANE), lambda i: (i, 0, 0)),
        out_shape=jax.ShapeDtypeStruct((n_dt, 2 * DT // LANE, LANE), I32),
        compiler_params=_cparams(("arbitrary",)),
        name="moe_plan",
    )(route_t, pad_starts.astype(F32)[:, None])


IDX_UNROLL = 8


def _row_loop(n_rows, fn):
    def body(jo, carry):
        for u in range(IDX_UNROLL):
            fn(jo * IDX_UNROLL + u, jo, u)
        return carry

    lax.fori_loop(0, n_rows // IDX_UNROLL, body, 0)


def _tile_indices(idx_smem, jo, u):
    d0 = pl.multiple_of(idx_smem[jo, u], ROW_SUB)
    d1 = pl.multiple_of(idx_smem[jo, IDX_UNROLL + u], ROW_SUB)
    return d0, d1


def _index_tiles(idx):
    n = idx.shape[0]
    d = idx.reshape(n, 2, DT // IDX_UNROLL, IDX_UNROLL).transpose(0, 2, 1, 3).reshape(n, DT // IDX_UNROLL, 2 * IDX_UNROLL)
    return jnp.pad(d, ((0, 0), (0, 0), (0, LANE - 2 * IDX_UNROLL)))


def _dispatch_kernel(pe_ref, idx_hbm, x_ref, xs_out, idx_smem, zbuf, sem_i, sem_o, sem_z):
    i = pl.program_id(0)
    cp = pltpu.make_async_copy(idx_hbm.at[i], idx_smem, sem_i)
    cp.start()
    blk_rows = MOE_BLK * ROW_SUB

    @pl.when(i == 0)
    def _():
        zbuf[...] = jnp.zeros(zbuf.shape, zbuf.dtype)
        for e in range(N_EXPERTS):
            start = pl.multiple_of(jnp.maximum(pe_ref[e] - MOE_BLK, 0) * ROW_SUB, blk_rows)
            pltpu.make_async_copy(zbuf, xs_out.at[pl.ds(start, blk_rows), :], sem_z).start()
        for e in range(N_EXPERTS):
            pltpu.make_async_copy(zbuf, xs_out.at[pl.ds(0, blk_rows), :], sem_z).wait()
        first_free = pe_ref[N_EXPERTS - 1] // MOE_BLK
        n_blocks = xs_out.shape[0] // blk_rows

        def clear(b, carry):
            start = pl.multiple_of(b * blk_rows, blk_rows)
            pltpu.make_async_copy(zbuf, xs_out.at[pl.ds(start, blk_rows), :], sem_z).start()
            return carry

        def clear_wait(b, carry):
            pltpu.make_async_copy(zbuf, xs_out.at[pl.ds(0, blk_rows), :], sem_z).wait()
            return carry

        lax.fori_loop(first_free, n_blocks, clear, 0)
        lax.fori_loop(first_free, n_blocks, clear_wait, 0)

    cp.wait()

    def issue(j, jo, u):
        d0, d1 = _tile_indices(idx_smem, jo, u)
        src = x_ref.at[pl.ds(pl.multiple_of(j * ROW_SUB, ROW_SUB), ROW_SUB), :]
        pltpu.make_async_copy(src, xs_out.at[pl.ds(d0, ROW_SUB), :], sem_o).start(priority=0)
        pltpu.make_async_copy(src, xs_out.at[pl.ds(d1, ROW_SUB), :], sem_o).start(priority=1)

    _row_loop(DT, issue)
    for _ in range(2):
        pltpu.make_async_copy(x_ref, xs_out.at[pl.ds(0, DT * ROW_SUB), :], sem_o).wait()


def _dispatch_call(pad_ends, idx, x_rows, n_buf_rows):
    n_tiles = x_rows.shape[0] // (DT * ROW_SUB)
    grid_spec = pltpu.PrefetchScalarGridSpec(
        num_scalar_prefetch=1,
        grid=(n_tiles,),
        in_specs=[
            pl.BlockSpec(memory_space=pl.ANY),
            pl.BlockSpec((DT * ROW_SUB, LANE), lambda i, pe: (i, 0)),
        ],
        out_specs=pl.BlockSpec(memory_space=pl.ANY),
        scratch_shapes=[
            pltpu.SMEM((DT // IDX_UNROLL, LANE), I32),
            pltpu.VMEM((MOE_BLK * ROW_SUB, LANE), F32),
            pltpu.SemaphoreType.DMA,
            pltpu.SemaphoreType.DMA,
            pltpu.SemaphoreType.DMA,
        ],
    )
    return pl.pallas_call(
        _dispatch_kernel,
        grid_spec=grid_spec,
        out_shape=jax.ShapeDtypeStruct((n_buf_rows * ROW_SUB, LANE), F32),
        compiler_params=_cparams(("arbitrary",)),
        name="moe_dispatch",
    )(pad_ends, idx, x_rows)


def _ffn_kernel(be_ref, nu_ref, ws_ref, nx_ref, x_ref, w1_hbm, w3_hbm, w2_hbm, y_ref,
                w1f, w3f, w2f, w1b, w3b, w2b, sem_w, *, layer):
    i = pl.program_id(0)
    nu = nu_ref[0]

    def weight_copies(e, s):
        return (pltpu.make_async_copy(w1_hbm.at[layer, e], w1f.at[s], sem_w.at[s]),
                pltpu.make_async_copy(w3_hbm.at[layer, e], w3f.at[s], sem_w.at[s]),
                pltpu.make_async_copy(w2_hbm.at[layer, e], w2f.at[s], sem_w.at[s]))

    @pl.when(i < nu)
    def _():
        e = be_ref[i]
        s = ws_ref[i]
        first_of_run = (i == 0) | (e != be_ref[jnp.maximum(i - 1, 0)])

        @pl.when(i == 0)
        def _():
            for cp in weight_copies(e, s):
                cp.start()

        @pl.when(first_of_run)
        def _():
            for cp in weight_copies(e, s):
                cp.wait()
            nxt = nx_ref[i]

            @pl.when(nxt >= 0)
            def _():
                for cp in weight_copies(nxt, 1 - s):
                    cp.start(priority=1)

            w1b[...] = w1f[s].astype(BF16)
            w3b[...] = w3f[s].astype(BF16)
            w2b[...] = w2f[s].astype(BF16)

        x = _load_rows(x_ref, MOE_BLK).astype(BF16)
        h1 = jnp.dot(x, w1b[...], preferred_element_type=F32)
        h3 = jnp.dot(x, w3b[...], preferred_element_type=F32)
        hh = (h1 * jax.nn.sigmoid(h1) * h3).astype(BF16)
        y = jnp.dot(hh, w2b[...], preferred_element_type=F32)
        _store_rows(y_ref, y)

    @pl.when(i >= nu)
    def _():
        y_ref[...] = jnp.zeros(y_ref.shape, y_ref.dtype)


def _ffn_call(block_expert, n_used, w_slot, next_expert, xs, w1, w3, w2, l):
    blk_rows = MOE_BLK * ROW_SUB
    NB = xs.shape[0] // blk_rows
    _, _, D, DE = w1.shape
    grid_spec = pltpu.PrefetchScalarGridSpec(
        num_scalar_prefetch=4,
        grid=(NB,),
        in_specs=[
            pl.BlockSpec((blk_rows, LANE), lambda i, be, nu, ws, nx: (jnp.minimum(i, nu[0] - 1), 0)),
            pl.BlockSpec(memory_space=pl.ANY),
            pl.BlockSpec(memory_space=pl.ANY),
            pl.BlockSpec(memory_space=pl.ANY),
        ],
        out_specs=pl.BlockSpec((blk_rows, LANE), lambda i, be, nu, ws, nx: (i, 0)),
        scratch_shapes=[
            pltpu.VMEM((2, D, DE), F32),
            pltpu.VMEM((2, D, DE), F32),
            pltpu.VMEM((2, DE, D), F32),
            pltpu.VMEM((D, DE), BF16),
            pltpu.VMEM((D, DE), BF16),
            pltpu.VMEM((DE, D), BF16),
            pltpu.SemaphoreType.DMA((2,)),
        ],
    )
    return pl.pallas_call(
        functools.partial(_ffn_kernel, layer=l),
        grid_spec=grid_spec,
        out_shape=jax.ShapeDtypeStruct(xs.shape, F32),
        compiler_params=_cparams(("arbitrary",)),
        name="moe_ffn",
    )(block_expert, n_used, w_slot, next_expert, xs, w1, w3, w2)


def _combine_kernel(idx_hbm, h1_ref, route_ref, mod_ref, gf_ref, y_hbm, o_ref, idx_smem, rows0, rows1, sem_i, sem_g,
                    *, final):
    D = h1_ref.shape[1]
    i = pl.program_id(0)
    n = pl.num_programs(0)
    slot = i & 1
    nxt = 1 - slot

    def idx_copy(tile, s):
        return pltpu.make_async_copy(idx_hbm.at[tile], idx_smem.at[s], sem_i.at[s])

    def gather(s):
        def issue(j, jo, u):
            d0, d1 = _tile_indices(idx_smem.at[s], jo, u)
            dst = pl.ds(pl.multiple_of(j * ROW_SUB, ROW_SUB), ROW_SUB)
            pltpu.make_async_copy(y_hbm.at[pl.ds(d0, ROW_SUB), :], rows0.at[s, dst, :], sem_g.at[s]).start(priority=0)
            pltpu.make_async_copy(y_hbm.at[pl.ds(d1, ROW_SUB), :], rows1.at[s, dst, :], sem_g.at[s]).start(priority=1)

        _row_loop(CT, issue)

    @pl.when(i == 0)
    def _():
        cp = idx_copy(0, 0)
        cp.start()
        cp.wait()
        gather(0)

        @pl.when(n > 1)
        def _():
            idx_copy(1, 1).start()

    @pl.when(i + 1 < n)
    def _():
        idx_copy(0, nxt).wait()
        gather(nxt)

    @pl.when(i + 2 < n)
    def _():
        idx_copy(i + 2, slot).start()

    for rows in (rows0, rows1):
        pltpu.make_async_copy(y_hbm.at[pl.ds(0, CT * ROW_SUB), :], rows.at[slot], sem_g.at[slot]).wait()

    route = route_ref[...]
    gate1 = route[:, 2:3]
    gate2 = route[:, 3:4]
    y = gate1 * _load_rows(rows0.at[slot], CT) + gate2 * _load_rows(rows1.at[slot], CT)
    g2 = mod_ref[0][:, 5 * D:6 * D]
    out = h1_ref[...] + g2 * y
    if final:
        msq = jnp.mean(out * out, axis=-1, keepdims=True)
        out = out * lax.rsqrt(msq + EPS) * gf_ref[...]
    o_ref[...] = out


def _combine_call(idx, h1, route, mod3, g_final, yb, n_lat_dt, dt_per_sample, n_mod_ctx, final):
    N, D = h1.shape
    n_tiles = N // CT
    idx = idx.reshape(n_tiles, CT // IDX_UNROLL, LANE)

    def mod_map(i):
        return (jnp.where(i < n_lat_dt, i // dt_per_sample, n_mod_ctx), 0, 0)

    return pl.pallas_call(
        functools.partial(_combine_kernel, final=final),
        grid=(n_tiles,),
        in_specs=[
            pl.BlockSpec(memory_space=pl.ANY),
            pl.BlockSpec((CT, D), lambda i: (i, 0)),
            pl.BlockSpec((CT, LANE), lambda i: (i, 0)),
            pl.BlockSpec((1, 1, 6 * D), mod_map),
            pl.BlockSpec((1, D), lambda i: (0, 0)),
            pl.BlockSpec(memory_space=pl.ANY),
        ],
        out_specs=pl.BlockSpec((CT, D), lambda i: (i, 0)),
        out_shape=jax.ShapeDtypeStruct((N, D), F32),
        scratch_shapes=[
            pltpu.SMEM((2, CT // IDX_UNROLL, LANE), I32),
            pltpu.VMEM((2, CT * ROW_SUB, LANE), F32),
            pltpu.VMEM((2, CT * ROW_SUB, LANE), F32),
            pltpu.SemaphoreType.DMA((2,)),
            pltpu.SemaphoreType.DMA((2,)),
        ],
        compiler_params=_cparams(("arbitrary",)),
        name="moe_combine",
    )(idx, h1, route, mod3, g_final, yb)


def _layer_weights(l, w_in, g_norm1, ln_v_g, ln_v_b, w_sp, b_sp, w_gate_up, b_gate, g_gla, w_out, g_norm2,
                   w_router_g, b_router_g, w_router_e, b_router_e):
    D = w_in.shape[1]
    d_in = w_in.shape[2]
    pad = (-d_in) % LANE
    w_in_p = jnp.pad(w_in[l], ((0, 0), (0, pad))).astype(BF16)
    KW = B_KEY_WIDTH
    wg = jnp.zeros((LANE, 2 * KW), F32)
    wg = wg.at[0:GATE_RANK, 0:KW].set(w_gate_up[l, 0])
    wg = wg.at[GATE_RANK:2 * GATE_RANK, KW:2 * KW].set(w_gate_up[l, 1])
    bg = jnp.concatenate([b_gate[l, 0], b_gate[l, 1]])[None, :]
    bsp = jnp.repeat(b_sp[l].T, A_GROUP_DIM, axis=1)
    wr = jnp.zeros((LANE, D), F32)
    wr = wr.at[0:N_EXPERTS, :].set(w_router_e[l].T)
    wr = wr.at[N_EXPERTS:N_EXPERTS + N_GROUPS, :].set(w_router_g[l].T)
    br = jnp.zeros((LANE, 1), F32)
    br = br.at[0:N_EXPERTS, 0].set(b_router_e[l])
    br = br.at[N_EXPERTS:N_EXPERTS + N_GROUPS, 0].set(b_router_g[l])
    r = jnp.arange(TT, dtype=I32)
    same = (r[:, None] // GLA_CHUNK) == (r[None, :] // GLA_CHUNK)
    tri_f = (same & (r[:, None] >= r[None, :])).astype(BF16)
    tri_b = (same & (r[:, None] <= r[None, :])).astype(BF16)
    u_strict = (r[:, None] < r[None, :]).astype(BF16)
    eye_k = jnp.eye(B_KEY_WIDTH, dtype=BF16)
    return dict(
        w_in=w_in_p, g_norm1=g_norm1[l][None, :], ln_g=ln_v_g[l][None, :], ln_b=ln_v_b[l][None, :],
        w_sp=w_sp[l].astype(BF16), b_sp=bsp, w_gate=wg.astype(BF16), b_gate=bg,
        g_gla=g_gla[l][None, :], w_out=w_out[l].astype(BF16), g_norm2=g_norm2[l][None, :],
        w_router=wr.astype(BF16), b_router=br, tri_f=tri_f, tri_b=tri_b, u_strict=u_strict, eye_k=eye_k)


def _segment_layout(counts_col, n_tokens):
    counts = counts_col[:N_EXPERTS, 0].astype(I32)
    padded = (counts + MOE_BLK - 1) // MOE_BLK * MOE_BLK
    pad_ends = jnp.cumsum(padded)
    pad_starts = pad_ends - padded
    n_blocks = (n_tokens * 2 + N_EXPERTS * (MOE_BLK - 1)) // MOE_BLK + 1
    block_start = jnp.arange(n_blocks, dtype=I32) * MOE_BLK
    block_expert = jnp.sum((pad_ends[None, :] <= block_start[:, None]).astype(I32), axis=1)
    block_expert = jnp.minimum(block_expert, N_EXPERTS - 1)
    n_used = (pad_ends[-1] // MOE_BLK).astype(I32)[None]
    blk_id = jnp.arange(n_blocks, dtype=I32)
    prev = jnp.concatenate([block_expert[:1], block_expert[:-1]])
    first = ((blk_id == 0) | (block_expert != prev)) & (blk_id < n_used[0])
    w_slot = (jnp.cumsum(first.astype(I32)) - 1) & 1
    e_id = jnp.arange(N_EXPERTS, dtype=I32)
    later_used = (e_id[None, :] > e_id[:, None]) & (padded[None, :] > 0)
    next_used = jnp.min(jnp.where(later_used, e_id[None, :], N_EXPERTS), axis=1)
    next_used = jnp.where(next_used >= N_EXPERTS, -1, next_used)
    next_expert = jnp.sum(jnp.where(block_expert[:, None] == e_id[None, :], next_used[None, :], 0), axis=1).astype(I32)
    return pad_starts, pad_ends, block_expert, n_used, w_slot.astype(I32), next_expert, n_blocks * MOE_BLK


def kernel(x, c, ctx, c_ctx, w_mod, b_mod, g_norm1, w_in, ln_v_g, ln_v_b, w_sp, b_sp, w_gate_up, b_gate, g_gla,
           w_out, g_norm2, w_router_g, b_router_g, w_router_e, b_router_e, w1, w3, w2, g_final):
    Bn, S, D = x.shape
    Lc = ctx.shape[1]
    depth = w_mod.shape[0]
    assert S % DT == 0 and S % ST == 0 and S % CT == 0 and Lc == TT
    assert (Bn * Lc) % DT == 0 and (Bn * Lc) % ST == 0 and (Bn * Lc) % CT == 0
    n_lat = Bn * S
    n_ctx = Bn * Lc
    n_lat_tiles = n_lat // TT
    tiles_per_sample = S // TT
    n_lat_steps = n_lat // ST
    steps_per_sample = S // ST

    cc = jnp.concatenate([c, c_ctx[None, :]], axis=0)
    mod_all = _modulation(cc, w_mod, b_mod)
    gf = g_final[None, :]

    lat, cx, ctx_off = x.reshape(n_lat, D), ctx.reshape(n_ctx, D), 0
    for l in range(depth):
        last = l == depth - 1
        lw = _layer_weights(l, w_in, g_norm1, ln_v_g, ln_v_b, w_sp, b_sp, w_gate_up, b_gate, g_gla, w_out,
                            g_norm2, w_router_g, b_router_g, w_router_e, b_router_e)
        mod3 = mod_all[l].reshape(Bn + 1, 1, 6 * D)
        a_out, qe, oi, sr, kv, dec = _pre_call(lat, cx, ctx_off, n_ctx // ST, mod3, lw, n_lat_steps,
                                               steps_per_sample, Bn)
        states = _scan_call(kv, dec, Bn, n_lat_tiles, tiles_per_sample)
        n_tok = n_lat if last else n_lat + n_ctx
        h1, hn2_rows, route, route_t, counts = _post_call(lat, cx, ctx_off, mod3, (a_out, qe, oi, sr), states, lw,
                                                          n_tok // ST, n_lat_steps, steps_per_sample, Bn)
        pad_starts, pad_ends, block_expert, n_used, w_slot, next_expert, n_buf_rows = _segment_layout(counts, n_tok)
        idx = _index_tiles(_plan_call(route_t, pad_starts))
        xs = _dispatch_call(pad_ends, idx, hn2_rows, n_buf_rows)
        yb = _ffn_call(block_expert, n_used, w_slot, next_expert, xs, w1, w3, w2, l)
        h_all = _combine_call(idx, h1, route, mod3, gf, yb, n_lat // CT, S // CT, Bn, last)
        lat, cx, ctx_off = h_all, h_all, n_lat_steps
    return h_all.reshape(Bn, S, D)
```

```python
import functools

import jax
import jax.numpy as jnp
from jax import lax
from jax.experimental import pallas as pl
from jax.experimental.pallas import tpu as pltpu

F32 = jnp.float32
BF16 = jnp.bfloat16
I32 = jnp.int32

EPS = 1e-6
LANE = 128
SUBLANE = 8

A_GROUPS = 4
A_GROUP_DIM = 128
A_WIDTH = A_GROUPS * A_GROUP_DIM
CHUNK_A = 128
B_HEADS = 4
B_DK = 64
B_DV = 128
B_KEY_WIDTH = B_HEADS * B_DK
B_WIDTH = B_HEADS * B_DV
GATE_RANK = 16
GATE_TAU = 16.0
CHUNK_B = 64
N_GROUPS = 8
EXPERTS_PER_GROUP = 8
N_EXPERTS = N_GROUPS * EXPERTS_PER_GROUP

TT = 256
GLA_CHUNK = 2 * CHUNK_B
CPT = TT // GLA_CHUNK
SUB = 4
ST = SUB * TT
MOE_BLK = 256
DT = 2048
CT = 512
ROW_SUB = 8

VMEM_LIMIT = 48 * 1024 * 1024
VMEM_LIMIT_TOKEN_LOCAL = 56 * 1024 * 1024


def _cparams(sem, vmem_limit=VMEM_LIMIT):
    return pltpu.CompilerParams(dimension_semantics=sem, vmem_limit_bytes=vmem_limit)


def _load_rows(ref, n):
    return jnp.concatenate([ref[pl.ds(s, n, stride=ROW_SUB), :] for s in range(ROW_SUB)], axis=1)


def _store_rows(ref, val, first=0):
    n = val.shape[0]
    for s in range(ROW_SUB):
        ref[pl.ds(first * ROW_SUB + s, n, stride=ROW_SUB), :] = val[:, s * LANE:(s + 1) * LANE]


def _mod_kernel(c_ref, w_ref, b_ref, o_ref):
    c = c_ref[...]
    s = (c * jax.nn.sigmoid(c)).astype(BF16)
    o_ref[0] = jnp.dot(s, w_ref[0].astype(BF16), preferred_element_type=F32) + b_ref[0]


def _modulation(cc, w_mod, b_mod):
    L, D, D6 = w_mod.shape
    R = cc.shape[0]
    tn = 1536
    return pl.pallas_call(
        _mod_kernel,
        grid=(L, D6 // tn),
        in_specs=[
            pl.BlockSpec((R, D), lambda l, j: (0, 0)),
            pl.BlockSpec((1, D, tn), lambda l, j: (l, 0, j)),
            pl.BlockSpec((1, 1, tn), lambda l, j: (l, 0, j)),
        ],
        out_specs=pl.BlockSpec((1, R, tn), lambda l, j: (l, 0, j)),
        out_shape=jax.ShapeDtypeStruct((L, R, D6), F32),
        compiler_params=_cparams(("arbitrary", "arbitrary")),
        name="modulation",
    )(cc, w_mod, b_mod.reshape(L, 1, D6))


def _select_tile(lat_ref, ctx_ref, n_lat_steps):
    return jnp.where(pl.program_id(0) < n_lat_steps, lat_ref[...], ctx_ref[...])


def _stream_specs(D, n_lat_steps, ctx_off):
    lat = pl.BlockSpec((ST, D), lambda i: (jnp.minimum(i, n_lat_steps - 1), 0))
    ctx = pl.BlockSpec((ST, D), lambda i: (ctx_off + jnp.maximum(i - n_lat_steps, 0), 0))
    return lat, ctx


def _pre_kernel(lat_ref, ctx_ref, mod_ref, g1_ref, win_ref, lng_ref, lnb_ref, wsp_ref, bsp_ref, wg_ref, bg_ref,
                trif_ref, trib_ref, eye_ref,
                a_ref, qe_ref, oi_ref, sr_ref, kv_ref, dec_ref, *, n_lat_steps):
    D = lat_ref.shape[1]
    h = _select_tile(lat_ref, ctx_ref, n_lat_steps)
    mod = mod_ref[0]
    sh1 = mod[:, 0:D]
    sc1 = mod[:, D:2 * D]
    ms = jnp.mean(h * h, axis=-1, keepdims=True)
    hn = h * lax.rsqrt(ms + EPS) * g1_ref[...]
    hn = hn * (1.0 + sc1) + sh1
    z = jnp.dot(hn.astype(BF16), win_ref[...], preferred_element_type=F32)
    o_av = A_WIDTH
    o_q = 2 * A_WIDTH
    o_r = o_q + B_KEY_WIDTH
    o_k = o_r + B_WIDTH
    o_v = o_k + B_KEY_WIDTH
    o_g = o_v + B_WIDTH
    zu = z[:, 0:o_av]
    zv = z[:, o_av:o_q]
    q_all = z[:, o_q:o_r] * (B_DK ** -0.5)
    zr = z[:, o_r:o_k]
    k_all = z[:, o_k:o_v]
    vv_all = z[:, o_v:o_g]
    zg = z[:, o_g:o_g + LANE]

    u = jax.nn.gelu(zu)
    v = jax.nn.gelu(zv)
    s_groups = []
    for g in range(A_GROUPS):
        sl = slice(g * A_GROUP_DIM, (g + 1) * A_GROUP_DIM)
        vg = v[:, sl]
        mu = jnp.mean(vg, axis=-1, keepdims=True)
        dv = vg - mu
        var = jnp.mean(dv * dv, axis=-1, keepdims=True)
        vn = (dv * lax.rsqrt(var + EPS)) * lng_ref[:, sl] + lnb_ref[:, sl]
        vnb = vn.astype(BF16)
        n_ch = ST // CHUNK_A
        rhs = jnp.concatenate([vnb[c * CHUNK_A:(c + 1) * CHUNK_A, :] for c in range(n_ch)], axis=1)
        mixed = jnp.dot(wsp_ref[g], rhs, preferred_element_type=F32)
        bias = bsp_ref[:, sl]
        s_groups.append(jnp.concatenate(
            [mixed[:, c * A_GROUP_DIM:(c + 1) * A_GROUP_DIM] + bias for c in range(n_ch)], axis=0))
    a_ref[...] = (u * jnp.concatenate(s_groups, axis=1)).astype(BF16)

    sr_ref[...] = (zr * jax.nn.sigmoid(zr)).astype(BF16)
    lg = jnp.dot(zg.astype(BF16), wg_ref[...], preferred_element_type=F32) + bg_ref[...]
    la_all = (jnp.minimum(lg, 0.0) - jnp.log1p(jnp.exp(-jnp.abs(lg)))) * (1.0 / GATE_TAU)
    for t in range(SUB):
        rs = slice(t * TT, (t + 1) * TT)
        _gla_local(t, rs, la_all[rs, :], q_all[rs, :], k_all[rs, :], vv_all[rs, :],
                   trif_ref, trib_ref, eye_ref, qe_ref, oi_ref, kv_ref, dec_ref)


def _gla_local(t, rs, la, q, k, vv, trif_ref, trib_ref, eye_ref, qe_ref, oi_ref, kv_ref, dec_ref):
    la_hi = la.astype(BF16)
    la_lo = (la - la_hi.astype(F32)).astype(BF16)
    KW = B_KEY_WIDTH
    bf = (jnp.dot(trif_ref[...], la_hi[:, :KW], preferred_element_type=F32)
          + jnp.dot(trif_ref[...], la_lo[:, :KW], preferred_element_type=F32))
    bb = (jnp.dot(trib_ref[...], la_hi[:, KW:], preferred_element_type=F32)
          + jnp.dot(trib_ref[...], la_lo[:, KW:], preferred_element_type=F32))

    def chunk_rows(x, r):
        return jnp.concatenate(
            [jnp.broadcast_to(x[c * GLA_CHUNK + r:c * GLA_CHUNK + r + 1, :], (GLA_CHUNK, KW)) for c in range(CPT)], axis=0)

    blf = chunk_rows(bf, GLA_CHUNK - 1)
    blb = chunk_rows(bb, 0)
    bmf = chunk_rows(bf, CHUNK_B - 1)
    bmb = chunk_rows(bb, CHUNK_B)
    qe_ref[rs, :] = jnp.concatenate([q * jnp.exp(bf), q * jnp.exp(bb)], axis=1).astype(BF16)
    kd_f = k * jnp.exp(blf - bf)
    kd_b = k * jnp.exp(blb - bb)
    qe_f = q * jnp.exp(bf - bmf)
    qe_b = q * jnp.exp(bb - bmb)
    ke_f = (k * jnp.exp(bmf - bf)).astype(BF16)
    ke_b = (k * jnp.exp(bmb - bb)).astype(BF16)
    nt = (((1,), (1,)), ((), ()))
    dec_rows = ([jnp.exp(bf[c * GLA_CHUNK + GLA_CHUNK - 1:(c + 1) * GLA_CHUNK, :]) for c in range(CPT)]
                + [jnp.exp(bb[c * GLA_CHUNK:c * GLA_CHUNK + 1, :]) for c in range(CPT)]
                + [jnp.zeros((LANE - 2 * CPT, KW), F32)])
    dec_pad = jnp.concatenate(dec_rows, axis=0)
    dec_hi = dec_pad.astype(BF16)
    dec_lo = (dec_pad - dec_hi.astype(F32)).astype(BF16)
    dec_ref[t] = (lax.dot_general(eye_ref[...], dec_hi, nt, preferred_element_type=F32)
                  + lax.dot_general(eye_ref[...], dec_lo, nt, preferred_element_type=F32))

    row = lax.broadcasted_iota(I32, (TT, TT), 0)
    col = lax.broadcasted_iota(I32, (TT, TT), 1)
    chunk_shift = GLA_CHUNK.bit_length() - 1
    same = lax.shift_right_logical(row, chunk_shift) == lax.shift_right_logical(col, chunk_shift)
    m_f = same & (row >= col)
    m_b = same & (row <= col)
    lane_head = lax.shift_right_logical(lax.broadcasted_iota(I32, (1, KW), 1), B_DK.bit_length() - 1)
    tok_chunk = lax.shift_right_logical(lax.broadcasted_iota(I32, (1, TT), 1), chunk_shift)
    vb = vv.astype(BF16)
    kdT_f = kd_f.T
    kdT_b = kd_b.T
    oi_heads = []
    for hd in range(B_HEADS):
        hm = lane_head == hd
        qf = jnp.where(hm, qe_f, 0.0).astype(BF16)
        qb = jnp.where(hm, qe_b, 0.0).astype(BF16)
        att_f = lax.dot_general(qf, ke_f, nt, preferred_element_type=F32)
        att_b = lax.dot_general(qb, ke_b, nt, preferred_element_type=F32)
        att = jnp.where(m_f, att_f, 0.0) + jnp.where(m_b, att_b, 0.0)
        v_h = vb[:, hd * B_DV:(hd + 1) * B_DV]
        oi_heads.append(jnp.dot(att.astype(BF16), v_h, preferred_element_type=F32))
        parts = []
        for kdT in (kdT_f, kdT_b):
            kh = kdT[hd * B_DK:(hd + 1) * B_DK, :]
            for c in range(CPT):
                parts.append(jnp.where(tok_chunk == c, kh, 0.0))
        lhs = jnp.concatenate(parts, axis=0).astype(BF16)
        kv_ref[t, hd] = jnp.dot(lhs, v_h, preferred_element_type=F32)
    oi_ref[rs, :] = jnp.concatenate(oi_heads, axis=1)


def _pre_call(lat, ctx, ctx_off, n_ctx_steps, mod3, lw, n_lat_steps, steps_per_sample, n_mod_ctx):
    D = lat.shape[1]
    n_steps = n_lat_steps + n_ctx_steps
    n_tiles = n_steps * SUB
    N = n_steps * ST
    lat_spec, ctx_spec = _stream_specs(D, n_lat_steps, ctx_off)
    DINP = lw["w_in"].shape[1]
    KVR = 2 * CPT * B_DK

    def mod_map(i):
        return (jnp.where(i < n_lat_steps, i // steps_per_sample, n_mod_ctx), 0, 0)

    const2 = lambda i: (0, 0)
    const3 = lambda i: (0, 0, 0)
    tile2 = lambda i: (i, 0)
    return pl.pallas_call(
        functools.partial(_pre_kernel, n_lat_steps=n_lat_steps),
        grid=(n_steps,),
        in_specs=[
            lat_spec,
            ctx_spec,
            pl.BlockSpec((1, 1, 6 * D), mod_map),
            pl.BlockSpec((1, D), const2),
            pl.BlockSpec((D, DINP), const2),
            pl.BlockSpec((1, A_WIDTH), const2),
            pl.BlockSpec((1, A_WIDTH), const2),
            pl.BlockSpec((A_GROUPS, CHUNK_A, CHUNK_A), const3),
            pl.BlockSpec((CHUNK_A, A_WIDTH), const2),
            pl.BlockSpec((LANE, 2 * B_KEY_WIDTH), const2),
            pl.BlockSpec((1, 2 * B_KEY_WIDTH), const2),
            pl.BlockSpec((TT, TT), const2),
            pl.BlockSpec((TT, TT), const2),
            pl.BlockSpec((B_KEY_WIDTH, B_KEY_WIDTH), const2),
        ],
        out_specs=[
            pl.BlockSpec((ST, A_WIDTH), tile2),
            pl.BlockSpec((ST, 2 * B_KEY_WIDTH), tile2),
            pl.BlockSpec((ST, B_WIDTH), tile2),
            pl.BlockSpec((ST, B_WIDTH), tile2),
            pl.BlockSpec((SUB, B_HEADS, KVR, B_DV), lambda i: (i, 0, 0, 0)),
            pl.BlockSpec((SUB, B_KEY_WIDTH, LANE), lambda i: (i, 0, 0)),
        ],
        out_shape=[
            jax.ShapeDtypeStruct((N, A_WIDTH), BF16),
            jax.ShapeDtypeStruct((N, 2 * B_KEY_WIDTH), BF16),
            jax.ShapeDtypeStruct((N, B_WIDTH), F32),
            jax.ShapeDtypeStruct((N, B_WIDTH), BF16),
            jax.ShapeDtypeStruct((n_tiles, B_HEADS, KVR, B_DV), F32),
            jax.ShapeDtypeStruct((n_tiles, B_KEY_WIDTH, LANE), F32),
        ],
        compiler_params=_cparams(("arbitrary",), VMEM_LIMIT_TOKEN_LOCAL),
        name="mix_pre",
    )(lat, ctx, mod3, lw["g_norm1"], lw["w_in"], lw["ln_g"], lw["ln_b"], lw["w_sp"], lw["b_sp"],
      lw["w_gate"], lw["b_gate"], lw["tri_f"], lw["tri_b"], lw["eye_k"])


def _scan_kernel(kvc_ref, kvl_ref, dcc_ref, dcl_ref, sc_ref, sl_ref):
    n_lat = kvl_ref.shape[0]
    fwd = [(kvc_ref, dcc_ref, sc_ref, 0, c) for c in range(CPT)]
    fwd += [(kvl_ref, dcl_ref, sl_ref, t, c) for t in range(n_lat) for c in range(CPT)]
    bwd = [(kvc_ref, dcc_ref, sc_ref, 0, c) for c in reversed(range(CPT))]
    bwd += [(kvl_ref, dcl_ref, sl_ref, t, c) for t in reversed(range(n_lat)) for c in reversed(range(CPT))]
    for hd in range(B_HEADS):
        for d, seq in enumerate((fwd, bwd)):
            s = jnp.zeros((B_DK, B_DV), F32)
            for kv_ref, dc_ref, out_ref, t, c in seq:
                r0 = (d * CPT + c) * B_DK
                out_ref[t, hd, r0:r0 + B_DK, :] = s.astype(BF16)
                dcol = dc_ref[t, hd * B_DK:(hd + 1) * B_DK, d * CPT + c:d * CPT + c + 1]
                s = dcol * s + kv_ref[t, hd, r0:r0 + B_DK, :]


def _scan_call(kv, dec, n_samples, n_lat_tiles, tiles_per_sample):
    n_tiles, _, KVR, _ = kv.shape
    kv_c = pl.BlockSpec((1, B_HEADS, KVR, B_DV), lambda b: (n_lat_tiles + b, 0, 0, 0))
    kv_l = pl.BlockSpec((tiles_per_sample, B_HEADS, KVR, B_DV), lambda b: (b, 0, 0, 0))
    dc_c = pl.BlockSpec((1, B_KEY_WIDTH, LANE), lambda b: (n_lat_tiles + b, 0, 0))
    dc_l = pl.BlockSpec((tiles_per_sample, B_KEY_WIDTH, LANE), lambda b: (b, 0, 0))
    s_ctx, s_lat = pl.pallas_call(
        _scan_kernel,
        grid=(n_samples,),
        in_specs=[kv_c, kv_l, dc_c, dc_l],
        out_specs=[
            pl.BlockSpec((1, B_HEADS, KVR, B_DV), lambda b: (b, 0, 0, 0)),
            pl.BlockSpec((tiles_per_sample, B_HEADS, KVR, B_DV), lambda b: (b, 0, 0, 0)),
        ],
        out_shape=[
            jax.ShapeDtypeStruct((n_samples, B_HEADS, KVR, B_DV), BF16),
            jax.ShapeDtypeStruct((n_lat_tiles, B_HEADS, KVR, B_DV), BF16),
        ],
        compiler_params=_cparams(("arbitrary",)),
        name="gla_scan",
    )(kv, kv, dec, dec)
    return jnp.concatenate([s_lat, s_ctx], axis=0)


def _post_kernel(lat_ref, ctx_ref, mod_ref, a_ref, qe_ref, oi_ref, sr_ref, st_ref, ggla_ref, wout_ref, g2_ref,
                 wr_ref, br_ref, ustr_ref,
                 h1_ref, hn2_ref, route_ref, routet_ref, cnt_ref,
                 rhs_scr, cnt_scr, *, n_lat_steps):
    D = lat_ref.shape[1]
    i = pl.program_id(0)

    @pl.when(i == 0)
    def _():
        rhs_scr[...] = jnp.zeros(rhs_scr.shape, rhs_scr.dtype)
        cnt_scr[...] = jnp.zeros(cnt_scr.shape, cnt_scr.dtype)

    qe = qe_ref[...]
    o_parts = []
    for c in range(SUB * CPT):
        t, ct = divmod(c, CPT)
        for d in range(2):
            for hd in range(B_HEADS):
                r0 = d * B_KEY_WIDTH + hd * B_DK
                s0 = (d * CPT + ct) * B_DK
                rhs_scr[c, r0:r0 + B_DK, hd * B_DV:(hd + 1) * B_DV] = st_ref[t, hd, s0:s0 + B_DK, :]
        o_parts.append(jnp.dot(qe[c * GLA_CHUNK:(c + 1) * GLA_CHUNK, :], rhs_scr[c], preferred_element_type=F32))
    o = oi_ref[...] + jnp.concatenate(o_parts, axis=0)

    heads = []
    for hd in range(B_HEADS):
        sl = slice(hd * B_DV, (hd + 1) * B_DV)
        oh = o[:, sl]
        msq = jnp.mean(oh * oh, axis=-1, keepdims=True)
        heads.append(oh * lax.rsqrt(msq + EPS) * ggla_ref[:, sl])
    b_out = jnp.concatenate(heads, axis=1) * sr_ref[...].astype(F32)
    mix_in = jnp.concatenate([a_ref[...], b_out.astype(BF16)], axis=1)
    mix = jnp.dot(mix_in, wout_ref[...], preferred_element_type=F32)

    mod = mod_ref[0]
    g1 = mod[:, 2 * D:3 * D]
    sh2 = mod[:, 3 * D:4 * D]
    sc2 = mod[:, 4 * D:5 * D]
    h_in = _select_tile(lat_ref, ctx_ref, n_lat_steps)
    nt = (((1,), (1,)), ((), ()))
    neg = jnp.float32(-3.0e38)
    big = jnp.float32(1.0e9)
    sub = lax.broadcasted_iota(I32, (EXPERTS_PER_GROUP, LANE), 0).astype(F32)
    route_parts = []
    for hf in range(ST // LANE):
        rs = slice(hf * LANE, (hf + 1) * LANE)
        h1 = h_in[rs, :] + g1 * mix[rs, :]
        h1_ref[rs, :] = h1
        msq = jnp.mean(h1 * h1, axis=-1, keepdims=True)
        hn2 = h1 * lax.rsqrt(msq + EPS) * g2_ref[...]
        hn2 = hn2 * (1.0 + sc2) + sh2
        _store_rows(hn2_ref, hn2, hf * LANE)

        lgt = lax.dot_general(wr_ref[...], hn2.astype(BF16), nt, preferred_element_type=F32) + br_ref[...]
        gl = lgt[N_EXPERTS:N_EXPERTS + N_GROUPS, :]
        gmax = jnp.max(gl, axis=0, keepdims=True)
        p_top = 1.0 / jnp.sum(jnp.exp(gl - gmax), axis=0, keepdims=True)
        gidx = jnp.min(jnp.where(gl == gmax, sub, big), axis=0, keepdims=True)
        el = jnp.zeros((EXPERTS_PER_GROUP, LANE), F32)
        for g in range(N_GROUPS):
            el = el + jnp.where(gidx == float(g), lgt[g * EXPERTS_PER_GROUP:(g + 1) * EXPERTS_PER_GROUP, :], 0.0)
        m1 = jnp.max(el, axis=0, keepdims=True)
        i1 = jnp.min(jnp.where(el == m1, sub, big), axis=0, keepdims=True)
        el2 = jnp.where(sub == i1, neg, el)
        m2 = jnp.max(el2, axis=0, keepdims=True)
        i2 = jnp.min(jnp.where(el2 == m2, sub, big), axis=0, keepdims=True)
        t = jnp.exp(m2 - m1)
        route_parts.append((gidx * float(EXPERTS_PER_GROUP) + i1, gidx * float(EXPERTS_PER_GROUP) + i2,
                            p_top / (1.0 + t), p_top * t / (1.0 + t)))
    e1_all, e2_all, gate1_all, gate2_all = (jnp.concatenate(p, axis=1) for p in zip(*route_parts))

    rowf = lax.broadcasted_iota(I32, (LANE, TT), 0).astype(F32)
    second = lax.broadcasted_iota(I32, (LANE, 1), 0) >= N_EXPERTS
    base = cnt_scr[...]
    for t in range(SUB):
        ls = slice(t * TT, (t + 1) * TT)
        e1, e2 = e1_all[:, ls], e2_all[:, ls]
        sel1 = rowf == e1
        sel2 = rowf == (e2 + float(N_EXPERTS))
        oh = jnp.where(sel1, 1.0, 0.0) + jnp.where(sel2, 1.0, 0.0)
        prefix = jnp.dot(oh.astype(BF16), ustr_ref[...], preferred_element_type=F32)
        tot = jnp.sum(oh, axis=1, keepdims=True)
        tot_sw = jnp.concatenate([tot[N_EXPERTS:, :], tot[:N_EXPERTS, :]], axis=0)
        val = prefix + (base + jnp.where(second, tot_sw, 0.0))
        r1 = jnp.sum(jnp.where(sel1, val, 0.0), axis=0, keepdims=True)
        r2 = jnp.sum(jnp.where(sel2, val, 0.0), axis=0, keepdims=True)
        base = base + tot + tot_sw
        zero = jnp.zeros_like(e1)
        route_t = jnp.concatenate([e1, e2, gate1_all[:, ls], gate2_all[:, ls], r1, r2, zero, zero], axis=0)
        routet_ref[:, ls] = route_t
        route_ref[ls, :] = jnp.concatenate([route_t, jnp.zeros((LANE - SUBLANE, TT), F32)], axis=0).T
    cnt_scr[...] = base
    cnt_ref[...] = base


def _post_call(lat, ctx, ctx_off, mod3, pre_outs, states, lw, n_post_steps, n_lat_steps, steps_per_sample, n_mod_ctx):
    a_out, qe, oi, sr = pre_outs
    D = lat.shape[1]
    KVR = 2 * CPT * B_DK
    NP = n_post_steps * ST
    lat_spec, ctx_spec = _stream_specs(D, n_lat_steps, ctx_off)

    def mod_map(i):
        return (jnp.where(i < n_lat_steps, i // steps_per_sample, n_mod_ctx), 0, 0)

    const2 = lambda i: (0, 0)
    tile2 = lambda i: (i, 0)
    return pl.pallas_call(
        functools.partial(_post_kernel, n_lat_steps=n_lat_steps),
        grid=(n_post_steps,),
        in_specs=[
            lat_spec,
            ctx_spec,
            pl.BlockSpec((1, 1, 6 * D), mod_map),
            pl.BlockSpec((ST, A_WIDTH), tile2),
            pl.BlockSpec((ST, 2 * B_KEY_WIDTH), tile2),
            pl.BlockSpec((ST, B_WIDTH), tile2),
            pl.BlockSpec((ST, B_WIDTH), tile2),
            pl.BlockSpec((SUB, B_HEADS, KVR, B_DV), lambda i: (i, 0, 0, 0)),
            pl.BlockSpec((1, B_WIDTH), const2),
            pl.BlockSpec((D, D), const2),
            pl.BlockSpec((1, D), const2),
            pl.BlockSpec((LANE, D), const2),
            pl.BlockSpec((LANE, 1), const2),
            pl.BlockSpec((TT, TT), const2),
        ],
        out_specs=[
            pl.BlockSpec((ST, D), tile2),
            pl.BlockSpec((ST * ROW_SUB, LANE), tile2),
            pl.BlockSpec((ST, LANE), tile2),
            pl.BlockSpec((SUBLANE, ST), lambda i: (0, i)),
            pl.BlockSpec((LANE, 1), const2),
        ],
        out_shape=[
            jax.ShapeDtypeStruct((NP, D), F32),
            jax.ShapeDtypeStruct((NP * ROW_SUB, LANE), F32),
            jax.ShapeDtypeStruct((NP, LANE), F32),
            jax.ShapeDtypeStruct((SUBLANE, NP), F32),
            jax.ShapeDtypeStruct((LANE, 1), F32),
        ],
        scratch_shapes=[
            pltpu.VMEM((SUB * CPT, 2 * B_KEY_WIDTH, B_WIDTH), BF16),
            pltpu.VMEM((LANE, 1), F32),
        ],
        compiler_params=_cparams(("arbitrary",), VMEM_LIMIT_TOKEN_LOCAL),
        name="mix_post",
    )(lat, ctx, mod3, a_out, qe, oi, sr, states, lw["g_gla"], lw["w_out"], lw["g_norm2"],
      lw["w_router"], lw["b_router"], lw["u_strict"])


def _plan_kernel(rt_ref, ps_ref, idx_ref):
    sub = lax.broadcasted_iota(I32, (N_EXPERTS, LANE), 0).astype(F32)
    ps = ps_ref[...]
    segs = DT // LANE
    for k in range(2):
        for seg in range(segs):
            sl = slice(seg * LANE, (seg + 1) * LANE)
            e = rt_ref[k:k + 1, sl]
            r = rt_ref[4 + k:5 + k, sl]
            base = jnp.sum(jnp.where(sub == e, ps, 0.0), axis=0, keepdims=True)
            idx_ref[0, k * segs + seg:k * segs + seg + 1, :] = ((base + r) * float(ROW_SUB)).astype(I32)


def _plan_call(route_t, pad_starts):
    n_dt = route_t.shape[1] // DT
    return pl.pallas_call(
        _plan_kernel,
        grid=(n_dt,),
        in_specs=[
            pl.BlockSpec((SUBLANE, DT), lambda i: (0, i)),
            pl.BlockSpec((N_EXPERTS, 1), lambda i: (0, 0)),
        ],
        out_specs=pl.BlockSpec((1, 2 * DT // LANE, LANE), lambda i: (i, 0, 0)),
        out_shape=jax.ShapeDtypeStruct((n_dt, 2 * DT // LANE, LANE), I32),
        compiler_params=_cparams(("arbitrary",)),
        name="moe_plan",
    )(route_t, pad_starts.astype(F32)[:, None])


IDX_UNROLL = 8


def _row_loop(n_rows, fn):
    def body(jo, carry):
        for u in range(IDX_UNROLL):
            fn(jo * IDX_UNROLL + u, jo, u)
        return carry

    lax.fori_loop(0, n_rows // IDX_UNROLL, body, 0)


def _tile_indices(idx_smem, jo, u):
    d0 = pl.multiple_of(idx_smem[jo, u], ROW_SUB)
    d1 = pl.multiple_of(idx_smem[jo, IDX_UNROLL + u], ROW_SUB)
    return d0, d1


def _index_tiles(idx):
    n = idx.shape[0]
    d = idx.reshape(n, 2, DT // IDX_UNROLL, IDX_UNROLL).transpose(0, 2, 1, 3).reshape(n, DT // IDX_UNROLL, 2 * IDX_UNROLL)
    return jnp.pad(d, ((0, 0), (0, 0), (0, LANE - 2 * IDX_UNROLL)))


def _dispatch_kernel(pe_ref, idx_hbm, x_ref, xs_out, idx_smem, zbuf, sem_i, sem_o, sem_z):
    i = pl.program_id(0)
    cp = pltpu.make_async_copy(idx_hbm.at[i], idx_smem, sem_i)
    cp.start()
    blk_rows = MOE_BLK * ROW_SUB

    @pl.when(i == 0)
    def _():
        zbuf[...] = jnp.zeros(zbuf.shape, zbuf.dtype)
        for e in range(N_EXPERTS):
            start = pl.multiple_of(jnp.maximum(pe_ref[e] - MOE_BLK, 0) * ROW_SUB, blk_rows)
            pltpu.make_async_copy(zbuf, xs_out.at[pl.ds(start, blk_rows), :], sem_z).start()
        for e in range(N_EXPERTS):
            pltpu.make_async_copy(zbuf, xs_out.at[pl.ds(0, blk_rows), :], sem_z).wait()
        first_free = pe_ref[N_EXPERTS - 1] // MOE_BLK
        n_blocks = xs_out.shape[0] // blk_rows

        def clear(b, carry):
            start = pl.multiple_of(b * blk_rows, blk_rows)
            pltpu.make_async_copy(zbuf, xs_out.at[pl.ds(start, blk_rows), :], sem_z).start()
            return carry

        def clear_wait(b, carry):
            pltpu.make_async_copy(zbuf, xs_out.at[pl.ds(0, blk_rows), :], sem_z).wait()
            return carry

        lax.fori_loop(first_free, n_blocks, clear, 0)
        lax.fori_loop(first_free, n_blocks, clear_wait, 0)

    cp.wait()

    n = pl.num_programs(0)
    slot = i & 1
    tile_rows = DT * ROW_SUB
    base = i * tile_rows

    def issue(j, jo, u):
        d0, d1 = _tile_indices(idx_smem, jo, u)
        src = x_ref.at[pl.ds(pl.multiple_of(base + j * ROW_SUB, ROW_SUB), ROW_SUB), :]
        pltpu.make_async_copy(src, xs_out.at[pl.ds(d0, ROW_SUB), :], sem_o.at[slot]).start(priority=0)
        pltpu.make_async_copy(src, xs_out.at[pl.ds(d1, ROW_SUB), :], sem_o.at[slot]).start(priority=1)

    _row_loop(DT, issue)

    def wait_step(s):
        for _ in range(2):
            pltpu.make_async_copy(x_ref.at[pl.ds(0, tile_rows), :], xs_out.at[pl.ds(0, tile_rows), :],
                                  sem_o.at[s]).wait()

    @pl.when(i > 0)
    def _():
        wait_step(1 - slot)

    @pl.when(i == n - 1)
    def _():
        wait_step(slot)


def _dispatch_call(pad_ends, idx, x_rows, n_buf_rows):
    n_tiles = x_rows.shape[0] // (DT * ROW_SUB)
    grid_spec = pltpu.PrefetchScalarGridSpec(
        num_scalar_prefetch=1,
        grid=(n_tiles,),
        in_specs=[
            pl.BlockSpec(memory_space=pl.ANY),
            pl.BlockSpec(memory_space=pl.ANY),
        ],
        out_specs=pl.BlockSpec(memory_space=pl.ANY),
        scratch_shapes=[
            pltpu.SMEM((DT // IDX_UNROLL, LANE), I32),
            pltpu.VMEM((MOE_BLK * ROW_SUB, LANE), F32),
            pltpu.SemaphoreType.DMA,
            pltpu.SemaphoreType.DMA((2,)),
            pltpu.SemaphoreType.DMA,
        ],
    )
    return pl.pallas_call(
        _dispatch_kernel,
        grid_spec=grid_spec,
        out_shape=jax.ShapeDtypeStruct((n_buf_rows * ROW_SUB, LANE), F32),
        compiler_params=_cparams(("arbitrary",)),
        name="moe_dispatch",
    )(pad_ends, idx, x_rows)


def _ffn_kernel(be_ref, nu_ref, ws_ref, nx_ref, x_ref, w1_hbm, w3_hbm, w2_hbm, y_ref,
                w1f, w3f, w2f, w1b, w3b, w2b, sem_w, *, layer):
    i = pl.program_id(0)
    nu = nu_ref[0]

    def weight_copies(e, s):
        return (pltpu.make_async_copy(w1_hbm.at[layer, e], w1f.at[s], sem_w.at[s]),
                pltpu.make_async_copy(w3_hbm.at[layer, e], w3f.at[s], sem_w.at[s]),
                pltpu.make_async_copy(w2_hbm.at[layer, e], w2f.at[s], sem_w.at[s]))

    @pl.when(i < nu)
    def _():
        e = be_ref[i]
        s = ws_ref[i]
        first_of_run = (i == 0) | (e != be_ref[jnp.maximum(i - 1, 0)])

        @pl.when(i == 0)
        def _():
            for cp in weight_copies(e, s):
                cp.start()

        @pl.when(first_of_run)
        def _():
            for cp in weight_copies(e, s):
                cp.wait()
            nxt = nx_ref[i]

            @pl.when(nxt >= 0)
            def _():
                for cp in weight_copies(nxt, 1 - s):
                    cp.start(priority=1)

            w1b[...] = w1f[s].astype(BF16)
            w3b[...] = w3f[s].astype(BF16)
            w2b[...] = w2f[s].astype(BF16)

        x = _load_rows(x_ref, MOE_BLK).astype(BF16)
        h1 = jnp.dot(x, w1b[...], preferred_element_type=F32)
        h3 = jnp.dot(x, w3b[...], preferred_element_type=F32)
        hh = (h1 * jax.nn.sigmoid(h1) * h3).astype(BF16)
        y = jnp.dot(hh, w2b[...], preferred_element_type=F32)
        _store_rows(y_ref, y)

    @pl.when(i >= nu)
    def _():
        y_ref[...] = jnp.zeros(y_ref.shape, y_ref.dtype)


def _ffn_call(block_expert, n_used, w_slot, next_expert, xs, w1, w3, w2, l):
    blk_rows = MOE_BLK * ROW_SUB
    NB = xs.shape[0] // blk_rows
    _, _, D, DE = w1.shape
    grid_spec = pltpu.PrefetchScalarGridSpec(
        num_scalar_prefetch=4,
        grid=(NB,),
        in_specs=[
            pl.BlockSpec((blk_rows, LANE), lambda i, be, nu, ws, nx: (jnp.minimum(i, nu[0] - 1), 0)),
            pl.BlockSpec(memory_space=pl.ANY),
            pl.BlockSpec(memory_space=pl.ANY),
            pl.BlockSpec(memory_space=pl.ANY),
        ],
        out_specs=pl.BlockSpec((blk_rows, LANE), lambda i, be, nu, ws, nx: (i, 0)),
        scratch_shapes=[
            pltpu.VMEM((2, D, DE), F32),
            pltpu.VMEM((2, D, DE), F32),
            pltpu.VMEM((2, DE, D), F32),
            pltpu.VMEM((D, DE), BF16),
            pltpu.VMEM((D, DE), BF16),
            pltpu.VMEM((DE, D), BF16),
            pltpu.SemaphoreType.DMA((2,)),
        ],
    )
    return pl.pallas_call(
        functools.partial(_ffn_kernel, layer=l),
        grid_spec=grid_spec,
        out_shape=jax.ShapeDtypeStruct(xs.shape, F32),
        compiler_params=_cparams(("arbitrary",)),
        name="moe_ffn",
    )(block_expert, n_used, w_slot, next_expert, xs, w1, w3, w2)


def _combine_kernel(idx_hbm, h1_ref, route_ref, mod_ref, gf_ref, y_hbm, o_ref, idx_smem, rows0, rows1, sem_i, sem_g,
                    *, final):
    D = h1_ref.shape[1]
    i = pl.program_id(0)
    n = pl.num_programs(0)
    slot = i & 1
    nxt = 1 - slot

    def idx_copy(tile, s):
        return pltpu.make_async_copy(idx_hbm.at[tile], idx_smem.at[s], sem_i.at[s])

    def gather(s):
        def issue(j, jo, u):
            d0, d1 = _tile_indices(idx_smem.at[s], jo, u)
            dst = pl.ds(pl.multiple_of(j * ROW_SUB, ROW_SUB), ROW_SUB)
            pltpu.make_async_copy(y_hbm.at[pl.ds(d0, ROW_SUB), :], rows0.at[s, dst, :], sem_g.at[s]).start(priority=0)
            pltpu.make_async_copy(y_hbm.at[pl.ds(d1, ROW_SUB), :], rows1.at[s, dst, :], sem_g.at[s]).start(priority=1)

        _row_loop(CT, issue)

    @pl.when(i == 0)
    def _():
        cp = idx_copy(0, 0)
        cp.start()
        cp.wait()
        gather(0)

        @pl.when(n > 1)
        def _():
            idx_copy(1, 1).start()

    @pl.when(i + 1 < n)
    def _():
        idx_copy(0, nxt).wait()
        gather(nxt)

    @pl.when(i + 2 < n)
    def _():
        idx_copy(i + 2, slot).start()

    for rows in (rows0, rows1):
        pltpu.make_async_copy(y_hbm.at[pl.ds(0, CT * ROW_SUB), :], rows.at[slot], sem_g.at[slot]).wait()

    route = route_ref[...]
    gate1 = route[:, 2:3]
    gate2 = route[:, 3:4]
    y = gate1 * _load_rows(rows0.at[slot], CT) + gate2 * _load_rows(rows1.at[slot], CT)
    g2 = mod_ref[0][:, 5 * D:6 * D]
    out = h1_ref[...] + g2 * y
    if final:
        msq = jnp.mean(out * out, axis=-1, keepdims=True)
        out = out * lax.rsqrt(msq + EPS) * gf_ref[...]
    o_ref[...] = out


def _combine_call(idx, h1, route, mod3, g_final, yb, n_lat_dt, dt_per_sample, n_mod_ctx, final):
    N, D = h1.shape
    n_tiles = N // CT
    idx = idx.reshape(n_tiles, CT // IDX_UNROLL, LANE)

    def mod_map(i):
        return (jnp.where(i < n_lat_dt, i // dt_per_sample, n_mod_ctx), 0, 0)

    return pl.pallas_call(
        functools.partial(_combine_kernel, final=final),
        grid=(n_tiles,),
        in_specs=[
            pl.BlockSpec(memory_space=pl.ANY),
            pl.BlockSpec((CT, D), lambda i: (i, 0)),
            pl.BlockSpec((CT, LANE), lambda i: (i, 0)),
            pl.BlockSpec((1, 1, 6 * D), mod_map),
            pl.BlockSpec((1, D), lambda i: (0, 0)),
            pl.BlockSpec(memory_space=pl.ANY),
        ],
        out_specs=pl.BlockSpec((CT, D), lambda i: (i, 0)),
        out_shape=jax.ShapeDtypeStruct((N, D), F32),
        scratch_shapes=[
            pltpu.SMEM((2, CT // IDX_UNROLL, LANE), I32),
            pltpu.VMEM((2, CT * ROW_SUB, LANE), F32),
            pltpu.VMEM((2, CT * ROW_SUB, LANE), F32),
            pltpu.SemaphoreType.DMA((2,)),
            pltpu.SemaphoreType.DMA((2,)),
        ],
        compiler_params=_cparams(("arbitrary",)),
        name="moe_combine",
    )(idx, h1, route, mod3, g_final, yb)


def _layer_weights(l, w_in, g_norm1, ln_v_g, ln_v_b, w_sp, b_sp, w_gate_up, b_gate, g_gla, w_out, g_norm2,
                   w_router_g, b_router_g, w_router_e, b_router_e):
    D = w_in.shape[1]
    d_in = w_in.shape[2]
    pad = (-d_in) % LANE
    w_in_p = jnp.pad(w_in[l], ((0, 0), (0, pad))).astype(BF16)
    KW = B_KEY_WIDTH
    wg = jnp.zeros((LANE, 2 * KW), F32)
    wg = wg.at[0:GATE_RANK, 0:KW].set(w_gate_up[l, 0])
    wg = wg.at[GATE_RANK:2 * GATE_RANK, KW:2 * KW].set(w_gate_up[l, 1])
    bg = jnp.concatenate([b_gate[l, 0], b_gate[l, 1]])[None, :]
    bsp = jnp.repeat(b_sp[l].T, A_GROUP_DIM, axis=1)
    wr = jnp.zeros((LANE, D), F32)
    wr = wr.at[0:N_EXPERTS, :].set(w_router_e[l].T)
    wr = wr.at[N_EXPERTS:N_EXPERTS + N_GROUPS, :].set(w_router_g[l].T)
    br = jnp.zeros((LANE, 1), F32)
    br = br.at[0:N_EXPERTS, 0].set(b_router_e[l])
    br = br.at[N_EXPERTS:N_EXPERTS + N_GROUPS, 0].set(b_router_g[l])
    r = jnp.arange(TT, dtype=I32)
    same = (r[:, None] // GLA_CHUNK) == (r[None, :] // GLA_CHUNK)
    tri_f = (same & (r[:, None] >= r[None, :])).astype(BF16)
    tri_b = (same & (r[:, None] <= r[None, :])).astype(BF16)
    u_strict = (r[:, None] < r[None, :]).astype(BF16)
    eye_k = jnp.eye(B_KEY_WIDTH, dtype=BF16)
    return dict(
        w_in=w_in_p, g_norm1=g_norm1[l][None, :], ln_g=ln_v_g[l][None, :], ln_b=ln_v_b[l][None, :],
        w_sp=w_sp[l].astype(BF16), b_sp=bsp, w_gate=wg.astype(BF16), b_gate=bg,
        g_gla=g_gla[l][None, :], w_out=w_out[l].astype(BF16), g_norm2=g_norm2[l][None, :],
        w_router=wr.astype(BF16), b_router=br, tri_f=tri_f, tri_b=tri_b, u_strict=u_strict, eye_k=eye_k)


def _segment_layout(counts_col, n_tokens):
    counts = counts_col[:N_EXPERTS, 0].astype(I32)
    padded = (counts + MOE_BLK - 1) // MOE_BLK * MOE_BLK
    pad_ends = jnp.cumsum(padded)
    pad_starts = pad_ends - padded
    n_blocks = (n_tokens * 2 + N_EXPERTS * (MOE_BLK - 1)) // MOE_BLK + 1
    block_start = jnp.arange(n_blocks, dtype=I32) * MOE_BLK
    block_expert = jnp.sum((pad_ends[None, :] <= block_start[:, None]).astype(I32), axis=1)
    block_expert = jnp.minimum(block_expert, N_EXPERTS - 1)
    n_used = (pad_ends[-1] // MOE_BLK).astype(I32)[None]
    blk_id = jnp.arange(n_blocks, dtype=I32)
    prev = jnp.concatenate([block_expert[:1], block_expert[:-1]])
    first = ((blk_id == 0) | (block_expert != prev)) & (blk_id < n_used[0])
    w_slot = (jnp.cumsum(first.astype(I32)) - 1) & 1
    e_id = jnp.arange(N_EXPERTS, dtype=I32)
    later_used = (e_id[None, :] > e_id[:, None]) & (padded[None, :] > 0)
    next_used = jnp.min(jnp.where(later_used, e_id[None, :], N_EXPERTS), axis=1)
    next_used = jnp.where(next_used >= N_EXPERTS, -1, next_used)
    next_expert = jnp.sum(jnp.where(block_expert[:, None] == e_id[None, :], next_used[None, :], 0), axis=1).astype(I32)
    return pad_starts, pad_ends, block_expert, n_used, w_slot.astype(I32), next_expert, n_blocks * MOE_BLK


def kernel(x, c, ctx, c_ctx, w_mod, b_mod, g_norm1, w_in, ln_v_g, ln_v_b, w_sp, b_sp, w_gate_up, b_gate, g_gla,
           w_out, g_norm2, w_router_g, b_router_g, w_router_e, b_router_e, w1, w3, w2, g_final):
    Bn, S, D = x.shape
    Lc = ctx.shape[1]
    depth = w_mod.shape[0]
    assert S % DT == 0 and S % ST == 0 and S % CT == 0 and Lc == TT
    assert (Bn * Lc) % DT == 0 and (Bn * Lc) % ST == 0 and (Bn * Lc) % CT == 0
    n_lat = Bn * S
    n_ctx = Bn * Lc
    n_lat_tiles = n_lat // TT
    tiles_per_sample = S // TT
    n_lat_steps = n_lat // ST
    steps_per_sample = S // ST

    cc = jnp.concatenate([c, c_ctx[None, :]], axis=0)
    mod_all = _modulation(cc, w_mod, b_mod)
    gf = g_final[None, :]

    lat, cx, ctx_off = x.reshape(n_lat, D), ctx.reshape(n_ctx, D), 0
    for l in range(depth):
        last = l == depth - 1
        lw = _layer_weights(l, w_in, g_norm1, ln_v_g, ln_v_b, w_sp, b_sp, w_gate_up, b_gate, g_gla, w_out,
                            g_norm2, w_router_g, b_router_g, w_router_e, b_router_e)
        mod3 = mod_all[l].reshape(Bn + 1, 1, 6 * D)
        a_out, qe, oi, sr, kv, dec = _pre_call(lat, cx, ctx_off, n_ctx // ST, mod3, lw, n_lat_steps,
                                               steps_per_sample, Bn)
        states = _scan_call(kv, dec, Bn, n_lat_tiles, tiles_per_sample)
        n_tok = n_lat if last else n_lat + n_ctx
        h1, hn2_rows, route, route_t, counts = _post_call(lat, cx, ctx_off, mod3, (a_out, qe, oi, sr), states, lw,
                                                          n_tok // ST, n_lat_steps, steps_per_sample, Bn)
        pad_starts, pad_ends, block_expert, n_used, w_slot, next_expert, n_buf_rows = _segment_layout(counts, n_tok)
        idx = _index_tiles(_plan_call(route_t, pad_starts))
        xs = _dispatch_call(pad_ends, idx, hn2_rows, n_buf_rows)
        yb = _ffn_call(block_expert, n_used, w_slot, next_expert, xs, w1, w3, w2, l)
        h_all = _combine_call(idx, h1, route, mod3, gf, yb, n_lat // CT, S // CT, Bn, last)
        lat, cx, ctx_off = h_all, h_all, n_lat_steps
    return h_all.reshape(Bn, S, D)
```
